```python
import math
import jax
import jax.numpy as jnp
from jax import lax
import numpy as np

D_MODEL = 2048
BATCH = 2
SEQ = 4096
DEPTH = 1
DEC_BATCH = 32
DEC_SEQ = 32
PAST_LEN = 4096

CHUNK = 64
N_META = 16
EPS = 1e-6
CONV_W = 4

SSD_DIM = D_MODEL
SSD_HEAD_DIM = 64
SSD_HEADS = SSD_DIM // SSD_HEAD_DIM
SSD_GROUPS = 4
SSD_STATE = 128
SSD_HEADS_PER_GROUP = SSD_HEADS // SSD_GROUPS
SSD_BC_DIM = SSD_GROUPS * SSD_STATE
SSD_CONV_DIM = SSD_DIM + 2 * SSD_BC_DIM

MLSTM_DIM = D_MODEL
MLSTM_HEADS = 8
MLSTM_HEAD_DIM = MLSTM_DIM // MLSTM_HEADS

MIX_DIM = SSD_DIM + MLSTM_DIM
SPLIT_Z = SSD_DIM
SPLIT_XBC = SPLIT_Z + SSD_CONV_DIM
SPLIT_DT = SPLIT_XBC + SSD_HEADS
SPLIT_XM = SPLIT_DT + MLSTM_DIM
IN_PROJ_DIM = SPLIT_XM + MLSTM_DIM

N_EXPERT_GROUPS = 4
EXPERTS_PER_GROUP = 8
N_EXPERTS = N_EXPERT_GROUPS * EXPERTS_PER_GROUP
TOP_K = 2
D_EXPERT = D_MODEL // 4

kernel_name = 'hymba_ssd_mlstm_hmoe_stream_step'


def rmsnorm(x, w):
    xf = x.astype(jnp.float32)
    y = xf * lax.rsqrt(jnp.mean(xf * xf, axis=-1, keepdims=True) + EPS)
    return (y * w.astype(jnp.float32)).astype(x.dtype)


def group_rmsnorm(x, w, groups):
    shp = x.shape
    xf = x.astype(jnp.float32).reshape(shp[:-1] + (groups, shp[-1] // groups))
    y = xf * lax.rsqrt(jnp.mean(xf * xf, axis=-1, keepdims=True) + EPS)
    return y.reshape(shp) * w.astype(jnp.float32)


def causal_conv(u, buf, w, b):
    full = jnp.concatenate([buf.astype(u.dtype), u], axis=1)
    out = lax.conv_general_dilated(full, w.astype(u.dtype)[:, None, :], window_strides=(1,), padding='VALID',
                                   dimension_numbers=('NWC', 'WIO', 'NWC'), feature_group_count=u.shape[-1])
    return out + b.astype(u.dtype), full[:, -(CONV_W - 1):]


def to_blocks(a, q_len):
    a = a.reshape((a.shape[0], a.shape[1] // q_len, q_len) + a.shape[2:])
    return jnp.moveaxis(a, 1, 0)


def from_blocks(a):
    a = jnp.moveaxis(a, 0, 1)
    return a.reshape((a.shape[0], a.shape[1] * a.shape[2]) + a.shape[3:])


def ssd_scan(xh, dt, b_ssm, c_ssm, a, s0):
    q_len = min(CHUNK, xh.shape[1])
    mask = jnp.tril(jnp.ones((q_len, q_len), dtype=bool))[None, :, :, None]

    def body(s, blk):
        x, d, bm, cm = blk
        acum = jnp.cumsum(d * a, axis=1)
        seg = jnp.where(mask, acum[:, :, None, :] - acum[:, None, :, :], -jnp.inf)
        cb = jnp.repeat(jnp.einsum('btgn,bsgn->btsg', cm, bm), SSD_HEADS_PER_GROUP, axis=-1)
        w = cb * jnp.exp(seg) * d[:, None, :, :]
        ch = jnp.repeat(cm, SSD_HEADS_PER_GROUP, axis=2)
        bh = jnp.repeat(bm, SSD_HEADS_PER_GROUP, axis=2)
        y = jnp.einsum('btsh,bshp->bthp', w, x) + jnp.exp(acum)[..., None] * jnp.einsum('bthn,bhpn->bthp', ch, s)
        tail = jnp.exp(acum[:, -1:, :] - acum) * d
        s = jnp.exp(acum[:, -1, :])[:, :, None, None] * s + jnp.einsum('bsh,bshp,bshn->bhpn', tail, x, bh)
        return s, y

    s, ys = lax.scan(body, s0, tuple(to_blocks(t, q_len) for t in (xh, dt, b_ssm, c_ssm)))
    return from_blocks(ys), s


def mlstm_scan(q, k, v, ig, lf, c0, n0, m0):
    q_len = min(CHUNK, q.shape[1])
    mask = jnp.tril(jnp.ones((q_len, q_len), dtype=bool))[None, :, :, None]

    def body(carry, blk):
        c, n, m = carry
        qb, kb, vb, ib, fb = blk
        bcum = jnp.cumsum(fb, axis=1)
        dlog = jnp.where(mask, bcum[:, :, None, :] - bcum[:, None, :, :] + ib[:, None, :, :], -jnp.inf)
        inter = bcum + m[:, None, :]
        mt = jnp.maximum(inter, jnp.max(dlog, axis=2))
        s = jnp.einsum('bthd,bshd->btsh', qb, kb) * jnp.exp(dlog - mt[:, :, None, :])
        g = jnp.exp(inter - mt)
        num = jnp.einsum('btsh,bshe->bthe', s, vb) + g[..., None] * jnp.einsum('bhed,bthd->bthe', c, qb)
        den = jnp.sum(s, axis=2) + g * jnp.einsum('bhd,bthd->bth', n, qb)
        h = num / jnp.maximum(jnp.abs(den), jnp.exp(-mt))[..., None]
        m_new = mt[:, -1, :]
        gs = jnp.exp(bcum[:, -1:, :] - bcum + ib - m_new[:, None, :])
        gc = jnp.exp(inter[:, -1, :] - m_new)
        c = gc[..., None, None] * c + jnp.einsum('bsh,bshe,bshd->bhed', gs, vb, kb)
        n = gc[..., None] * n + jnp.einsum('bsh,bshd->bhd', gs, kb)
        return (c, n, m_new), h

    (c, n, m), hs = lax.scan(body, (c0, n0, m0), tuple(to_blocks(t, q_len) for t in (q, k, v, ig, lf)))
    return from_blocks(hs), c, n, m


def mixer_scans(seg, s0, c0, n0, m0, a):
    xh, dt, b_ssm, c_ssm, q, k, v, ig, lf = seg
    y_s, s = ssd_scan(xh, dt, b_ssm, c_ssm, a, s0)
    h_m, c, n, m = mlstm_scan(q, k, v, ig, lf, c0, n0, m0)
    return y_s, h_m, s, c, n, m


def hybrid_mixer(h, conv_ssd_buf, conv_mlstm_buf, s0, c0, n0, m0, n_lead,
                 w_in, conv_ssd_w, conv_ssd_b, dt_bias, a_log, d_skip, ssd_norm_w,
                 conv_mlstm_w, conv_mlstm_b, w_q, w_k, w_v, w_igate, b_igate,
                 w_fgate, b_fgate, mlstm_norm_w, w_out):
    f32 = jnp.float32
    bsz, L, _ = h.shape
    proj = h @ w_in.astype(h.dtype)
    z_s = proj[..., :SPLIT_Z]
    xbc = proj[..., SPLIT_Z:SPLIT_XBC]
    dt_raw = proj[..., SPLIT_XBC:SPLIT_DT]
    x_m = proj[..., SPLIT_DT:SPLIT_XM]
    o_raw = proj[..., SPLIT_XM:]
    xbc, conv_ssd_new = causal_conv(xbc, conv_ssd_buf, conv_ssd_w, conv_ssd_b)
    xbc = jax.nn.silu(xbc).astype(f32)
    xh = xbc[..., :SSD_DIM].reshape(bsz, L, SSD_HEADS, SSD_HEAD_DIM)
    b_ssm = xbc[..., SSD_DIM:SSD_DIM + SSD_BC_DIM].reshape(bsz, L, SSD_GROUPS, SSD_STATE)
    c_ssm = xbc[..., SSD_DIM + SSD_BC_DIM:].reshape(bsz, L, SSD_GROUPS, SSD_STATE)
    dt = jax.nn.softplus(dt_raw.astype(f32) + dt_bias.astype(f32))
    a = -jnp.exp(a_log.astype(f32))
    xc, conv_m_new = causal_conv(x_m, conv_mlstm_buf, conv_mlstm_w, conv_mlstm_b)
    xc = jax.nn.silu(xc).astype(f32).reshape(bsz, L, MLSTM_HEADS, MLSTM_HEAD_DIM)
    xv = x_m.astype(f32).reshape(bsz, L, MLSTM_HEADS, MLSTM_HEAD_DIM)
    q = jnp.einsum('blhd,hde->blhe', xc, w_q.astype(f32))
    k = jnp.einsum('blhd,hde->blhe', xc, w_k.astype(f32)) * (MLSTM_HEAD_DIM ** -0.5)
    v = jnp.einsum('blhd,hde->blhe', xv, w_v.astype(f32))
    qkv = jnp.concatenate([q, k, v], axis=2).reshape(bsz, L, 3 * MLSTM_DIM)
    ig = qkv @ w_igate.astype(f32) + b_igate.astype(f32)
    lf = jax.nn.log_sigmoid(qkv @ w_fgate.astype(f32) + b_fgate.astype(f32))
    seg = (xh, dt, b_ssm, c_ssm, q, k, v, ig, lf)
    st = (s0.astype(f32), c0.astype(f32), n0.astype(f32), m0.astype(f32))
    if n_lead:
        y1, h1, s, c, n, m = mixer_scans(tuple(t[:, :n_lead] for t in seg), *st, a)
        y2, h2, s, c, n, m = mixer_scans(tuple(t[:, n_lead:] for t in seg), s, c, n, m, a)
        y_s = jnp.concatenate([y1, y2], axis=1)
        h_m = jnp.concatenate([h1, h2], axis=1)
    else:
        y_s, h_m, s, c, n, m = mixer_scans(seg, *st, a)
    y = (y_s + xh * d_skip.astype(f32)[:, None]).reshape(bsz, L, SSD_DIM)
    y = group_rmsnorm(y * jax.nn.silu(z_s.astype(f32)), ssd_norm_w, SSD_GROUPS)
    hm = group_rmsnorm(h_m.reshape(bsz, L, MLSTM_DIM), mlstm_norm_w, MLSTM_HEADS) * jax.nn.sigmoid(o_raw.astype(f32))
    mixed = jnp.concatenate([y, hm], axis=-1).astype(h.dtype)
    return mixed @ w_out.astype(h.dtype), conv_ssd_new, s, conv_m_new, c, n, m


def hier_moe(h, w_group, b_group, w_router, b_router, w_gate, w_up, w_down):
    f32 = jnp.float32
    shp = h.shape
    x2 = h.reshape(-1, D_MODEL)
    t = x2.shape[0]
    gl = (x2 @ w_group.astype(x2.dtype)).astype(f32) + b_group.astype(f32)
    gp = jax.nn.softmax(gl, axis=-1)
    g_sel = jnp.argmax(gl, axis=-1)
    p_g = jnp.take_along_axis(gp, g_sel[:, None], axis=-1)
    el = ((x2 @ w_router.astype(x2.dtype)).astype(f32) + b_router.astype(f32)).reshape(t, N_EXPERT_GROUPS, EXPERTS_PER_GROUP)
    el_g = jnp.take_along_axis(el, g_sel[:, None, None], axis=1)[:, 0]
    top_p, top_i = lax.top_k(jax.nn.softmax(el_g, axis=-1), TOP_K)
    weights = p_g * top_p / jnp.sum(top_p, axis=-1, keepdims=True)
    expert_ids = g_sel[:, None] * EXPERTS_PER_GROUP + top_i
    dense_w = jnp.einsum('tk,tke->te', weights, jax.nn.one_hot(expert_ids, N_EXPERTS, dtype=f32)).astype(x2.dtype)
    y = jnp.zeros_like(x2)
    for e in range(N_EXPERTS):
        hid = jax.nn.silu(x2 @ w_gate[e].astype(x2.dtype)) * (x2 @ w_up[e].astype(x2.dtype))
        y = y + dense_w[:, e:e + 1] * (hid @ w_down[e].astype(x2.dtype))
    return y.reshape(shp)


def setup_inputs(seed: int = 0) -> dict:
    key = jax.random.key(seed)
    ks = iter(jax.random.split(key, 48))
    f32 = jnp.float32

    def nrm(shape, scale):
        return jax.random.normal(next(ks), shape, f32) * scale

    x_prompt = nrm((BATCH, SEQ, D_MODEL), 1.0)
    x_sample = nrm((DEC_BATCH, DEC_SEQ, D_MODEL), 1.0)
    state_ssd_conv = nrm((DEPTH, DEC_BATCH, CONV_W - 1, SSD_CONV_DIM), 1.0)
    state_ssd = nrm((DEPTH, DEC_BATCH, SSD_HEADS, SSD_HEAD_DIM, SSD_STATE), 0.1)
    state_mlstm_conv = nrm((DEPTH, DEC_BATCH, CONV_W - 1, MLSTM_DIM), 1.0)
    state_mlstm_c = nrm((DEPTH, DEC_BATCH, MLSTM_HEADS, MLSTM_HEAD_DIM, MLSTM_HEAD_DIM), 0.1)
    state_mlstm_n = nrm((DEPTH, DEC_BATCH, MLSTM_HEADS, MLSTM_HEAD_DIM), 0.1)
    state_mlstm_m = nrm((DEPTH, DEC_BATCH, MLSTM_HEADS), 1.0)
    meta_tokens = nrm((N_META, D_MODEL), 1.0)
    norm_mix_w = 1.0 + nrm((DEPTH, D_MODEL), 0.02)
    w_in = nrm((DEPTH, D_MODEL, IN_PROJ_DIM), D_MODEL ** -0.5)
    conv_ssd_w = nrm((DEPTH, CONV_W, SSD_CONV_DIM), CONV_W ** -0.5)
    conv_ssd_b = nrm((DEPTH, SSD_CONV_DIM), 0.02)
    dt0 = jnp.exp(jax.random.uniform(next(ks), (DEPTH, SSD_HEADS), f32, math.log(1e-3), math.log(1e-1)))
    dt_bias = dt0 + jnp.log(-jnp.expm1(-dt0))
    a_log = jnp.log(jax.random.uniform(next(ks), (DEPTH, SSD_HEADS), f32, 1.0, 16.0))
    d_skip = 1.0 + nrm((DEPTH, SSD_HEADS), 0.02)
    ssd_norm_w = 1.0 + nrm((DEPTH, SSD_DIM), 0.02)
    conv_mlstm_w = nrm((DEPTH, CONV_W, MLSTM_DIM), CONV_W ** -0.5)
    conv_mlstm_b = nrm((DEPTH, MLSTM_DIM), 0.02)
    w_q = nrm((DEPTH, MLSTM_HEADS, MLSTM_HEAD_DIM, MLSTM_HEAD_DIM), MLSTM_HEAD_DIM ** -0.5)
    w_k = nrm((DEPTH, MLSTM_HEADS, MLSTM_HEAD_DIM, MLSTM_HEAD_DIM), MLSTM_HEAD_DIM ** -0.5)
    w_v = nrm((DEPTH, MLSTM_HEADS, MLSTM_HEAD_DIM, MLSTM_HEAD_DIM), MLSTM_HEAD_DIM ** -0.5)
    w_igate = nrm((DEPTH, 3 * MLSTM_DIM, MLSTM_HEADS), (3 * MLSTM_DIM) ** -0.5)
    b_igate = nrm((DEPTH, MLSTM_HEADS), 0.1)
    w_fgate = nrm((DEPTH, 3 * MLSTM_DIM, MLSTM_HEADS), (3 * MLSTM_DIM) ** -0.5)
    b_fgate = 3.0 + 3.0 * jax.random.uniform(next(ks), (DEPTH, MLSTM_HEADS), f32)
    mlstm_norm_w = 1.0 + nrm((DEPTH, MLSTM_DIM), 0.02)
    w_out = nrm((DEPTH, MIX_DIM, D_MODEL), MIX_DIM ** -0.5)
    norm_ffn_w = 1.0 + nrm((DEPTH, D_MODEL), 0.02)
    w_group = nrm((DEPTH, D_MODEL, N_EXPERT_GROUPS), D_MODEL ** -0.5)
    b_group = nrm((DEPTH, N_EXPERT_GROUPS), 0.01)
    w_router = nrm((DEPTH, D_MODEL, N_EXPERTS), D_MODEL ** -0.5)
    b_router = nrm((DEPTH, N_EXPERTS), 0.01)
    w_gate = nrm((DEPTH, N_EXPERTS, D_MODEL, D_EXPERT), D_MODEL ** -0.5)
    w_up = nrm((DEPTH, N_EXPERTS, D_MODEL, D_EXPERT), D_MODEL ** -0.5)
    w_down = nrm((DEPTH, N_EXPERTS, D_EXPERT, D_MODEL), D_EXPERT ** -0.5)
    final_norm_w = 1.0 + nrm((D_MODEL,), 0.02)
    return {'x_prompt': x_prompt, 'x_sample': x_sample,
            'state_ssd_conv': state_ssd_conv, 'state_ssd': state_ssd,
            'state_mlstm_conv': state_mlstm_conv, 'state_mlstm_c': state_mlstm_c,
            'state_mlstm_n': state_mlstm_n, 'state_mlstm_m': state_mlstm_m,
            'meta_tokens': meta_tokens, 'norm_mix_w': norm_mix_w, 'w_in': w_in,
            'conv_ssd_w': conv_ssd_w, 'conv_ssd_b': conv_ssd_b, 'dt_bias': dt_bias, 'a_log': a_log,
            'd_skip': d_skip, 'ssd_norm_w': ssd_norm_w, 'conv_mlstm_w': conv_mlstm_w,
            'conv_mlstm_b': conv_mlstm_b, 'w_q': w_q, 'w_k': w_k, 'w_v': w_v,
            'w_igate': w_igate, 'b_igate': b_igate, 'w_fgate': w_fgate, 'b_fgate': b_fgate,
            'mlstm_norm_w': mlstm_norm_w, 'w_out': w_out, 'norm_ffn_w': norm_ffn_w,
            'w_group': w_group, 'b_group': b_group, 'w_router': w_router, 'b_router': b_router,
            'w_gate': w_gate, 'w_up': w_up, 'w_down': w_down, 'final_norm_w': final_norm_w}


def reference(x_prompt, x_sample, state_ssd_conv, state_ssd, state_mlstm_conv, state_mlstm_c,
              state_mlstm_n, state_mlstm_m, meta_tokens, norm_mix_w, w_in, conv_ssd_w, conv_ssd_b,
              dt_bias, a_log, d_skip, ssd_norm_w, conv_mlstm_w, conv_mlstm_b, w_q, w_k, w_v,
              w_igate, b_igate, w_fgate, b_fgate, mlstm_norm_w, w_out, norm_ffn_w, w_group, b_group,
              w_router, b_router, w_gate, w_up, w_down, final_norm_w):
    f32 = jnp.float32
    dty = x_prompt.dtype
    bp = x_prompt.shape[0]
    meta = jnp.broadcast_to(meta_tokens.astype(dty)[None], (bp, N_META, D_MODEL))
    xp = jnp.concatenate([meta, x_prompt], axis=1)
    xs = x_sample
    p_new = [[] for _ in range(6)]
    s_new = [[] for _ in range(6)]
    for l in range(DEPTH):
        mix_w = (w_in[l], conv_ssd_w[l], conv_ssd_b[l], dt_bias[l], a_log[l], d_skip[l], ssd_norm_w[l],
                 conv_mlstm_w[l], conv_mlstm_b[l], w_q[l], w_k[l], w_v[l], w_igate[l], b_igate[l],
                 w_fgate[l], b_fgate[l], mlstm_norm_w[l], w_out[l])
        zp = (jnp.zeros((bp, CONV_W - 1, SSD_CONV_DIM), dty), jnp.zeros((bp, CONV_W - 1, MLSTM_DIM), dty),
              jnp.zeros((bp, SSD_HEADS, SSD_HEAD_DIM, SSD_STATE), f32),
              jnp.zeros((bp, MLSTM_HEADS, MLSTM_HEAD_DIM, MLSTM_HEAD_DIM), f32),
              jnp.zeros((bp, MLSTM_HEADS, MLSTM_HEAD_DIM), f32), jnp.zeros((bp, MLSTM_HEADS), f32))
        out_p, *st_p = hybrid_mixer(rmsnorm(xp, norm_mix_w[l]), *zp, N_META, *mix_w)
        out_s, *st_s = hybrid_mixer(rmsnorm(xs, norm_mix_w[l]), state_ssd_conv[l], state_mlstm_conv[l],
                                    state_ssd[l], state_mlstm_c[l], state_mlstm_n[l], state_mlstm_m[l],
                                    0, *mix_w)
        xp = xp + out_p
        xs = xs + out_s
        moe_w = (w_group[l], b_group[l], w_router[l], b_router[l], w_gate[l], w_up[l], w_down[l])
        xp = xp + hier_moe(rmsnorm(xp, norm_ffn_w[l]), *moe_w)
        xs = xs + hier_moe(rmsnorm(xs, norm_ffn_w[l]), *moe_w)
        for i in range(6):
            p_new[i].append(st_p[i])
            s_new[i].append(st_s[i])
    y_prompt = rmsnorm(xp, final_norm_w)[:, N_META:]
    y_sample = rmsnorm(xs, final_norm_w)
    p_ssd_conv, p_ssd, p_mlstm_conv, p_c, p_n, p_m = [jnp.stack(a) for a in p_new]
    s_ssd_conv, s_ssd, s_mlstm_conv, s_c, s_n, s_m = [jnp.stack(a) for a in s_new]
    return (y_prompt, y_sample, p_ssd_conv, p_ssd, p_mlstm_conv, p_c, p_n, p_m,
            s_ssd_conv, s_ssd, s_mlstm_conv, s_c, s_n, s_m)
```

```python
import functools
from typing import NamedTuple

import jax
import jax.numpy as jnp
from jax import lax
from jax.experimental import pallas as pl
from jax.experimental.pallas import tpu as pltpu

F32 = jnp.float32
BF16 = jnp.bfloat16
EPS = 1e-6
LANES = 128
SUBLANES = 8
VMEM_LIMIT = 52 * 1024 * 1024
HI = lax.Precision.HIGHEST


class Cfg(NamedTuple):
    d_model: int = 2048
    ssd_heads: int = 32
    ssd_head_dim: int = 64
    ssd_groups: int = 4
    ssd_state: int = 128
    ml_heads: int = 8
    ml_head_dim: int = 256
    n_groups: int = 4
    experts_per_group: int = 8
    d_expert: int = 512
    n_meta: int = 16
    conv_w: int = 4
    chunk: int = 128

    @property
    def bc_dim(self):
        return self.ssd_groups * self.ssd_state

    @property
    def xbc_dim(self):
        return self.d_model + 2 * self.bc_dim

    @property
    def n_experts(self):
        return self.n_groups * self.experts_per_group


def _cparams(sem):
    return pltpu.CompilerParams(dimension_semantics=sem, vmem_limit_bytes=VMEM_LIMIT)


def _softplus(x):
    return jnp.maximum(x, 0.0) + jnp.log1p(jnp.exp(-jnp.abs(x)))


def _sigmoid(x):
    return 1.0 / (1.0 + jnp.exp(-x))


def _silu(x):
    return x * _sigmoid(x)


def _rms(x, w):
    return x * lax.rsqrt(jnp.mean(x * x, axis=-1, keepdims=True) + EPS) * w


def _inproj_kernel(x_ref, nw_ref, w_ref, wdt_ref, wdtT_ref, bdt_row_ref, bdt_col_ref,
                   proj_ref, d_ref, dT_ref, h_ref):
    @pl.when(pl.program_id(1) == 0)
    def _():
        hb = _rms(x_ref[...], nw_ref[...]).astype(BF16)
        h_ref[...] = hb
        dt = jnp.dot(hb, wdt_ref[...], preferred_element_type=F32)
        d_ref[...] = _softplus(dt + bdt_row_ref[...])
        dtT = lax.dot_general(wdtT_ref[...], hb, (((1,), (1,)), ((), ())), preferred_element_type=F32)
        dT_ref[...] = _softplus(dtT + bdt_col_ref[...])

    proj_ref[...] = jnp.dot(h_ref[...], w_ref[...], preferred_element_type=F32)


def in_proj(x, norm_w, w_main, w_dt, w_dtT, bdt_row, bdt_col, bm, bn):
    m, d = x.shape
    n = w_main.shape[1]
    grid = (m // bm, n // bn)
    return pl.pallas_call(
        _inproj_kernel,
        out_shape=(jax.ShapeDtypeStruct((m, n), F32), jax.ShapeDtypeStruct((m, LANES), F32),
                   jax.ShapeDtypeStruct((LANES, m), F32)),
        grid=grid,
        in_specs=[pl.BlockSpec((bm, d), lambda i, j: (i, 0)),
                  pl.BlockSpec((1, d), lambda i, j: (0, 0)),
                  pl.BlockSpec((d, bn), lambda i, j: (0, j)),
                  pl.BlockSpec((d, LANES), lambda i, j: (0, 0)),
                  pl.BlockSpec((LANES, d), lambda i, j: (0, 0)),
                  pl.BlockSpec((1, LANES), lambda i, j: (0, 0)),
                  pl.BlockSpec((LANES, 1), lambda i, j: (0, 0))],
        out_specs=(pl.BlockSpec((bm, bn), lambda i, j: (i, j)),
                   pl.BlockSpec((bm, LANES), lambda i, j: (i, 0)),
                   pl.BlockSpec((LANES, bm), lambda i, j: (0, i))),
        scratch_shapes=[pltpu.VMEM((bm, d), BF16)],
        compiler_params=_cparams(("arbitrary", "arbitrary")),
        name="in_proj",
    )(x, norm_w, w_main, w_dt, w_dtT, bdt_row, bdt_col)


def _causal_conv(u, prev, w_ref, b_ref, conv_w):
    lt = u.shape[0]
    row8 = lax.broadcasted_iota(jnp.int32, (SUBLANES, u.shape[1]), 0)
    acc = u * w_ref[conv_w - 1:conv_w, :] + b_ref[...]
    for s in range(1, conv_w):
        rolled = pltpu.roll(u, s, axis=0)
        head = jnp.where(row8 < s, pltpu.roll(prev, s, axis=0), rolled[0:SUBLANES])
        shifted = head if lt == SUBLANES else jnp.concatenate([head, rolled[SUBLANES:]], axis=0)
        acc = acc + shifted * w_ref[conv_w - 1 - s:conv_w - s, :]
    return acc


def _conv_qkv_kernel(cfg, xbc_ref, xm_ref, hxbc_ref, hxm_ref, cwx_ref, cbx_ref, cwm_ref, cbm_ref,
                     wq_ref, wk_ref, wv_ref, wg_ref, wgT_ref, bg_row_ref, bg_col_ref,
                     xa_ref, q_ref, k_ref, v_ref, g_ref, gT_ref, px_ref, pm_ref):
    lt = xbc_ref.shape[0]
    hd = cfg.ml_head_dim
    nh = cfg.ml_heads

    @pl.when(pl.program_id(1) == 0)
    def _():
        px_ref[...] = hxbc_ref[0]
        pm_ref[...] = hxm_ref[0]

    u = xbc_ref[...]
    xa_ref[...] = _silu(_causal_conv(u, px_ref[...], cwx_ref, cbx_ref, cfg.conv_w))
    px_ref[...] = u[lt - SUBLANES:lt]

    xm = xm_ref[...]
    xc = _silu(_causal_conv(xm, pm_ref[...], cwm_ref, cbm_ref, cfg.conv_w)).astype(BF16)
    pm_ref[...] = xm[lt - SUBLANES:lt]
    xv = xm.astype(BF16)
    kscale = hd ** -0.5
    for h in range(nh):
        sl = slice(h * hd, (h + 1) * hd)
        q_ref[:, sl] = jnp.dot(xc[:, sl], wq_ref[h], preferred_element_type=F32)
        k_ref[:, sl] = jnp.dot(xc[:, sl], wk_ref[h], preferred_element_type=F32) * kscale
        v_ref[:, sl] = jnp.dot(xv[:, sl], wv_ref[h], preferred_element_type=F32)
    d = nh * hd
    qb = q_ref[...].astype(BF16)
    kb = k_ref[...].astype(BF16)
    vb = v_ref[...].astype(BF16)
    gcol = (jnp.dot(qb, wg_ref[0:d, :], preferred_element_type=F32)
            + jnp.dot(kb, wg_ref[d:2 * d, :], preferred_element_type=F32)
            + jnp.dot(vb, wg_ref[2 * d:3 * d, :], preferred_element_type=F32)) + bg_row_ref[...]
    nt = (((1,), (1,)), ((), ()))
    grow = (lax.dot_general(wgT_ref[:, 0:d], qb, nt, preferred_element_type=F32)
            + lax.dot_general(wgT_ref[:, d:2 * d], kb, nt, preferred_element_type=F32)
            + lax.dot_general(wgT_ref[:, 2 * d:3 * d], vb, nt, preferred_element_type=F32)) + bg_col_ref[...]
    lane = lax.broadcasted_iota(jnp.int32, gcol.shape, 1)
    g_ref[...] = jnp.where(lane < nh, gcol, -_softplus(-gcol))
    row = lax.broadcasted_iota(jnp.int32, grow.shape, 0)
    gT_ref[0, 0] = jnp.where(row < nh, grow, -_softplus(-grow))


def conv_qkv(cfg, proj, hist_xbc, hist_xm, cwx, cbx, cwm, cbm, wq, wk, wv, wg, wgT, bg_row, bg_col,
             n_seq, seq_len, lt):
    d = cfg.d_model
    xbc = cfg.xbc_dim
    m = n_seq * seq_len
    nt = seq_len // lt
    ng = 2 * cfg.ml_heads
    xbc_blk = (3 * d) // xbc
    row = lambda s, l: (s * nt + l, 0)
    const2 = lambda s, l: (0, 0)
    const3 = lambda s, l: (0, 0, 0)
    return pl.pallas_call(
        functools.partial(_conv_qkv_kernel, cfg),
        out_shape=(jax.ShapeDtypeStruct((m, xbc), F32),
                   jax.ShapeDtypeStruct((m, d), F32), jax.ShapeDtypeStruct((m, d), F32),
                   jax.ShapeDtypeStruct((m, d), F32),
                   jax.ShapeDtypeStruct((m, LANES), F32),
                   jax.ShapeDtypeStruct((n_seq, nt, ng, lt), F32)),
        grid=(n_seq, nt),
        in_specs=[pl.BlockSpec((lt, xbc), lambda s, l: (s * nt + l, xbc_blk)),
                  pl.BlockSpec((lt, d), lambda s, l: (s * nt + l, 1)),
                  pl.BlockSpec((1, SUBLANES, xbc), lambda s, l: (s, 0, 0)),
                  pl.BlockSpec((1, SUBLANES, d), lambda s, l: (s, 0, 0)),
                  pl.BlockSpec((cfg.conv_w, xbc), const2), pl.BlockSpec((1, xbc), const2),
                  pl.BlockSpec((cfg.conv_w, d), const2), pl.BlockSpec((1, d), const2),
                  pl.BlockSpec(wq.shape, const3), pl.BlockSpec(wk.shape, const3),
                  pl.BlockSpec(wv.shape, const3),
                  pl.BlockSpec(wg.shape, const2), pl.BlockSpec(wgT.shape, const2),
                  pl.BlockSpec((1, LANES), const2), pl.BlockSpec((ng, 1), const2)],
        out_specs=(pl.BlockSpec((lt, xbc), row), pl.BlockSpec((lt, d), row), pl.BlockSpec((lt, d), row),
                   pl.BlockSpec((lt, d), row), pl.BlockSpec((lt, LANES), row),
                   pl.BlockSpec((1, 1, ng, lt), lambda s, l: (s, l, 0, 0))),
        scratch_shapes=[pltpu.VMEM((SUBLANES, xbc), F32), pltpu.VMEM((SUBLANES, d), F32)],
        compiler_params=_cparams(("arbitrary", "arbitrary")),
        name="conv_qkv",
    )(proj, proj, hist_xbc, hist_xm, cwx, cbx, cwm, cbm, wq, wk, wv, wg, wgT, bg_row, bg_col)


def _tri(q, lower):
    r = lax.broadcasted_iota(jnp.int32, (q, q), 0)
    c = lax.broadcasted_iota(jnp.int32, (q, q), 1)
    return (c <= r) if lower else (r <= c)


def _ssd_kernel(cfg, xa_ref, d_ref, dT_ref, s0_ref, arow_ref, acol_ref, dskip_ref,
                y_ref, sout_ref, st_ref):
    q = xa_ref.shape[0]
    dm = cfg.d_model
    ns = cfg.ssd_state
    hp = cfg.ssd_head_dim
    hpg = cfg.ssd_heads // cfg.ssd_groups
    heads_per_tile = LANES // hp
    n_tiles = cfg.ssd_heads // heads_per_tile

    @pl.when(pl.program_id(1) == 0)
    def _():
        st_ref[...] = s0_ref[0]

    causal = _tri(q, True)
    tril = causal.astype(F32)
    triu = _tri(q, False).astype(F32)
    dcol = d_ref[...]
    drow = dT_ref[0, 0]
    acum = jnp.dot(tril, dcol * arow_ref[...], precision=HI, preferred_element_type=F32)
    acumT = jnp.dot(drow * acol_ref[...], triu, precision=HI, preferred_element_type=F32)
    nt_dims = (((1,), (1,)), ((), ()))
    tn_dims = (((0,), (0,)), ((), ()))
    lane = lax.broadcasted_iota(jnp.int32, (q, LANES), 1)
    srow = lax.broadcasted_iota(jnp.int32, (LANES, ns), 0)

    cbs = []
    bgs = []
    cgs = []
    for g in range(cfg.ssd_groups):
        bg = xa_ref[:, dm + g * ns: dm + (g + 1) * ns].astype(BF16)
        cg = xa_ref[:, dm + cfg.bc_dim + g * ns: dm + cfg.bc_dim + (g + 1) * ns].astype(BF16)
        cbs.append(lax.dot_general(cg, bg, nt_dims, preferred_element_type=F32))
        bgs.append(bg)
        cgs.append(cg)

    for t in range(n_tiles):
        h0 = t * heads_per_tile
        g = h0 // hpg
        cols = slice(t * LANES, (t + 1) * LANES)
        x = xa_ref[:, cols]
        dsel = jnp.zeros((q, LANES), F32)
        esel = jnp.zeros((q, LANES), F32)
        tsel = jnp.zeros((q, LANES), F32)
        rdec = jnp.zeros((LANES, ns), F32)
        for i in range(heads_per_tile):
            h = h0 + i
            in_head = (lane >= i * hp) & (lane < (i + 1) * hp)
            a_col = acum[:, h:h + 1]
            a_last = acum[q - 1:q, h:h + 1]
            dsel = jnp.where(in_head, dcol[:, h:h + 1], dsel)
            esel = jnp.where(in_head, jnp.exp(a_col), esel)
            tsel = jnp.where(in_head, jnp.exp(a_last - a_col), tsel)
            rdec = jnp.where((srow >= i * hp) & (srow < (i + 1) * hp), jnp.exp(a_last), rdec)
        xd = x * dsel
        y = x * dskip_ref[:, cols]
        for i in range(heads_per_tile):
            h = h0 + i
            in_head = (lane >= i * hp) & (lane < (i + 1) * hp)
            seg = jnp.where(causal, acum[:, h:h + 1] - acumT[h:h + 1, :], -jnp.inf)
            w = (cbs[g] * jnp.exp(seg)).astype(BF16)
            xdh = jnp.where(in_head, xd, 0.0).astype(BF16)
            y = y + jnp.dot(w, xdh, preferred_element_type=F32)
        s_old = st_ref[cols, :]
        ys = lax.dot_general(cgs[g], s_old.astype(BF16), nt_dims, preferred_element_type=F32)
        y_ref[:, cols] = y + esel * ys
        upd = lax.dot_general((xd * tsel).astype(BF16), bgs[g], tn_dims, preferred_element_type=F32)
        st_ref[cols, :] = rdec * s_old + upd

    @pl.when(pl.program_id(1) == pl.num_programs(1) - 1)
    def _():
        sout_ref[0] = st_ref[...]


def ssd_scan(cfg, xa, d, dT, s0, a_row, a_col, dskip, n_seq, seq_len, q):
    dm = cfg.d_model
    nc = seq_len // q
    m = n_seq * seq_len
    row = lambda s, c: (s * nc + c, 0)
    const2 = lambda s, c: (0, 0)
    return pl.pallas_call(
        functools.partial(_ssd_kernel, cfg),
        out_shape=(jax.ShapeDtypeStruct((m, dm), F32),
                   jax.ShapeDtypeStruct((n_seq, dm, cfg.ssd_state), F32)),
        grid=(n_seq, nc),
        in_specs=[pl.BlockSpec((q, cfg.xbc_dim), row),
                  pl.BlockSpec((q, LANES), row),
                  pl.BlockSpec((1, 1, cfg.ssd_heads, q), lambda s, c: (s, c, 0, 0)),
                  pl.BlockSpec((1, dm, cfg.ssd_state), lambda s, c: (s, 0, 0)),
                  pl.BlockSpec((1, LANES), const2),
                  pl.BlockSpec((cfg.ssd_heads, 1), const2),
                  pl.BlockSpec((1, dm), const2)],
        out_specs=(pl.BlockSpec((q, dm), row),
                   pl.BlockSpec((1, dm, cfg.ssd_state), lambda s, c: (s, 0, 0))),
        scratch_shapes=[pltpu.VMEM((dm, cfg.ssd_state), F32)],
        compiler_params=_cparams(("arbitrary", "arbitrary")),
        name="ssd_scan",
    )(xa, d, dT, s0, a_row, a_col, dskip)


def _mlstm_kernel(cfg, q_ref, k_ref, v_ref, g_ref, gT_ref, c0_ref, n0_ref, m0_ref,
                  h_ref, cout_ref, nout_ref, mout_ref, c_ref, n_ref, m_ref):
    ql = q_ref.shape[0]
    hd = cfg.ml_head_dim
    nh = cfg.ml_heads

    @pl.when(pl.program_id(1) == 0)
    def _():
        c_ref[...] = c0_ref[0]
        n_ref[...] = n0_ref[0]
        m_ref[...] = m0_ref[0]

    causal = _tri(ql, True)
    gcol = g_ref[...]
    grow = gT_ref[0, 0]
    bcum = jnp.dot(causal.astype(F32), gcol, precision=HI, preferred_element_type=F32)
    bcumT = jnp.dot(grow, _tri(ql, False).astype(F32), precision=HI, preferred_element_type=F32)
    nt_dims = (((1,), (1,)), ((), ()))
    tn_dims = (((0,), (0,)), ((), ()))

    for h in range(nh):
        sl = slice(h * hd, (h + 1) * hd)
        b_col = bcum[:, nh + h:nh + h + 1]
        b_row = bcumT[nh + h:nh + h + 1, :]
        i_col = gcol[:, h:h + 1]
        i_row = grow[h:h + 1, :]
        m_prev = m_ref[h:h + 1, 0:1]
        dlog = jnp.where(causal, b_col - b_row + i_row, -jnp.inf)
        inter = b_col + m_prev
        mt = jnp.maximum(inter, jnp.max(dlog, axis=1, keepdims=True))
        qh = q_ref[:, sl]
        kh = k_ref[:, sl]
        vh = v_ref[:, sl]
        qb = qh.astype(BF16)
        kb = kh.astype(BF16)
        s = lax.dot_general(qb, kb, nt_dims, preferred_element_type=F32) * jnp.exp(dlog - mt)
        gdec = jnp.exp(inter - mt)
        c_old = c_ref[sl, :]
        n_old = n_ref[h:h + 1, :]
        qc = lax.dot_general(qb, c_old.astype(BF16), nt_dims, preferred_element_type=F32)
        num = jnp.dot(s.astype(BF16), vh.astype(BF16), preferred_element_type=F32) + gdec * qc
        den = jnp.sum(s, axis=1, keepdims=True) + gdec * jnp.sum(qh * n_old, axis=1, keepdims=True)
        h_ref[:, sl] = num / jnp.maximum(jnp.abs(den), jnp.exp(-mt))
        m_new = mt[ql - 1:ql, :]
        gs = jnp.exp(b_col[ql - 1:ql, :] - b_col + i_col - m_new)
        gc = jnp.exp(inter[ql - 1:ql, :] - m_new)
        upd = lax.dot_general((vh * gs).astype(BF16), kb, tn_dims, preferred_element_type=F32)
        c_ref[sl, :] = gc * c_old + upd
        n_ref[h:h + 1, :] = gc * n_old + jnp.sum(gs * kh, axis=0, keepdims=True)
        m_ref[h:h + 1, :] = jnp.broadcast_to(m_new, (1, LANES))

    @pl.when(pl.program_id(1) == pl.num_programs(1) - 1)
    def _():
        cout_ref[0] = c_ref[...]
        nout_ref[0] = n_ref[...]
        mout_ref[0] = m_ref[...]


def mlstm_scan(cfg, qa, ka, va, g, gT, c0, n0, m0, n_seq, seq_len, q):
    d = cfg.d_model
    hd = cfg.ml_head_dim
    nh = cfg.ml_heads
    nc = seq_len // q
    m = n_seq * seq_len
    row = lambda s, c: (s * nc + c, 0)
    st3 = lambda s, c: (s, 0, 0)
    return pl.pallas_call(
        functools.partial(_mlstm_kernel, cfg),
        out_shape=(jax.ShapeDtypeStruct((m, d), F32),
                   jax.ShapeDtypeStruct((n_seq, d, hd), F32),
                   jax.ShapeDtypeStruct((n_seq, nh, hd), F32),
                   jax.ShapeDtypeStruct((n_seq, nh, LANES), F32)),
        grid=(n_seq, nc),
        in_specs=[pl.BlockSpec((q, d), row), pl.BlockSpec((q, d), row), pl.BlockSpec((q, d), row),
                  pl.BlockSpec((q, LANES), row),
                  pl.BlockSpec((1, 1, 2 * nh, q), lambda s, c: (s, c, 0, 0)),
                  pl.BlockSpec((1, d, hd), st3), pl.BlockSpec((1, nh, hd), st3),
                  pl.BlockSpec((1, nh, LANES), st3)],
        out_specs=(pl.BlockSpec((q, d), row),
                   pl.BlockSpec((1, d, hd), st3), pl.BlockSpec((1, nh, hd), st3),
                   pl.BlockSpec((1, nh, LANES), st3)),
        scratch_shapes=[pltpu.VMEM((d, hd), F32), pltpu.VMEM((nh, hd), F32), pltpu.VMEM((nh, LANES), F32)],
        compiler_params=_cparams(("arbitrary", "arbitrary")),
        name="mlstm_scan",
    )(qa, ka, va, g, gT, c0, n0, m0)


def _group_norm(x, w_ref, col0, groups, width):
    parts = []
    for g in range(groups):
        seg = x[:, g * width:(g + 1) * width]
        parts.append(seg * lax.rsqrt(jnp.mean(seg * seg, axis=-1, keepdims=True) + EPS)
                     * w_ref[:, col0 + g * width: col0 + (g + 1) * width])
    return parts


def _outproj_kernel(cfg, ys_ref, z_ref, hm_ref, o_ref, x_ref, nws_ref, nwm_ref, w_ref, out_ref, mix_ref):
    d = cfg.d_model

    @pl.when(pl.program_id(1) == 0)
    def _():
        ws = d // cfg.ssd_groups
        yz = ys_ref[...] * _silu(z_ref[...])
        for g, part in enumerate(_group_norm(yz, nws_ref, 0, cfg.ssd_groups, ws)):
            mix_ref[:, g * ws:(g + 1) * ws] = part.astype(BF16)
        gate = _sigmoid(o_ref[...])
        wm = cfg.ml_head_dim
        for g, part in enumerate(_group_norm(hm_ref[...], nwm_ref, 0, cfg.ml_heads, wm)):
            mix_ref[:, d + g * wm: d + (g + 1) * wm] = (part * gate[:, g * wm:(g + 1) * wm]).astype(BF16)

    out_ref[...] = x_ref[...] + jnp.dot(mix_ref[...], w_ref[...], preferred_element_type=F32)


def out_proj(cfg, ys, proj, hm, x, nws, nwm, w_out, bm, bn):
    m, d = x.shape
    const2 = lambda i, j: (0, 0)
    full = lambda i, j: (i, 0)
    return pl.pallas_call(
        functools.partial(_outproj_kernel, cfg),
        out_shape=jax.ShapeDtypeStruct((m, d), F32),
        grid=(m // bm, d // bn),
        in_specs=[pl.BlockSpec((bm, d), full),
                  pl.BlockSpec((bm, d), lambda i, j: (i, 0)),
                  pl.BlockSpec((bm, d), full),
                  pl.BlockSpec((bm, d), lambda i, j: (i, 2)),
                  pl.BlockSpec((bm, bn), lambda i, j: (i, j)),
                  pl.BlockSpec((1, d), const2), pl.BlockSpec((1, d), const2),
                  pl.BlockSpec((2 * d, bn), lambda i, j: (0, j))],
        out_specs=pl.BlockSpec((bm, bn), lambda i, j: (i, j)),
        scratch_shapes=[pltpu.VMEM((bm, 2 * d), BF16)],
        compiler_params=_cparams(("arbitrary", "arbitrary")),
        name="out_proj",
    )(ys, proj, hm, proj, x, nws, nwm, w_out)


def _router_kernel(cfg, x_ref, nw_ref, wr_ref, br_ref, h_ref, dw_ref):
    ne = cfg.n_experts
    epg = cfg.experts_per_group
    ngr = cfg.n_groups
    hb = _rms(x_ref[...], nw_ref[...]).astype(BF16)
    h_ref[...] = hb
    logits = jnp.dot(hb, wr_ref[...], preferred_element_type=F32) + br_ref[...]
    lane = lax.broadcasted_iota(jnp.int32, logits.shape, 1)
    big = jnp.int32(2 ** 30)
    neg = -jnp.inf

    def first_argmax(vals):
        mx = jnp.max(vals, axis=-1, keepdims=True)
        idx = jnp.min(jnp.where(vals == mx, lane, big), axis=-1, keepdims=True)
        return mx, idx

    is_group = (lane >= ne) & (lane < ne + ngr)
    gl = jnp.where(is_group, logits, neg)
    gmax, gidx = first_argmax(gl)
    p_g = 1.0 / jnp.sum(jnp.exp(gl - gmax), axis=-1, keepdims=True)
    e_lo = (gidx - ne) * epg
    in_sel = (lane >= e_lo) & (lane < e_lo + epg)
    el = jnp.where(in_sel, logits, neg)
    pe = jnp.exp(el - jnp.max(el, axis=-1, keepdims=True))
    pe = jnp.where(in_sel, pe / jnp.sum(pe, axis=-1, keepdims=True), -1.0)
    p1, i1 = first_argmax(pe)
    p2, i2 = first_argmax(jnp.where(lane == i1, -1.0, pe))
    wsum = p1 + p2
    dw_ref[...] = jnp.where(lane == i1, p_g * p1 / wsum, 0.0) + jnp.where(lane == i2, p_g * p2 / wsum, 0.0)


def router(cfg, x1, nw, wr, br, bm):
    m, d = x1.shape
    const2 = lambda i: (0, 0)
    return pl.pallas_call(
        functools.partial(_router_kernel, cfg),
        out_shape=(jax.ShapeDtypeStruct((m, d), BF16), jax.ShapeDtypeStruct((m, LANES), F32)),
        grid=(m // bm,),
        in_specs=[pl.BlockSpec((bm, d), lambda i: (i, 0)), pl.BlockSpec((1, d), const2),
                  pl.BlockSpec((d, LANES), const2), pl.BlockSpec((1, LANES), const2)],
        out_specs=(pl.BlockSpec((bm, d), lambda i: (i, 0)), pl.BlockSpec((bm, LANES), lambda i: (i, 0))),
        compiler_params=_cparams(("arbitrary",)),
        name="router",
    )(x1, nw, wr, br)


def _moe_kernel(cfg, h_ref, dw_ref, x1_ref, wg_ref, wu_ref, wd_ref, fw_ref, y_ref, acc_ref):
    e = pl.program_id(1)

    @pl.when(e == 0)
    def _():
        acc_ref[...] = x1_ref[...]

    hb = h_ref[...]
    hid = _silu(jnp.dot(hb, wg_ref[0], preferred_element_type=F32)) * jnp.dot(hb, wu_ref[0],
                                                                            preferred_element_type=F32)
    dw = dw_ref[...]
    lane = lax.broadcasted_iota(jnp.int32, dw.shape, 1)
    col = jnp.sum(jnp.where(lane == e, dw, 0.0), axis=-1, keepdims=True)
    acc_ref[...] += col * jnp.dot(hid.astype(BF16), wd_ref[0], preferred_element_type=F32)

    @pl.when(e == pl.num_programs(1) - 1)
    def _():
        y_ref[...] = _rms(acc_ref[...], fw_ref[...])


def moe_dense(cfg, h2, dw, x1, wg, wu, wd, fw, bm):
    m, d = x1.shape
    de = cfg.d_expert
    return pl.pallas_call(
        functools.partial(_moe_kernel, cfg),
        out_shape=jax.ShapeDtypeStruct((m, d), F32),
        grid=(m // bm, cfg.n_experts),
        in_specs=[pl.BlockSpec((bm, d), lambda i, e: (i, 0)),
                  pl.BlockSpec((bm, LANES), lambda i, e: (i, 0)),
                  pl.BlockSpec((bm, d), lambda i, e: (i, 0)),
                  pl.BlockSpec((1, d, de), lambda i, e: (e, 0, 0)),
                  pl.BlockSpec((1, d, de), lambda i, e: (e, 0, 0)),
                  pl.BlockSpec((1, de, d), lambda i, e: (e, 0, 0)),
                  pl.BlockSpec((1, d), lambda i, e: (0, 0))],
        out_specs=pl.BlockSpec((bm, d), lambda i, e: (i, 0)),
        scratch_shapes=[pltpu.VMEM((bm, d), F32)],
        compiler_params=_cparams(("arbitrary", "arbitrary")),
        name="moe",
    )(h2, dw, x1, wg, wu, wd, fw)


def _pad_hist(hist):
    return jnp.pad(hist, ((0, 0), (SUBLANES - hist.shape[1], 0), (0, 0)))


def _tile(m, pref):
    return pref if m % pref == 0 else m


def _mixer_segment(cfg, x2d, n_seq, seq_len, hist_xbc, hist_xm, s0, c0, n0, m0, p):
    d = cfg.d_model
    m = n_seq * seq_len
    q = min(cfg.chunk, seq_len)
    nc = seq_len // q
    proj, dcol, dT = in_proj(x2d, p["norm_mix_w"], p["w_main"], p["w_dt"], p["w_dtT"], p["bdt_row"], p["bdt_col"],
                             _tile(m, 512), 1024 if p["w_main"].shape[1] % 1024 == 0 else p["w_main"].shape[1])
    xa, qa, ka, va, g, gT = conv_qkv(cfg, proj, hist_xbc, hist_xm, p["cwx"], p["cbx"], p["cwm"], p["cbm"],
                                     p["wq"], p["wk"], p["wv"], p["wg"], p["wgT"], p["bg_row"], p["bg_col"],
                                     n_seq, seq_len, q)
    dTc = dT[:cfg.ssd_heads].reshape(cfg.ssd_heads, n_seq, nc, q).transpose(1, 2, 0, 3)
    ys, s_new = ssd_scan(cfg, xa, dcol, dTc, s0, p["a_row"], p["a_col"], p["dskip"], n_seq, seq_len, q)
    hm, c_new, n_new, m_new = mlstm_scan(cfg, qa, ka, va, g, gT, c0, n0, m0, n_seq, seq_len, q)
    tail = proj.reshape(n_seq, seq_len, -1)[:, seq_len - (cfg.conv_w - 1):]
    tail_xbc = tail[:, :, 3 * d:]
    tail_xm = tail[:, :, d:2 * d]
    return proj, ys, hm, (tail_xbc, s_new, tail_xm, c_new, n_new, m_new)


def _ffn_segment(cfg, x2d, proj, ys, hm, p):
    m = x2d.shape[0]
    x1 = out_proj(cfg, ys, proj, hm, x2d, p["ssd_norm_w"], p["mlstm_norm_w"], p["w_out"],
                  _tile(m, 256), _tile(cfg.d_model, 512))
    h2, dw = router(cfg, x1, p["norm_ffn_w"], p["wr"], p["br"], _tile(m, 256))
    return moe_dense(cfg, h2, dw, x1, p["w_gate"], p["w_up"], p["w_down"], p["final_norm_w"], _tile(m, 512))


def _prep_params(cfg, norm_mix_w, w_in, conv_ssd_w, conv_ssd_b, dt_bias, a_log, d_skip, ssd_norm_w,
                 conv_mlstm_w, conv_mlstm_b, w_q, w_k, w_v, w_igate, b_igate, w_fgate, b_fgate, mlstm_norm_w,
                 w_out, norm_ffn_w, w_group, b_group, w_router, b_router, w_gate, w_up, w_down, final_norm_w):
    d = cfg.d_model
    hs = cfg.ssd_heads
    nh = cfg.ml_heads
    o_z, o_xbc = d, d + cfg.xbc_dim
    o_dt = o_xbc + hs
    o_xm = o_dt + d
    w_z, w_xbc, w_dt, w_xm, w_o = (w_in[:, :o_z], w_in[:, o_z:o_xbc], w_in[:, o_xbc:o_dt],
                                   w_in[:, o_dt:o_xm], w_in[:, o_xm:])
    row = lambda v: v.reshape(1, -1).astype(F32)
    pad_lanes = lambda a: jnp.pad(a, ((0, 0), (0, LANES - a.shape[1])))
    w_dt_p = pad_lanes(w_dt)
    a = -jnp.exp(a_log.astype(F32))
    w_gates = jnp.concatenate([w_igate, w_fgate], axis=1)
    b_gates = jnp.concatenate([b_igate, b_fgate]).astype(F32)
    ne = cfg.n_experts
    wr = pad_lanes(jnp.concatenate([w_router, w_group], axis=1))
    br = pad_lanes(jnp.concatenate([b_router, b_group]).reshape(1, -1).astype(F32))
    return dict(
        norm_mix_w=row(norm_mix_w),
        w_main=jnp.concatenate([w_z, w_xm, w_o, w_xbc], axis=1).astype(BF16),
        w_dt=w_dt_p.astype(BF16), w_dtT=w_dt_p.T.astype(BF16),
        bdt_row=pad_lanes(row(dt_bias)), bdt_col=pad_lanes(row(dt_bias)).T,
        cwx=conv_ssd_w.astype(F32), cbx=row(conv_ssd_b), cwm=conv_mlstm_w.astype(F32), cbm=row(conv_mlstm_b),
        wq=w_q.astype(BF16), wk=w_k.astype(BF16), wv=w_v.astype(BF16),
        wg=pad_lanes(w_gates).astype(BF16), wgT=w_gates.T.astype(BF16),
        bg_row=pad_lanes(row(b_gates)), bg_col=b_gates.reshape(-1, 1),
        a_row=pad_lanes(row(a)), a_col=a.reshape(-1, 1),
        dskip=row(jnp.repeat(d_skip.astype(F32), cfg.ssd_head_dim)),
        ssd_norm_w=row(ssd_norm_w), mlstm_norm_w=row(mlstm_norm_w), w_out=w_out.astype(BF16),
        norm_ffn_w=row(norm_ffn_w), wr=wr.astype(BF16), br=br,
        w_gate=w_gate.astype(BF16), w_up=w_up.astype(BF16), w_down=w_down.astype(BF16),
        final_norm_w=row(final_norm_w),
    )


def forward(cfg, x_prompt, x_sample, state_ssd_conv, state_ssd, state_mlstm_conv, state_mlstm_c,
            state_mlstm_n, state_mlstm_m, meta_tokens, *weights):
    d = cfg.d_model
    nh = cfg.ml_heads
    hd = cfg.ml_head_dim
    assert state_ssd.shape[0] == 1, "single-layer kernel"
    p = _prep_params(cfg, *[w[0] for w in weights[:-1]], weights[-1])
    bp, lp, _ = x_prompt.shape
    bs, ls, _ = x_sample.shape
    n_meta = meta_tokens.shape[0]

    zeros = lambda *s: jnp.zeros(s, F32)
    _, _, _, st_meta = _mixer_segment(
        cfg, meta_tokens.astype(F32), 1, n_meta, zeros(1, SUBLANES, cfg.xbc_dim), zeros(1, SUBLANES, d),
        zeros(1, d, cfg.ssd_state), zeros(1, d, hd), zeros(1, nh, hd), zeros(1, nh, LANES), p)
    mt_xbc, mt_s, mt_xm, mt_c, mt_n, mt_m = st_meta
    rep = lambda a: jnp.broadcast_to(a, (bp,) + a.shape[1:])

    xp = x_prompt.reshape(bp * lp, d)
    proj_p, ys_p, hm_p, st_p = _mixer_segment(
        cfg, xp, bp, lp, rep(_pad_hist(mt_xbc)), rep(_pad_hist(mt_xm)), rep(mt_s), rep(mt_c), rep(mt_n),
        rep(mt_m), p)
    y_prompt = _ffn_segment(cfg, xp, proj_p, ys_p, hm_p, p).reshape(bp, lp, d)

    xs = x_sample.reshape(bs * ls, d)
    m0 = jnp.broadcast_to(state_mlstm_m[0].astype(F32)[:, :, None], (bs, nh, LANES))
    proj_s, ys_s, hm_s, st_s = _mixer_segment(
        cfg, xs, bs, ls, _pad_hist(state_ssd_conv[0]), _pad_hist(state_mlstm_conv[0]),
        state_ssd[0].reshape(bs, d, cfg.ssd_state), state_mlstm_c[0].reshape(bs, d, hd),
        state_mlstm_n[0], m0, p)
    y_sample = _ffn_segment(cfg, xs, proj_s, ys_s, hm_s, p).reshape(bs, ls, d)

    def pack(st, b):
        t_xbc, s_new, t_xm, c_new, n_new, m_new = st
        return (t_xbc[None], s_new.reshape(1, b, cfg.ssd_heads, cfg.ssd_head_dim, cfg.ssd_state),
                t_xm[None], c_new.reshape(1, b, nh, hd, hd), n_new[None], m_new[None, :, :, 0])

    return (y_prompt, y_sample) + pack(st_p, bp) + pack(st_s, bs)


def kernel(x_prompt, x_sample, state_ssd_conv, state_ssd, state_mlstm_conv, state_mlstm_c, state_mlstm_n, state_mlstm_m, meta_tokens, norm_mix_w, w_in, conv_ssd_w, conv_ssd_b, dt_bias, a_log, d_skip, ssd_norm_w, conv_mlstm_w, conv_mlstm_b, w_q, w_k, w_v, w_igate, b_igate, w_fgate, b_fgate, mlstm_norm_w, w_out, norm_ffn_w, w_group, b_group, w_router, b_router, w_gate, w_up, w_down, final_norm_w):
    return forward(Cfg(), x_prompt, x_sample, state_ssd_conv, state_ssd, state_mlstm_conv, state_mlstm_c,
                   state_mlstm_n, state_mlstm_m, meta_tokens, norm_mix_w, w_in, conv_ssd_w, conv_ssd_b, dt_bias,
                   a_log, d_skip, ssd_norm_w, conv_mlstm_w, conv_mlstm_b, w_q, w_k, w_v, w_igate, b_igate,
                   w_fgate, b_fgate, mlstm_norm_w, w_out, norm_ffn_w, w_group, b_group, w_router, b_router,
                   w_gate, w_up, w_down, final_norm_w)
```

```python
import functools
from typing import NamedTuple

import jax
import jax.numpy as jnp
from jax import lax
from jax.experimental import pallas as pl
from jax.experimental.pallas import tpu as pltpu

F32 = jnp.float32
BF16 = jnp.bfloat16
EPS = 1e-6
LANES = 128
SUBLANES = 8
VMEM_LIMIT = 52 * 1024 * 1024
HI = lax.Precision.HIGHEST


class Cfg(NamedTuple):
    d_model: int = 2048
    ssd_heads: int = 32
    ssd_head_dim: int = 64
    ssd_groups: int = 4
    ssd_state: int = 128
    ml_heads: int = 8
    ml_head_dim: int = 256
    n_groups: int = 4
    experts_per_group: int = 8
    d_expert: int = 512
    n_meta: int = 16
    conv_w: int = 4
    chunk: int = 128

    @property
    def bc_dim(self):
        return self.ssd_groups * self.ssd_state

    @property
    def xbc_dim(self):
        return self.d_model + 2 * self.bc_dim

    @property
    def n_experts(self):
        return self.n_groups * self.experts_per_group


def _cparams(sem):
    return pltpu.CompilerParams(dimension_semantics=sem, vmem_limit_bytes=VMEM_LIMIT)


def _softplus(x):
    return jnp.maximum(x, 0.0) + jnp.log1p(jnp.exp(-jnp.abs(x)))


def _sigmoid(x):
    return 1.0 / (1.0 + jnp.exp(-x))


def _silu(x):
    return x * _sigmoid(x)


def _rms(x, w):
    return x * lax.rsqrt(jnp.mean(x * x, axis=-1, keepdims=True) + EPS) * w


def _inproj_kernel(x_ref, nw_ref, w_ref, wdt_ref, wdtT_ref, bdt_row_ref, bdt_col_ref,
                   proj_ref, d_ref, dT_ref, h_ref):
    @pl.when(pl.program_id(1) == 0)
    def _():
        hb = _rms(x_ref[...], nw_ref[...]).astype(BF16)
        h_ref[...] = hb
        dt = jnp.dot(hb, wdt_ref[...], preferred_element_type=F32)
        d_ref[...] = _softplus(dt + bdt_row_ref[...])
        dtT = lax.dot_general(wdtT_ref[...], hb, (((1,), (1,)), ((), ())), preferred_element_type=F32)
        dT_ref[...] = _softplus(dtT + bdt_col_ref[...])

    proj_ref[...] = jnp.dot(h_ref[...], w_ref[...], preferred_element_type=F32)


def in_proj(x, norm_w, w_main, w_dt, w_dtT, bdt_row, bdt_col, bm, bn):
    m, d = x.shape
    n = w_main.shape[1]
    grid = (m // bm, n // bn)
    return pl.pallas_call(
        _inproj_kernel,
        out_shape=(jax.ShapeDtypeStruct((m, n), F32), jax.ShapeDtypeStruct((m, LANES), F32),
                   jax.ShapeDtypeStruct((LANES, m), F32)),
        grid=grid,
        in_specs=[pl.BlockSpec((bm, d), lambda i, j: (i, 0)),
                  pl.BlockSpec((1, d), lambda i, j: (0, 0)),
                  pl.BlockSpec((d, bn), lambda i, j: (0, j)),
                  pl.BlockSpec((d, LANES), lambda i, j: (0, 0)),
                  pl.BlockSpec((LANES, d), lambda i, j: (0, 0)),
                  pl.BlockSpec((1, LANES), lambda i, j: (0, 0)),
                  pl.BlockSpec((LANES, 1), lambda i, j: (0, 0))],
        out_specs=(pl.BlockSpec((bm, bn), lambda i, j: (i, j)),
                   pl.BlockSpec((bm, LANES), lambda i, j: (i, 0)),
                   pl.BlockSpec((LANES, bm), lambda i, j: (0, i))),
        scratch_shapes=[pltpu.VMEM((bm, d), BF16)],
        compiler_params=_cparams(("arbitrary", "arbitrary")),
        name="in_proj",
    )(x, norm_w, w_main, w_dt, w_dtT, bdt_row, bdt_col)


def _causal_conv(u, prev, w_ref, b_ref, conv_w):
    lt = u.shape[0]
    row8 = lax.broadcasted_iota(jnp.int32, (SUBLANES, u.shape[1]), 0)
    acc = u * w_ref[conv_w - 1:conv_w, :] + b_ref[...]
    for s in range(1, conv_w):
        rolled = pltpu.roll(u, s, axis=0)
        head = jnp.where(row8 < s, pltpu.roll(prev, s, axis=0), rolled[0:SUBLANES])
        shifted = head if lt == SUBLANES else jnp.concatenate([head, rolled[SUBLANES:]], axis=0)
        acc = acc + shifted * w_ref[conv_w - 1 - s:conv_w - s, :]
    return acc


def _conv_qkv_kernel(cfg, xbc_ref, xm_ref, hxbc_ref, hxm_ref, cwx_ref, cbx_ref, cwm_ref, cbm_ref,
                     wq_ref, wk_ref, wv_ref, wg_ref, wgT_ref, bg_row_ref, bg_col_ref,
                     xa_ref, q_ref, k_ref, v_ref, g_ref, gT_ref, px_ref, pm_ref):
    lt = xbc_ref.shape[0]
    hd = cfg.ml_head_dim
    nh = cfg.ml_heads

    @pl.when(pl.program_id(1) == 0)
    def _():
        px_ref[...] = hxbc_ref[0]
        pm_ref[...] = hxm_ref[0]

    u = xbc_ref[...]
    xa_ref[...] = _silu(_causal_conv(u, px_ref[...], cwx_ref, cbx_ref, cfg.conv_w))
    px_ref[...] = u[lt - SUBLANES:lt]

    xm = xm_ref[...]
    xc = _silu(_causal_conv(xm, pm_ref[...], cwm_ref, cbm_ref, cfg.conv_w)).astype(BF16)
    pm_ref[...] = xm[lt - SUBLANES:lt]
    xv = xm.astype(BF16)
    kscale = hd ** -0.5
    for h in range(nh):
        sl = slice(h * hd, (h + 1) * hd)
        q_ref[:, sl] = jnp.dot(xc[:, sl], wq_ref[h], preferred_element_type=F32)
        k_ref[:, sl] = jnp.dot(xc[:, sl], wk_ref[h], preferred_element_type=F32) * kscale
        v_ref[:, sl] = jnp.dot(xv[:, sl], wv_ref[h], preferred_element_type=F32)
    d = nh * hd
    qb = q_ref[...].astype(BF16)
    kb = k_ref[...].astype(BF16)
    vb = v_ref[...].astype(BF16)
    gcol = (jnp.dot(qb, wg_ref[0:d, :], preferred_element_type=F32)
            + jnp.dot(kb, wg_ref[d:2 * d, :], preferred_element_type=F32)
            + jnp.dot(vb, wg_ref[2 * d:3 * d, :], preferred_element_type=F32)) + bg_row_ref[...]
    nt = (((1,), (1,)), ((), ()))
    grow = (lax.dot_general(wgT_ref[:, 0:d], qb, nt, preferred_element_type=F32)
            + lax.dot_general(wgT_ref[:, d:2 * d], kb, nt, preferred_element_type=F32)
            + lax.dot_general(wgT_ref[:, 2 * d:3 * d], vb, nt, preferred_element_type=F32)) + bg_col_ref[...]
    lane = lax.broadcasted_iota(jnp.int32, gcol.shape, 1)
    g_ref[...] = jnp.where(lane < nh, gcol, -_softplus(-gcol))
    row = lax.broadcasted_iota(jnp.int32, grow.shape, 0)
    gT_ref[0, 0] = jnp.where(row < nh, grow, -_softplus(-grow))


def conv_qkv(cfg, proj, hist_xbc, hist_xm, cwx, cbx, cwm, cbm, wq, wk, wv, wg, wgT, bg_row, bg_col,
             n_seq, seq_len, lt):
    d = cfg.d_model
    xbc = cfg.xbc_dim
    m = n_seq * seq_len
    nt = seq_len // lt
    ng = 2 * cfg.ml_heads
    xbc_blk = (3 * d) // xbc
    row = lambda s, l: (s * nt + l, 0)
    const2 = lambda s, l: (0, 0)
    const3 = lambda s, l: (0, 0, 0)
    return pl.pallas_call(
        functools.partial(_conv_qkv_kernel, cfg),
        out_shape=(jax.ShapeDtypeStruct((m, xbc), F32),
                   jax.ShapeDtypeStruct((m, d), F32), jax.ShapeDtypeStruct((m, d), F32),
                   jax.ShapeDtypeStruct((m, d), F32),
                   jax.ShapeDtypeStruct((m, LANES), F32),
                   jax.ShapeDtypeStruct((n_seq, nt, ng, lt), F32)),
        grid=(n_seq, nt),
        in_specs=[pl.BlockSpec((lt, xbc), lambda s, l: (s * nt + l, xbc_blk)),
                  pl.BlockSpec((lt, d), lambda s, l: (s * nt + l, 1)),
                  pl.BlockSpec((1, SUBLANES, xbc), lambda s, l: (s, 0, 0)),
                  pl.BlockSpec((1, SUBLANES, d), lambda s, l: (s, 0, 0)),
                  pl.BlockSpec((cfg.conv_w, xbc), const2), pl.BlockSpec((1, xbc), const2),
                  pl.BlockSpec((cfg.conv_w, d), const2), pl.BlockSpec((1, d), const2),
                  pl.BlockSpec(wq.shape, const3), pl.BlockSpec(wk.shape, const3),
                  pl.BlockSpec(wv.shape, const3),
                  pl.BlockSpec(wg.shape, const2), pl.BlockSpec(wgT.shape, const2),
                  pl.BlockSpec((1, LANES), const2), pl.BlockSpec((ng, 1), const2)],
        out_specs=(pl.BlockSpec((lt, xbc), row), pl.BlockSpec((lt, d), row), pl.BlockSpec((lt, d), row),
                   pl.BlockSpec((lt, d), row), pl.BlockSpec((lt, LANES), row),
                   pl.BlockSpec((1, 1, ng, lt), lambda s, l: (s, l, 0, 0))),
        scratch_shapes=[pltpu.VMEM((SUBLANES, xbc), F32), pltpu.VMEM((SUBLANES, d), F32)],
        compiler_params=_cparams(("arbitrary", "arbitrary")),
        name="conv_qkv",
    )(proj, proj, hist_xbc, hist_xm, cwx, cbx, cwm, cbm, wq, wk, wv, wg, wgT, bg_row, bg_col)


def _tri(q, lower):
    r = lax.broadcasted_iota(jnp.int32, (q, q), 0)
    c = lax.broadcasted_iota(jnp.int32, (q, q), 1)
    return (c <= r) if lower else (r <= c)


def _ssd_kernel(cfg, xa_ref, d_ref, dT_ref, s0_ref, arow_ref, acol_ref, dskip_ref,
                y_ref, sout_ref, st_ref):
    q = xa_ref.shape[0]
    dm = cfg.d_model
    ns = cfg.ssd_state
    hp = cfg.ssd_head_dim
    hpg = cfg.ssd_heads // cfg.ssd_groups
    heads_per_tile = LANES // hp
    n_tiles = cfg.ssd_heads // heads_per_tile

    @pl.when(pl.program_id(1) == 0)
    def _():
        st_ref[...] = s0_ref[0]

    causal = _tri(q, True)
    tril = causal.astype(F32)
    triu = _tri(q, False).astype(F32)
    dcol = d_ref[...]
    drow = dT_ref[0, 0]
    acum = jnp.dot(tril, dcol * arow_ref[...], precision=HI, preferred_element_type=F32)
    acumT = jnp.dot(drow * acol_ref[...], triu, precision=HI, preferred_element_type=F32)
    nt_dims = (((1,), (1,)), ((), ()))
    tn_dims = (((0,), (0,)), ((), ()))
    lane = lax.broadcasted_iota(jnp.int32, (q, LANES), 1)
    srow = lax.broadcasted_iota(jnp.int32, (LANES, ns), 0)

    cbs = []
    bgs = []
    cgs = []
    for g in range(cfg.ssd_groups):
        bg = xa_ref[:, dm + g * ns: dm + (g + 1) * ns].astype(BF16)
        cg = xa_ref[:, dm + cfg.bc_dim + g * ns: dm + cfg.bc_dim + (g + 1) * ns].astype(BF16)
        cbs.append(lax.dot_general(cg, bg, nt_dims, preferred_element_type=F32))
        bgs.append(bg)
        cgs.append(cg)

    for t in range(n_tiles):
        h0 = t * heads_per_tile
        g = h0 // hpg
        cols = slice(t * LANES, (t + 1) * LANES)
        x = xa_ref[:, cols]
        dsel = jnp.zeros((q, LANES), F32)
        esel = jnp.zeros((q, LANES), F32)
        tsel = jnp.zeros((q, LANES), F32)
        rdec = jnp.zeros((LANES, ns), F32)
        for i in range(heads_per_tile):
            h = h0 + i
            in_head = (lane >= i * hp) & (lane < (i + 1) * hp)
            a_col = acum[:, h:h + 1]
            a_last = acum[q - 1:q, h:h + 1]
            dsel = jnp.where(in_head, dcol[:, h:h + 1], dsel)
            esel = jnp.where(in_head, jnp.exp(a_col), esel)
            tsel = jnp.where(in_head, jnp.exp(a_last - a_col), tsel)
            rdec = jnp.where((srow >= i * hp) & (srow < (i + 1) * hp), jnp.exp(a_last), rdec)
        xd = x * dsel
        y = x * dskip_ref[:, cols]
        for i in range(heads_per_tile):
            h = h0 + i
            in_head = (lane >= i * hp) & (lane < (i + 1) * hp)
            seg = jnp.where(causal, acum[:, h:h + 1] - acumT[h:h + 1, :], -jnp.inf)
            w = (cbs[g] * jnp.exp(seg)).astype(BF16)
            xdh = jnp.where(in_head, xd, 0.0).astype(BF16)
            y = y + jnp.dot(w, xdh, preferred_element_type=F32)
        s_old = st_ref[cols, :]
        ys = lax.dot_general(cgs[g], s_old.astype(BF16), nt_dims, preferred_element_type=F32)
        y_ref[:, cols] = y + esel * ys
        upd = lax.dot_general((xd * tsel).astype(BF16), bgs[g], tn_dims, preferred_element_type=F32)
        st_ref[cols, :] = rdec * s_old + upd

    @pl.when(pl.program_id(1) == pl.num_programs(1) - 1)
    def _():
        sout_ref[0] = st_ref[...]


def ssd_scan(cfg, xa, d, dT, s0, a_row, a_col, dskip, n_seq, seq_len, q):
    dm = cfg.d_model
    nc = seq_len // q
    m = n_seq * seq_len
    row = lambda s, c: (s * nc + c, 0)
    const2 = lambda s, c: (0, 0)
    return pl.pallas_call(
        functools.partial(_ssd_kernel, cfg),
        out_shape=(jax.ShapeDtypeStruct((m, dm), F32),
                   jax.ShapeDtypeStruct((n_seq, dm, cfg.ssd_state), F32)),
        grid=(n_seq, nc),
        in_specs=[pl.BlockSpec((q, cfg.xbc_dim), row),
                  pl.BlockSpec((q, LANES), row),
                  pl.BlockSpec((1, 1, cfg.ssd_heads, q), lambda s, c: (s, c, 0, 0)),
                  pl.BlockSpec((1, dm, cfg.ssd_state), lambda s, c: (s, 0, 0)),
                  pl.BlockSpec((1, LANES), const2),
                  pl.BlockSpec((cfg.ssd_heads, 1), const2),
                  pl.BlockSpec((1, dm), const2)],
        out_specs=(pl.BlockSpec((q, dm), row),
                   pl.BlockSpec((1, dm, cfg.ssd_state), lambda s, c: (s, 0, 0))),
        scratch_shapes=[pltpu.VMEM((dm, cfg.ssd_state), F32)],
        compiler_params=_cparams(("arbitrary", "arbitrary")),
        name="ssd_scan",
    )(xa, d, dT, s0, a_row, a_col, dskip)


def _mlstm_kernel(cfg, q_ref, k_ref, v_ref, g_ref, gT_ref, c0_ref, n0_ref, m0_ref,
                  h_ref, cout_ref, nout_ref, mout_ref, c_ref, n_ref, m_ref):
    ql = q_ref.shape[0]
    hd = cfg.ml_head_dim
    nh = cfg.ml_heads

    @pl.when(pl.program_id(1) == 0)
    def _():
        c_ref[...] = c0_ref[0]
        n_ref[...] = n0_ref[0]
        m_ref[...] = m0_ref[0]

    causal = _tri(ql, True)
    gcol = g_ref[...]
    grow = gT_ref[0, 0]
    bcum = jnp.dot(causal.astype(F32), gcol, precision=HI, preferred_element_type=F32)
    bcumT = jnp.dot(grow, _tri(ql, False).astype(F32), precision=HI, preferred_element_type=F32)
    nt_dims = (((1,), (1,)), ((), ()))
    tn_dims = (((0,), (0,)), ((), ()))

    for h in range(nh):
        sl = slice(h * hd, (h + 1) * hd)
        b_col = bcum[:, nh + h:nh + h + 1]
        b_row = bcumT[nh + h:nh + h + 1, :]
        i_col = gcol[:, h:h + 1]
        i_row = grow[h:h + 1, :]
        m_prev = m_ref[h:h + 1, 0:1]
        dlog = jnp.where(causal, b_col - b_row + i_row, -jnp.inf)
        inter = b_col + m_prev
        mt = jnp.maximum(inter, jnp.max(dlog, axis=1, keepdims=True))
        qh = q_ref[:, sl]
        kh = k_ref[:, sl]
        vh = v_ref[:, sl]
        qb = qh.astype(BF16)
        kb = kh.astype(BF16)
        s = lax.dot_general(qb, kb, nt_dims, preferred_element_type=F32) * jnp.exp(dlog - mt)
        gdec = jnp.exp(inter - mt)
        c_old = c_ref[sl, :]
        n_old = n_ref[h:h + 1, :]
        qc = lax.dot_general(qb, c_old.astype(BF16), nt_dims, preferred_element_type=F32)
        num = jnp.dot(s.astype(BF16), vh.astype(BF16), preferred_element_type=F32) + gdec * qc
        den = jnp.sum(s, axis=1, keepdims=True) + gdec * jnp.sum(qh * n_old, axis=1, keepdims=True)
        h_ref[:, sl] = num / jnp.maximum(jnp.abs(den), jnp.exp(-mt))
        m_new = mt[ql - 1:ql, :]
        gs = jnp.exp(b_col[ql - 1:ql, :] - b_col + i_col - m_new)
        gc = jnp.exp(inter[ql - 1:ql, :] - m_new)
        upd = lax.dot_general((vh * gs).astype(BF16), kb, tn_dims, preferred_element_type=F32)
        c_ref[sl, :] = gc * c_old + upd
        n_ref[h:h + 1, :] = gc * n_old + jnp.sum(gs * kh, axis=0, keepdims=True)
        m_ref[h:h + 1, :] = jnp.broadcast_to(m_new, (1, LANES))

    @pl.when(pl.program_id(1) == pl.num_programs(1) - 1)
    def _():
        cout_ref[0] = c_ref[...]
        nout_ref[0] = n_ref[...]
        mout_ref[0] = m_ref[...]


def mlstm_scan(cfg, qa, ka, va, g, gT, c0, n0, m0, n_seq, seq_len, q):
    d = cfg.d_model
    hd = cfg.ml_head_dim
    nh = cfg.ml_heads
    nc = seq_len // q
    m = n_seq * seq_len
    row = lambda s, c: (s * nc + c, 0)
    st3 = lambda s, c: (s, 0, 0)
    return pl.pallas_call(
        functools.partial(_mlstm_kernel, cfg),
        out_shape=(jax.ShapeDtypeStruct((m, d), F32),
                   jax.ShapeDtypeStruct((n_seq, d, hd), F32),
                   jax.ShapeDtypeStruct((n_seq, nh, hd), F32),
                   jax.ShapeDtypeStruct((n_seq, nh, LANES), F32)),
        grid=(n_seq, nc),
        in_specs=[pl.BlockSpec((q, d), row), pl.BlockSpec((q, d), row), pl.BlockSpec((q, d), row),
                  pl.BlockSpec((q, LANES), row),
                  pl.BlockSpec((1, 1, 2 * nh, q), lambda s, c: (s, c, 0, 0)),
                  pl.BlockSpec((1, d, hd), st3), pl.BlockSpec((1, nh, hd), st3),
                  pl.BlockSpec((1, nh, LANES), st3)],
        out_specs=(pl.BlockSpec((q, d), row),
                   pl.BlockSpec((1, d, hd), st3), pl.BlockSpec((1, nh, hd), st3),
                   pl.BlockSpec((1, nh, LANES), st3)),
        scratch_shapes=[pltpu.VMEM((d, hd), F32), pltpu.VMEM((nh, hd), F32), pltpu.VMEM((nh, LANES), F32)],
        compiler_params=_cparams(("arbitrary", "arbitrary")),
        name="mlstm_scan",
    )(qa, ka, va, g, gT, c0, n0, m0)


def _group_norm(x, w_ref, col0, groups, width):
    parts = []
    for g in range(groups):
        seg = x[:, g * width:(g + 1) * width]
        parts.append(seg * lax.rsqrt(jnp.mean(seg * seg, axis=-1, keepdims=True) + EPS)
                     * w_ref[:, col0 + g * width: col0 + (g + 1) * width])
    return parts


def _outproj_kernel(cfg, ys_ref, z_ref, hm_ref, o_ref, x_ref, nws_ref, nwm_ref, w_ref, out_ref, mix_ref):
    d = cfg.d_model

    @pl.when(pl.program_id(1) == 0)
    def _():
        ws = d // cfg.ssd_groups
        yz = ys_ref[...] * _silu(z_ref[...])
        for g, part in enumerate(_group_norm(yz, nws_ref, 0, cfg.ssd_groups, ws)):
            mix_ref[:, g * ws:(g + 1) * ws] = part.astype(BF16)
        gate = _sigmoid(o_ref[...])
        wm = cfg.ml_head_dim
        for g, part in enumerate(_group_norm(hm_ref[...], nwm_ref, 0, cfg.ml_heads, wm)):
            mix_ref[:, d + g * wm: d + (g + 1) * wm] = (part * gate[:, g * wm:(g + 1) * wm]).astype(BF16)

    out_ref[...] = x_ref[...] + jnp.dot(mix_ref[...], w_ref[...], preferred_element_type=F32)


def out_proj(cfg, ys, proj, hm, x, nws, nwm, w_out, bm, bn):
    m, d = x.shape
    const2 = lambda i, j: (0, 0)
    full = lambda i, j: (i, 0)
    w_mode = dict(pipeline_mode=pl.Buffered(1)) if bn == d else {}
    return pl.pallas_call(
        functools.partial(_outproj_kernel, cfg),
        out_shape=jax.ShapeDtypeStruct((m, d), F32),
        grid=(m // bm, d // bn),
        in_specs=[pl.BlockSpec((bm, d), full),
                  pl.BlockSpec((bm, d), lambda i, j: (i, 0)),
                  pl.BlockSpec((bm, d), full),
                  pl.BlockSpec((bm, d), lambda i, j: (i, 2)),
                  pl.BlockSpec((bm, bn), lambda i, j: (i, j)),
                  pl.BlockSpec((1, d), const2), pl.BlockSpec((1, d), const2),
                  pl.BlockSpec((2 * d, bn), lambda i, j: (0, j), **w_mode)],
        out_specs=pl.BlockSpec((bm, bn), lambda i, j: (i, j)),
        scratch_shapes=[pltpu.VMEM((bm, 2 * d), BF16)],
        compiler_params=_cparams(("arbitrary", "arbitrary")),
        name="out_proj",
    )(ys, proj, hm, proj, x, nws, nwm, w_out)


def _router_kernel(cfg, x_ref, nw_ref, wr_ref, br_ref, cnt_in_ref, ei_ref, wt_ref, cnt_out_ref, cnt_ref):
    ne = cfg.n_experts
    epg = cfg.experts_per_group
    ngr = cfg.n_groups
    bm = x_ref.shape[0]

    @pl.when(pl.program_id(0) == 0)
    def _():
        cnt_ref[...] = cnt_in_ref[...]

    hb = _rms(x_ref[...], nw_ref[...]).astype(BF16)
    logits = jnp.dot(hb, wr_ref[...], preferred_element_type=F32) + br_ref[...]
    lane = lax.broadcasted_iota(jnp.int32, logits.shape, 1)
    big = jnp.int32(2 ** 30)
    neg = -jnp.inf

    def first_argmax(vals):
        mx = jnp.max(vals, axis=-1, keepdims=True)
        idx = jnp.min(jnp.where(vals == mx, lane, big), axis=-1, keepdims=True)
        return mx, idx

    is_group = (lane >= ne) & (lane < ne + ngr)
    gl = jnp.where(is_group, logits, neg)
    gmax, gidx = first_argmax(gl)
    p_g = 1.0 / jnp.sum(jnp.exp(gl - gmax), axis=-1, keepdims=True)
    e_lo = (gidx - ne) * epg
    in_sel = (lane >= e_lo) & (lane < e_lo + epg)
    el = jnp.where(in_sel, logits, neg)
    pe = jnp.exp(el - jnp.max(el, axis=-1, keepdims=True))
    pe = jnp.where(in_sel, pe / jnp.sum(pe, axis=-1, keepdims=True), -1.0)
    p1, i1 = first_argmax(pe)
    p2, i2 = first_argmax(jnp.where(lane == i1, -1.0, pe))
    wsum = p1 + p2
    wt_ref[...] = jnp.where(lane == 0, p_g * p1 / wsum, jnp.where(lane == 1, p_g * p2 / wsum, 0.0))

    oh1 = jnp.where(lane == i1, 1.0, 0.0)
    oh2 = jnp.where(lane == i2, 1.0, 0.0)
    r = lax.broadcasted_iota(jnp.int32, (bm, bm), 0)
    c = lax.broadcasted_iota(jnp.int32, (bm, bm), 1)
    before = jnp.where(c < r, 1.0, 0.0).astype(BF16)
    ahead1 = jnp.dot(before, oh1.astype(BF16), preferred_element_type=F32)
    ahead2 = jnp.dot(before, oh2.astype(BF16), preferred_element_type=F32)
    cnt = cnt_ref[...]
    tot1 = jnp.sum(oh1, axis=0, keepdims=True)
    rank1 = jnp.sum(oh1 * (cnt + ahead1), axis=-1, keepdims=True)
    rank2 = jnp.sum(oh2 * (cnt + tot1 + ahead2), axis=-1, keepdims=True)
    cnt_new = cnt + tot1 + jnp.sum(oh2, axis=0, keepdims=True)
    cnt_ref[...] = cnt_new
    cnt_out_ref[...] = cnt_new
    ei_ref[...] = jnp.where(lane == 0, i1, jnp.where(lane == 1, i2, jnp.where(
        lane == 2, rank1.astype(jnp.int32), jnp.where(lane == 3, rank2.astype(jnp.int32), 0))))


def router(cfg, x1, nw, wr, br, cnt_in, bm):
    m, d = x1.shape
    const2 = lambda i: (0, 0)
    return pl.pallas_call(
        functools.partial(_router_kernel, cfg),
        out_shape=(jax.ShapeDtypeStruct((m, LANES), jnp.int32), jax.ShapeDtypeStruct((m, LANES), F32),
                   jax.ShapeDtypeStruct((1, LANES), F32)),
        grid=(m // bm,),
        in_specs=[pl.BlockSpec((bm, d), lambda i: (i, 0)), pl.BlockSpec((1, d), const2),
                  pl.BlockSpec((d, LANES), const2), pl.BlockSpec((1, LANES), const2),
                  pl.BlockSpec((1, LANES), const2)],
        out_specs=(pl.BlockSpec((bm, LANES), lambda i: (i, 0)), pl.BlockSpec((bm, LANES), lambda i: (i, 0)),
                   pl.BlockSpec((1, LANES), const2)),
        scratch_shapes=[pltpu.VMEM((1, LANES), F32)],
        compiler_params=_cparams(("arbitrary",)),
        name="router",
    )(x1, nw, wr, br, cnt_in)


def _rows_to_lanes(g):
    t = pltpu.einshape("rcl->crl", g)
    return jnp.concatenate([t[c] for c in range(t.shape[0])], axis=-1)


def _lanes_to_rows(x):
    parts = jnp.stack([x[:, c * LANES:(c + 1) * LANES] for c in range(x.shape[1] // LANES)], axis=0)
    return pltpu.einshape("crl->rcl", parts)


def _gather_rows(idx_ref, src_hbm, dst, sem, n):
    def body(r, carry):
        pltpu.make_async_copy(src_hbm.at[idx_ref[0, 0, r]], dst.at[r], sem).start()
        return carry
    lax.fori_loop(0, n, body, 0, unroll=8)


def _wait_rows(src_hbm, dst, sem, n):
    pltpu.make_async_copy(src_hbm.at[pl.ds(0, n)], dst, sem).wait()


def _moe_kernel(cfg, te_ref, nv_ref, src_ref, srcn_ref, wrow_ref, x_hbm, nw_ref, wg_ref, wu_ref, wd_ref,
                ys_ref, xbuf, sem, wgb, wub, wdb):
    j = pl.program_id(0)
    tm = xbuf.shape[1]
    n_valid = nv_ref[0]

    @pl.when(j == 0)
    def _():
        _gather_rows(src_ref, x_hbm, xbuf.at[0], sem.at[0], tm)

    @pl.when(j + 1 < n_valid)
    def _():
        nslot = (j + 1) % 2
        _gather_rows(srcn_ref, x_hbm, xbuf.at[nslot], sem.at[nslot], tm)

    @pl.when(j < n_valid)
    def _():
        slot = j % 2
        _wait_rows(x_hbm, xbuf.at[slot], sem.at[slot], tm)

        @pl.when((j == 0) | (te_ref[j] != te_ref[jnp.maximum(j - 1, 0)]))
        def _():
            wgb[...] = wg_ref[0].astype(BF16)
            wub[...] = wu_ref[0].astype(BF16)
            wdb[...] = wd_ref[0].astype(BF16)

        hb = _rms(_rows_to_lanes(xbuf[slot]), nw_ref[...]).astype(BF16)
        hid = (_silu(jnp.dot(hb, wgb[...], preferred_element_type=F32))
               * jnp.dot(hb, wub[...], preferred_element_type=F32))
        y = jnp.dot(hid.astype(BF16), wdb[...], preferred_element_type=F32) * wrow_ref[...]
        ys_ref[...] = _lanes_to_rows(y)

    @pl.when(j >= n_valid)
    def _():
        ys_ref[...] = jnp.zeros_like(ys_ref)


def moe_routed(cfg, tile_expert, n_valid, src, wsort, x_rows, nw, wg, wu, wd, tm):
    n_tiles = src.shape[0]
    d = cfg.d_model
    de = cfg.d_expert
    nch = d // LANES
    cur = lambda j, te, nv: jnp.minimum(j, nv[0] - 1)
    wmap = lambda j, te, nv: (te[cur(j, te, nv)], 0, 0)
    return pl.pallas_call(
        functools.partial(_moe_kernel, cfg),
        out_shape=jax.ShapeDtypeStruct((n_tiles * tm, nch, LANES), F32),
        grid_spec=pltpu.PrefetchScalarGridSpec(
            num_scalar_prefetch=2,
            grid=(n_tiles,),
            in_specs=[pl.BlockSpec((1, 1, tm), lambda j, te, nv: (j, 0, 0), memory_space=pltpu.SMEM),
                      pl.BlockSpec((1, 1, tm), lambda j, te, nv: (jnp.minimum(j + 1, n_tiles - 1), 0, 0),
                                   memory_space=pltpu.SMEM),
                      pl.BlockSpec((tm, 1), lambda j, te, nv: (cur(j, te, nv), 0)),
                      pl.BlockSpec(memory_space=pl.ANY),
                      pl.BlockSpec((1, d), lambda j, te, nv: (0, 0)),
                      pl.BlockSpec((1, d, de), wmap), pl.BlockSpec((1, d, de), wmap),
                      pl.BlockSpec((1, de, d), wmap)],
            out_specs=pl.BlockSpec((tm, nch, LANES), lambda j, te, nv: (j, 0, 0)),
            scratch_shapes=[pltpu.VMEM((2, tm, nch, LANES), F32), pltpu.SemaphoreType.DMA((2,)),
                            pltpu.VMEM((d, de), BF16), pltpu.VMEM((d, de), BF16), pltpu.VMEM((de, d), BF16)]),
        compiler_params=_cparams(("arbitrary",)),
        name="moe",
    )(tile_expert, n_valid, src, src, wsort, x_rows, nw, wg, wu, wd)


def _combine_kernel(d0_ref, d1_ref, x1_ref, fw_ref, ys_hbm, y_ref, gbuf, sem):
    bm = x1_ref.shape[0]
    _gather_rows(d0_ref, ys_hbm, gbuf.at[0], sem.at[0], bm)
    _gather_rows(d1_ref, ys_hbm, gbuf.at[1], sem.at[1], bm)
    _wait_rows(ys_hbm, gbuf.at[0], sem.at[0], bm)
    _wait_rows(ys_hbm, gbuf.at[1], sem.at[1], bm)
    y_ref[...] = _rms(x1_ref[...] + _rows_to_lanes(gbuf[0] + gbuf[1]), fw_ref[...])


def moe_combine(cfg, dest0, dest1, x1, fw, ys, bm):
    m, d = x1.shape
    nch = d // LANES
    smem_blk = lambda: pl.BlockSpec((1, 1, bm), lambda i: (i, 0, 0), memory_space=pltpu.SMEM)
    return pl.pallas_call(
        _combine_kernel,
        out_shape=jax.ShapeDtypeStruct((m, d), F32),
        grid=(m // bm,),
        in_specs=[smem_blk(), smem_blk(),
                  pl.BlockSpec((bm, d), lambda i: (i, 0)), pl.BlockSpec((1, d), lambda i: (0, 0)),
                  pl.BlockSpec(memory_space=pl.ANY)],
        out_specs=pl.BlockSpec((bm, d), lambda i: (i, 0)),
        scratch_shapes=[pltpu.VMEM((2, bm, nch, LANES), F32), pltpu.SemaphoreType.DMA((2,))],
        compiler_params=_cparams(("arbitrary",)),
        name="moe_combine",
    )(dest0, dest1, x1, fw, ys)


def _route_tables(cfg, ei, wt, cnt, tm, n_tiles):
    ne = cfg.n_experts
    i32 = jnp.int32
    counts = cnt[0, :ne].astype(i32)
    tiles_e = (counts + tm - 1) // tm
    tile_end = jnp.cumsum(tiles_e)
    row_off = (tile_end - tiles_e) * tm
    tile_expert = jnp.minimum(jnp.searchsorted(tile_end, jnp.arange(n_tiles, dtype=i32), side="right"),
                              ne - 1).astype(i32)
    dest = row_off[ei[:, 0:2]] + ei[:, 2:4]
    tok = jnp.broadcast_to(jnp.arange(ei.shape[0], dtype=i32)[:, None], dest.shape)
    flat = dest.reshape(-1)
    src = jnp.zeros((n_tiles * tm,), i32).at[flat].set(tok.reshape(-1))
    wsort = jnp.zeros((n_tiles * tm,), F32).at[flat].set(wt[:, 0:2].reshape(-1))
    return tile_expert, tile_end[-1:].astype(i32), src.reshape(n_tiles, 1, tm), wsort.reshape(-1, 1), dest


def _pad_hist(hist):
    return jnp.pad(hist, ((0, 0), (SUBLANES - hist.shape[1], 0), (0, 0)))


def _tile(m, pref):
    return pref if m % pref == 0 else m


def _mixer_segment(cfg, x2d, n_seq, seq_len, hist_xbc, hist_xm, s0, c0, n0, m0, p):
    d = cfg.d_model
    m = n_seq * seq_len
    q = min(cfg.chunk, seq_len)
    nc = seq_len // q
    proj, dcol, dT = in_proj(x2d, p["norm_mix_w"], p["w_main"], p["w_dt"], p["w_dtT"], p["bdt_row"], p["bdt_col"],
                             _tile(m, 512), 1024 if p["w_main"].shape[1] % 1024 == 0 else p["w_main"].shape[1])
    xa, qa, ka, va, g, gT = conv_qkv(cfg, proj, hist_xbc, hist_xm, p["cwx"], p["cbx"], p["cwm"], p["cbm"],
                                     p["wq"], p["wk"], p["wv"], p["wg"], p["wgT"], p["bg_row"], p["bg_col"],
                                     n_seq, seq_len, q)
    dTc = dT[:cfg.ssd_heads].reshape(cfg.ssd_heads, n_seq, nc, q).transpose(1, 2, 0, 3)
    ys, s_new = ssd_scan(cfg, xa, dcol, dTc, s0, p["a_row"], p["a_col"], p["dskip"], n_seq, seq_len, q)
    hm, c_new, n_new, m_new = mlstm_scan(cfg, qa, ka, va, g, gT, c0, n0, m0, n_seq, seq_len, q)
    tail = proj.reshape(n_seq, seq_len, -1)[:, seq_len - (cfg.conv_w - 1):]
    tail_xbc = tail[:, :, 3 * d:]
    tail_xm = tail[:, :, d:2 * d]
    return proj, ys, hm, (tail_xbc, s_new, tail_xm, c_new, n_new, m_new)


MOE_TILE = 256


def _ffn(cfg, segments, p):
    d = cfg.d_model
    x1s = [out_proj(cfg, ys, proj, hm, x2d, p["ssd_norm_w"], p["mlstm_norm_w"], p["w_out"],
                    _tile(x2d.shape[0], 256), d) for x2d, proj, ys, hm in segments]
    cnt = jnp.zeros((1, LANES), F32)
    eis, wts = [], []
    for x1 in x1s:
        ei, wt, cnt = router(cfg, x1, p["norm_ffn_w"], p["wr"], p["br"], cnt, _tile(x1.shape[0], 256))
        eis.append(ei)
        wts.append(wt)
    n_tok = sum(x1.shape[0] for x1 in x1s)
    n_tiles = (2 * n_tok + cfg.n_experts * (MOE_TILE - 1)) // MOE_TILE
    tile_expert, n_valid, src, wsort, dest = _route_tables(
        cfg, jnp.concatenate(eis), jnp.concatenate(wts), cnt, MOE_TILE, n_tiles)
    x_rows = jnp.concatenate(x1s).reshape(n_tok, d // LANES, LANES)
    ys_sorted = moe_routed(cfg, tile_expert, n_valid, src, wsort, x_rows, p["norm_ffn_w"],
                           p["w_gate"], p["w_up"], p["w_down"], MOE_TILE)
    outs = []
    off = 0
    for x1 in x1s:
        m = x1.shape[0]
        bm = _tile(m, 256)
        dseg = dest[off:off + m]
        outs.append(moe_combine(cfg, dseg[:, 0].reshape(m // bm, 1, bm), dseg[:, 1].reshape(m // bm, 1, bm),
                                x1, p["final_norm_w"], ys_sorted, bm))
        off += m
    return outs


def _prep_params(cfg, norm_mix_w, w_in, conv_ssd_w, conv_ssd_b, dt_bias, a_log, d_skip, ssd_norm_w,
                 conv_mlstm_w, conv_mlstm_b, w_q, w_k, w_v, w_igate, b_igate, w_fgate, b_fgate, mlstm_norm_w,
                 w_out, norm_ffn_w, w_group, b_group, w_router, b_router, w_gate, w_up, w_down, final_norm_w):
    d = cfg.d_model
    hs = cfg.ssd_heads
    nh = cfg.ml_heads
    o_z, o_xbc = d, d + cfg.xbc_dim
    o_dt = o_xbc + hs
    o_xm = o_dt + d
    w_z, w_xbc, w_dt, w_xm, w_o = (w_in[:, :o_z], w_in[:, o_z:o_xbc], w_in[:, o_xbc:o_dt],
                                   w_in[:, o_dt:o_xm], w_in[:, o_xm:])
    row = lambda v: v.reshape(1, -1).astype(F32)
    pad_lanes = lambda a: jnp.pad(a, ((0, 0), (0, LANES - a.shape[1])))
    w_dt_p = pad_lanes(w_dt)
    a = -jnp.exp(a_log.astype(F32))
    w_gates = jnp.concatenate([w_igate, w_fgate], axis=1)
    b_gates = jnp.concatenate([b_igate, b_fgate]).astype(F32)
    ne = cfg.n_experts
    wr = pad_lanes(jnp.concatenate([w_router, w_group], axis=1))
    br = pad_lanes(jnp.concatenate([b_router, b_group]).reshape(1, -1).astype(F32))
    return dict(
        norm_mix_w=row(norm_mix_w),
        w_main=jnp.concatenate([w_z, w_xm, w_o, w_xbc], axis=1).astype(BF16),
        w_dt=w_dt_p.astype(BF16), w_dtT=w_dt_p.T.astype(BF16),
        bdt_row=pad_lanes(row(dt_bias)), bdt_col=pad_lanes(row(dt_bias)).T,
        cwx=conv_ssd_w.astype(F32), cbx=row(conv_ssd_b), cwm=conv_mlstm_w.astype(F32), cbm=row(conv_mlstm_b),
        wq=w_q.astype(BF16), wk=w_k.astype(BF16), wv=w_v.astype(BF16),
        wg=pad_lanes(w_gates).astype(BF16), wgT=w_gates.T.astype(BF16),
        bg_row=pad_lanes(row(b_gates)), bg_col=b_gates.reshape(-1, 1),
        a_row=pad_lanes(row(a)), a_col=a.reshape(-1, 1),
        dskip=row(jnp.repeat(d_skip.astype(F32), cfg.ssd_head_dim)),
        ssd_norm_w=row(ssd_norm_w), mlstm_norm_w=row(mlstm_norm_w), w_out=w_out.astype(BF16),
        norm_ffn_w=row(norm_ffn_w), wr=wr.astype(BF16), br=br,
        w_gate=w_gate.astype(F32), w_up=w_up.astype(F32), w_down=w_down.astype(F32),
        final_norm_w=row(final_norm_w),
    )


def forward(cfg, x_prompt, x_sample, state_ssd_conv, state_ssd, state_mlstm_conv, state_mlstm_c,
            state_mlstm_n, state_mlstm_m, meta_tokens, *weights):
    d = cfg.d_model
    nh = cfg.ml_heads
    hd = cfg.ml_head_dim
    assert state_ssd.shape[0] == 1, "single-layer kernel"
    p = _prep_params(cfg, *[w[0] for w in weights[:-1]], weights[-1])
    bp, lp, _ = x_prompt.shape
    bs, ls, _ = x_sample.shape
    n_meta = meta_tokens.shape[0]

    zeros = lambda *s: jnp.zeros(s, F32)
    _, _, _, st_meta = _mixer_segment(
        cfg, meta_tokens.astype(F32), 1, n_meta, zeros(1, SUBLANES, cfg.xbc_dim), zeros(1, SUBLANES, d),
        zeros(1, d, cfg.ssd_state), zeros(1, d, hd), zeros(1, nh, hd), zeros(1, nh, LANES), p)
    mt_xbc, mt_s, mt_xm, mt_c, mt_n, mt_m = st_meta
    rep = lambda a: jnp.broadcast_to(a, (bp,) + a.shape[1:])

    xp = x_prompt.reshape(bp * lp, d)
    proj_p, ys_p, hm_p, st_p = _mixer_segment(
        cfg, xp, bp, lp, rep(_pad_hist(mt_xbc)), rep(_pad_hist(mt_xm)), rep(mt_s), rep(mt_c), rep(mt_n),
        rep(mt_m), p)

    xs = x_sample.reshape(bs * ls, d)
    m0 = jnp.broadcast_to(state_mlstm_m[0].astype(F32)[:, :, None], (bs, nh, LANES))
    proj_s, ys_s, hm_s, st_s = _mixer_segment(
        cfg, xs, bs, ls, _pad_hist(state_ssd_conv[0]), _pad_hist(state_mlstm_conv[0]),
        state_ssd[0].reshape(bs, d, cfg.ssd_state), state_mlstm_c[0].reshape(bs, d, hd),
        state_mlstm_n[0], m0, p)
    y_p, y_s = _ffn(cfg, [(xp, proj_p, ys_p, hm_p), (xs, proj_s, ys_s, hm_s)], p)
    y_prompt = y_p.reshape(bp, lp, d)
    y_sample = y_s.reshape(bs, ls, d)

    def pack(st, b):
        t_xbc, s_new, t_xm, c_new, n_new, m_new = st
        return (t_xbc[None], s_new.reshape(1, b, cfg.ssd_heads, cfg.ssd_head_dim, cfg.ssd_state),
                t_xm[None], c_new.reshape(1, b, nh, hd, hd), n_new[None], m_new[None, :, :, 0])

    return (y_prompt, y_sample) + pack(st_p, bp) + pack(st_s, bs)


def kernel(x_prompt, x_sample, state_ssd_conv, state_ssd, state_mlstm_conv, state_mlstm_c, state_mlstm_n, state_mlstm_m, meta_tokens, norm_mix_w, w_in, conv_ssd_w, conv_ssd_b, dt_bias, a_log, d_skip, ssd_norm_w, conv_mlstm_w, conv_mlstm_b, w_q, w_k, w_v, w_igate, b_igate, w_fgate, b_fgate, mlstm_norm_w, w_out, norm_ffn_w, w_group, b_group, w_router, b_router, w_gate, w_up, w_down, final_norm_w):
    return forward(Cfg(), x_prompt, x_sample, state_ssd_conv, state_ssd, state_mlstm_conv, state_mlstm_c,
                   state_mlstm_n, state_mlstm_m, meta_tokens, norm_mix_w, w_in, conv_ssd_w, conv_ssd_b, dt_bias,
                   a_log, d_skip, ssd_norm_w, conv_mlstm_w, conv_mlstm_b, w_q, w_k, w_v, w_igate, b_igate,
                   w_fgate, b_fgate, mlstm_norm_w, w_out, norm_ffn_w, w_group, b_group, w_router, b_router,
                   w_gate, w_up, w_down, final_norm_w)
```

```python
import functools
from typing import NamedTuple

import jax
import jax.numpy as jnp
from jax import lax
from jax.experimental import pallas as pl
from jax.experimental.pallas import tpu as pltpu

F32 = jnp.float32
BF16 = jnp.bfloat16
EPS = 1e-6
LANES = 128
SUBLANES = 8
VMEM_LIMIT = 52 * 1024 * 1024
HI = lax.Precision.HIGHEST


class Cfg(NamedTuple):
    d_model: int = 2048
    ssd_heads: int = 32
    ssd_head_dim: int = 64
    ssd_groups: int = 4
    ssd_state: int = 128
    ml_heads: int = 8
    ml_head_dim: int = 256
    n_groups: int = 4
    experts_per_group: int = 8
    d_expert: int = 512
    n_meta: int = 16
    conv_w: int = 4
    chunk: int = 128

    @property
    def bc_dim(self):
        return self.ssd_groups * self.ssd_state

    @property
    def xbc_dim(self):
        return self.d_model + 2 * self.bc_dim

    @property
    def n_experts(self):
        return self.n_groups * self.experts_per_group


def _cparams(sem):
    return pltpu.CompilerParams(dimension_semantics=sem, vmem_limit_bytes=VMEM_LIMIT)


def _softplus(x):
    return jnp.maximum(x, 0.0) + jnp.log1p(jnp.exp(-jnp.abs(x)))


def _sigmoid(x):
    return 1.0 / (1.0 + jnp.exp(-x))


def _silu(x):
    return x * _sigmoid(x)


def _rms(x, w):
    return x * lax.rsqrt(jnp.mean(x * x, axis=-1, keepdims=True) + EPS) * w


def _prenorm_kernel(x_ref, nw_ref, wdt_ref, wdtT_ref, bdt_row_ref, bdt_col_ref, h_ref, d_ref, dT_ref):
    hb = _rms(x_ref[...], nw_ref[...]).astype(BF16)
    h_ref[...] = hb
    dt = jnp.dot(hb, wdt_ref[...], preferred_element_type=F32)
    d_ref[...] = _softplus(dt + bdt_row_ref[...])
    dtT = lax.dot_general(wdtT_ref[...], hb, (((1,), (1,)), ((), ())), preferred_element_type=F32)
    dT_ref[...] = _softplus(dtT + bdt_col_ref[...])


def pre_norm(x, norm_w, w_dt, w_dtT, bdt_row, bdt_col, bm):
    m, d = x.shape
    const2 = lambda i: (0, 0)
    return pl.pallas_call(
        _prenorm_kernel,
        out_shape=(jax.ShapeDtypeStruct((m, d), BF16), jax.ShapeDtypeStruct((m, LANES), F32),
                   jax.ShapeDtypeStruct((LANES, m), F32)),
        grid=(m // bm,),
        in_specs=[pl.BlockSpec((bm, d), lambda i: (i, 0)), pl.BlockSpec((1, d), const2),
                  pl.BlockSpec((d, LANES), const2), pl.BlockSpec((LANES, d), const2),
                  pl.BlockSpec((1, LANES), const2), pl.BlockSpec((LANES, 1), const2)],
        out_specs=(pl.BlockSpec((bm, d), lambda i: (i, 0)), pl.BlockSpec((bm, LANES), lambda i: (i, 0)),
                   pl.BlockSpec((LANES, bm), lambda i: (0, i))),
        compiler_params=_cparams(("arbitrary",)),
        name="pre_norm",
    )(x, norm_w, w_dt, w_dtT, bdt_row, bdt_col)


def _inproj_kernel(n_a, h_ref, wa_ref, wb_ref, proj_ref, w_ref):
    j = pl.program_id(0)

    @pl.when(pl.program_id(1) == 0)
    def _():
        @pl.when(j < n_a)
        def _():
            w_ref[...] = wa_ref[...].astype(BF16)

        @pl.when(j >= n_a)
        def _():
            w_ref[...] = wb_ref[...]

    proj_ref[...] = jnp.dot(h_ref[...], w_ref[...], preferred_element_type=F32)


def in_proj(cfg, h, w_in, w_b, bm, bn):
    m, d = h.shape
    nz = d // bn
    nx = cfg.xbc_dim // bn
    n_a = nz + nx
    n_blocks = n_a + 2 * nz
    out_col = lambda j: jnp.where(j < nz, j, jnp.where(j < n_a, j + 2 * nz, j - nx))
    return pl.pallas_call(
        functools.partial(_inproj_kernel, n_a),
        out_shape=jax.ShapeDtypeStruct((m, n_blocks * bn), F32),
        grid=(n_blocks, m // bm),
        in_specs=[pl.BlockSpec((bm, d), lambda j, i: (i, 0)),
                  pl.BlockSpec((d, bn), lambda j, i: (0, jnp.minimum(j, n_a - 1))),
                  pl.BlockSpec((d, bn), lambda j, i: (0, jnp.clip(j - n_a, 0, 2 * nz - 1)))],
        out_specs=pl.BlockSpec((bm, bn), lambda j, i: (i, out_col(j))),
        scratch_shapes=[pltpu.VMEM((d, bn), BF16)],
        compiler_params=_cparams(("arbitrary", "arbitrary")),
        name="in_proj",
    )(h, w_in, w_b)


def _causal_conv(u, prev, w_ref, b_ref, conv_w):
    lt = u.shape[0]
    row8 = lax.broadcasted_iota(jnp.int32, (SUBLANES, u.shape[1]), 0)
    acc = u * w_ref[conv_w - 1:conv_w, :] + b_ref[...]
    for s in range(1, conv_w):
        rolled = pltpu.roll(u, s, axis=0)
        head = jnp.where(row8 < s, pltpu.roll(prev, s, axis=0), rolled[0:SUBLANES])
        shifted = head if lt == SUBLANES else jnp.concatenate([head, rolled[SUBLANES:]], axis=0)
        acc = acc + shifted * w_ref[conv_w - 1 - s:conv_w - s, :]
    return acc


def _conv_qkv_kernel(cfg, xbc_ref, xm_ref, hxbc_ref, hxm_ref, cwx_ref, cbx_ref, cwm_ref, cbm_ref,
                     wq_ref, wk_ref, wv_ref, wg_ref, wgT_ref, bg_row_ref, bg_col_ref,
                     xa_ref, q_ref, k_ref, v_ref, g_ref, gT_ref, px_ref, pm_ref):
    lt = xbc_ref.shape[0]
    hd = cfg.ml_head_dim
    nh = cfg.ml_heads

    @pl.when(pl.program_id(1) == 0)
    def _():
        px_ref[...] = hxbc_ref[0]
        pm_ref[...] = hxm_ref[0]

    u = xbc_ref[...]
    xa_ref[...] = _silu(_causal_conv(u, px_ref[...], cwx_ref, cbx_ref, cfg.conv_w))
    px_ref[...] = u[lt - SUBLANES:lt]

    xm = xm_ref[...]
    xc = _silu(_causal_conv(xm, pm_ref[...], cwm_ref, cbm_ref, cfg.conv_w)).astype(BF16)
    pm_ref[...] = xm[lt - SUBLANES:lt]
    xv = xm.astype(BF16)
    kscale = hd ** -0.5
    for h in range(nh):
        sl = slice(h * hd, (h + 1) * hd)
        q_ref[:, sl] = jnp.dot(xc[:, sl], wq_ref[h], preferred_element_type=F32)
        k_ref[:, sl] = jnp.dot(xc[:, sl], wk_ref[h], preferred_element_type=F32) * kscale
        v_ref[:, sl] = jnp.dot(xv[:, sl], wv_ref[h], preferred_element_type=F32)
    d = nh * hd
    qb = q_ref[...].astype(BF16)
    kb = k_ref[...].astype(BF16)
    vb = v_ref[...].astype(BF16)
    gcol = (jnp.dot(qb, wg_ref[0:d, :], preferred_element_type=F32)
            + jnp.dot(kb, wg_ref[d:2 * d, :], preferred_element_type=F32)
            + jnp.dot(vb, wg_ref[2 * d:3 * d, :], preferred_element_type=F32)) + bg_row_ref[...]
    nt = (((1,), (1,)), ((), ()))
    grow = (lax.dot_general(wgT_ref[:, 0:d], qb, nt, preferred_element_type=F32)
            + lax.dot_general(wgT_ref[:, d:2 * d], kb, nt, preferred_element_type=F32)
            + lax.dot_general(wgT_ref[:, 2 * d:3 * d], vb, nt, preferred_element_type=F32)) + bg_col_ref[...]
    lane = lax.broadcasted_iota(jnp.int32, gcol.shape, 1)
    g_ref[...] = jnp.where(lane < nh, gcol, -_softplus(-gcol))
    row = lax.broadcasted_iota(jnp.int32, grow.shape, 0)
    gT_ref[0, 0] = jnp.where(row < nh, grow, -_softplus(-grow))


def conv_qkv(cfg, proj, hist_xbc, hist_xm, cwx, cbx, cwm, cbm, wq, wk, wv, wg, wgT, bg_row, bg_col,
             n_seq, seq_len, lt):
    d = cfg.d_model
    xbc = cfg.xbc_dim
    m = n_seq * seq_len
    nt = seq_len // lt
    ng = 2 * cfg.ml_heads
    xbc_blk = (3 * d) // xbc
    row = lambda s, l: (s * nt + l, 0)
    const2 = lambda s, l: (0, 0)
    const3 = lambda s, l: (0, 0, 0)
    return pl.pallas_call(
        functools.partial(_conv_qkv_kernel, cfg),
        out_shape=(jax.ShapeDtypeStruct((m, xbc), F32),
                   jax.ShapeDtypeStruct((m, d), F32), jax.ShapeDtypeStruct((m, d), F32),
                   jax.ShapeDtypeStruct((m, d), F32),
                   jax.ShapeDtypeStruct((m, LANES), F32),
                   jax.ShapeDtypeStruct((n_seq, nt, ng, lt), F32)),
        grid=(n_seq, nt),
        in_specs=[pl.BlockSpec((lt, xbc), lambda s, l: (s * nt + l, xbc_blk)),
                  pl.BlockSpec((lt, d), lambda s, l: (s * nt + l, 1)),
                  pl.BlockSpec((1, SUBLANES, xbc), lambda s, l: (s, 0, 0)),
                  pl.BlockSpec((1, SUBLANES, d), lambda s, l: (s, 0, 0)),
                  pl.BlockSpec((cfg.conv_w, xbc), const2), pl.BlockSpec((1, xbc), const2),
                  pl.BlockSpec((cfg.conv_w, d), const2), pl.BlockSpec((1, d), const2),
                  pl.BlockSpec(wq.shape, const3), pl.BlockSpec(wk.shape, const3),
                  pl.BlockSpec(wv.shape, const3),
                  pl.BlockSpec(wg.shape, const2), pl.BlockSpec(wgT.shape, const2),
                  pl.BlockSpec((1, LANES), const2), pl.BlockSpec((ng, 1), const2)],
        out_specs=(pl.BlockSpec((lt, xbc), row), pl.BlockSpec((lt, d), row), pl.BlockSpec((lt, d), row),
                   pl.BlockSpec((lt, d), row), pl.BlockSpec((lt, LANES), row),
                   pl.BlockSpec((1, 1, ng, lt), lambda s, l: (s, l, 0, 0))),
        scratch_shapes=[pltpu.VMEM((SUBLANES, xbc), F32), pltpu.VMEM((SUBLANES, d), F32)],
        compiler_params=_cparams(("arbitrary", "arbitrary")),
        name="conv_qkv",
    )(proj, proj, hist_xbc, hist_xm, cwx, cbx, cwm, cbm, wq, wk, wv, wg, wgT, bg_row, bg_col)


def _tri(q, lower):
    r = lax.broadcasted_iota(jnp.int32, (q, q), 0)
    c = lax.broadcasted_iota(jnp.int32, (q, q), 1)
    return (c <= r) if lower else (r <= c)


def _ssd_kernel(cfg, n_chunks, xa_ref, d_ref, dT_ref, s0_ref, arow_ref, acol_ref, dskip_ref,
                y_ref, sout_ref, st_ref):
    q = xa_ref.shape[0]
    dm = cfg.d_model
    ns = cfg.ssd_state
    hp = cfg.ssd_head_dim
    hpg = cfg.ssd_heads // cfg.ssd_groups
    heads_per_tile = LANES // hp
    n_tiles = cfg.ssd_heads // heads_per_tile

    @pl.when(pl.program_id(1) == 0)
    def _():
        st_ref[...] = s0_ref[0]

    causal = _tri(q, True)
    tril = causal.astype(F32)
    triu = _tri(q, False).astype(F32)
    dcol = d_ref[...]
    drow = dT_ref[0, 0]
    acum = jnp.dot(tril, dcol * arow_ref[...], precision=HI, preferred_element_type=F32)
    acumT = jnp.dot(drow * acol_ref[...], triu, precision=HI, preferred_element_type=F32)
    nt_dims = (((1,), (1,)), ((), ()))
    tn_dims = (((0,), (0,)), ((), ()))
    lane = lax.broadcasted_iota(jnp.int32, (q, LANES), 1)
    srow = lax.broadcasted_iota(jnp.int32, (LANES, ns), 0)

    cbs = []
    bgs = []
    cgs = []
    for g in range(cfg.ssd_groups):
        bg = xa_ref[:, dm + g * ns: dm + (g + 1) * ns].astype(BF16)
        cg = xa_ref[:, dm + cfg.bc_dim + g * ns: dm + cfg.bc_dim + (g + 1) * ns].astype(BF16)
        cbs.append(lax.dot_general(cg, bg, nt_dims, preferred_element_type=F32))
        bgs.append(bg)
        cgs.append(cg)

    for t in range(n_tiles):
        h0 = t * heads_per_tile
        g = h0 // hpg
        cols = slice(t * LANES, (t + 1) * LANES)
        x = xa_ref[:, cols]
        dsel = jnp.zeros((q, LANES), F32)
        esel = jnp.zeros((q, LANES), F32)
        tsel = jnp.zeros((q, LANES), F32)
        rdec = jnp.zeros((LANES, ns), F32)
        for i in range(heads_per_tile):
            h = h0 + i
            in_head = (lane >= i * hp) & (lane < (i + 1) * hp)
            a_col = acum[:, h:h + 1]
            a_last = acum[q - 1:q, h:h + 1]
            dsel = jnp.where(in_head, dcol[:, h:h + 1], dsel)
            esel = jnp.where(in_head, jnp.exp(a_col), esel)
            tsel = jnp.where(in_head, jnp.exp(a_last - a_col), tsel)
            rdec = jnp.where((srow >= i * hp) & (srow < (i + 1) * hp), jnp.exp(a_last), rdec)
        xd = x * dsel
        y = x * dskip_ref[:, cols]
        for i in range(heads_per_tile):
            h = h0 + i
            in_head = (lane >= i * hp) & (lane < (i + 1) * hp)
            seg = jnp.where(causal, acum[:, h:h + 1] - acumT[h:h + 1, :], -jnp.inf)
            w = (cbs[g] * jnp.exp(seg)).astype(BF16)
            xdh = jnp.where(in_head, xd, 0.0).astype(BF16)
            y = y + jnp.dot(w, xdh, preferred_element_type=F32)
        s_old = st_ref[cols, :]
        ys = lax.dot_general(cgs[g], s_old.astype(BF16), nt_dims, preferred_element_type=F32)
        y_ref[:, cols] = y + esel * ys
        upd = lax.dot_general((xd * tsel).astype(BF16), bgs[g], tn_dims, preferred_element_type=F32)
        st_ref[cols, :] = rdec * s_old + upd

    @pl.when(pl.program_id(1) == n_chunks - 1)
    def _():
        sout_ref[0] = st_ref[...]


def ssd_scan(cfg, xa, d, dT, s0, a_row, a_col, dskip, n_seq, seq_len, q):
    dm = cfg.d_model
    nc = seq_len // q
    m = n_seq * seq_len
    row = lambda s, c: (s * nc + c, 0)
    const2 = lambda s, c: (0, 0)
    return pl.pallas_call(
        functools.partial(_ssd_kernel, cfg, nc),
        out_shape=(jax.ShapeDtypeStruct((m, dm), F32),
                   jax.ShapeDtypeStruct((n_seq, dm, cfg.ssd_state), F32)),
        grid=(n_seq, nc),
        in_specs=[pl.BlockSpec((q, cfg.xbc_dim), row),
                  pl.BlockSpec((q, LANES), row),
                  pl.BlockSpec((1, 1, cfg.ssd_heads, q), lambda s, c: (s, c, 0, 0)),
                  pl.BlockSpec((1, dm, cfg.ssd_state), lambda s, c: (s, 0, 0)),
                  pl.BlockSpec((1, LANES), const2),
                  pl.BlockSpec((cfg.ssd_heads, 1), const2),
                  pl.BlockSpec((1, dm), const2)],
        out_specs=(pl.BlockSpec((q, dm), row),
                   pl.BlockSpec((1, dm, cfg.ssd_state), lambda s, c: (s, 0, 0))),
        scratch_shapes=[pltpu.VMEM((dm, cfg.ssd_state), F32)],
        compiler_params=_cparams(("arbitrary", "arbitrary")),
        name="ssd_scan",
    )(xa, d, dT, s0, a_row, a_col, dskip)


def _mlstm_kernel(cfg, n_chunks, q_ref, k_ref, v_ref, g_ref, gT_ref, c0_ref, n0_ref, m0_ref,
                  h_ref, cout_ref, nout_ref, mout_ref, c_ref, n_ref, m_ref):
    ql = q_ref.shape[0]
    hd = cfg.ml_head_dim
    nh = cfg.ml_heads

    @pl.when(pl.program_id(1) == 0)
    def _():
        c_ref[...] = c0_ref[0]
        n_ref[...] = n0_ref[0]
        m_ref[...] = m0_ref[0]

    causal = _tri(ql, True)
    gcol = g_ref[...]
    grow = gT_ref[0, 0]
    bcum = jnp.dot(causal.astype(F32), gcol, precision=HI, preferred_element_type=F32)
    bcumT = jnp.dot(grow, _tri(ql, False).astype(F32), precision=HI, preferred_element_type=F32)
    nt_dims = (((1,), (1,)), ((), ()))
    tn_dims = (((0,), (0,)), ((), ()))

    for h in range(nh):
        sl = slice(h * hd, (h + 1) * hd)
        b_col = bcum[:, nh + h:nh + h + 1]
        b_row = bcumT[nh + h:nh + h + 1, :]
        i_col = gcol[:, h:h + 1]
        i_row = grow[h:h + 1, :]
        m_prev = m_ref[h:h + 1, 0:1]
        dlog = jnp.where(causal, b_col - b_row + i_row, -jnp.inf)
        inter = b_col + m_prev
        mt = jnp.maximum(inter, jnp.max(dlog, axis=1, keepdims=True))
        qh = q_ref[:, sl]
        kh = k_ref[:, sl]
        vh = v_ref[:, sl]
        qb = qh.astype(BF16)
        kb = kh.astype(BF16)
        s = lax.dot_general(qb, kb, nt_dims, preferred_element_type=F32) * jnp.exp(dlog - mt)
        gdec = jnp.exp(inter - mt)
        c_old = c_ref[sl, :]
        n_old = n_ref[h:h + 1, :]
        qc = lax.dot_general(qb, c_old.astype(BF16), nt_dims, preferred_element_type=F32)
        num = jnp.dot(s.astype(BF16), vh.astype(BF16), preferred_element_type=F32) + gdec * qc
        den = jnp.sum(s, axis=1, keepdims=True) + gdec * jnp.sum(qh * n_old, axis=1, keepdims=True)
        h_ref[:, sl] = num / jnp.maximum(jnp.abs(den), jnp.exp(-mt))
        m_new = mt[ql - 1:ql, :]
        gs = jnp.exp(b_col[ql - 1:ql, :] - b_col + i_col - m_new)
        gc = jnp.exp(inter[ql - 1:ql, :] - m_new)
        upd = lax.dot_general((vh * gs).astype(BF16), kb, tn_dims, preferred_element_type=F32)
        c_ref[sl, :] = gc * c_old + upd
        n_ref[h:h + 1, :] = gc * n_old + jnp.sum(gs * kh, axis=0, keepdims=True)
        m_ref[h:h + 1, :] = jnp.broadcast_to(m_new, (1, LANES))

    @pl.when(pl.program_id(1) == n_chunks - 1)
    def _():
        cout_ref[0] = c_ref[...]
        nout_ref[0] = n_ref[...]
        mout_ref[0] = m_ref[...]


def mlstm_scan(cfg, qa, ka, va, g, gT, c0, n0, m0, n_seq, seq_len, q):
    d = cfg.d_model
    hd = cfg.ml_head_dim
    nh = cfg.ml_heads
    nc = seq_len // q
    m = n_seq * seq_len
    row = lambda s, c: (s * nc + c, 0)
    st3 = lambda s, c: (s, 0, 0)
    return pl.pallas_call(
        functools.partial(_mlstm_kernel, cfg, nc),
        out_shape=(jax.ShapeDtypeStruct((m, d), F32),
                   jax.ShapeDtypeStruct((n_seq, d, hd), F32),
                   jax.ShapeDtypeStruct((n_seq, nh, hd), F32),
                   jax.ShapeDtypeStruct((n_seq, nh, LANES), F32)),
        grid=(n_seq, nc),
        in_specs=[pl.BlockSpec((q, d), row), pl.BlockSpec((q, d), row), pl.BlockSpec((q, d), row),
                  pl.BlockSpec((q, LANES), row),
                  pl.BlockSpec((1, 1, 2 * nh, q), lambda s, c: (s, c, 0, 0)),
                  pl.BlockSpec((1, d, hd), st3), pl.BlockSpec((1, nh, hd), st3),
                  pl.BlockSpec((1, nh, LANES), st3)],
        out_specs=(pl.BlockSpec((q, d), row),
                   pl.BlockSpec((1, d, hd), st3), pl.BlockSpec((1, nh, hd), st3),
                   pl.BlockSpec((1, nh, LANES), st3)),
        scratch_shapes=[pltpu.VMEM((d, hd), F32), pltpu.VMEM((nh, hd), F32), pltpu.VMEM((nh, LANES), F32)],
        compiler_params=_cparams(("arbitrary", "arbitrary")),
        name="mlstm_scan",
    )(qa, ka, va, g, gT, c0, n0, m0)


def _group_norm(x, w_ref, col0, groups, width):
    parts = []
    for g in range(groups):
        seg = x[:, g * width:(g + 1) * width]
        parts.append(seg * lax.rsqrt(jnp.mean(seg * seg, axis=-1, keepdims=True) + EPS)
                     * w_ref[:, col0 + g * width: col0 + (g + 1) * width])
    return parts


def _outproj_kernel(cfg, ys_ref, z_ref, hm_ref, o_ref, x_ref, nws_ref, nwm_ref, w_ref, out_ref, mix_ref):
    d = cfg.d_model

    @pl.when(pl.program_id(1) == 0)
    def _():
        ws = d // cfg.ssd_groups
        yz = ys_ref[...] * _silu(z_ref[...])
        for g, part in enumerate(_group_norm(yz, nws_ref, 0, cfg.ssd_groups, ws)):
            mix_ref[:, g * ws:(g + 1) * ws] = part.astype(BF16)
        gate = _sigmoid(o_ref[...])
        wm = cfg.ml_head_dim
        for g, part in enumerate(_group_norm(hm_ref[...], nwm_ref, 0, cfg.ml_heads, wm)):
            mix_ref[:, d + g * wm: d + (g + 1) * wm] = (part * gate[:, g * wm:(g + 1) * wm]).astype(BF16)

    out_ref[...] = x_ref[...] + jnp.dot(mix_ref[...], w_ref[...], preferred_element_type=F32)


def out_proj(cfg, ys, proj, hm, x, nws, nwm, w_out, bm, bn):
    m, d = x.shape
    const2 = lambda i, j: (0, 0)
    full = lambda i, j: (i, 0)
    w_mode = dict(pipeline_mode=pl.Buffered(1)) if bn == d else {}
    return pl.pallas_call(
        functools.partial(_outproj_kernel, cfg),
        out_shape=jax.ShapeDtypeStruct((m, d), F32),
        grid=(m // bm, d // bn),
        in_specs=[pl.BlockSpec((bm, d), full),
                  pl.BlockSpec((bm, d), lambda i, j: (i, 0)),
                  pl.BlockSpec((bm, d), full),
                  pl.BlockSpec((bm, d), lambda i, j: (i, 2)),
                  pl.BlockSpec((bm, bn), lambda i, j: (i, j)),
                  pl.BlockSpec((1, d), const2), pl.BlockSpec((1, d), const2),
                  pl.BlockSpec((2 * d, bn), lambda i, j: (0, j), **w_mode)],
        out_specs=pl.BlockSpec((bm, bn), lambda i, j: (i, j)),
        scratch_shapes=[pltpu.VMEM((bm, 2 * d), BF16)],
        compiler_params=_cparams(("arbitrary", "arbitrary")),
        name="out_proj",
    )(ys, proj, hm, proj, x, nws, nwm, w_out)


def _router_kernel(cfg, n_a, xa_ref, xb_ref, nw_ref, wr_ref, br_ref, ei_ref, wt_ref, cnt_out_ref, rows_ref,
                   cnt_ref, x_ref):
    ne = cfg.n_experts
    epg = cfg.experts_per_group
    ngr = cfg.n_groups
    bm = x_ref.shape[0]
    i = pl.program_id(0)

    @pl.when(i == 0)
    def _():
        cnt_ref[...] = jnp.zeros_like(cnt_ref)

    @pl.when(i < n_a)
    def _():
        x_ref[...] = xa_ref[...]

    @pl.when(i >= n_a)
    def _():
        x_ref[...] = xb_ref[...]

    rows_ref[...] = _lanes_to_rows(x_ref[...])
    hb = _rms(x_ref[...], nw_ref[...]).astype(BF16)
    logits = jnp.dot(hb, wr_ref[...], preferred_element_type=F32) + br_ref[...]
    lane = lax.broadcasted_iota(jnp.int32, logits.shape, 1)
    big = jnp.int32(2 ** 30)
    neg = -jnp.inf

    def first_argmax(vals):
        mx = jnp.max(vals, axis=-1, keepdims=True)
        idx = jnp.min(jnp.where(vals == mx, lane, big), axis=-1, keepdims=True)
        return mx, idx

    is_group = (lane >= ne) & (lane < ne + ngr)
    gl = jnp.where(is_group, logits, neg)
    gmax, gidx = first_argmax(gl)
    p_g = 1.0 / jnp.sum(jnp.exp(gl - gmax), axis=-1, keepdims=True)
    e_lo = (gidx - ne) * epg
    in_sel = (lane >= e_lo) & (lane < e_lo + epg)
    el = jnp.where(in_sel, logits, neg)
    pe = jnp.exp(el - jnp.max(el, axis=-1, keepdims=True))
    pe = jnp.where(in_sel, pe / jnp.sum(pe, axis=-1, keepdims=True), -1.0)
    p1, i1 = first_argmax(pe)
    p2, i2 = first_argmax(jnp.where(lane == i1, -1.0, pe))
    wsum = p1 + p2
    wt_ref[...] = jnp.where(lane == 0, p_g * p1 / wsum, jnp.where(lane == 1, p_g * p2 / wsum, 0.0))

    oh1 = jnp.where(lane == i1, 1.0, 0.0)
    oh2 = jnp.where(lane == i2, 1.0, 0.0)
    r = lax.broadcasted_iota(jnp.int32, (bm, bm), 0)
    c = lax.broadcasted_iota(jnp.int32, (bm, bm), 1)
    before = jnp.where(c < r, 1.0, 0.0).astype(BF16)
    ahead1 = jnp.dot(before, oh1.astype(BF16), preferred_element_type=F32)
    ahead2 = jnp.dot(before, oh2.astype(BF16), preferred_element_type=F32)
    cnt = cnt_ref[...]
    tot1 = jnp.sum(oh1, axis=0, keepdims=True)
    rank1 = jnp.sum(oh1 * (cnt + ahead1), axis=-1, keepdims=True)
    rank2 = jnp.sum(oh2 * (cnt + tot1 + ahead2), axis=-1, keepdims=True)
    cnt_new = cnt + tot1 + jnp.sum(oh2, axis=0, keepdims=True)
    cnt_ref[...] = cnt_new
    cnt_out_ref[...] = cnt_new
    ei_ref[...] = jnp.where(lane == 0, i1, jnp.where(lane == 1, i2, jnp.where(
        lane == 2, rank1.astype(jnp.int32), jnp.where(lane == 3, rank2.astype(jnp.int32), 0))))


def router(cfg, xa, xb, nw, wr, br, bm):
    d = xa.shape[1]
    n_a = xa.shape[0] // bm
    n_b = xb.shape[0] // bm
    m = xa.shape[0] + xb.shape[0]
    nch = d // LANES
    const2 = lambda i: (0, 0)
    return pl.pallas_call(
        functools.partial(_router_kernel, cfg, n_a),
        out_shape=(jax.ShapeDtypeStruct((m, LANES), jnp.int32), jax.ShapeDtypeStruct((m, LANES), F32),
                   jax.ShapeDtypeStruct((1, LANES), F32), jax.ShapeDtypeStruct((m, nch, LANES), F32)),
        grid=(n_a + n_b,),
        in_specs=[pl.BlockSpec((bm, d), lambda i: (jnp.minimum(i, n_a - 1), 0)),
                  pl.BlockSpec((bm, d), lambda i: (jnp.maximum(i - n_a, 0), 0)),
                  pl.BlockSpec((1, d), const2),
                  pl.BlockSpec((d, LANES), const2), pl.BlockSpec((1, LANES), const2)],
        out_specs=(pl.BlockSpec((bm, LANES), lambda i: (i, 0)), pl.BlockSpec((bm, LANES), lambda i: (i, 0)),
                   pl.BlockSpec((1, LANES), const2), pl.BlockSpec((bm, nch, LANES), lambda i: (i, 0, 0))),
        scratch_shapes=[pltpu.VMEM((1, LANES), F32), pltpu.VMEM((bm, d), F32)],
        compiler_params=_cparams(("arbitrary",)),
        name="router",
    )(xa, xb, nw, wr, br)


def _rows_to_lanes(g):
    t = pltpu.einshape("rcl->crl", g)
    return jnp.concatenate([t[c] for c in range(t.shape[0])], axis=-1)


def _lanes_to_rows(x):
    parts = jnp.stack([x[:, c * LANES:(c + 1) * LANES] for c in range(x.shape[1] // LANES)], axis=0)
    return pltpu.einshape("crl->rcl", parts)


GATHER_GROUP = 8


def _gather_rows(idx_ref, src_hbm, dst, sem, n_groups):
    def body(g, carry):
        for u in range(GATHER_GROUP):
            r = g * GATHER_GROUP + u
            pltpu.make_async_copy(src_hbm.at[idx_ref[0, 0, r]], dst.at[r], sem).start()
        return carry
    lax.fori_loop(0, n_groups, body, 0)


def _wait_rows(src_hbm, dst, sem, n):
    pltpu.make_async_copy(src_hbm.at[pl.ds(0, n)], dst.at[pl.ds(0, n)], sem).wait()


def _moe_kernel(cfg, te_ref, nr_ref, nv_ref, src_ref, srcn_ref, x_hbm, nw_ref, wg_ref, wu_ref, wd_ref,
                ys_ref, xbuf, sem, wgb, wub, wdb):
    j = pl.program_id(0)
    n_valid = nv_ref[0]

    @pl.when(j == 0)
    def _():
        xbuf[...] = jnp.zeros_like(xbuf)
        _gather_rows(src_ref, x_hbm, xbuf.at[0], sem.at[0], nr_ref[0] // GATHER_GROUP)

    @pl.when(j + 1 < n_valid)
    def _():
        nslot = (j + 1) % 2
        _gather_rows(srcn_ref, x_hbm, xbuf.at[nslot], sem.at[nslot], nr_ref[j + 1] // GATHER_GROUP)

    @pl.when(j < n_valid)
    def _():
        slot = j % 2
        _wait_rows(x_hbm, xbuf.at[slot], sem.at[slot], nr_ref[j])

        @pl.when((j == 0) | (te_ref[j] != te_ref[jnp.maximum(j - 1, 0)]))
        def _():
            wgb[...] = wg_ref[0].astype(BF16)
            wub[...] = wu_ref[0].astype(BF16)
            wdb[...] = wd_ref[0].astype(BF16)

        hb = _rms(_rows_to_lanes(xbuf[slot]), nw_ref[...]).astype(BF16)
        hid = (_silu(jnp.dot(hb, wgb[...], preferred_element_type=F32))
               * jnp.dot(hb, wub[...], preferred_element_type=F32))
        y = jnp.dot(hid.astype(BF16), wdb[...], preferred_element_type=F32)
        ys_ref[...] = _lanes_to_rows(y)

    @pl.when(j >= n_valid)
    def _():
        ys_ref[...] = jnp.zeros_like(ys_ref)


def moe_routed(cfg, tile_expert, tile_rows, n_valid, src, x_rows, nw, wg, wu, wd, tm):
    n_tiles = src.shape[0]
    d = cfg.d_model
    de = cfg.d_expert
    nch = d // LANES
    wmap = lambda j, te, nr, nv: (te[jnp.minimum(j, nv[0] - 1)], 0, 0)
    return pl.pallas_call(
        functools.partial(_moe_kernel, cfg),
        out_shape=jax.ShapeDtypeStruct((n_tiles * tm, nch, LANES), F32),
        grid_spec=pltpu.PrefetchScalarGridSpec(
            num_scalar_prefetch=3,
            grid=(n_tiles,),
            in_specs=[pl.BlockSpec((1, 1, tm), lambda j, te, nr, nv: (j, 0, 0), memory_space=pltpu.SMEM),
                      pl.BlockSpec((1, 1, tm), lambda j, te, nr, nv: (jnp.minimum(j + 1, n_tiles - 1), 0, 0),
                                   memory_space=pltpu.SMEM),
                      pl.BlockSpec(memory_space=pl.ANY),
                      pl.BlockSpec((1, d), lambda j, te, nr, nv: (0, 0)),
                      pl.BlockSpec((1, d, de), wmap), pl.BlockSpec((1, d, de), wmap),
                      pl.BlockSpec((1, de, d), wmap)],
            out_specs=pl.BlockSpec((tm, nch, LANES), lambda j, te, nr, nv: (j, 0, 0)),
            scratch_shapes=[pltpu.VMEM((2, tm, nch, LANES), F32), pltpu.SemaphoreType.DMA((2,)),
                            pltpu.VMEM((d, de), BF16), pltpu.VMEM((d, de), BF16), pltpu.VMEM((de, d), BF16)]),
        compiler_params=_cparams(("arbitrary",)),
        name="moe",
    )(tile_expert, tile_rows, n_valid, src, src, x_rows, nw, wg, wu, wd)


def _combine_kernel(n_steps, d0_ref, d1_ref, d0n_ref, d1n_ref, wt_ref, x1_ref, fw_ref, ys_hbm, y_ref, gbuf, sem):
    i = pl.program_id(0)
    groups = x1_ref.shape[0] // GATHER_GROUP

    def fetch(r0_ref, r1_ref, slot):
        _gather_rows(r0_ref, ys_hbm, gbuf.at[slot, 0], sem.at[slot, 0], groups)
        _gather_rows(r1_ref, ys_hbm, gbuf.at[slot, 1], sem.at[slot, 1], groups)

    @pl.when(i == 0)
    def _():
        fetch(d0_ref, d1_ref, 0)

    @pl.when(i + 1 < n_steps)
    def _():
        fetch(d0n_ref, d1n_ref, (i + 1) % 2)

    slot = i % 2
    bm = x1_ref.shape[0]
    _wait_rows(ys_hbm, gbuf.at[slot, 0], sem.at[slot, 0], bm)
    _wait_rows(ys_hbm, gbuf.at[slot, 1], sem.at[slot, 1], bm)
    wt = wt_ref[...]
    acc = (x1_ref[...] + wt[:, 0:1] * _rows_to_lanes(gbuf[slot, 0])
           + wt[:, 1:2] * _rows_to_lanes(gbuf[slot, 1]))
    y_ref[...] = _rms(acc, fw_ref[...])


def moe_combine(cfg, dest0, dest1, wt, wt_blk0, x1, fw, ys, bm):
    m, d = x1.shape
    nch = d // LANES
    n = m // bm
    cur = lambda: pl.BlockSpec((1, 1, bm), lambda i: (i, 0, 0), memory_space=pltpu.SMEM)
    nxt = lambda: pl.BlockSpec((1, 1, bm), lambda i: (jnp.minimum(i + 1, n - 1), 0, 0), memory_space=pltpu.SMEM)
    return pl.pallas_call(
        functools.partial(_combine_kernel, n),
        out_shape=jax.ShapeDtypeStruct((m, d), F32),
        grid=(n,),
        in_specs=[cur(), cur(), nxt(), nxt(),
                  pl.BlockSpec((bm, LANES), lambda i: (i + wt_blk0, 0)),
                  pl.BlockSpec((bm, d), lambda i: (i, 0)), pl.BlockSpec((1, d), lambda i: (0, 0)),
                  pl.BlockSpec(memory_space=pl.ANY)],
        out_specs=pl.BlockSpec((bm, d), lambda i: (i, 0)),
        scratch_shapes=[pltpu.VMEM((2, 2, bm, nch, LANES), F32), pltpu.SemaphoreType.DMA((2, 2))],
        compiler_params=_cparams(("arbitrary",)),
        name="moe_combine",
    )(dest0, dest1, dest0, dest1, wt, x1, fw, ys)


def _route_tables(cfg, ei, cnt, tm, n_tiles):
    ne = cfg.n_experts
    i32 = jnp.int32
    counts = cnt[0, :ne].astype(i32)
    tiles_e = (counts + tm - 1) // tm
    tile_end = jnp.cumsum(tiles_e)
    tile_start = tile_end - tiles_e
    row_off = tile_start * tm
    tile_id = jnp.arange(n_tiles, dtype=i32)
    tile_expert = jnp.minimum(jnp.sum((tile_id[:, None] >= tile_end[None, :]).astype(i32), axis=1), ne - 1)
    used = jnp.clip(counts[tile_expert] - (tile_id - tile_start[tile_expert]) * tm, 0, tm)
    used = jnp.where(tile_id < tile_end[-1], used, 0)
    tile_rows = (used + GATHER_GROUP - 1) // GATHER_GROUP * GATHER_GROUP
    dest = row_off[ei[:, 0:2]] + ei[:, 2:4]
    tok = jnp.broadcast_to(jnp.arange(ei.shape[0], dtype=i32)[:, None], dest.shape)
    src = jnp.zeros((n_tiles * tm,), i32).at[dest.reshape(-1)].set(
        tok.reshape(-1), unique_indices=True, mode="promise_in_bounds")
    return tile_expert, tile_rows, tile_end[-1:].astype(i32), src.reshape(n_tiles, 1, tm), dest


def _pad_hist(hist):
    return jnp.pad(hist, ((0, 0), (SUBLANES - hist.shape[1], 0), (0, 0)))


def _tile(m, pref):
    return pref if m % pref == 0 else m


def _mixer_segment(cfg, x2d, n_seq, seq_len, hist_xbc, hist_xm, s0, c0, n0, m0, p):
    d = cfg.d_model
    m = n_seq * seq_len
    q = min(cfg.chunk, seq_len)
    nc = seq_len // q
    h, dcol, dT = pre_norm(x2d, p["norm_mix_w"], p["w_dt"], p["w_dtT"], p["bdt_row"], p["bdt_col"], _tile(m, 512))
    proj = in_proj(cfg, h, p["w_in"], p["w_xm_o"], _tile(m, 1024), d // 2)
    xa, qa, ka, va, g, gT = conv_qkv(cfg, proj, hist_xbc, hist_xm, p["cwx"], p["cbx"], p["cwm"], p["cbm"],
                                     p["wq"], p["wk"], p["wv"], p["wg"], p["wgT"], p["bg_row"], p["bg_col"],
                                     n_seq, seq_len, q)
    dTc = dT[:cfg.ssd_heads].reshape(cfg.ssd_heads, n_seq, nc, q).transpose(1, 2, 0, 3)
    ys, s_new = ssd_scan(cfg, xa, dcol, dTc, s0, p["a_row"], p["a_col"], p["dskip"], n_seq, seq_len, q)
    hm, c_new, n_new, m_new = mlstm_scan(cfg, qa, ka, va, g, gT, c0, n0, m0, n_seq, seq_len, q)
    tail = proj.reshape(n_seq, seq_len, -1)[:, seq_len - (cfg.conv_w - 1):]
    tail_xbc = tail[:, :, 3 * d:]
    tail_xm = tail[:, :, d:2 * d]
    return proj, ys, hm, (tail_xbc, s_new, tail_xm, c_new, n_new, m_new)


MOE_TILE = 256


def _ffn(cfg, segments, p):
    d = cfg.d_model
    x1s = [out_proj(cfg, ys, proj, hm, x2d, p["ssd_norm_w"], p["mlstm_norm_w"], p["w_out"],
                    _tile(x2d.shape[0], 256), d) for x2d, proj, ys, hm in segments]
    bm = 256 if all(x1.shape[0] % 256 == 0 for x1 in x1s) else 128
    ei, wt, cnt, x_rows = router(cfg, x1s[0], x1s[1], p["norm_ffn_w"], p["wr"], p["br"], bm)
    n_tok = ei.shape[0]
    n_tiles = (2 * n_tok + cfg.n_experts * (MOE_TILE - 1)) // MOE_TILE
    tile_expert, tile_rows, n_valid, src, dest = _route_tables(cfg, ei, cnt, MOE_TILE, n_tiles)
    ys_sorted = moe_routed(cfg, tile_expert, tile_rows, n_valid, src, x_rows, p["norm_ffn_w"],
                           p["w_gate"], p["w_up"], p["w_down"], MOE_TILE)
    outs = []
    off = 0
    for x1 in x1s:
        m = x1.shape[0]
        dseg = dest[off:off + m]
        outs.append(moe_combine(cfg, dseg[:, 0].reshape(m // bm, 1, bm), dseg[:, 1].reshape(m // bm, 1, bm),
                                wt, off // bm, x1, p["final_norm_w"], ys_sorted, bm))
        off += m
    return outs


def _prep_params(cfg, norm_mix_w, w_in, conv_ssd_w, conv_ssd_b, dt_bias, a_log, d_skip, ssd_norm_w,
                 conv_mlstm_w, conv_mlstm_b, w_q, w_k, w_v, w_igate, b_igate, w_fgate, b_fgate, mlstm_norm_w,
                 w_out, norm_ffn_w, w_group, b_group, w_router, b_router, w_gate, w_up, w_down, final_norm_w):
    d = cfg.d_model
    hs = cfg.ssd_heads
    nh = cfg.ml_heads
    o_z, o_xbc = d, d + cfg.xbc_dim
    o_dt = o_xbc + hs
    o_xm = o_dt + d
    w_z, w_xbc, w_dt, w_xm, w_o = (w_in[:, :o_z], w_in[:, o_z:o_xbc], w_in[:, o_xbc:o_dt],
                                   w_in[:, o_dt:o_xm], w_in[:, o_xm:])
    row = lambda v: v.reshape(1, -1).astype(F32)
    pad_lanes = lambda a: jnp.pad(a, ((0, 0), (0, LANES - a.shape[1])))
    w_dt_p = pad_lanes(w_dt)
    a = -jnp.exp(a_log.astype(F32))
    w_gates = jnp.concatenate([w_igate, w_fgate], axis=1)
    b_gates = jnp.concatenate([b_igate, b_fgate]).astype(F32)
    ne = cfg.n_experts
    wr = pad_lanes(jnp.concatenate([w_router, w_group], axis=1))
    br = pad_lanes(jnp.concatenate([b_router, b_group]).reshape(1, -1).astype(F32))
    return dict(
        norm_mix_w=row(norm_mix_w),
        w_in=w_in.astype(F32), w_xm_o=w_in[:, o_dt:].astype(BF16),
        w_dt=w_dt_p.astype(BF16), w_dtT=w_dt_p.T.astype(BF16),
        bdt_row=pad_lanes(row(dt_bias)), bdt_col=pad_lanes(row(dt_bias)).T,
        cwx=conv_ssd_w.astype(F32), cbx=row(conv_ssd_b), cwm=conv_mlstm_w.astype(F32), cbm=row(conv_mlstm_b),
        wq=w_q.astype(BF16), wk=w_k.astype(BF16), wv=w_v.astype(BF16),
        wg=pad_lanes(w_gates).astype(BF16), wgT=w_gates.T.astype(BF16),
        bg_row=pad_lanes(row(b_gates)), bg_col=b_gates.reshape(-1, 1),
        a_row=pad_lanes(row(a)), a_col=a.reshape(-1, 1),
        dskip=row(jnp.repeat(d_skip.astype(F32), cfg.ssd_head_dim)),
        ssd_norm_w=row(ssd_norm_w), mlstm_norm_w=row(mlstm_norm_w), w_out=w_out.astype(BF16),
        norm_ffn_w=row(norm_ffn_w), wr=wr.astype(BF16), br=br,
        w_gate=w_gate.astype(F32), w_up=w_up.astype(F32), w_down=w_down.astype(F32),
        final_norm_w=row(final_norm_w),
    )


def forward(cfg, x_prompt, x_sample, state_ssd_conv, state_ssd, state_mlstm_conv, state_mlstm_c,
            state_mlstm_n, state_mlstm_m, meta_tokens, *weights):
    d = cfg.d_model
    nh = cfg.ml_heads
    hd = cfg.ml_head_dim
    assert state_ssd.shape[0] == 1, "single-layer kernel"
    p = _prep_params(cfg, *[w[0] for w in weights[:-1]], weights[-1])
    bp, lp, _ = x_prompt.shape
    bs, ls, _ = x_sample.shape
    n_meta = meta_tokens.shape[0]

    zeros = lambda *s: jnp.zeros(s, F32)
    _, _, _, st_meta = _mixer_segment(
        cfg, meta_tokens.astype(F32), 1, n_meta, zeros(1, SUBLANES, cfg.xbc_dim), zeros(1, SUBLANES, d),
        zeros(1, d, cfg.ssd_state), zeros(1, d, hd), zeros(1, nh, hd), zeros(1, nh, LANES), p)
    mt_xbc, mt_s, mt_xm, mt_c, mt_n, mt_m = st_meta
    rep = lambda a: jnp.broadcast_to(a, (bp,) + a.shape[1:])

    xp = x_prompt.reshape(bp * lp, d)
    proj_p, ys_p, hm_p, st_p = _mixer_segment(
        cfg, xp, bp, lp, rep(_pad_hist(mt_xbc)), rep(_pad_hist(mt_xm)), rep(mt_s), rep(mt_c), rep(mt_n),
        rep(mt_m), p)

    xs = x_sample.reshape(bs * ls, d)
    m0 = jnp.broadcast_to(state_mlstm_m[0].astype(F32)[:, :, None], (bs, nh, LANES))
    proj_s, ys_s, hm_s, st_s = _mixer_segment(
        cfg, xs, bs, ls, _pad_hist(state_ssd_conv[0]), _pad_hist(state_mlstm_conv[0]),
        state_ssd[0].reshape(bs, d, cfg.ssd_state), state_mlstm_c[0].reshape(bs, d, hd),
        state_mlstm_n[0], m0, p)
    y_p, y_s = _ffn(cfg, [(xp, proj_p, ys_p, hm_p), (xs, proj_s, ys_s, hm_s)], p)
    y_prompt = y_p.reshape(bp, lp, d)
    y_sample = y_s.reshape(bs, ls, d)

    def pack(st, b):
        t_xbc, s_new, t_xm, c_new, n_new, m_new = st
        return (t_xbc[None], s_new.reshape(1, b, cfg.ssd_heads, cfg.ssd_head_dim, cfg.ssd_state),
                t_xm[None], c_new.reshape(1, b, nh, hd, hd), n_new[None], m_new[None, :, :, 0])

    return (y_prompt, y_sample) + pack(st_p, bp) + pack(st_s, bs)


def kernel(x_prompt, x_sample, state_ssd_conv, state_ssd, state_mlstm_conv, state_mlstm_c, state_mlstm_n, state_mlstm_m, meta_tokens, norm_mix_w, w_in, conv_ssd_w, conv_ssd_b, dt_bias, a_log, d_skip, ssd_norm_w, conv_mlstm_w, conv_mlstm_b, w_q, w_k, w_v, w_igate, b_igate, w_fgate, b_fgate, mlstm_norm_w, w_out, norm_ffn_w, w_group, b_group, w_router, b_router, w_gate, w_up, w_down, final_norm_w):
    return forward(Cfg(), x_prompt, x_sample, state_ssd_conv, state_ssd, state_mlstm_conv, state_mlstm_c,
                   state_mlstm_n, state_mlstm_m, meta_tokens, norm_mix_w, w_in, conv_ssd_w, conv_ssd_b, dt_bias,
                   a_log, d_skip, ssd_norm_w, conv_mlstm_w, conv_mlstm_b, w_q, w_k, w_v, w_igate, b_igate,
                   w_fgate, b_fgate, mlstm_norm_w, w_out, norm_ffn_w, w_group, b_group, w_router, b_router,
                   w_gate, w_up, w_down, final_norm_w)
```

```python
import functools
from typing import NamedTuple

import jax
import jax.numpy as jnp
from jax import lax
from jax.experimental import pallas as pl
from jax.experimental.pallas import tpu as pltpu

F32 = jnp.float32
BF16 = jnp.bfloat16
EPS = 1e-6
LANES = 128
SUBLANES = 8
VMEM_LIMIT = 52 * 1024 * 1024
HI = lax.Precision.HIGHEST


class Cfg(NamedTuple):
    d_model: int = 2048
    ssd_heads: int = 32
    ssd_head_dim: int = 64
    ssd_groups: int = 4
    ssd_state: int = 128
    ml_heads: int = 8
    ml_head_dim: int = 256
    n_groups: int = 4
    experts_per_group: int = 8
    d_expert: int = 512
    n_meta: int = 16
    conv_w: int = 4
    chunk: int = 128

    @property
    def bc_dim(self):
        return self.ssd_groups * self.ssd_state

    @property
    def xbc_dim(self):
        return self.d_model + 2 * self.bc_dim

    @property
    def n_experts(self):
        return self.n_groups * self.experts_per_group


def _cparams(sem):
    return pltpu.CompilerParams(dimension_semantics=sem, vmem_limit_bytes=VMEM_LIMIT)


def _softplus(x):
    return jnp.maximum(x, 0.0) + jnp.log1p(jnp.exp(-jnp.abs(x)))


def _sigmoid(x):
    return 1.0 / (1.0 + jnp.exp(-x))


def _silu(x):
    return x * _sigmoid(x)


def _rms(x, w):
    return x * lax.rsqrt(jnp.mean(x * x, axis=-1, keepdims=True) + EPS) * w


def _prenorm_kernel(x_ref, nw_ref, wdt_ref, wdtT_ref, bdt_row_ref, bdt_col_ref, h_ref, d_ref, dT_ref):
    hb = _rms(x_ref[...], nw_ref[...]).astype(BF16)
    h_ref[...] = hb
    dt = jnp.dot(hb, wdt_ref[...], preferred_element_type=F32)
    d_ref[...] = _softplus(dt + bdt_row_ref[...])
    dtT = lax.dot_general(wdtT_ref[...], hb, (((1,), (1,)), ((), ())), preferred_element_type=F32)
    dT_ref[...] = _softplus(dtT + bdt_col_ref[...])


def pre_norm(x, norm_w, w_dt, w_dtT, bdt_row, bdt_col, bm):
    m, d = x.shape
    const2 = lambda i: (0, 0)
    return pl.pallas_call(
        _prenorm_kernel,
        out_shape=(jax.ShapeDtypeStruct((m, d), BF16), jax.ShapeDtypeStruct((m, LANES), F32),
                   jax.ShapeDtypeStruct((LANES, m), F32)),
        grid=(m // bm,),
        in_specs=[pl.BlockSpec((bm, d), lambda i: (i, 0)), pl.BlockSpec((1, d), const2),
                  pl.BlockSpec((d, LANES), const2), pl.BlockSpec((LANES, d), const2),
                  pl.BlockSpec((1, LANES), const2), pl.BlockSpec((LANES, 1), const2)],
        out_specs=(pl.BlockSpec((bm, d), lambda i: (i, 0)), pl.BlockSpec((bm, LANES), lambda i: (i, 0)),
                   pl.BlockSpec((LANES, bm), lambda i: (0, i))),
        compiler_params=_cparams(("arbitrary",)),
        name="pre_norm",
    )(x, norm_w, w_dt, w_dtT, bdt_row, bdt_col)


def _inproj_kernel(h_ref, wt_ref, proj_ref, w_ref):
    @pl.when(pl.program_id(1) == 0)
    def _():
        w_ref[...] = wt_ref[...].astype(BF16)

    proj_ref[...] = lax.dot_general(h_ref[...], w_ref[...], (((1,), (1,)), ((), ())),
                                    preferred_element_type=F32)


def in_proj(cfg, h, w_in_t, bm, bn):
    m, d = h.shape
    nz = d // bn
    nx = cfg.xbc_dim // bn
    n_a = nz + nx
    n_blocks = n_a + 2 * nz
    skip = cfg.ssd_heads
    assert skip % SUBLANES == 0
    w_row = lambda j: pl.multiple_of(jnp.where(j < n_a, j * bn, j * bn + skip), SUBLANES)
    out_col = lambda j: jnp.where(j < nz, j, jnp.where(j < n_a, j + 2 * nz, j - nx))
    return pl.pallas_call(
        _inproj_kernel,
        out_shape=jax.ShapeDtypeStruct((m, n_blocks * bn), F32),
        grid=(n_blocks, m // bm),
        in_specs=[pl.BlockSpec((bm, d), lambda j, i: (i, 0)),
                  pl.BlockSpec((pl.Element(bn), pl.Element(d)), lambda j, i: (w_row(j), 0))],
        out_specs=pl.BlockSpec((bm, bn), lambda j, i: (i, out_col(j))),
        scratch_shapes=[pltpu.VMEM((bn, d), BF16)],
        compiler_params=_cparams(("arbitrary", "arbitrary")),
        name="in_proj",
    )(h, w_in_t)


def _causal_conv(u, prev, w_ref, b_ref, conv_w):
    lt = u.shape[0]
    row8 = lax.broadcasted_iota(jnp.int32, (SUBLANES, u.shape[1]), 0)
    acc = u * w_ref[conv_w - 1:conv_w, :] + b_ref[...]
    for s in range(1, conv_w):
        rolled = pltpu.roll(u, s, axis=0)
        head = jnp.where(row8 < s, pltpu.roll(prev, s, axis=0), rolled[0:SUBLANES])
        shifted = head if lt == SUBLANES else jnp.concatenate([head, rolled[SUBLANES:]], axis=0)
        acc = acc + shifted * w_ref[conv_w - 1 - s:conv_w - s, :]
    return acc


def _conv_qkv_kernel(cfg, xbc_ref, xm_ref, hxbc_ref, hxm_ref, cwx_ref, cbx_ref, cwm_ref, cbm_ref,
                     wq_ref, wk_ref, wv_ref, wg_ref, wgT_ref, bg_row_ref, bg_col_ref,
                     xa_ref, q_ref, k_ref, v_ref, g_ref, gT_ref, px_ref, pm_ref):
    lt = xbc_ref.shape[0]
    hd = cfg.ml_head_dim
    nh = cfg.ml_heads

    @pl.when(pl.program_id(1) == 0)
    def _():
        px_ref[...] = hxbc_ref[0]
        pm_ref[...] = hxm_ref[0]

    u = xbc_ref[...]
    xa_ref[...] = _silu(_causal_conv(u, px_ref[...], cwx_ref, cbx_ref, cfg.conv_w))
    px_ref[...] = u[lt - SUBLANES:lt]

    xm = xm_ref[...]
    xc = _silu(_causal_conv(xm, pm_ref[...], cwm_ref, cbm_ref, cfg.conv_w)).astype(BF16)
    pm_ref[...] = xm[lt - SUBLANES:lt]
    xv = xm.astype(BF16)
    kscale = hd ** -0.5
    for h in range(nh):
        sl = slice(h * hd, (h + 1) * hd)
        q_ref[:, sl] = jnp.dot(xc[:, sl], wq_ref[h], preferred_element_type=F32)
        k_ref[:, sl] = jnp.dot(xc[:, sl], wk_ref[h], preferred_element_type=F32) * kscale
        v_ref[:, sl] = jnp.dot(xv[:, sl], wv_ref[h], preferred_element_type=F32)
    d = nh * hd
    qb = q_ref[...].astype(BF16)
    kb = k_ref[...].astype(BF16)
    vb = v_ref[...].astype(BF16)
    gcol = (jnp.dot(qb, wg_ref[0:d, :], preferred_element_type=F32)
            + jnp.dot(kb, wg_ref[d:2 * d, :], preferred_element_type=F32)
            + jnp.dot(vb, wg_ref[2 * d:3 * d, :], preferred_element_type=F32)) + bg_row_ref[...]
    nt = (((1,), (1,)), ((), ()))
    grow = (lax.dot_general(wgT_ref[:, 0:d], qb, nt, preferred_element_type=F32)
            + lax.dot_general(wgT_ref[:, d:2 * d], kb, nt, preferred_element_type=F32)
            + lax.dot_general(wgT_ref[:, 2 * d:3 * d], vb, nt, preferred_element_type=F32)) + bg_col_ref[...]
    lane = lax.broadcasted_iota(jnp.int32, gcol.shape, 1)
    g_ref[...] = jnp.where(lane < nh, gcol, -_softplus(-gcol))
    row = lax.broadcasted_iota(jnp.int32, grow.shape, 0)
    gT_ref[0, 0] = jnp.where(row < nh, grow, -_softplus(-grow))


def conv_qkv(cfg, proj, hist_xbc, hist_xm, cwx, cbx, cwm, cbm, wq, wk, wv, wg, wgT, bg_row, bg_col,
             n_seq, seq_len, lt):
    d = cfg.d_model
    xbc = cfg.xbc_dim
    m = n_seq * seq_len
    nt = seq_len // lt
    ng = 2 * cfg.ml_heads
    xbc_blk = (3 * d) // xbc
    row = lambda s, l: (s * nt + l, 0)
    const2 = lambda s, l: (0, 0)
    const3 = lambda s, l: (0, 0, 0)
    return pl.pallas_call(
        functools.partial(_conv_qkv_kernel, cfg),
        out_shape=(jax.ShapeDtypeStruct((m, xbc), F32),
                   jax.ShapeDtypeStruct((m, d), F32), jax.ShapeDtypeStruct((m, d), F32),
                   jax.ShapeDtypeStruct((m, d), F32),
                   jax.ShapeDtypeStruct((m, LANES), F32),
                   jax.ShapeDtypeStruct((n_seq, nt, ng, lt), F32)),
        grid=(n_seq, nt),
        in_specs=[pl.BlockSpec((lt, xbc), lambda s, l: (s * nt + l, xbc_blk)),
                  pl.BlockSpec((lt, d), lambda s, l: (s * nt + l, 1)),
                  pl.BlockSpec((1, SUBLANES, xbc), lambda s, l: (s, 0, 0)),
                  pl.BlockSpec((1, SUBLANES, d), lambda s, l: (s, 0, 0)),
                  pl.BlockSpec((cfg.conv_w, xbc), const2), pl.BlockSpec((1, xbc), const2),
                  pl.BlockSpec((cfg.conv_w, d), const2), pl.BlockSpec((1, d), const2),
                  pl.BlockSpec(wq.shape, const3), pl.BlockSpec(wk.shape, const3),
                  pl.BlockSpec(wv.shape, const3),
                  pl.BlockSpec(wg.shape, const2), pl.BlockSpec(wgT.shape, const2),
                  pl.BlockSpec((1, LANES), const2), pl.BlockSpec((ng, 1), const2)],
        out_specs=(pl.BlockSpec((lt, xbc), row), pl.BlockSpec((lt, d), row), pl.BlockSpec((lt, d), row),
                   pl.BlockSpec((lt, d), row), pl.BlockSpec((lt, LANES), row),
                   pl.BlockSpec((1, 1, ng, lt), lambda s, l: (s, l, 0, 0))),
        scratch_shapes=[pltpu.VMEM((SUBLANES, xbc), F32), pltpu.VMEM((SUBLANES, d), F32)],
        compiler_params=_cparams(("arbitrary", "arbitrary")),
        name="conv_qkv",
    )(proj, proj, hist_xbc, hist_xm, cwx, cbx, cwm, cbm, wq, wk, wv, wg, wgT, bg_row, bg_col)


def _tri(q, lower):
    r = lax.broadcasted_iota(jnp.int32, (q, q), 0)
    c = lax.broadcasted_iota(jnp.int32, (q, q), 1)
    return (c <= r) if lower else (r <= c)


def _ssd_kernel(cfg, n_chunks, xa_ref, d_ref, dT_ref, s0_ref, arow_ref, acol_ref, dskip_ref,
                y_ref, sout_ref, st_ref):
    q = xa_ref.shape[0]
    dm = cfg.d_model
    ns = cfg.ssd_state
    hp = cfg.ssd_head_dim
    hpg = cfg.ssd_heads // cfg.ssd_groups
    heads_per_tile = LANES // hp
    n_tiles = cfg.ssd_heads // heads_per_tile

    @pl.when(pl.program_id(1) == 0)
    def _():
        st_ref[...] = s0_ref[0]

    causal = _tri(q, True)
    tril = causal.astype(F32)
    triu = _tri(q, False).astype(F32)
    dcol = d_ref[...]
    drow = dT_ref[0, 0]
    acum = jnp.dot(tril, dcol * arow_ref[...], precision=HI, preferred_element_type=F32)
    acumT = jnp.dot(drow * acol_ref[...], triu, precision=HI, preferred_element_type=F32)
    nt_dims = (((1,), (1,)), ((), ()))
    tn_dims = (((0,), (0,)), ((), ()))
    lane = lax.broadcasted_iota(jnp.int32, (q, LANES), 1)
    srow = lax.broadcasted_iota(jnp.int32, (LANES, ns), 0)

    cbs = []
    bgs = []
    cgs = []
    for g in range(cfg.ssd_groups):
        bg = xa_ref[:, dm + g * ns: dm + (g + 1) * ns].astype(BF16)
        cg = xa_ref[:, dm + cfg.bc_dim + g * ns: dm + cfg.bc_dim + (g + 1) * ns].astype(BF16)
        cbs.append(lax.dot_general(cg, bg, nt_dims, preferred_element_type=F32))
        bgs.append(bg)
        cgs.append(cg)

    for t in range(n_tiles):
        h0 = t * heads_per_tile
        g = h0 // hpg
        cols = slice(t * LANES, (t + 1) * LANES)
        x = xa_ref[:, cols]
        dsel = jnp.zeros((q, LANES), F32)
        esel = jnp.zeros((q, LANES), F32)
        tsel = jnp.zeros((q, LANES), F32)
        rdec = jnp.zeros((LANES, ns), F32)
        for i in range(heads_per_tile):
            h = h0 + i
            in_head = (lane >= i * hp) & (lane < (i + 1) * hp)
            a_col = acum[:, h:h + 1]
            a_last = acum[q - 1:q, h:h + 1]
            dsel = jnp.where(in_head, dcol[:, h:h + 1], dsel)
            esel = jnp.where(in_head, jnp.exp(a_col), esel)
            tsel = jnp.where(in_head, jnp.exp(a_last - a_col), tsel)
            rdec = jnp.where((srow >= i * hp) & (srow < (i + 1) * hp), jnp.exp(a_last), rdec)
        xd = x * dsel
        y = x * dskip_ref[:, cols]
        for i in range(heads_per_tile):
            h = h0 + i
            in_head = (lane >= i * hp) & (lane < (i + 1) * hp)
            seg = jnp.where(causal, acum[:, h:h + 1] - acumT[h:h + 1, :], -jnp.inf)
            w = (cbs[g] * jnp.exp(seg)).astype(BF16)
            xdh = jnp.where(in_head, xd, 0.0).astype(BF16)
            y = y + jnp.dot(w, xdh, preferred_element_type=F32)
        s_old = st_ref[cols, :]
        ys = lax.dot_general(cgs[g], s_old.astype(BF16), nt_dims, preferred_element_type=F32)
        y_ref[:, cols] = y + esel * ys
        upd = lax.dot_general((xd * tsel).astype(BF16), bgs[g], tn_dims, preferred_element_type=F32)
        st_ref[cols, :] = rdec * s_old + upd

    @pl.when(pl.program_id(1) == n_chunks - 1)
    def _():
        sout_ref[0] = st_ref[...]


def ssd_scan(cfg, xa, d, dT, s0, a_row, a_col, dskip, n_seq, seq_len, q):
    dm = cfg.d_model
    nc = seq_len // q
    m = n_seq * seq_len
    row = lambda s, c: (s * nc + c, 0)
    const2 = lambda s, c: (0, 0)
    return pl.pallas_call(
        functools.partial(_ssd_kernel, cfg, nc),
        out_shape=(jax.ShapeDtypeStruct((m, dm), F32),
                   jax.ShapeDtypeStruct((n_seq, dm, cfg.ssd_state), F32)),
        grid=(n_seq, nc),
        in_specs=[pl.BlockSpec((q, cfg.xbc_dim), row),
                  pl.BlockSpec((q, LANES), row),
                  pl.BlockSpec((1, 1, cfg.ssd_heads, q), lambda s, c: (s, c, 0, 0)),
                  pl.BlockSpec((1, dm, cfg.ssd_state), lambda s, c: (s, 0, 0)),
                  pl.BlockSpec((1, LANES), const2),
                  pl.BlockSpec((cfg.ssd_heads, 1), const2),
                  pl.BlockSpec((1, dm), const2)],
        out_specs=(pl.BlockSpec((q, dm), row),
                   pl.BlockSpec((1, dm, cfg.ssd_state), lambda s, c: (s, 0, 0))),
        scratch_shapes=[pltpu.VMEM((dm, cfg.ssd_state), F32)],
        compiler_params=_cparams(("arbitrary", "arbitrary")),
        name="ssd_scan",
    )(xa, d, dT, s0, a_row, a_col, dskip)


def _mlstm_kernel(cfg, n_chunks, q_ref, k_ref, v_ref, g_ref, gT_ref, c0_ref, n0_ref, m0_ref,
                  h_ref, cout_ref, nout_ref, mout_ref, c_ref, n_ref, m_ref):
    ql = q_ref.shape[0]
    hd = cfg.ml_head_dim
    nh = cfg.ml_heads

    @pl.when(pl.program_id(1) == 0)
    def _():
        c_ref[...] = c0_ref[0]
        n_ref[...] = n0_ref[0]
        m_ref[...] = m0_ref[0]

    causal = _tri(ql, True)
    gcol = g_ref[...]
    grow = gT_ref[0, 0]
    bcum = jnp.dot(causal.astype(F32), gcol, precision=HI, preferred_element_type=F32)
    bcumT = jnp.dot(grow, _tri(ql, False).astype(F32), precision=HI, preferred_element_type=F32)
    nt_dims = (((1,), (1,)), ((), ()))
    tn_dims = (((0,), (0,)), ((), ()))

    for h in range(nh):
        sl = slice(h * hd, (h + 1) * hd)
        b_col = bcum[:, nh + h:nh + h + 1]
        b_row = bcumT[nh + h:nh + h + 1, :]
        i_col = gcol[:, h:h + 1]
        i_row = grow[h:h + 1, :]
        m_prev = m_ref[h:h + 1, 0:1]
        dlog = jnp.where(causal, b_col - b_row + i_row, -jnp.inf)
        inter = b_col + m_prev
        mt = jnp.maximum(inter, jnp.max(dlog, axis=1, keepdims=True))
        qh = q_ref[:, sl]
        kh = k_ref[:, sl]
        vh = v_ref[:, sl]
        qb = qh.astype(BF16)
        kb = kh.astype(BF16)
        s = lax.dot_general(qb, kb, nt_dims, preferred_element_type=F32) * jnp.exp(dlog - mt)
        gdec = jnp.exp(inter - mt)
        c_old = c_ref[sl, :]
        n_old = n_ref[h:h + 1, :]
        qc = lax.dot_general(qb, c_old.astype(BF16), nt_dims, preferred_element_type=F32)
        num = jnp.dot(s.astype(BF16), vh.astype(BF16), preferred_element_type=F32) + gdec * qc
        den = jnp.sum(s, axis=1, keepdims=True) + gdec * jnp.sum(qh * n_old, axis=1, keepdims=True)
        h_ref[:, sl] = num / jnp.maximum(jnp.abs(den), jnp.exp(-mt))
        m_new = mt[ql - 1:ql, :]
        gs = jnp.exp(b_col[ql - 1:ql, :] - b_col + i_col - m_new)
        gc = jnp.exp(inter[ql - 1:ql, :] - m_new)
        upd = lax.dot_general((vh * gs).astype(BF16), kb, tn_dims, preferred_element_type=F32)
        c_ref[sl, :] = gc * c_old + upd
        n_ref[h:h + 1, :] = gc * n_old + jnp.sum(gs * kh, axis=0, keepdims=True)
        m_ref[h:h + 1, :] = jnp.broadcast_to(m_new, (1, LANES))

    @pl.when(pl.program_id(1) == n_chunks - 1)
    def _():
        cout_ref[0] = c_ref[...]
        nout_ref[0] = n_ref[...]
        mout_ref[0] = m_ref[...]


def mlstm_scan(cfg, qa, ka, va, g, gT, c0, n0, m0, n_seq, seq_len, q):
    d = cfg.d_model
    hd = cfg.ml_head_dim
    nh = cfg.ml_heads
    nc = seq_len // q
    m = n_seq * seq_len
    row = lambda s, c: (s * nc + c, 0)
    st3 = lambda s, c: (s, 0, 0)
    return pl.pallas_call(
        functools.partial(_mlstm_kernel, cfg, nc),
        out_shape=(jax.ShapeDtypeStruct((m, d), F32),
                   jax.ShapeDtypeStruct((n_seq, d, hd), F32),
                   jax.ShapeDtypeStruct((n_seq, nh, hd), F32),
                   jax.ShapeDtypeStruct((n_seq, nh, LANES), F32)),
        grid=(n_seq, nc),
        in_specs=[pl.BlockSpec((q, d), row), pl.BlockSpec((q, d), row), pl.BlockSpec((q, d), row),
                  pl.BlockSpec((q, LANES), row),
                  pl.BlockSpec((1, 1, 2 * nh, q), lambda s, c: (s, c, 0, 0)),
                  pl.BlockSpec((1, d, hd), st3), pl.BlockSpec((1, nh, hd), st3),
                  pl.BlockSpec((1, nh, LANES), st3)],
        out_specs=(pl.BlockSpec((q, d), row),
                   pl.BlockSpec((1, d, hd), st3), pl.BlockSpec((1, nh, hd), st3),
                   pl.BlockSpec((1, nh, LANES), st3)),
        scratch_shapes=[pltpu.VMEM((d, hd), F32), pltpu.VMEM((nh, hd), F32), pltpu.VMEM((nh, LANES), F32)],
        compiler_params=_cparams(("arbitrary", "arbitrary")),
        name="mlstm_scan",
    )(qa, ka, va, g, gT, c0, n0, m0)


def _group_norm(x, w_ref, col0, groups, width):
    parts = []
    for g in range(groups):
        seg = x[:, g * width:(g + 1) * width]
        parts.append(seg * lax.rsqrt(jnp.mean(seg * seg, axis=-1, keepdims=True) + EPS)
                     * w_ref[:, col0 + g * width: col0 + (g + 1) * width])
    return parts


def _outproj_kernel(cfg, ys_ref, z_ref, hm_ref, o_ref, x_ref, nws_ref, nwm_ref, w_ref, out_ref, mix_ref):
    d = cfg.d_model

    @pl.when(pl.program_id(1) == 0)
    def _():
        ws = d // cfg.ssd_groups
        yz = ys_ref[...] * _silu(z_ref[...])
        for g, part in enumerate(_group_norm(yz, nws_ref, 0, cfg.ssd_groups, ws)):
            mix_ref[:, g * ws:(g + 1) * ws] = part.astype(BF16)
        gate = _sigmoid(o_ref[...])
        wm = cfg.ml_head_dim
        for g, part in enumerate(_group_norm(hm_ref[...], nwm_ref, 0, cfg.ml_heads, wm)):
            mix_ref[:, d + g * wm: d + (g + 1) * wm] = (part * gate[:, g * wm:(g + 1) * wm]).astype(BF16)

    out_ref[...] = x_ref[...] + jnp.dot(mix_ref[...], w_ref[...], preferred_element_type=F32)


def out_proj(cfg, ys, proj, hm, x, nws, nwm, w_out, bm, bn):
    m, d = x.shape
    const2 = lambda i, j: (0, 0)
    full = lambda i, j: (i, 0)
    w_mode = dict(pipeline_mode=pl.Buffered(1)) if bn == d else {}
    return pl.pallas_call(
        functools.partial(_outproj_kernel, cfg),
        out_shape=jax.ShapeDtypeStruct((m, d), F32),
        grid=(m // bm, d // bn),
        in_specs=[pl.BlockSpec((bm, d), full),
                  pl.BlockSpec((bm, d), lambda i, j: (i, 0)),
                  pl.BlockSpec((bm, d), full),
                  pl.BlockSpec((bm, d), lambda i, j: (i, 2)),
                  pl.BlockSpec((bm, bn), lambda i, j: (i, j)),
                  pl.BlockSpec((1, d), const2), pl.BlockSpec((1, d), const2),
                  pl.BlockSpec((2 * d, bn), lambda i, j: (0, j), **w_mode)],
        out_specs=pl.BlockSpec((bm, bn), lambda i, j: (i, j)),
        scratch_shapes=[pltpu.VMEM((bm, 2 * d), BF16)],
        compiler_params=_cparams(("arbitrary", "arbitrary")),
        name="out_proj",
    )(ys, proj, hm, proj, x, nws, nwm, w_out)


def _router_kernel(cfg, n_a, xa_ref, xb_ref, nw_ref, wr_ref, br_ref, ei_ref, wt_ref, cnt_out_ref, rows_ref,
                   cnt_ref, x_ref):
    ne = cfg.n_experts
    epg = cfg.experts_per_group
    ngr = cfg.n_groups
    bm = x_ref.shape[0]
    i = pl.program_id(0)

    @pl.when(i == 0)
    def _():
        cnt_ref[...] = jnp.zeros_like(cnt_ref)

    @pl.when(i < n_a)
    def _():
        x_ref[...] = xa_ref[...]

    @pl.when(i >= n_a)
    def _():
        x_ref[...] = xb_ref[...]

    rows_ref[...] = _lanes_to_rows(x_ref[...])
    hb = _rms(x_ref[...], nw_ref[...]).astype(BF16)
    logits = jnp.dot(hb, wr_ref[...], preferred_element_type=F32) + br_ref[...]
    lane = lax.broadcasted_iota(jnp.int32, logits.shape, 1)
    big = jnp.int32(2 ** 30)
    neg = -jnp.inf

    def first_argmax(vals):
        mx = jnp.max(vals, axis=-1, keepdims=True)
        idx = jnp.min(jnp.where(vals == mx, lane, big), axis=-1, keepdims=True)
        return mx, idx

    is_group = (lane >= ne) & (lane < ne + ngr)
    gl = jnp.where(is_group, logits, neg)
    gmax, gidx = first_argmax(gl)
    p_g = 1.0 / jnp.sum(jnp.exp(gl - gmax), axis=-1, keepdims=True)
    e_lo = (gidx - ne) * epg
    in_sel = (lane >= e_lo) & (lane < e_lo + epg)
    el = jnp.where(in_sel, logits, neg)
    pe = jnp.exp(el - jnp.max(el, axis=-1, keepdims=True))
    pe = jnp.where(in_sel, pe / jnp.sum(pe, axis=-1, keepdims=True), -1.0)
    p1, i1 = first_argmax(pe)
    p2, i2 = first_argmax(jnp.where(lane == i1, -1.0, pe))
    wsum = p1 + p2
    wt_ref[...] = jnp.where(lane == 0, p_g * p1 / wsum, jnp.where(lane == 1, p_g * p2 / wsum, 0.0))

    oh1 = jnp.where(lane == i1, 1.0, 0.0)
    oh2 = jnp.where(lane == i2, 1.0, 0.0)
    r = lax.broadcasted_iota(jnp.int32, (bm, bm), 0)
    c = lax.broadcasted_iota(jnp.int32, (bm, bm), 1)
    before = jnp.where(c < r, 1.0, 0.0).astype(BF16)
    ahead1 = jnp.dot(before, oh1.astype(BF16), preferred_element_type=F32)
    ahead2 = jnp.dot(before, oh2.astype(BF16), preferred_element_type=F32)
    cnt = cnt_ref[...]
    tot1 = jnp.sum(oh1, axis=0, keepdims=True)
    rank1 = jnp.sum(oh1 * (cnt + ahead1), axis=-1, keepdims=True)
    rank2 = jnp.sum(oh2 * (cnt + tot1 + ahead2), axis=-1, keepdims=True)
    cnt_new = cnt + tot1 + jnp.sum(oh2, axis=0, keepdims=True)
    cnt_ref[...] = cnt_new
    cnt_out_ref[...] = cnt_new
    ei_ref[...] = jnp.where(lane == 0, i1, jnp.where(lane == 1, i2, jnp.where(
        lane == 2, rank1.astype(jnp.int32), jnp.where(lane == 3, rank2.astype(jnp.int32), 0))))


def router(cfg, xa, xb, nw, wr, br, bm):
    d = xa.shape[1]
    n_a = xa.shape[0] // bm
    n_b = xb.shape[0] // bm
    m = xa.shape[0] + xb.shape[0]
    nch = d // LANES
    const2 = lambda i: (0, 0)
    return pl.pallas_call(
        functools.partial(_router_kernel, cfg, n_a),
        out_shape=(jax.ShapeDtypeStruct((m, LANES), jnp.int32), jax.ShapeDtypeStruct((m, LANES), F32),
                   jax.ShapeDtypeStruct((1, LANES), F32), jax.ShapeDtypeStruct((m, nch, LANES), F32)),
        grid=(n_a + n_b,),
        in_specs=[pl.BlockSpec((bm, d), lambda i: (jnp.minimum(i, n_a - 1), 0)),
                  pl.BlockSpec((bm, d), lambda i: (jnp.maximum(i - n_a, 0), 0)),
                  pl.BlockSpec((1, d), const2),
                  pl.BlockSpec((d, LANES), const2), pl.BlockSpec((1, LANES), const2)],
        out_specs=(pl.BlockSpec((bm, LANES), lambda i: (i, 0)), pl.BlockSpec((bm, LANES), lambda i: (i, 0)),
                   pl.BlockSpec((1, LANES), const2), pl.BlockSpec((bm, nch, LANES), lambda i: (i, 0, 0))),
        scratch_shapes=[pltpu.VMEM((1, LANES), F32), pltpu.VMEM((bm, d), F32)],
        compiler_params=_cparams(("arbitrary",)),
        name="router",
    )(xa, xb, nw, wr, br)


def _rows_to_lanes(g):
    t = pltpu.einshape("rcl->crl", g)
    return jnp.concatenate([t[c] for c in range(t.shape[0])], axis=-1)


def _lanes_to_rows(x):
    parts = jnp.stack([x[:, c * LANES:(c + 1) * LANES] for c in range(x.shape[1] // LANES)], axis=0)
    return pltpu.einshape("crl->rcl", parts)


GATHER_GROUP = 8


def _gather_rows(idx_ref, src_hbm, dst, sem, n_groups):
    def body(g, carry):
        for u in range(GATHER_GROUP):
            r = g * GATHER_GROUP + u
            pltpu.make_async_copy(src_hbm.at[idx_ref[0, 0, r]], dst.at[r], sem).start()
        return carry
    lax.fori_loop(0, n_groups, body, 0)


def _wait_rows(src_hbm, dst, sem, n):
    pltpu.make_async_copy(src_hbm.at[pl.ds(0, n)], dst.at[pl.ds(0, n)], sem).wait()


def _moe_kernel(cfg, te_ref, nr_ref, first_ref, wslot_ref, nexte_ref, nv_ref, src_ref, srcn_ref, x_hbm, nw_ref,
                wg_hbm, wu_hbm, wd_hbm, ys_ref, xbuf, sem, wgf, wuf, wdf, wsem, wgb, wub, wdb):
    j = pl.program_id(0)
    n_valid = nv_ref[0]

    def weight_copies(e, slot):
        return (pltpu.make_async_copy(wg_hbm.at[e], wgf.at[slot], wsem.at[slot, 0]),
                pltpu.make_async_copy(wu_hbm.at[e], wuf.at[slot], wsem.at[slot, 1]),
                pltpu.make_async_copy(wd_hbm.at[e], wdf.at[slot], wsem.at[slot, 2]))

    @pl.when(j == 0)
    def _():
        for cp in weight_copies(te_ref[0], 0):
            cp.start()
        xbuf[...] = jnp.zeros_like(xbuf)
        _gather_rows(src_ref, x_hbm, xbuf.at[0], sem.at[0], nr_ref[0] // GATHER_GROUP)

    @pl.when(j + 1 < n_valid)
    def _():
        nslot = (j + 1) % 2
        _gather_rows(srcn_ref, x_hbm, xbuf.at[nslot], sem.at[nslot], nr_ref[j + 1] // GATHER_GROUP)

    @pl.when(j < n_valid)
    def _():
        @pl.when(first_ref[j] == 1)
        def _():
            ws = wslot_ref[j]
            for cp in weight_copies(te_ref[j], ws):
                cp.wait()

            @pl.when(nexte_ref[j] >= 0)
            def _():
                for cp in weight_copies(nexte_ref[j], 1 - ws):
                    cp.start()

            wgb[...] = wgf[ws].astype(BF16)
            wub[...] = wuf[ws].astype(BF16)
            wdb[...] = wdf[ws].astype(BF16)

        slot = j % 2
        _wait_rows(x_hbm, xbuf.at[slot], sem.at[slot], nr_ref[j])
        hb = _rms(_rows_to_lanes(xbuf[slot]), nw_ref[...]).astype(BF16)
        hid = (_silu(jnp.dot(hb, wgb[...], preferred_element_type=F32))
               * jnp.dot(hb, wub[...], preferred_element_type=F32))
        y = jnp.dot(hid.astype(BF16), wdb[...], preferred_element_type=F32)
        ys_ref[...] = _lanes_to_rows(y)

    @pl.when(j >= n_valid)
    def _():
        ys_ref[...] = jnp.zeros_like(ys_ref)


def moe_routed(cfg, tables, src, x_rows, nw, wg, wu, wd, tm):
    n_tiles = src.shape[0]
    d = cfg.d_model
    de = cfg.d_expert
    nch = d // LANES
    return pl.pallas_call(
        functools.partial(_moe_kernel, cfg),
        out_shape=jax.ShapeDtypeStruct((n_tiles * tm, nch, LANES), F32),
        grid_spec=pltpu.PrefetchScalarGridSpec(
            num_scalar_prefetch=len(tables),
            grid=(n_tiles,),
            in_specs=[pl.BlockSpec((1, 1, tm), lambda j, *_: (j, 0, 0), memory_space=pltpu.SMEM),
                      pl.BlockSpec((1, 1, tm), lambda j, *_: (jnp.minimum(j + 1, n_tiles - 1), 0, 0),
                                   memory_space=pltpu.SMEM),
                      pl.BlockSpec(memory_space=pl.ANY),
                      pl.BlockSpec((1, d), lambda j, *_: (0, 0)),
                      pl.BlockSpec(memory_space=pl.ANY), pl.BlockSpec(memory_space=pl.ANY),
                      pl.BlockSpec(memory_space=pl.ANY)],
            out_specs=pl.BlockSpec((tm, nch, LANES), lambda j, *_: (j, 0, 0)),
            scratch_shapes=[pltpu.VMEM((2, tm, nch, LANES), F32), pltpu.SemaphoreType.DMA((2,)),
                            pltpu.VMEM((2, d, de), F32), pltpu.VMEM((2, d, de), F32), pltpu.VMEM((2, de, d), F32),
                            pltpu.SemaphoreType.DMA((2, 3)),
                            pltpu.VMEM((d, de), BF16), pltpu.VMEM((d, de), BF16), pltpu.VMEM((de, d), BF16)]),
        compiler_params=_cparams(("arbitrary",)),
        name="moe",
    )(*tables, src, src, x_rows, nw, wg, wu, wd)


def _combine_kernel(n_steps, d0_ref, d1_ref, d0n_ref, d1n_ref, wt_ref, x1_ref, fw_ref, ys_hbm, y_ref, gbuf, sem):
    i = pl.program_id(0)
    groups = x1_ref.shape[0] // GATHER_GROUP

    def fetch(r0_ref, r1_ref, slot):
        _gather_rows(r0_ref, ys_hbm, gbuf.at[slot, 0], sem.at[slot, 0], groups)
        _gather_rows(r1_ref, ys_hbm, gbuf.at[slot, 1], sem.at[slot, 1], groups)

    @pl.when(i == 0)
    def _():
        fetch(d0_ref, d1_ref, 0)

    @pl.when(i + 1 < n_steps)
    def _():
        fetch(d0n_ref, d1n_ref, (i + 1) % 2)

    slot = i % 2
    bm = x1_ref.shape[0]
    _wait_rows(ys_hbm, gbuf.at[slot, 0], sem.at[slot, 0], bm)
    _wait_rows(ys_hbm, gbuf.at[slot, 1], sem.at[slot, 1], bm)
    wt = wt_ref[...]
    acc = (x1_ref[...] + wt[:, 0:1] * _rows_to_lanes(gbuf[slot, 0])
           + wt[:, 1:2] * _rows_to_lanes(gbuf[slot, 1]))
    y_ref[...] = _rms(acc, fw_ref[...])


def moe_combine(cfg, dest0, dest1, wt, wt_blk0, x1, fw, ys, bm):
    m, d = x1.shape
    nch = d // LANES
    n = m // bm
    cur = lambda: pl.BlockSpec((1, 1, bm), lambda i: (i, 0, 0), memory_space=pltpu.SMEM)
    nxt = lambda: pl.BlockSpec((1, 1, bm), lambda i: (jnp.minimum(i + 1, n - 1), 0, 0), memory_space=pltpu.SMEM)
    return pl.pallas_call(
        functools.partial(_combine_kernel, n),
        out_shape=jax.ShapeDtypeStruct((m, d), F32),
        grid=(n,),
        in_specs=[cur(), cur(), nxt(), nxt(),
                  pl.BlockSpec((bm, LANES), lambda i: (i + wt_blk0, 0)),
                  pl.BlockSpec((bm, d), lambda i: (i, 0)), pl.BlockSpec((1, d), lambda i: (0, 0)),
                  pl.BlockSpec(memory_space=pl.ANY)],
        out_specs=pl.BlockSpec((bm, d), lambda i: (i, 0)),
        scratch_shapes=[pltpu.VMEM((2, 2, bm, nch, LANES), F32), pltpu.SemaphoreType.DMA((2, 2))],
        compiler_params=_cparams(("arbitrary",)),
        name="moe_combine",
    )(dest0, dest1, dest0, dest1, wt, x1, fw, ys)


def _route_tables(cfg, ei, cnt, tm, n_tiles):
    ne = cfg.n_experts
    i32 = jnp.int32
    counts = cnt[0, :ne].astype(i32)
    tiles_e = (counts + tm - 1) // tm
    tile_end = jnp.cumsum(tiles_e)
    tile_start = tile_end - tiles_e
    row_off = tile_start * tm
    tile_id = jnp.arange(n_tiles, dtype=i32)
    tile_expert = jnp.minimum(jnp.sum((tile_id[:, None] >= tile_end[None, :]).astype(i32), axis=1), ne - 1)
    used = jnp.clip(counts[tile_expert] - (tile_id - tile_start[tile_expert]) * tm, 0, tm)
    used = jnp.where(tile_id < tile_end[-1], used, 0)
    tile_rows = (used + GATHER_GROUP - 1) // GATHER_GROUP * GATHER_GROUP
    n_valid = tile_end[-1]
    prev_expert = jnp.concatenate([jnp.full((1,), -1, i32), tile_expert[:-1]])
    first = ((tile_expert != prev_expert) & (tile_id < n_valid)).astype(i32)
    wslot = (jnp.cumsum(first) - 1) % 2
    e_id = jnp.arange(ne, dtype=i32)
    later = (e_id[None, :] > e_id[:, None]) & (tiles_e[None, :] > 0)
    next_e = jnp.min(jnp.where(later, e_id[None, :], ne), axis=1)
    next_e = jnp.where(next_e < ne, next_e, -1)[tile_expert]
    tables = (tile_expert, tile_rows, first, wslot.astype(i32), next_e.astype(i32), n_valid[None].astype(i32))
    picks = ei[:, 0:4].T
    dest = row_off[picks[0:2]] + picks[2:4]
    tok = jnp.broadcast_to(jnp.arange(ei.shape[0], dtype=i32)[None, :], dest.shape)
    src = jnp.zeros((n_tiles * tm,), i32).at[dest.reshape(-1)].set(
        tok.reshape(-1), unique_indices=True, mode="promise_in_bounds")
    return tables, src.reshape(n_tiles, 1, tm), dest


def _pad_hist(hist):
    return jnp.pad(hist, ((0, 0), (SUBLANES - hist.shape[1], 0), (0, 0)))


def _tile(m, pref):
    return pref if m % pref == 0 else m


def _mixer_segment(cfg, x2d, n_seq, seq_len, hist_xbc, hist_xm, s0, c0, n0, m0, p):
    d = cfg.d_model
    m = n_seq * seq_len
    q = min(cfg.chunk, seq_len)
    nc = seq_len // q
    h, dcol, dT = pre_norm(x2d, p["norm_mix_w"], p["w_dt"], p["w_dtT"], p["bdt_row"], p["bdt_col"], _tile(m, 512))
    proj = in_proj(cfg, h, p["w_in_t"], _tile(m, 1024), d // 2)
    xa, qa, ka, va, g, gT = conv_qkv(cfg, proj, hist_xbc, hist_xm, p["cwx"], p["cbx"], p["cwm"], p["cbm"],
                                     p["wq"], p["wk"], p["wv"], p["wg"], p["wgT"], p["bg_row"], p["bg_col"],
                                     n_seq, seq_len, q)
    dTc = dT[:cfg.ssd_heads].reshape(cfg.ssd_heads, n_seq, nc, q).transpose(1, 2, 0, 3)
    ys, s_new = ssd_scan(cfg, xa, dcol, dTc, s0, p["a_row"], p["a_col"], p["dskip"], n_seq, seq_len, q)
    hm, c_new, n_new, m_new = mlstm_scan(cfg, qa, ka, va, g, gT, c0, n0, m0, n_seq, seq_len, q)
    tail = proj.reshape(n_seq, seq_len, -1)[:, seq_len - (cfg.conv_w - 1):]
    tail_xbc = tail[:, :, 3 * d:]
    tail_xm = tail[:, :, d:2 * d]
    return proj, ys, hm, (tail_xbc, s_new, tail_xm, c_new, n_new, m_new)


MOE_TILE = 256


def _ffn(cfg, segments, p):
    d = cfg.d_model
    x1s = [out_proj(cfg, ys, proj, hm, x2d, p["ssd_norm_w"], p["mlstm_norm_w"], p["w_out"],
                    _tile(x2d.shape[0], 256), d) for x2d, proj, ys, hm in segments]
    bm = 256 if all(x1.shape[0] % 256 == 0 for x1 in x1s) else 128
    ei, wt, cnt, x_rows = router(cfg, x1s[0], x1s[1], p["norm_ffn_w"], p["wr"], p["br"], bm)
    n_tok = ei.shape[0]
    n_tiles = (2 * n_tok + cfg.n_experts * (MOE_TILE - 1)) // MOE_TILE
    tables, src, dest = _route_tables(cfg, ei, cnt, MOE_TILE, n_tiles)
    ys_sorted = moe_routed(cfg, tables, src, x_rows, p["norm_ffn_w"], p["w_gate"], p["w_up"], p["w_down"], MOE_TILE)
    outs = []
    off = 0
    for x1 in x1s:
        m = x1.shape[0]
        dseg = dest[:, off:off + m].reshape(2, m // bm, 1, bm)
        outs.append(moe_combine(cfg, dseg[0], dseg[1], wt, off // bm, x1, p["final_norm_w"], ys_sorted, bm))
        off += m
    return outs


def _prep_params(cfg, norm_mix_w, w_in, conv_ssd_w, conv_ssd_b, dt_bias, a_log, d_skip, ssd_norm_w,
                 conv_mlstm_w, conv_mlstm_b, w_q, w_k, w_v, w_igate, b_igate, w_fgate, b_fgate, mlstm_norm_w,
                 w_out, norm_ffn_w, w_group, b_group, w_router, b_router, w_gate, w_up, w_down, final_norm_w):
    d = cfg.d_model
    hs = cfg.ssd_heads
    nh = cfg.ml_heads
    o_xbc = d + cfg.xbc_dim
    row = lambda v: v.reshape(1, -1).astype(F32)
    pad_lanes = lambda a: jnp.pad(a, ((0, 0), (0, LANES - a.shape[1])))
    w_in_t = w_in.T.astype(F32)
    w_dt_t = jnp.pad(w_in_t[o_xbc:o_xbc + hs], ((0, LANES - hs), (0, 0)))
    a = -jnp.exp(a_log.astype(F32))
    w_gates = jnp.concatenate([w_igate, w_fgate], axis=1)
    b_gates = jnp.concatenate([b_igate, b_fgate]).astype(F32)
    ne = cfg.n_experts
    wr = pad_lanes(jnp.concatenate([w_router, w_group], axis=1))
    br = pad_lanes(jnp.concatenate([b_router, b_group]).reshape(1, -1).astype(F32))
    return dict(
        norm_mix_w=row(norm_mix_w),
        w_in_t=w_in_t, w_dt=w_dt_t.T.astype(BF16), w_dtT=w_dt_t.astype(BF16),
        bdt_row=pad_lanes(row(dt_bias)), bdt_col=pad_lanes(row(dt_bias)).T,
        cwx=conv_ssd_w.astype(F32), cbx=row(conv_ssd_b), cwm=conv_mlstm_w.astype(F32), cbm=row(conv_mlstm_b),
        wq=w_q.astype(BF16), wk=w_k.astype(BF16), wv=w_v.astype(BF16),
        wg=pad_lanes(w_gates).astype(BF16), wgT=w_gates.T.astype(BF16),
        bg_row=pad_lanes(row(b_gates)), bg_col=b_gates.reshape(-1, 1),
        a_row=pad_lanes(row(a)), a_col=a.reshape(-1, 1),
        dskip=row(jnp.repeat(d_skip.astype(F32), cfg.ssd_head_dim)),
        ssd_norm_w=row(ssd_norm_w), mlstm_norm_w=row(mlstm_norm_w), w_out=w_out.astype(BF16),
        norm_ffn_w=row(norm_ffn_w), wr=wr.astype(BF16), br=br,
        w_gate=w_gate.astype(F32), w_up=w_up.astype(F32), w_down=w_down.astype(F32),
        final_norm_w=row(final_norm_w),
    )


def forward(cfg, x_prompt, x_sample, state_ssd_conv, state_ssd, state_mlstm_conv, state_mlstm_c,
            state_mlstm_n, state_mlstm_m, meta_tokens, *weights):
    d = cfg.d_model
    nh = cfg.ml_heads
    hd = cfg.ml_head_dim
    assert state_ssd.shape[0] == 1, "single-layer kernel"
    p = _prep_params(cfg, *[w[0] for w in weights[:-1]], weights[-1])
    bp, lp, _ = x_prompt.shape
    bs, ls, _ = x_sample.shape
    n_meta = meta_tokens.shape[0]

    zeros = lambda *s: jnp.zeros(s, F32)
    _, _, _, st_meta = _mixer_segment(
        cfg, meta_tokens.astype(F32), 1, n_meta, zeros(1, SUBLANES, cfg.xbc_dim), zeros(1, SUBLANES, d),
        zeros(1, d, cfg.ssd_state), zeros(1, d, hd), zeros(1, nh, hd), zeros(1, nh, LANES), p)
    mt_xbc, mt_s, mt_xm, mt_c, mt_n, mt_m = st_meta
    rep = lambda a: jnp.broadcast_to(a, (bp,) + a.shape[1:])

    xp = x_prompt.reshape(bp * lp, d)
    proj_p, ys_p, hm_p, st_p = _mixer_segment(
        cfg, xp, bp, lp, rep(_pad_hist(mt_xbc)), rep(_pad_hist(mt_xm)), rep(mt_s), rep(mt_c), rep(mt_n),
        rep(mt_m), p)

    xs = x_sample.reshape(bs * ls, d)
    m0 = jnp.broadcast_to(state_mlstm_m[0].astype(F32)[:, :, None], (bs, nh, LANES))
    proj_s, ys_s, hm_s, st_s = _mixer_segment(
        cfg, xs, bs, ls, _pad_hist(state_ssd_conv[0]), _pad_hist(state_mlstm_conv[0]),
        state_ssd[0].reshape(bs, d, cfg.ssd_state), state_mlstm_c[0].reshape(bs, d, hd),
        state_mlstm_n[0], m0, p)
    y_p, y_s = _ffn(cfg, [(xp, proj_p, ys_p, hm_p), (xs, proj_s, ys_s, hm_s)], p)
    y_prompt = y_p.reshape(bp, lp, d)
    y_sample = y_s.reshape(bs, ls, d)

    def pack(st, b):
        t_xbc, s_new, t_xm, c_new, n_new, m_new = st
        return (t_xbc[None], s_new.reshape(1, b, cfg.ssd_heads, cfg.ssd_head_dim, cfg.ssd_state),
                t_xm[None], c_new.reshape(1, b, nh, hd, hd), n_new[None], m_new[None, :, :, 0])

    return (y_prompt, y_sample) + pack(st_p, bp) + pack(st_s, bs)


def kernel(x_prompt, x_sample, state_ssd_conv, state_ssd, state_mlstm_conv, state_mlstm_c, state_mlstm_n, state_mlstm_m, meta_tokens, norm_mix_w, w_in, conv_ssd_w, conv_ssd_b, dt_bias, a_log, d_skip, ssd_norm_w, conv_mlstm_w, conv_mlstm_b, w_q, w_k, w_v, w_igate, b_igate, w_fgate, b_fgate, mlstm_norm_w, w_out, norm_ffn_w, w_group, b_group, w_router, b_router, w_gate, w_up, w_down, final_norm_w):
    return forward(Cfg(), x_prompt, x_sample, state_ssd_conv, state_ssd, state_mlstm_conv, state_mlstm_c,
                   state_mlstm_n, state_mlstm_m, meta_tokens, norm_mix_w, w_in, conv_ssd_w, conv_ssd_b, dt_bias,
                   a_log, d_skip, ssd_norm_w, conv_mlstm_w, conv_mlstm_b, w_q, w_k, w_v, w_igate, b_igate,
                   w_fgate, b_fgate, mlstm_norm_w, w_out, norm_ffn_w, w_group, b_group, w_router, b_router,
                   w_gate, w_up, w_down, final_norm_w)
```

```python
import functools
from typing import NamedTuple

import jax
import jax.numpy as jnp
from jax import lax
from jax.experimental import pallas as pl
from jax.experimental.pallas import tpu as pltpu

F32 = jnp.float32
BF16 = jnp.bfloat16
EPS = 1e-6
LANES = 128
SUBLANES = 8
VMEM_LIMIT = 52 * 1024 * 1024
HI = lax.Precision.HIGHEST


class Cfg(NamedTuple):
    d_model: int = 2048
    ssd_heads: int = 32
    ssd_head_dim: int = 64
    ssd_groups: int = 4
    ssd_state: int = 128
    ml_heads: int = 8
    ml_head_dim: int = 256
    n_groups: int = 4
    experts_per_group: int = 8
    d_expert: int = 512
    n_meta: int = 16
    conv_w: int = 4
    chunk: int = 128

    @property
    def bc_dim(self):
        return self.ssd_groups * self.ssd_state

    @property
    def xbc_dim(self):
        return self.d_model + 2 * self.bc_dim

    @property
    def n_experts(self):
        return self.n_groups * self.experts_per_group


def _cparams(sem):
    return pltpu.CompilerParams(dimension_semantics=sem, vmem_limit_bytes=VMEM_LIMIT)


def _softplus(x):
    return jnp.maximum(x, 0.0) + jnp.log1p(jnp.exp(-jnp.abs(x)))


def _sigmoid(x):
    return 1.0 / (1.0 + jnp.exp(-x))


def _silu(x):
    return x * _sigmoid(x)


def _rms(x, w):
    return x * lax.rsqrt(jnp.mean(x * x, axis=-1, keepdims=True) + EPS) * w


def _prenorm_kernel(x_ref, nw_ref, wdt_ref, wdtT_ref, bdt_row_ref, bdt_col_ref, h_ref, d_ref, dT_ref):
    hb = _rms(x_ref[...], nw_ref[...]).astype(BF16)
    h_ref[...] = hb
    dt = jnp.dot(hb, wdt_ref[...], preferred_element_type=F32)
    d_ref[...] = _softplus(dt + bdt_row_ref[...])
    dtT = lax.dot_general(wdtT_ref[...], hb, (((1,), (1,)), ((), ())), preferred_element_type=F32)
    dT_ref[...] = _softplus(dtT + bdt_col_ref[...])


def pre_norm(x, norm_w, w_dt, w_dtT, bdt_row, bdt_col, bm):
    m, d = x.shape
    const2 = lambda i: (0, 0)
    return pl.pallas_call(
        _prenorm_kernel,
        out_shape=(jax.ShapeDtypeStruct((m, d), BF16), jax.ShapeDtypeStruct((m, LANES), F32),
                   jax.ShapeDtypeStruct((LANES, m), F32)),
        grid=(m // bm,),
        in_specs=[pl.BlockSpec((bm, d), lambda i: (i, 0)), pl.BlockSpec((1, d), const2),
                  pl.BlockSpec((d, LANES), const2), pl.BlockSpec((LANES, d), const2),
                  pl.BlockSpec((1, LANES), const2), pl.BlockSpec((LANES, 1), const2)],
        out_specs=(pl.BlockSpec((bm, d), lambda i: (i, 0)), pl.BlockSpec((bm, LANES), lambda i: (i, 0)),
                   pl.BlockSpec((LANES, bm), lambda i: (0, i))),
        compiler_params=_cparams(("arbitrary",)),
        name="pre_norm",
    )(x, norm_w, w_dt, w_dtT, bdt_row, bdt_col)


def _inproj_kernel(h_ref, wt_ref, proj_ref, w_ref):
    @pl.when(pl.program_id(1) == 0)
    def _():
        w_ref[...] = wt_ref[...].astype(BF16)

    proj_ref[...] = lax.dot_general(h_ref[...], w_ref[...], (((1,), (1,)), ((), ())),
                                    preferred_element_type=F32)


def in_proj(cfg, h, w_in_t, bm, bn):
    m, d = h.shape
    nz = d // bn
    nx = cfg.xbc_dim // bn
    n_a = nz + nx
    n_blocks = n_a + 2 * nz
    skip = cfg.ssd_heads
    assert skip % SUBLANES == 0
    w_row = lambda j: pl.multiple_of(jnp.where(j < n_a, j * bn, j * bn + skip), SUBLANES)
    out_col = lambda j: jnp.where(j < nz, j, jnp.where(j < n_a, j + 2 * nz, j - nx))
    return pl.pallas_call(
        _inproj_kernel,
        out_shape=jax.ShapeDtypeStruct((m, n_blocks * bn), F32),
        grid=(n_blocks, m // bm),
        in_specs=[pl.BlockSpec((bm, d), lambda j, i: (i, 0)),
                  pl.BlockSpec((pl.Element(bn), pl.Element(d)), lambda j, i: (w_row(j), 0))],
        out_specs=pl.BlockSpec((bm, bn), lambda j, i: (i, out_col(j))),
        scratch_shapes=[pltpu.VMEM((bn, d), BF16)],
        compiler_params=_cparams(("arbitrary", "arbitrary")),
        name="in_proj",
    )(h, w_in_t)


CONV_LANES = 256


def _causal_conv(u, prev, w_ref, b_ref, cols, conv_w):
    lt = u.shape[0]
    row8 = lax.broadcasted_iota(jnp.int32, (SUBLANES, u.shape[1]), 0)
    acc = u * w_ref[conv_w - 1:conv_w, cols] + b_ref[:, cols]
    for s in range(1, conv_w):
        rolled = pltpu.roll(u, s, axis=0)
        head = jnp.where(row8 < s, pltpu.roll(prev, s, axis=0), rolled[0:SUBLANES])
        shifted = head if lt == SUBLANES else jnp.concatenate([head, rolled[SUBLANES:]], axis=0)
        acc = acc + shifted * w_ref[conv_w - 1 - s:conv_w - s, cols]
    return acc


def _conv_qkv_kernel(cfg, xbc_ref, xm_ref, hxbc_ref, hxm_ref, cwx_ref, cbx_ref, cwm_ref, cbm_ref,
                     wq_ref, wk_ref, wv_ref, wg_ref, wgT_ref, bg_row_ref, bg_col_ref,
                     xa_ref, q_ref, k_ref, v_ref, g_ref, gT_ref, px_ref, pm_ref):
    lt = xbc_ref.shape[0]
    hd = cfg.ml_head_dim
    nh = cfg.ml_heads

    @pl.when(pl.program_id(1) == 0)
    def _():
        px_ref[...] = hxbc_ref[0]
        pm_ref[...] = hxm_ref[0]

    for c0 in range(0, xbc_ref.shape[1], CONV_LANES):
        cols = slice(c0, min(c0 + CONV_LANES, xbc_ref.shape[1]))
        u = xbc_ref[:, cols]
        xa_ref[:, cols] = _silu(_causal_conv(u, px_ref[:, cols], cwx_ref, cbx_ref, cols, cfg.conv_w))
        px_ref[:, cols] = u[lt - SUBLANES:lt]

    d = nh * hd
    kscale = hd ** -0.5
    nt = (((1,), (1,)), ((), ()))
    gcol = jnp.zeros(g_ref.shape, F32) + bg_row_ref[...]
    grow = jnp.zeros(gT_ref.shape[2:], F32) + bg_col_ref[...]
    for h in range(nh):
        sl = slice(h * hd, (h + 1) * hd)
        xm = xm_ref[:, sl]
        xc = _silu(_causal_conv(xm, pm_ref[:, sl], cwm_ref, cbm_ref, sl, cfg.conv_w)).astype(BF16)
        pm_ref[:, sl] = xm[lt - SUBLANES:lt]
        qh = jnp.dot(xc, wq_ref[h], preferred_element_type=F32)
        kh = jnp.dot(xc, wk_ref[h], preferred_element_type=F32) * kscale
        vh = jnp.dot(xm.astype(BF16), wv_ref[h], preferred_element_type=F32)
        q_ref[:, sl] = qh
        k_ref[:, sl] = kh
        v_ref[:, sl] = vh
        for part, val in enumerate((qh, kh, vh)):
            vb = val.astype(BF16)
            rows = slice(part * d + h * hd, part * d + (h + 1) * hd)
            gcol = gcol + jnp.dot(vb, wg_ref[rows, :], preferred_element_type=F32)
            grow = grow + lax.dot_general(wgT_ref[:, rows], vb, nt, preferred_element_type=F32)
    lane = lax.broadcasted_iota(jnp.int32, gcol.shape, 1)
    g_ref[...] = jnp.where(lane < nh, gcol, -_softplus(-gcol))
    row = lax.broadcasted_iota(jnp.int32, grow.shape, 0)
    gT_ref[0, 0] = jnp.where(row < nh, grow, -_softplus(-grow))


def conv_qkv(cfg, proj, hist_xbc, hist_xm, cwx, cbx, cwm, cbm, wq, wk, wv, wg, wgT, bg_row, bg_col,
             n_seq, seq_len, lt):
    d = cfg.d_model
    xbc = cfg.xbc_dim
    m = n_seq * seq_len
    nt = seq_len // lt
    ng = 2 * cfg.ml_heads
    xbc_blk = (3 * d) // xbc
    row = lambda s, l: (s * nt + l, 0)
    const2 = lambda s, l: (0, 0)
    const3 = lambda s, l: (0, 0, 0)
    return pl.pallas_call(
        functools.partial(_conv_qkv_kernel, cfg),
        out_shape=(jax.ShapeDtypeStruct((m, xbc), F32),
                   jax.ShapeDtypeStruct((m, d), F32), jax.ShapeDtypeStruct((m, d), F32),
                   jax.ShapeDtypeStruct((m, d), F32),
                   jax.ShapeDtypeStruct((m, LANES), F32),
                   jax.ShapeDtypeStruct((n_seq, nt, ng, lt), F32)),
        grid=(n_seq, nt),
        in_specs=[pl.BlockSpec((lt, xbc), lambda s, l: (s * nt + l, xbc_blk)),
                  pl.BlockSpec((lt, d), lambda s, l: (s * nt + l, 1)),
                  pl.BlockSpec((1, SUBLANES, xbc), lambda s, l: (s, 0, 0)),
                  pl.BlockSpec((1, SUBLANES, d), lambda s, l: (s, 0, 0)),
                  pl.BlockSpec((cfg.conv_w, xbc), const2), pl.BlockSpec((1, xbc), const2),
                  pl.BlockSpec((cfg.conv_w, d), const2), pl.BlockSpec((1, d), const2),
                  pl.BlockSpec(wq.shape, const3), pl.BlockSpec(wk.shape, const3),
                  pl.BlockSpec(wv.shape, const3),
                  pl.BlockSpec(wg.shape, const2), pl.BlockSpec(wgT.shape, const2),
                  pl.BlockSpec((1, LANES), const2), pl.BlockSpec((ng, 1), const2)],
        out_specs=(pl.BlockSpec((lt, xbc), row), pl.BlockSpec((lt, d), row), pl.BlockSpec((lt, d), row),
                   pl.BlockSpec((lt, d), row), pl.BlockSpec((lt, LANES), row),
                   pl.BlockSpec((1, 1, ng, lt), lambda s, l: (s, l, 0, 0))),
        scratch_shapes=[pltpu.VMEM((SUBLANES, xbc), F32), pltpu.VMEM((SUBLANES, d), F32)],
        compiler_params=_cparams(("arbitrary", "arbitrary")),
        name="conv_qkv",
    )(proj, proj, hist_xbc, hist_xm, cwx, cbx, cwm, cbm, wq, wk, wv, wg, wgT, bg_row, bg_col)


def _tri(q, lower):
    r = lax.broadcasted_iota(jnp.int32, (q, q), 0)
    c = lax.broadcasted_iota(jnp.int32, (q, q), 1)
    return (c <= r) if lower else (r <= c)


def _ssd_chunk(cfg, xa_ref, d_ref, dT_ref, arow_ref, acol_ref, dskip_ref, y_ref, st_ref):
    q = xa_ref.shape[0]
    dm = cfg.d_model
    ns = cfg.ssd_state
    hp = cfg.ssd_head_dim
    hpg = cfg.ssd_heads // cfg.ssd_groups
    heads_per_tile = LANES // hp
    n_tiles = cfg.ssd_heads // heads_per_tile

    causal = _tri(q, True)
    tril = causal.astype(F32)
    triu = _tri(q, False).astype(F32)
    dcol = d_ref[...]
    drow = dT_ref[...]
    acum = jnp.dot(tril, dcol * arow_ref[...], precision=HI, preferred_element_type=F32)
    acumT = jnp.dot(drow * acol_ref[...], triu, precision=HI, preferred_element_type=F32)
    nt_dims = (((1,), (1,)), ((), ()))
    tn_dims = (((0,), (0,)), ((), ()))
    lane = lax.broadcasted_iota(jnp.int32, (q, LANES), 1)
    srow = lax.broadcasted_iota(jnp.int32, (LANES, ns), 0)

    cbs = []
    bgs = []
    cgs = []
    for g in range(cfg.ssd_groups):
        bg = xa_ref[:, dm + g * ns: dm + (g + 1) * ns].astype(BF16)
        cg = xa_ref[:, dm + cfg.bc_dim + g * ns: dm + cfg.bc_dim + (g + 1) * ns].astype(BF16)
        cbs.append(lax.dot_general(cg, bg, nt_dims, preferred_element_type=F32))
        bgs.append(bg)
        cgs.append(cg)

    for t in range(n_tiles):
        h0 = t * heads_per_tile
        g = h0 // hpg
        cols = slice(t * LANES, (t + 1) * LANES)
        x = xa_ref[:, cols]
        dsel = jnp.zeros((q, LANES), F32)
        esel = jnp.zeros((q, LANES), F32)
        tsel = jnp.zeros((q, LANES), F32)
        rdec = jnp.zeros((LANES, ns), F32)
        for i in range(heads_per_tile):
            h = h0 + i
            in_head = (lane >= i * hp) & (lane < (i + 1) * hp)
            a_col = acum[:, h:h + 1]
            a_last = acum[q - 1:q, h:h + 1]
            dsel = jnp.where(in_head, dcol[:, h:h + 1], dsel)
            esel = jnp.where(in_head, jnp.exp(a_col), esel)
            tsel = jnp.where(in_head, jnp.exp(a_last - a_col), tsel)
            rdec = jnp.where((srow >= i * hp) & (srow < (i + 1) * hp), jnp.exp(a_last), rdec)
        xd = x * dsel
        y = x * dskip_ref[:, cols]
        for i in range(heads_per_tile):
            h = h0 + i
            in_head = (lane >= i * hp) & (lane < (i + 1) * hp)
            seg = jnp.where(causal, acum[:, h:h + 1] - acumT[h:h + 1, :], -jnp.inf)
            w = (cbs[g] * jnp.exp(seg)).astype(BF16)
            xdh = jnp.where(in_head, xd, 0.0).astype(BF16)
            y = y + jnp.dot(w, xdh, preferred_element_type=F32)
        s_old = st_ref[cols, :]
        ys = lax.dot_general(cgs[g], s_old.astype(BF16), nt_dims, preferred_element_type=F32)
        y_ref[:, cols] = y + esel * ys
        upd = lax.dot_general((xd * tsel).astype(BF16), bgs[g], tn_dims, preferred_element_type=F32)
        st_ref[cols, :] = rdec * s_old + upd


def _ssd_kernel(cfg, n_chunks, xa_ref, d_ref, dT_ref, s0_ref, arow_ref, acol_ref, dskip_ref,
                y_ref, sout_ref, st_ref):
    @pl.when(pl.program_id(1) == 0)
    def _():
        st_ref[...] = s0_ref[...]

    for b in range(xa_ref.shape[0]):
        _ssd_chunk(cfg, xa_ref.at[b], d_ref.at[b], dT_ref.at[b, 0], arow_ref, acol_ref, dskip_ref,
                   y_ref.at[b], st_ref.at[b])

    @pl.when(pl.program_id(1) == n_chunks - 1)
    def _():
        sout_ref[...] = st_ref[...]


def ssd_scan(cfg, xa, d, dT, s0, a_row, a_col, dskip, n_seq, seq_len, q, nb):
    dm = cfg.d_model
    nc = seq_len // q
    blk3 = lambda s, c: (s, c, 0)
    st3 = lambda s, c: (s, 0, 0)
    const2 = lambda s, c: (0, 0)
    return pl.pallas_call(
        functools.partial(_ssd_kernel, cfg, nc),
        out_shape=(jax.ShapeDtypeStruct((n_seq, seq_len, dm), F32),
                   jax.ShapeDtypeStruct((n_seq, dm, cfg.ssd_state), F32)),
        grid=(n_seq // nb, nc),
        in_specs=[pl.BlockSpec((nb, q, cfg.xbc_dim), blk3),
                  pl.BlockSpec((nb, q, LANES), blk3),
                  pl.BlockSpec((nb, 1, cfg.ssd_heads, q), lambda s, c: (s, c, 0, 0)),
                  pl.BlockSpec((nb, dm, cfg.ssd_state), st3),
                  pl.BlockSpec((1, LANES), const2),
                  pl.BlockSpec((cfg.ssd_heads, 1), const2),
                  pl.BlockSpec((1, dm), const2)],
        out_specs=(pl.BlockSpec((nb, q, dm), blk3),
                   pl.BlockSpec((nb, dm, cfg.ssd_state), st3)),
        scratch_shapes=[pltpu.VMEM((nb, dm, cfg.ssd_state), F32)],
        compiler_params=_cparams(("arbitrary", "arbitrary")),
        name="ssd_scan",
    )(xa, d, dT, s0, a_row, a_col, dskip)


def _mlstm_chunk(cfg, q_ref, k_ref, v_ref, g_ref, gT_ref, h_ref, c_ref, n_ref, m_ref):
    ql = q_ref.shape[0]
    hd = cfg.ml_head_dim
    nh = cfg.ml_heads

    causal = _tri(ql, True)
    gcol = g_ref[...]
    grow = gT_ref[...]
    bcum = jnp.dot(causal.astype(F32), gcol, precision=HI, preferred_element_type=F32)
    bcumT = jnp.dot(grow, _tri(ql, False).astype(F32), precision=HI, preferred_element_type=F32)
    nt_dims = (((1,), (1,)), ((), ()))
    tn_dims = (((0,), (0,)), ((), ()))

    for h in range(nh):
        sl = slice(h * hd, (h + 1) * hd)
        b_col = bcum[:, nh + h:nh + h + 1]
        b_row = bcumT[nh + h:nh + h + 1, :]
        i_col = gcol[:, h:h + 1]
        i_row = grow[h:h + 1, :]
        m_prev = m_ref[h:h + 1, 0:1]
        dlog = jnp.where(causal, b_col - b_row + i_row, -jnp.inf)
        inter = b_col + m_prev
        mt = jnp.maximum(inter, jnp.max(dlog, axis=1, keepdims=True))
        qh = q_ref[:, sl]
        kh = k_ref[:, sl]
        vh = v_ref[:, sl]
        qb = qh.astype(BF16)
        kb = kh.astype(BF16)
        s = lax.dot_general(qb, kb, nt_dims, preferred_element_type=F32) * jnp.exp(dlog - mt)
        gdec = jnp.exp(inter - mt)
        c_old = c_ref[sl, :]
        n_old = n_ref[h:h + 1, :]
        qc = lax.dot_general(qb, c_old.astype(BF16), nt_dims, preferred_element_type=F32)
        num = jnp.dot(s.astype(BF16), vh.astype(BF16), preferred_element_type=F32) + gdec * qc
        den = jnp.sum(s, axis=1, keepdims=True) + gdec * jnp.sum(qh * n_old, axis=1, keepdims=True)
        h_ref[:, sl] = num / jnp.maximum(jnp.abs(den), jnp.exp(-mt))
        m_new = mt[ql - 1:ql, :]
        gs = jnp.exp(b_col[ql - 1:ql, :] - b_col + i_col - m_new)
        gc = jnp.exp(inter[ql - 1:ql, :] - m_new)
        upd = lax.dot_general((vh * gs).astype(BF16), kb, tn_dims, preferred_element_type=F32)
        c_ref[sl, :] = gc * c_old + upd
        n_ref[h:h + 1, :] = gc * n_old + jnp.sum(gs * kh, axis=0, keepdims=True)
        m_ref[h:h + 1, :] = jnp.broadcast_to(m_new, (1, LANES))


def _mlstm_kernel(cfg, n_chunks, q_ref, k_ref, v_ref, g_ref, gT_ref, c0_ref, n0_ref, m0_ref,
                  h_ref, cout_ref, nout_ref, mout_ref, c_ref, n_ref, m_ref):
    @pl.when(pl.program_id(1) == 0)
    def _():
        c_ref[...] = c0_ref[...]
        n_ref[...] = n0_ref[...]
        m_ref[...] = m0_ref[...]

    for b in range(q_ref.shape[0]):
        _mlstm_chunk(cfg, q_ref.at[b], k_ref.at[b], v_ref.at[b], g_ref.at[b], gT_ref.at[b, 0],
                     h_ref.at[b], c_ref.at[b], n_ref.at[b], m_ref.at[b])

    @pl.when(pl.program_id(1) == n_chunks - 1)
    def _():
        cout_ref[...] = c_ref[...]
        nout_ref[...] = n_ref[...]
        mout_ref[...] = m_ref[...]


def mlstm_scan(cfg, qa, ka, va, g, gT, c0, n0, m0, n_seq, seq_len, q, nb):
    d = cfg.d_model
    hd = cfg.ml_head_dim
    nh = cfg.ml_heads
    nc = seq_len // q
    blk3 = lambda s, c: (s, c, 0)
    st3 = lambda s, c: (s, 0, 0)
    return pl.pallas_call(
        functools.partial(_mlstm_kernel, cfg, nc),
        out_shape=(jax.ShapeDtypeStruct((n_seq, seq_len, d), F32),
                   jax.ShapeDtypeStruct((n_seq, d, hd), F32),
                   jax.ShapeDtypeStruct((n_seq, nh, hd), F32),
                   jax.ShapeDtypeStruct((n_seq, nh, LANES), F32)),
        grid=(n_seq // nb, nc),
        in_specs=[pl.BlockSpec((nb, q, d), blk3), pl.BlockSpec((nb, q, d), blk3), pl.BlockSpec((nb, q, d), blk3),
                  pl.BlockSpec((nb, q, LANES), blk3),
                  pl.BlockSpec((nb, 1, 2 * nh, q), lambda s, c: (s, c, 0, 0)),
                  pl.BlockSpec((nb, d, hd), st3), pl.BlockSpec((nb, nh, hd), st3),
                  pl.BlockSpec((nb, nh, LANES), st3)],
        out_specs=(pl.BlockSpec((nb, q, d), blk3),
                   pl.BlockSpec((nb, d, hd), st3), pl.BlockSpec((nb, nh, hd), st3),
                   pl.BlockSpec((nb, nh, LANES), st3)),
        scratch_shapes=[pltpu.VMEM((nb, d, hd), F32), pltpu.VMEM((nb, nh, hd), F32),
                        pltpu.VMEM((nb, nh, LANES), F32)],
        compiler_params=_cparams(("arbitrary", "arbitrary")),
        name="mlstm_scan",
    )(qa, ka, va, g, gT, c0, n0, m0)


def _group_norm(x, w_ref, col0, groups, width):
    parts = []
    for g in range(groups):
        seg = x[:, g * width:(g + 1) * width]
        parts.append(seg * lax.rsqrt(jnp.mean(seg * seg, axis=-1, keepdims=True) + EPS)
                     * w_ref[:, col0 + g * width: col0 + (g + 1) * width])
    return parts


def _outproj_kernel(cfg, ys_ref, z_ref, hm_ref, o_ref, x_ref, nws_ref, nwm_ref, w_ref, out_ref, mix_ref):
    d = cfg.d_model

    @pl.when(pl.program_id(1) == 0)
    def _():
        ws = d // cfg.ssd_groups
        yz = ys_ref[...] * _silu(z_ref[...])
        for g, part in enumerate(_group_norm(yz, nws_ref, 0, cfg.ssd_groups, ws)):
            mix_ref[:, g * ws:(g + 1) * ws] = part.astype(BF16)
        gate = _sigmoid(o_ref[...])
        wm = cfg.ml_head_dim
        for g, part in enumerate(_group_norm(hm_ref[...], nwm_ref, 0, cfg.ml_heads, wm)):
            mix_ref[:, d + g * wm: d + (g + 1) * wm] = (part * gate[:, g * wm:(g + 1) * wm]).astype(BF16)

    out_ref[...] = x_ref[...] + jnp.dot(mix_ref[...], w_ref[...], preferred_element_type=F32)


def out_proj(cfg, ys, proj, hm, x, nws, nwm, w_out, bm, bn):
    m, d = x.shape
    const2 = lambda i, j: (0, 0)
    full = lambda i, j: (i, 0)
    w_mode = dict(pipeline_mode=pl.Buffered(1)) if bn == d else {}
    return pl.pallas_call(
        functools.partial(_outproj_kernel, cfg),
        out_shape=jax.ShapeDtypeStruct((m, d), F32),
        grid=(m // bm, d // bn),
        in_specs=[pl.BlockSpec((bm, d), full),
                  pl.BlockSpec((bm, d), lambda i, j: (i, 0)),
                  pl.BlockSpec((bm, d), full),
                  pl.BlockSpec((bm, d), lambda i, j: (i, 2)),
                  pl.BlockSpec((bm, bn), lambda i, j: (i, j)),
                  pl.BlockSpec((1, d), const2), pl.BlockSpec((1, d), const2),
                  pl.BlockSpec((2 * d, bn), lambda i, j: (0, j), **w_mode)],
        out_specs=pl.BlockSpec((bm, bn), lambda i, j: (i, j)),
        scratch_shapes=[pltpu.VMEM((bm, 2 * d), BF16)],
        compiler_params=_cparams(("arbitrary", "arbitrary")),
        name="out_proj",
    )(ys, proj, hm, proj, x, nws, nwm, w_out)


def _router_kernel(cfg, n_a, xa_ref, xb_ref, nw_ref, wr_ref, br_ref, ei_ref, wt_ref, cnt_out_ref, rows_ref,
                   cnt_ref, x_ref):
    ne = cfg.n_experts
    epg = cfg.experts_per_group
    ngr = cfg.n_groups
    bm = x_ref.shape[0]
    i = pl.program_id(0)

    @pl.when(i == 0)
    def _():
        cnt_ref[...] = jnp.zeros_like(cnt_ref)

    @pl.when(i < n_a)
    def _():
        x_ref[...] = xa_ref[...]

    @pl.when(i >= n_a)
    def _():
        x_ref[...] = xb_ref[...]

    rows_ref[...] = _lanes_to_rows(x_ref[...])
    hb = _rms(x_ref[...], nw_ref[...]).astype(BF16)
    logits = jnp.dot(hb, wr_ref[...], preferred_element_type=F32) + br_ref[...]
    lane = lax.broadcasted_iota(jnp.int32, logits.shape, 1)
    big = jnp.int32(2 ** 30)
    neg = -jnp.inf

    def first_argmax(vals):
        mx = jnp.max(vals, axis=-1, keepdims=True)
        idx = jnp.min(jnp.where(vals == mx, lane, big), axis=-1, keepdims=True)
        return mx, idx

    is_group = (lane >= ne) & (lane < ne + ngr)
    gl = jnp.where(is_group, logits, neg)
    gmax, gidx = first_argmax(gl)
    p_g = 1.0 / jnp.sum(jnp.exp(gl - gmax), axis=-1, keepdims=True)
    e_lo = (gidx - ne) * epg
    in_sel = (lane >= e_lo) & (lane < e_lo + epg)
    el = jnp.where(in_sel, logits, neg)
    pe = jnp.exp(el - jnp.max(el, axis=-1, keepdims=True))
    pe = jnp.where(in_sel, pe / jnp.sum(pe, axis=-1, keepdims=True), -1.0)
    p1, i1 = first_argmax(pe)
    p2, i2 = first_argmax(jnp.where(lane == i1, -1.0, pe))
    wsum = p1 + p2
    wt_ref[...] = jnp.where(lane == 0, p_g * p1 / wsum, jnp.where(lane == 1, p_g * p2 / wsum, 0.0))

    oh1 = jnp.where(lane == i1, 1.0, 0.0)
    oh2 = jnp.where(lane == i2, 1.0, 0.0)
    r = lax.broadcasted_iota(jnp.int32, (bm, bm), 0)
    c = lax.broadcasted_iota(jnp.int32, (bm, bm), 1)
    before = jnp.where(c < r, 1.0, 0.0).astype(BF16)
    ahead1 = jnp.dot(before, oh1.astype(BF16), preferred_element_type=F32)
    ahead2 = jnp.dot(before, oh2.astype(BF16), preferred_element_type=F32)
    cnt = cnt_ref[...]
    tot1 = jnp.sum(oh1, axis=0, keepdims=True)
    rank1 = jnp.sum(oh1 * (cnt + ahead1), axis=-1, keepdims=True)
    rank2 = jnp.sum(oh2 * (cnt + tot1 + ahead2), axis=-1, keepdims=True)
    cnt_new = cnt + tot1 + jnp.sum(oh2, axis=0, keepdims=True)
    cnt_ref[...] = cnt_new
    cnt_out_ref[...] = cnt_new
    ei_ref[...] = jnp.where(lane == 0, i1, jnp.where(lane == 1, i2, jnp.where(
        lane == 2, rank1.astype(jnp.int32), jnp.where(lane == 3, rank2.astype(jnp.int32), 0))))


def router(cfg, xa, xb, nw, wr, br, bm):
    d = xa.shape[1]
    n_a = xa.shape[0] // bm
    n_b = xb.shape[0] // bm
    m = xa.shape[0] + xb.shape[0]
    nch = d // LANES
    const2 = lambda i: (0, 0)
    return pl.pallas_call(
        functools.partial(_router_kernel, cfg, n_a),
        out_shape=(jax.ShapeDtypeStruct((m, LANES), jnp.int32), jax.ShapeDtypeStruct((m, LANES), F32),
                   jax.ShapeDtypeStruct((1, LANES), F32), jax.ShapeDtypeStruct((m, nch, LANES), F32)),
        grid=(n_a + n_b,),
        in_specs=[pl.BlockSpec((bm, d), lambda i: (jnp.minimum(i, n_a - 1), 0)),
                  pl.BlockSpec((bm, d), lambda i: (jnp.maximum(i - n_a, 0), 0)),
                  pl.BlockSpec((1, d), const2),
                  pl.BlockSpec((d, LANES), const2), pl.BlockSpec((1, LANES), const2)],
        out_specs=(pl.BlockSpec((bm, LANES), lambda i: (i, 0)), pl.BlockSpec((bm, LANES), lambda i: (i, 0)),
                   pl.BlockSpec((1, LANES), const2), pl.BlockSpec((bm, nch, LANES), lambda i: (i, 0, 0))),
        scratch_shapes=[pltpu.VMEM((1, LANES), F32), pltpu.VMEM((bm, d), F32)],
        compiler_params=_cparams(("arbitrary",)),
        name="router",
    )(xa, xb, nw, wr, br)


def _rows_to_lanes(g):
    t = pltpu.einshape("rcl->crl", g)
    return jnp.concatenate([t[c] for c in range(t.shape[0])], axis=-1)


def _lanes_to_rows(x):
    parts = jnp.stack([x[:, c * LANES:(c + 1) * LANES] for c in range(x.shape[1] // LANES)], axis=0)
    return pltpu.einshape("crl->rcl", parts)


GATHER_GROUP = 8


def _gather_rows(idx_ref, src_hbm, dst, sem, n_groups):
    def body(g, carry):
        for u in range(GATHER_GROUP):
            r = g * GATHER_GROUP + u
            pltpu.make_async_copy(src_hbm.at[idx_ref[0, 0, r]], dst.at[r], sem).start()
        return carry
    lax.fori_loop(0, n_groups, body, 0)


def _wait_rows(src_hbm, dst, sem, n):
    pltpu.make_async_copy(src_hbm.at[pl.ds(0, n)], dst.at[pl.ds(0, n)], sem).wait()


def _moe_kernel(cfg, te_ref, nr_ref, first_ref, wslot_ref, nexte_ref, nv_ref, src_ref, srcn_ref, x_hbm, nw_ref,
                wg_hbm, wu_hbm, wd_hbm, ys_ref, xbuf, sem, wgf, wuf, wdf, wsem, wgb, wub, wdb):
    j = pl.program_id(0)
    n_valid = nv_ref[0]

    def weight_copies(e, slot):
        return (pltpu.make_async_copy(wg_hbm.at[e], wgf.at[slot], wsem.at[slot, 0]),
                pltpu.make_async_copy(wu_hbm.at[e], wuf.at[slot], wsem.at[slot, 1]),
                pltpu.make_async_copy(wd_hbm.at[e], wdf.at[slot], wsem.at[slot, 2]))

    @pl.when(j == 0)
    def _():
        for cp in weight_copies(te_ref[0], 0):
            cp.start()
        xbuf[...] = jnp.zeros_like(xbuf)
        _gather_rows(src_ref, x_hbm, xbuf.at[0], sem.at[0], nr_ref[0] // GATHER_GROUP)

    @pl.when(j + 1 < n_valid)
    def _():
        nslot = (j + 1) % 2
        _gather_rows(srcn_ref, x_hbm, xbuf.at[nslot], sem.at[nslot], nr_ref[j + 1] // GATHER_GROUP)

    @pl.when(j < n_valid)
    def _():
        @pl.when(first_ref[j] == 1)
        def _():
            ws = wslot_ref[j]
            for cp in weight_copies(te_ref[j], ws):
                cp.wait()

            @pl.when(nexte_ref[j] >= 0)
            def _():
                for cp in weight_copies(nexte_ref[j], 1 - ws):
                    cp.start()

            wgb[...] = wgf[ws].astype(BF16)
            wub[...] = wuf[ws].astype(BF16)
            wdb[...] = wdf[ws].astype(BF16)

        slot = j % 2
        _wait_rows(x_hbm, xbuf.at[slot], sem.at[slot], nr_ref[j])
        hb = _rms(_rows_to_lanes(xbuf[slot]), nw_ref[...]).astype(BF16)
        hid = (_silu(jnp.dot(hb, wgb[...], preferred_element_type=F32))
               * jnp.dot(hb, wub[...], preferred_element_type=F32))
        y = jnp.dot(hid.astype(BF16), wdb[...], preferred_element_type=F32)
        ys_ref[...] = _lanes_to_rows(y)

    @pl.when(j >= n_valid)
    def _():
        ys_ref[...] = jnp.zeros_like(ys_ref)


def moe_routed(cfg, tables, src, x_rows, nw, wg, wu, wd, tm):
    n_tiles = src.shape[0]
    d = cfg.d_model
    de = cfg.d_expert
    nch = d // LANES
    return pl.pallas_call(
        functools.partial(_moe_kernel, cfg),
        out_shape=jax.ShapeDtypeStruct((n_tiles * tm, nch, LANES), F32),
        grid_spec=pltpu.PrefetchScalarGridSpec(
            num_scalar_prefetch=len(tables),
            grid=(n_tiles,),
            in_specs=[pl.BlockSpec((1, 1, tm), lambda j, *_: (j, 0, 0), memory_space=pltpu.SMEM),
                      pl.BlockSpec((1, 1, tm), lambda j, *_: (jnp.minimum(j + 1, n_tiles - 1), 0, 0),
                                   memory_space=pltpu.SMEM),
                      pl.BlockSpec(memory_space=pl.ANY),
                      pl.BlockSpec((1, d), lambda j, *_: (0, 0)),
                      pl.BlockSpec(memory_space=pl.ANY), pl.BlockSpec(memory_space=pl.ANY),
                      pl.BlockSpec(memory_space=pl.ANY)],
            out_specs=pl.BlockSpec((tm, nch, LANES), lambda j, *_: (j, 0, 0)),
            scratch_shapes=[pltpu.VMEM((2, tm, nch, LANES), F32), pltpu.SemaphoreType.DMA((2,)),
                            pltpu.VMEM((2, d, de), F32), pltpu.VMEM((2, d, de), F32), pltpu.VMEM((2, de, d), F32),
                            pltpu.SemaphoreType.DMA((2, 3)),
                            pltpu.VMEM((d, de), BF16), pltpu.VMEM((d, de), BF16), pltpu.VMEM((de, d), BF16)]),
        compiler_params=_cparams(("arbitrary",)),
        name="moe",
    )(*tables, src, src, x_rows, nw, wg, wu, wd)


def _combine_kernel(n_steps, d0_ref, d1_ref, d0n_ref, d1n_ref, wt_ref, x1_ref, fw_ref, ys_hbm, y_ref, gbuf, sem):
    i = pl.program_id(0)
    groups = x1_ref.shape[0] // GATHER_GROUP

    def fetch(r0_ref, r1_ref, slot):
        _gather_rows(r0_ref, ys_hbm, gbuf.at[slot, 0], sem.at[slot, 0], groups)
        _gather_rows(r1_ref, ys_hbm, gbuf.at[slot, 1], sem.at[slot, 1], groups)

    @pl.when(i == 0)
    def _():
        fetch(d0_ref, d1_ref, 0)

    @pl.when(i + 1 < n_steps)
    def _():
        fetch(d0n_ref, d1n_ref, (i + 1) % 2)

    slot = i % 2
    bm = x1_ref.shape[0]
    _wait_rows(ys_hbm, gbuf.at[slot, 0], sem.at[slot, 0], bm)
    _wait_rows(ys_hbm, gbuf.at[slot, 1], sem.at[slot, 1], bm)
    wt = wt_ref[...]
    acc = (x1_ref[...] + wt[:, 0:1] * _rows_to_lanes(gbuf[slot, 0])
           + wt[:, 1:2] * _rows_to_lanes(gbuf[slot, 1]))
    y_ref[...] = _rms(acc, fw_ref[...])


def moe_combine(cfg, dest0, dest1, wt, wt_blk0, x1, fw, ys, bm):
    m, d = x1.shape
    nch = d // LANES
    n = m // bm
    cur = lambda: pl.BlockSpec((1, 1, bm), lambda i: (i, 0, 0), memory_space=pltpu.SMEM)
    nxt = lambda: pl.BlockSpec((1, 1, bm), lambda i: (jnp.minimum(i + 1, n - 1), 0, 0), memory_space=pltpu.SMEM)
    return pl.pallas_call(
        functools.partial(_combine_kernel, n),
        out_shape=jax.ShapeDtypeStruct((m, d), F32),
        grid=(n,),
        in_specs=[cur(), cur(), nxt(), nxt(),
                  pl.BlockSpec((bm, LANES), lambda i: (i + wt_blk0, 0)),
                  pl.BlockSpec((bm, d), lambda i: (i, 0)), pl.BlockSpec((1, d), lambda i: (0, 0)),
                  pl.BlockSpec(memory_space=pl.ANY)],
        out_specs=pl.BlockSpec((bm, d), lambda i: (i, 0)),
        scratch_shapes=[pltpu.VMEM((2, 2, bm, nch, LANES), F32), pltpu.SemaphoreType.DMA((2, 2))],
        compiler_params=_cparams(("arbitrary",)),
        name="moe_combine",
    )(dest0, dest1, dest0, dest1, wt, x1, fw, ys)


def _route_tables(cfg, ei, cnt, tm, n_tiles):
    ne = cfg.n_experts
    i32 = jnp.int32
    counts = cnt[0, :ne].astype(i32)
    tiles_e = (counts + tm - 1) // tm
    tile_end = jnp.cumsum(tiles_e)
    tile_start = tile_end - tiles_e
    row_off = tile_start * tm
    tile_id = jnp.arange(n_tiles, dtype=i32)
    tile_expert = jnp.minimum(jnp.sum((tile_id[:, None] >= tile_end[None, :]).astype(i32), axis=1), ne - 1)
    used = jnp.clip(counts[tile_expert] - (tile_id - tile_start[tile_expert]) * tm, 0, tm)
    used = jnp.where(tile_id < tile_end[-1], used, 0)
    tile_rows = (used + GATHER_GROUP - 1) // GATHER_GROUP * GATHER_GROUP
    n_valid = tile_end[-1]
    prev_expert = jnp.concatenate([jnp.full((1,), -1, i32), tile_expert[:-1]])
    first = ((tile_expert != prev_expert) & (tile_id < n_valid)).astype(i32)
    wslot = (jnp.cumsum(first) - 1) % 2
    e_id = jnp.arange(ne, dtype=i32)
    later = (e_id[None, :] > e_id[:, None]) & (tiles_e[None, :] > 0)
    next_e = jnp.min(jnp.where(later, e_id[None, :], ne), axis=1)
    next_e = jnp.where(next_e < ne, next_e, -1)[tile_expert]
    tables = (tile_expert, tile_rows, first, wslot.astype(i32), next_e.astype(i32), n_valid[None].astype(i32))
    picks = ei[:, 0:4].T
    pick_off = jnp.sum(jnp.where(picks[None, 0:2] == e_id[:, None, None], row_off[:, None, None], 0), axis=0)
    dest = pick_off + picks[2:4]
    tok = jnp.broadcast_to(jnp.arange(ei.shape[0], dtype=i32)[None, :], dest.shape)
    src = jnp.zeros((n_tiles * tm,), i32).at[dest.reshape(-1)].set(
        tok.reshape(-1), unique_indices=True, mode="promise_in_bounds")
    return tables, src.reshape(n_tiles, 1, tm), dest


def _pad_hist(hist):
    return jnp.pad(hist, ((0, 0), (SUBLANES - hist.shape[1], 0), (0, 0)))


def _tile(m, pref):
    return pref if m % pref == 0 else m


def _mixer_segment(cfg, x2d, n_seq, seq_len, hist_xbc, hist_xm, s0, c0, n0, m0, p):
    d = cfg.d_model
    m = n_seq * seq_len
    q = min(cfg.chunk, seq_len)
    nc = seq_len // q
    h, dcol, dT = pre_norm(x2d, p["norm_mix_w"], p["w_dt"], p["w_dtT"], p["bdt_row"], p["bdt_col"], _tile(m, 512))
    proj = in_proj(cfg, h, p["w_in_t"], _tile(m, 1024), d // 2)
    xa, qa, ka, va, g, gT = conv_qkv(cfg, proj, hist_xbc, hist_xm, p["cwx"], p["cbx"], p["cwm"], p["cbm"],
                                     p["wq"], p["wk"], p["wv"], p["wg"], p["wgT"], p["bg_row"], p["bg_col"],
                                     n_seq, seq_len, q)
    dTc = dT[:cfg.ssd_heads].reshape(cfg.ssd_heads, n_seq, nc, q).transpose(1, 2, 0, 3)
    nb = 2 if (nc == 1 and n_seq % 2 == 0) else 1
    seq3 = lambda a: a.reshape(n_seq, seq_len, a.shape[-1])
    ys, s_new = ssd_scan(cfg, seq3(xa), seq3(dcol), dTc, s0, p["a_row"], p["a_col"], p["dskip"], n_seq, seq_len, q, nb)
    hm, c_new, n_new, m_new = mlstm_scan(cfg, seq3(qa), seq3(ka), seq3(va), seq3(g), gT, c0, n0, m0,
                                         n_seq, seq_len, q, nb)
    ys = ys.reshape(m, d)
    hm = hm.reshape(m, d)
    tail = proj.reshape(n_seq, seq_len, -1)[:, seq_len - (cfg.conv_w - 1):]
    tail_xbc = tail[:, :, 3 * d:]
    tail_xm = tail[:, :, d:2 * d]
    return proj, ys, hm, (tail_xbc, s_new, tail_xm, c_new, n_new, m_new)


MOE_TILE = 256


def _ffn(cfg, segments, p):
    d = cfg.d_model
    x1s = [out_proj(cfg, ys, proj, hm, x2d, p["ssd_norm_w"], p["mlstm_norm_w"], p["w_out"],
                    _tile(x2d.shape[0], 256), d) for x2d, proj, ys, hm in segments]
    bm = 256 if all(x1.shape[0] % 256 == 0 for x1 in x1s) else 128
    ei, wt, cnt, x_rows = router(cfg, x1s[0], x1s[1], p["norm_ffn_w"], p["wr"], p["br"], bm)
    n_tok = ei.shape[0]
    n_tiles = (2 * n_tok + cfg.n_experts * (MOE_TILE - 1)) // MOE_TILE
    tables, src, dest = _route_tables(cfg, ei, cnt, MOE_TILE, n_tiles)
    ys_sorted = moe_routed(cfg, tables, src, x_rows, p["norm_ffn_w"], p["w_gate"], p["w_up"], p["w_down"], MOE_TILE)
    outs = []
    off = 0
    for x1 in x1s:
        m = x1.shape[0]
        dseg = dest[:, off:off + m].reshape(2, m // bm, 1, bm)
        outs.append(moe_combine(cfg, dseg[0], dseg[1], wt, off // bm, x1, p["final_norm_w"], ys_sorted, bm))
        off += m
    return outs


def _prep_params(cfg, norm_mix_w, w_in, conv_ssd_w, conv_ssd_b, dt_bias, a_log, d_skip, ssd_norm_w,
                 conv_mlstm_w, conv_mlstm_b, w_q, w_k, w_v, w_igate, b_igate, w_fgate, b_fgate, mlstm_norm_w,
                 w_out, norm_ffn_w, w_group, b_group, w_router, b_router, w_gate, w_up, w_down, final_norm_w):
    d = cfg.d_model
    hs = cfg.ssd_heads
    nh = cfg.ml_heads
    o_xbc = d + cfg.xbc_dim
    row = lambda v: v.reshape(1, -1).astype(F32)
    pad_lanes = lambda a: jnp.pad(a, ((0, 0), (0, LANES - a.shape[1])))
    w_in_t = w_in.T.astype(F32)
    w_dt_t = jnp.pad(w_in_t[o_xbc:o_xbc + hs], ((0, LANES - hs), (0, 0)))
    a = -jnp.exp(a_log.astype(F32))
    w_gates = jnp.concatenate([w_igate, w_fgate], axis=1)
    b_gates = jnp.concatenate([b_igate, b_fgate]).astype(F32)
    ne = cfg.n_experts
    wr = pad_lanes(jnp.concatenate([w_router, w_group], axis=1))
    br = pad_lanes(jnp.concatenate([b_router, b_group]).reshape(1, -1).astype(F32))
    return dict(
        norm_mix_w=row(norm_mix_w),
        w_in_t=w_in_t, w_dt=w_dt_t.T.astype(BF16), w_dtT=w_dt_t.astype(BF16),
        bdt_row=pad_lanes(row(dt_bias)), bdt_col=pad_lanes(row(dt_bias)).T,
        cwx=conv_ssd_w.astype(F32), cbx=row(conv_ssd_b), cwm=conv_mlstm_w.astype(F32), cbm=row(conv_mlstm_b),
        wq=w_q.astype(BF16), wk=w_k.astype(BF16), wv=w_v.astype(BF16),
        wg=pad_lanes(w_gates).astype(BF16), wgT=w_gates.T.astype(BF16),
        bg_row=pad_lanes(row(b_gates)), bg_col=b_gates.reshape(-1, 1),
        a_row=pad_lanes(row(a)), a_col=a.reshape(-1, 1),
        dskip=row(jnp.repeat(d_skip.astype(F32), cfg.ssd_head_dim)),
        ssd_norm_w=row(ssd_norm_w), mlstm_norm_w=row(mlstm_norm_w), w_out=w_out.astype(BF16),
        norm_ffn_w=row(norm_ffn_w), wr=wr.astype(BF16), br=br,
        w_gate=w_gate.astype(F32), w_up=w_up.astype(F32), w_down=w_down.astype(F32),
        final_norm_w=row(final_norm_w),
    )


def forward(cfg, x_prompt, x_sample, state_ssd_conv, state_ssd, state_mlstm_conv, state_mlstm_c,
            state_mlstm_n, state_mlstm_m, meta_tokens, *weights):
    d = cfg.d_model
    nh = cfg.ml_heads
    hd = cfg.ml_head_dim
    assert state_ssd.shape[0] == 1, "single-layer kernel"
    p = _prep_params(cfg, *[w[0] for w in weights[:-1]], weights[-1])
    bp, lp, _ = x_prompt.shape
    bs, ls, _ = x_sample.shape
    n_meta = meta_tokens.shape[0]

    zeros = lambda *s: jnp.zeros(s, F32)
    _, _, _, st_meta = _mixer_segment(
        cfg, meta_tokens.astype(F32), 1, n_meta, zeros(1, SUBLANES, cfg.xbc_dim), zeros(1, SUBLANES, d),
        zeros(1, d, cfg.ssd_state), zeros(1, d, hd), zeros(1, nh, hd), zeros(1, nh, LANES), p)
    mt_xbc, mt_s, mt_xm, mt_c, mt_n, mt_m = st_meta
    rep = lambda a: jnp.broadcast_to(a, (bp,) + a.shape[1:])

    xp = x_prompt.reshape(bp * lp, d)
    proj_p, ys_p, hm_p, st_p = _mixer_segment(
        cfg, xp, bp, lp, rep(_pad_hist(mt_xbc)), rep(_pad_hist(mt_xm)), rep(mt_s), rep(mt_c), rep(mt_n),
        rep(mt_m), p)

    xs = x_sample.reshape(bs * ls, d)
    m0 = jnp.broadcast_to(state_mlstm_m[0].astype(F32)[:, :, None], (bs, nh, LANES))
    proj_s, ys_s, hm_s, st_s = _mixer_segment(
        cfg, xs, bs, ls, _pad_hist(state_ssd_conv[0]), _pad_hist(state_mlstm_conv[0]),
        state_ssd[0].reshape(bs, d, cfg.ssd_state), state_mlstm_c[0].reshape(bs, d, hd),
        state_mlstm_n[0], m0, p)
    y_p, y_s = _ffn(cfg, [(xp, proj_p, ys_p, hm_p), (xs, proj_s, ys_s, hm_s)], p)
    y_prompt = y_p.reshape(bp, lp, d)
    y_sample = y_s.reshape(bs, ls, d)

    def pack(st, b):
        t_xbc, s_new, t_xm, c_new, n_new, m_new = st
        return (t_xbc[None], s_new.reshape(1, b, cfg.ssd_heads, cfg.ssd_head_dim, cfg.ssd_state),
                t_xm[None], c_new.reshape(1, b, nh, hd, hd), n_new[None], m_new[None, :, :, 0])

    return (y_prompt, y_sample) + pack(st_p, bp) + pack(st_s, bs)


def kernel(x_prompt, x_sample, state_ssd_conv, state_ssd, state_mlstm_conv, state_mlstm_c, state_mlstm_n, state_mlstm_m, meta_tokens, norm_mix_w, w_in, conv_ssd_w, conv_ssd_b, dt_bias, a_log, d_skip, ssd_norm_w, conv_mlstm_w, conv_mlstm_b, w_q, w_k, w_v, w_igate, b_igate, w_fgate, b_fgate, mlstm_norm_w, w_out, norm_ffn_w, w_group, b_group, w_router, b_router, w_gate, w_up, w_down, final_norm_w):
    return forward(Cfg(), x_prompt, x_sample, state_ssd_conv, state_ssd, state_mlstm_conv, state_mlstm_c,
                   state_mlstm_n, state_mlstm_m, meta_tokens, norm_mix_w, w_in, conv_ssd_w, conv_ssd_b, dt_bias,
                   a_log, d_skip, ssd_norm_w, conv_mlstm_w, conv_mlstm_b, w_q, w_k, w_v, w_igate, b_igate,
                   w_fgate, b_fgate, mlstm_norm_w, w_out, norm_ffn_w, w_group, b_group, w_router, b_router,
                   w_gate, w_up, w_down, final_norm_w)
```

```python
import functools
from typing import NamedTuple

import jax
import jax.numpy as jnp
from jax import lax
from jax.experimental import pallas as pl
from jax.experimental.pallas import tpu as pltpu

F32 = jnp.float32
BF16 = jnp.bfloat16
EPS = 1e-6
LANES = 128
SUBLANES = 8
VMEM_LIMIT = 52 * 1024 * 1024
HI = lax.Precision.HIGHEST


class Cfg(NamedTuple):
    d_model: int = 2048
    ssd_heads: int = 32
    ssd_head_dim: int = 64
    ssd_groups: int = 4
    ssd_state: int = 128
    ml_heads: int = 8
    ml_head_dim: int = 256
    n_groups: int = 4
    experts_per_group: int = 8
    d_expert: int = 512
    n_meta: int = 16
    conv_w: int = 4
    chunk: int = 128

    @property
    def bc_dim(self):
        return self.ssd_groups * self.ssd_state

    @property
    def xbc_dim(self):
        return self.d_model + 2 * self.bc_dim

    @property
    def n_experts(self):
        return self.n_groups * self.experts_per_group


def _cparams(sem):
    return pltpu.CompilerParams(dimension_semantics=sem, vmem_limit_bytes=VMEM_LIMIT)


def _softplus(x):
    return jnp.maximum(x, 0.0) + jnp.log1p(jnp.exp(-jnp.abs(x)))


def _sigmoid(x):
    return 1.0 / (1.0 + jnp.exp(-x))


def _silu(x):
    return x * _sigmoid(x)


def _rms(x, w):
    return x * lax.rsqrt(jnp.mean(x * x, axis=-1, keepdims=True) + EPS) * w


def _prenorm_kernel(x_ref, nw_ref, wdt_ref, wdtT_ref, bdt_row_ref, bdt_col_ref, h_ref, d_ref, dT_ref):
    hb = _rms(x_ref[...], nw_ref[...]).astype(BF16)
    h_ref[...] = hb
    dt = jnp.dot(hb, wdt_ref[...], preferred_element_type=F32)
    d_ref[...] = _softplus(dt + bdt_row_ref[...])
    dtT = lax.dot_general(wdtT_ref[...], hb, (((1,), (1,)), ((), ())), preferred_element_type=F32)
    dT_ref[...] = _softplus(dtT + bdt_col_ref[...])


def _prenorm_pair_kernel(n_a, xa_ref, xb_ref, nw_ref, wdt_ref, wdtT_ref, bdt_row_ref, bdt_col_ref,
                         h_ref, d_ref, dT_ref, x_ref):
    i = pl.program_id(0)

    @pl.when(i < n_a)
    def _():
        x_ref[...] = xa_ref[...]

    @pl.when(i >= n_a)
    def _():
        x_ref[...] = xb_ref[...]

    _prenorm_kernel(x_ref, nw_ref, wdt_ref, wdtT_ref, bdt_row_ref, bdt_col_ref, h_ref, d_ref, dT_ref)


def pre_norm(xa, xb, norm_w, w_dt, w_dtT, bdt_row, bdt_col, bm):
    d = xa.shape[1]
    n_a = xa.shape[0] // bm
    n_b = 0 if xb is None else xb.shape[0] // bm
    m = (n_a + n_b) * bm
    const2 = lambda i: (0, 0)
    w_specs = [pl.BlockSpec((1, d), const2), pl.BlockSpec((d, LANES), const2), pl.BlockSpec((LANES, d), const2),
               pl.BlockSpec((1, LANES), const2), pl.BlockSpec((LANES, 1), const2)]
    if xb is None:
        body, x_specs, xs, scratch = _prenorm_kernel, [pl.BlockSpec((bm, d), lambda i: (i, 0))], (xa,), []
    else:
        body = functools.partial(_prenorm_pair_kernel, n_a)
        x_specs = [pl.BlockSpec((bm, d), lambda i: (jnp.minimum(i, n_a - 1), 0)),
                   pl.BlockSpec((bm, d), lambda i: (jnp.maximum(i - n_a, 0), 0))]
        xs, scratch = (xa, xb), [pltpu.VMEM((bm, d), F32)]
    return pl.pallas_call(
        body,
        out_shape=(jax.ShapeDtypeStruct((m, d), BF16), jax.ShapeDtypeStruct((m, LANES), F32),
                   jax.ShapeDtypeStruct((LANES, m), F32)),
        grid=(n_a + n_b,),
        in_specs=x_specs + w_specs,
        out_specs=(pl.BlockSpec((bm, d), lambda i: (i, 0)), pl.BlockSpec((bm, LANES), lambda i: (i, 0)),
                   pl.BlockSpec((LANES, bm), lambda i: (0, i))),
        scratch_shapes=scratch,
        compiler_params=_cparams(("arbitrary",)),
        name="pre_norm",
    )(*xs, norm_w, w_dt, w_dtT, bdt_row, bdt_col)


def _inproj_kernel(h_ref, hs_ref, wt_ref, proj_ref, projs_ref, w_ref):
    nt_dims = (((1,), (1,)), ((), ()))

    @pl.when(pl.program_id(1) == 0)
    def _():
        w_ref[...] = wt_ref[...].astype(BF16)
        projs_ref[...] = lax.dot_general(hs_ref[...], w_ref[...], nt_dims, preferred_element_type=F32)

    proj_ref[...] = lax.dot_general(h_ref[...], w_ref[...], nt_dims, preferred_element_type=F32)


def in_proj(cfg, h, h_small, w_in_t, bm, bn):
    m, d = h.shape
    ms = h_small.shape[0]
    nz = d // bn
    nx = cfg.xbc_dim // bn
    n_a = nz + nx
    n_blocks = n_a + 2 * nz
    skip = cfg.ssd_heads
    assert skip % SUBLANES == 0
    w_row = lambda j: pl.multiple_of(jnp.where(j < n_a, j * bn, j * bn + skip), SUBLANES)
    out_col = lambda j: jnp.where(j < nz, j, jnp.where(j < n_a, j + 2 * nz, j - nx))
    return pl.pallas_call(
        _inproj_kernel,
        out_shape=(jax.ShapeDtypeStruct((m, n_blocks * bn), F32), jax.ShapeDtypeStruct((ms, n_blocks * bn), F32)),
        grid=(n_blocks, m // bm),
        in_specs=[pl.BlockSpec((bm, d), lambda j, i: (i, 0)),
                  pl.BlockSpec((ms, d), lambda j, i: (0, 0)),
                  pl.BlockSpec((pl.Element(bn), pl.Element(d)), lambda j, i: (w_row(j), 0))],
        out_specs=(pl.BlockSpec((bm, bn), lambda j, i: (i, out_col(j))),
                   pl.BlockSpec((ms, bn), lambda j, i: (0, out_col(j)))),
        scratch_shapes=[pltpu.VMEM((bn, d), BF16)],
        compiler_params=_cparams(("arbitrary", "arbitrary")),
        name="in_proj",
    )(h, h_small, w_in_t)


CONV_LANES = 256


def _causal_conv(u, prev, w_ref, b_ref, cols, conv_w):
    lt = u.shape[0]
    row8 = lax.broadcasted_iota(jnp.int32, (SUBLANES, u.shape[1]), 0)
    acc = u * w_ref[conv_w - 1:conv_w, cols] + b_ref[:, cols]
    for s in range(1, conv_w):
        rolled = pltpu.roll(u, s, axis=0)
        head = jnp.where(row8 < s, pltpu.roll(prev, s, axis=0), rolled[0:SUBLANES])
        shifted = head if lt == SUBLANES else jnp.concatenate([head, rolled[SUBLANES:]], axis=0)
        acc = acc + shifted * w_ref[conv_w - 1 - s:conv_w - s, cols]
    return acc


def _conv_qkv_kernel(cfg, xbc_ref, xm_ref, hxbc_ref, hxm_ref, cwx_ref, cbx_ref, cwm_ref, cbm_ref,
                     wq_ref, wk_ref, wv_ref, wg_ref, wgT_ref, bg_row_ref, bg_col_ref,
                     xa_ref, q_ref, k_ref, v_ref, g_ref, gT_ref, px_ref, pm_ref):
    lt = xbc_ref.shape[0]
    hd = cfg.ml_head_dim
    nh = cfg.ml_heads

    @pl.when(pl.program_id(1) == 0)
    def _():
        px_ref[...] = hxbc_ref[0]
        pm_ref[...] = hxm_ref[0]

    for c0 in range(0, xbc_ref.shape[1], CONV_LANES):
        cols = slice(c0, min(c0 + CONV_LANES, xbc_ref.shape[1]))
        u = xbc_ref[:, cols]
        xa_ref[:, cols] = _silu(_causal_conv(u, px_ref[:, cols], cwx_ref, cbx_ref, cols, cfg.conv_w))
        px_ref[:, cols] = u[lt - SUBLANES:lt]

    d = nh * hd
    kscale = hd ** -0.5
    nt = (((1,), (1,)), ((), ()))
    gcol = jnp.zeros(g_ref.shape, F32) + bg_row_ref[...]
    grow = jnp.zeros(gT_ref.shape[2:], F32) + bg_col_ref[...]
    for h in range(nh):
        sl = slice(h * hd, (h + 1) * hd)
        xm = xm_ref[:, sl]
        xc = _silu(_causal_conv(xm, pm_ref[:, sl], cwm_ref, cbm_ref, sl, cfg.conv_w)).astype(BF16)
        pm_ref[:, sl] = xm[lt - SUBLANES:lt]
        qh = jnp.dot(xc, wq_ref[h], preferred_element_type=F32)
        kh = jnp.dot(xc, wk_ref[h], preferred_element_type=F32) * kscale
        vh = jnp.dot(xm.astype(BF16), wv_ref[h], preferred_element_type=F32)
        q_ref[:, sl] = qh
        k_ref[:, sl] = kh
        v_ref[:, sl] = vh
        for part, val in enumerate((qh, kh, vh)):
            vb = val.astype(BF16)
            rows = slice(part * d + h * hd, part * d + (h + 1) * hd)
            gcol = gcol + jnp.dot(vb, wg_ref[rows, :], preferred_element_type=F32)
            grow = grow + lax.dot_general(wgT_ref[:, rows], vb, nt, preferred_element_type=F32)
    lane = lax.broadcasted_iota(jnp.int32, gcol.shape, 1)
    g_ref[...] = jnp.where(lane < nh, gcol, -_softplus(-gcol))
    row = lax.broadcasted_iota(jnp.int32, grow.shape, 0)
    gT_ref[0, 0] = jnp.where(row < nh, grow, -_softplus(-grow))


def conv_qkv(cfg, proj, hist_xbc, hist_xm, cwx, cbx, cwm, cbm, wq, wk, wv, wg, wgT, bg_row, bg_col,
             row0, n_seq, seq_len, lt):
    d = cfg.d_model
    xbc = cfg.xbc_dim
    m = n_seq * seq_len
    nt = seq_len // lt
    ng = 2 * cfg.ml_heads
    xbc_blk = (3 * d) // xbc
    blk0 = row0 // lt
    row = lambda s, l: (s * nt + l, 0)
    const2 = lambda s, l: (0, 0)
    const3 = lambda s, l: (0, 0, 0)
    return pl.pallas_call(
        functools.partial(_conv_qkv_kernel, cfg),
        out_shape=(jax.ShapeDtypeStruct((m, xbc), F32),
                   jax.ShapeDtypeStruct((m, d), F32), jax.ShapeDtypeStruct((m, d), F32),
                   jax.ShapeDtypeStruct((m, d), F32),
                   jax.ShapeDtypeStruct((m, LANES), F32),
                   jax.ShapeDtypeStruct((n_seq, nt, ng, lt), F32)),
        grid=(n_seq, nt),
        in_specs=[pl.BlockSpec((lt, xbc), lambda s, l: (blk0 + s * nt + l, xbc_blk)),
                  pl.BlockSpec((lt, d), lambda s, l: (blk0 + s * nt + l, 1)),
                  pl.BlockSpec((1, SUBLANES, xbc), lambda s, l: (s, 0, 0)),
                  pl.BlockSpec((1, SUBLANES, d), lambda s, l: (s, 0, 0)),
                  pl.BlockSpec((cfg.conv_w, xbc), const2), pl.BlockSpec((1, xbc), const2),
                  pl.BlockSpec((cfg.conv_w, d), const2), pl.BlockSpec((1, d), const2),
                  pl.BlockSpec(wq.shape, const3), pl.BlockSpec(wk.shape, const3),
                  pl.BlockSpec(wv.shape, const3),
                  pl.BlockSpec(wg.shape, const2), pl.BlockSpec(wgT.shape, const2),
                  pl.BlockSpec((1, LANES), const2), pl.BlockSpec((ng, 1), const2)],
        out_specs=(pl.BlockSpec((lt, xbc), row), pl.BlockSpec((lt, d), row), pl.BlockSpec((lt, d), row),
                   pl.BlockSpec((lt, d), row), pl.BlockSpec((lt, LANES), row),
                   pl.BlockSpec((1, 1, ng, lt), lambda s, l: (s, l, 0, 0))),
        scratch_shapes=[pltpu.VMEM((SUBLANES, xbc), F32), pltpu.VMEM((SUBLANES, d), F32)],
        compiler_params=_cparams(("arbitrary", "arbitrary")),
        name="conv_qkv",
    )(proj, proj, hist_xbc, hist_xm, cwx, cbx, cwm, cbm, wq, wk, wv, wg, wgT, bg_row, bg_col)


def _tri(q, lower):
    r = lax.broadcasted_iota(jnp.int32, (q, q), 0)
    c = lax.broadcasted_iota(jnp.int32, (q, q), 1)
    return (c <= r) if lower else (r <= c)


def _ssd_chunk(cfg, xa_ref, d_ref, dT_ref, arow_ref, acol_ref, dskip_ref, y_ref, st_ref):
    q = xa_ref.shape[0]
    dm = cfg.d_model
    ns = cfg.ssd_state
    hp = cfg.ssd_head_dim
    hpg = cfg.ssd_heads // cfg.ssd_groups
    heads_per_tile = LANES // hp
    n_tiles = cfg.ssd_heads // heads_per_tile

    causal = _tri(q, True)
    tril = causal.astype(F32)
    triu = _tri(q, False).astype(F32)
    dcol = d_ref[...]
    drow = dT_ref[...]
    acum = jnp.dot(tril, dcol * arow_ref[...], precision=HI, preferred_element_type=F32)
    acumT = jnp.dot(drow * acol_ref[...], triu, precision=HI, preferred_element_type=F32)
    nt_dims = (((1,), (1,)), ((), ()))
    tn_dims = (((0,), (0,)), ((), ()))
    lane = lax.broadcasted_iota(jnp.int32, (q, LANES), 1)
    srow = lax.broadcasted_iota(jnp.int32, (LANES, ns), 0)

    cbs = []
    bgs = []
    cgs = []
    for g in range(cfg.ssd_groups):
        bg = xa_ref[:, dm + g * ns: dm + (g + 1) * ns].astype(BF16)
        cg = xa_ref[:, dm + cfg.bc_dim + g * ns: dm + cfg.bc_dim + (g + 1) * ns].astype(BF16)
        cbs.append(lax.dot_general(cg, bg, nt_dims, preferred_element_type=F32))
        bgs.append(bg)
        cgs.append(cg)

    for t in range(n_tiles):
        h0 = t * heads_per_tile
        g = h0 // hpg
        cols = slice(t * LANES, (t + 1) * LANES)
        x = xa_ref[:, cols]
        dsel = jnp.zeros((q, LANES), F32)
        esel = jnp.zeros((q, LANES), F32)
        tsel = jnp.zeros((q, LANES), F32)
        rdec = jnp.zeros((LANES, ns), F32)
        for i in range(heads_per_tile):
            h = h0 + i
            in_head = (lane >= i * hp) & (lane < (i + 1) * hp)
            a_col = acum[:, h:h + 1]
            a_last = acum[q - 1:q, h:h + 1]
            dsel = jnp.where(in_head, dcol[:, h:h + 1], dsel)
            esel = jnp.where(in_head, jnp.exp(a_col), esel)
            tsel = jnp.where(in_head, jnp.exp(a_last - a_col), tsel)
            rdec = jnp.where((srow >= i * hp) & (srow < (i + 1) * hp), jnp.exp(a_last), rdec)
        xd = x * dsel
        y = x * dskip_ref[:, cols]
        for i in range(heads_per_tile):
            h = h0 + i
            in_head = (lane >= i * hp) & (lane < (i + 1) * hp)
            seg = jnp.where(causal, acum[:, h:h + 1] - acumT[h:h + 1, :], -jnp.inf)
            w = (cbs[g] * jnp.exp(seg)).astype(BF16)
            xdh = jnp.where(in_head, xd, 0.0).astype(BF16)
            y = y + jnp.dot(w, xdh, preferred_element_type=F32)
        s_old = st_ref[cols, :]
        ys = lax.dot_general(cgs[g], s_old.astype(BF16), nt_dims, preferred_element_type=F32)
        y_ref[:, cols] = y + esel * ys
        upd = lax.dot_general((xd * tsel).astype(BF16), bgs[g], tn_dims, preferred_element_type=F32)
        st_ref[cols, :] = rdec * s_old + upd


def _ssd_kernel(cfg, n_chunks, xa_ref, d_ref, dT_ref, s0_ref, arow_ref, acol_ref, dskip_ref,
                y_ref, sout_ref, st_ref):
    @pl.when(pl.program_id(1) == 0)
    def _():
        st_ref[...] = s0_ref[...]

    for b in range(xa_ref.shape[0]):
        _ssd_chunk(cfg, xa_ref.at[b], d_ref.at[b], dT_ref.at[b, 0], arow_ref, acol_ref, dskip_ref,
                   y_ref.at[b], st_ref.at[b])

    @pl.when(pl.program_id(1) == n_chunks - 1)
    def _():
        sout_ref[...] = st_ref[...]


def ssd_scan(cfg, xa, d, dT, s0, a_row, a_col, dskip, n_seq, seq_len, q, nb):
    dm = cfg.d_model
    nc = seq_len // q
    blk3 = lambda s, c: (s, c, 0)
    st3 = lambda s, c: (s, 0, 0)
    const2 = lambda s, c: (0, 0)
    return pl.pallas_call(
        functools.partial(_ssd_kernel, cfg, nc),
        out_shape=(jax.ShapeDtypeStruct((n_seq, seq_len, dm), F32),
                   jax.ShapeDtypeStruct((n_seq, dm, cfg.ssd_state), F32)),
        grid=(n_seq // nb, nc),
        in_specs=[pl.BlockSpec((nb, q, cfg.xbc_dim), blk3),
                  pl.BlockSpec((nb, q, LANES), blk3),
                  pl.BlockSpec((nb, 1, cfg.ssd_heads, q), lambda s, c: (s, c, 0, 0)),
                  pl.BlockSpec((nb, dm, cfg.ssd_state), st3),
                  pl.BlockSpec((1, LANES), const2),
                  pl.BlockSpec((cfg.ssd_heads, 1), const2),
                  pl.BlockSpec((1, dm), const2)],
        out_specs=(pl.BlockSpec((nb, q, dm), blk3),
                   pl.BlockSpec((nb, dm, cfg.ssd_state), st3)),
        scratch_shapes=[pltpu.VMEM((nb, dm, cfg.ssd_state), F32)],
        compiler_params=_cparams(("arbitrary", "arbitrary")),
        name="ssd_scan",
    )(xa, d, dT, s0, a_row, a_col, dskip)


def _mlstm_chunk(cfg, q_ref, k_ref, v_ref, g_ref, gT_ref, h_ref, c_ref, n_ref, m_ref):
    ql = q_ref.shape[0]
    hd = cfg.ml_head_dim
    nh = cfg.ml_heads

    causal = _tri(ql, True)
    gcol = g_ref[...]
    grow = gT_ref[...]
    bcum = jnp.dot(causal.astype(F32), gcol, precision=HI, preferred_element_type=F32)
    bcumT = jnp.dot(grow, _tri(ql, False).astype(F32), precision=HI, preferred_element_type=F32)
    nt_dims = (((1,), (1,)), ((), ()))
    tn_dims = (((0,), (0,)), ((), ()))

    for h in range(nh):
        sl = slice(h * hd, (h + 1) * hd)
        b_col = bcum[:, nh + h:nh + h + 1]
        b_row = bcumT[nh + h:nh + h + 1, :]
        i_col = gcol[:, h:h + 1]
        i_row = grow[h:h + 1, :]
        m_prev = m_ref[h:h + 1, 0:1]
        dlog = jnp.where(causal, b_col - b_row + i_row, -jnp.inf)
        inter = b_col + m_prev
        mt = jnp.maximum(inter, jnp.max(dlog, axis=1, keepdims=True))
        qh = q_ref[:, sl]
        kh = k_ref[:, sl]
        vh = v_ref[:, sl]
        qb = qh.astype(BF16)
        kb = kh.astype(BF16)
        s = lax.dot_general(qb, kb, nt_dims, preferred_element_type=F32) * jnp.exp(dlog - mt)
        gdec = jnp.exp(inter - mt)
        c_old = c_ref[sl, :]
        n_old = n_ref[h:h + 1, :]
        qc = lax.dot_general(qb, c_old.astype(BF16), nt_dims, preferred_element_type=F32)
        num = jnp.dot(s.astype(BF16), vh.astype(BF16), preferred_element_type=F32) + gdec * qc
        den = jnp.sum(s, axis=1, keepdims=True) + gdec * jnp.sum(qh * n_old, axis=1, keepdims=True)
        h_ref[:, sl] = num / jnp.maximum(jnp.abs(den), jnp.exp(-mt))
        m_new = mt[ql - 1:ql, :]
        gs = jnp.exp(b_col[ql - 1:ql, :] - b_col + i_col - m_new)
        gc = jnp.exp(inter[ql - 1:ql, :] - m_new)
        upd = lax.dot_general((vh * gs).astype(BF16), kb, tn_dims, preferred_element_type=F32)
        c_ref[sl, :] = gc * c_old + upd
        n_ref[h:h + 1, :] = gc * n_old + jnp.sum(gs * kh, axis=0, keepdims=True)
        m_ref[h:h + 1, :] = jnp.broadcast_to(m_new, (1, LANES))


def _mlstm_kernel(cfg, n_chunks, q_ref, k_ref, v_ref, g_ref, gT_ref, c0_ref, n0_ref, m0_ref,
                  h_ref, cout_ref, nout_ref, mout_ref, c_ref, n_ref, m_ref):
    @pl.when(pl.program_id(1) == 0)
    def _():
        c_ref[...] = c0_ref[...]
        n_ref[...] = n0_ref[...]
        m_ref[...] = m0_ref[...]

    for b in range(q_ref.shape[0]):
        _mlstm_chunk(cfg, q_ref.at[b], k_ref.at[b], v_ref.at[b], g_ref.at[b], gT_ref.at[b, 0],
                     h_ref.at[b], c_ref.at[b], n_ref.at[b], m_ref.at[b])

    @pl.when(pl.program_id(1) == n_chunks - 1)
    def _():
        cout_ref[...] = c_ref[...]
        nout_ref[...] = n_ref[...]
        mout_ref[...] = m_ref[...]


def mlstm_scan(cfg, qa, ka, va, g, gT, c0, n0, m0, n_seq, seq_len, q, nb):
    d = cfg.d_model
    hd = cfg.ml_head_dim
    nh = cfg.ml_heads
    nc = seq_len // q
    blk3 = lambda s, c: (s, c, 0)
    st3 = lambda s, c: (s, 0, 0)
    return pl.pallas_call(
        functools.partial(_mlstm_kernel, cfg, nc),
        out_shape=(jax.ShapeDtypeStruct((n_seq, seq_len, d), F32),
                   jax.ShapeDtypeStruct((n_seq, d, hd), F32),
                   jax.ShapeDtypeStruct((n_seq, nh, hd), F32),
                   jax.ShapeDtypeStruct((n_seq, nh, LANES), F32)),
        grid=(n_seq // nb, nc),
        in_specs=[pl.BlockSpec((nb, q, d), blk3), pl.BlockSpec((nb, q, d), blk3), pl.BlockSpec((nb, q, d), blk3),
                  pl.BlockSpec((nb, q, LANES), blk3),
                  pl.BlockSpec((nb, 1, 2 * nh, q), lambda s, c: (s, c, 0, 0)),
                  pl.BlockSpec((nb, d, hd), st3), pl.BlockSpec((nb, nh, hd), st3),
                  pl.BlockSpec((nb, nh, LANES), st3)],
        out_specs=(pl.BlockSpec((nb, q, d), blk3),
                   pl.BlockSpec((nb, d, hd), st3), pl.BlockSpec((nb, nh, hd), st3),
                   pl.BlockSpec((nb, nh, LANES), st3)),
        scratch_shapes=[pltpu.VMEM((nb, d, hd), F32), pltpu.VMEM((nb, nh, hd), F32),
                        pltpu.VMEM((nb, nh, LANES), F32)],
        compiler_params=_cparams(("arbitrary", "arbitrary")),
        name="mlstm_scan",
    )(qa, ka, va, g, gT, c0, n0, m0)


def _group_norm(x, w_ref, col0, groups, width):
    parts = []
    for g in range(groups):
        seg = x[:, g * width:(g + 1) * width]
        parts.append(seg * lax.rsqrt(jnp.mean(seg * seg, axis=-1, keepdims=True) + EPS)
                     * w_ref[:, col0 + g * width: col0 + (g + 1) * width])
    return parts


def _outproj_kernel(cfg, ys_ref, z_ref, hm_ref, o_ref, x_ref, nws_ref, nwm_ref, w_ref, out_ref, mix_ref):
    d = cfg.d_model

    @pl.when(pl.program_id(1) == 0)
    def _():
        ws = d // cfg.ssd_groups
        yz = ys_ref[...] * _silu(z_ref[...])
        for g, part in enumerate(_group_norm(yz, nws_ref, 0, cfg.ssd_groups, ws)):
            mix_ref[:, g * ws:(g + 1) * ws] = part.astype(BF16)
        gate = _sigmoid(o_ref[...])
        wm = cfg.ml_head_dim
        for g, part in enumerate(_group_norm(hm_ref[...], nwm_ref, 0, cfg.ml_heads, wm)):
            mix_ref[:, d + g * wm: d + (g + 1) * wm] = (part * gate[:, g * wm:(g + 1) * wm]).astype(BF16)

    out_ref[...] = x_ref[...] + jnp.dot(mix_ref[...], w_ref[...], preferred_element_type=F32)


def out_proj(cfg, ys, proj, proj_row0, hm, x, nws, nwm, w_out, bm, bn):
    m, d = x.shape
    blk0 = proj_row0 // bm
    const2 = lambda i, j: (0, 0)
    full = lambda i, j: (i, 0)
    w_mode = dict(pipeline_mode=pl.Buffered(1)) if bn == d else {}
    return pl.pallas_call(
        functools.partial(_outproj_kernel, cfg),
        out_shape=jax.ShapeDtypeStruct((m, d), F32),
        grid=(m // bm, d // bn),
        in_specs=[pl.BlockSpec((bm, d), full),
                  pl.BlockSpec((bm, d), lambda i, j: (i + blk0, 0)),
                  pl.BlockSpec((bm, d), full),
                  pl.BlockSpec((bm, d), lambda i, j: (i + blk0, 2)),
                  pl.BlockSpec((bm, bn), lambda i, j: (i, j)),
                  pl.BlockSpec((1, d), const2), pl.BlockSpec((1, d), const2),
                  pl.BlockSpec((2 * d, bn), lambda i, j: (0, j), **w_mode)],
        out_specs=pl.BlockSpec((bm, bn), lambda i, j: (i, j)),
        scratch_shapes=[pltpu.VMEM((bm, 2 * d), BF16)],
        compiler_params=_cparams(("arbitrary", "arbitrary")),
        name="out_proj",
    )(ys, proj, hm, proj, x, nws, nwm, w_out)


def _router_kernel(cfg, n_a, xa_ref, xb_ref, nw_ref, wr_ref, br_ref, ei_ref, wt_ref, cnt_out_ref, rows_ref,
                   cnt_ref, x_ref):
    ne = cfg.n_experts
    epg = cfg.experts_per_group
    ngr = cfg.n_groups
    bm = x_ref.shape[0]
    i = pl.program_id(0)

    @pl.when(i == 0)
    def _():
        cnt_ref[...] = jnp.zeros_like(cnt_ref)

    @pl.when(i < n_a)
    def _():
        x_ref[...] = xa_ref[...]

    @pl.when(i >= n_a)
    def _():
        x_ref[...] = xb_ref[...]

    hb = _rms(x_ref[...], nw_ref[...]).astype(BF16)
    rows_ref[...] = _lanes_to_rows(_pack_bf16_pairs(hb))
    logits = jnp.dot(hb, wr_ref[...], preferred_element_type=F32) + br_ref[...]
    lane = lax.broadcasted_iota(jnp.int32, logits.shape, 1)
    big = jnp.int32(2 ** 30)
    neg = -jnp.inf

    def first_argmax(vals):
        mx = jnp.max(vals, axis=-1, keepdims=True)
        idx = jnp.min(jnp.where(vals == mx, lane, big), axis=-1, keepdims=True)
        return mx, idx

    is_group = (lane >= ne) & (lane < ne + ngr)
    gl = jnp.where(is_group, logits, neg)
    gmax, gidx = first_argmax(gl)
    p_g = 1.0 / jnp.sum(jnp.exp(gl - gmax), axis=-1, keepdims=True)
    e_lo = (gidx - ne) * epg
    in_sel = (lane >= e_lo) & (lane < e_lo + epg)
    el = jnp.where(in_sel, logits, neg)
    pe = jnp.exp(el - jnp.max(el, axis=-1, keepdims=True))
    pe = jnp.where(in_sel, pe / jnp.sum(pe, axis=-1, keepdims=True), -1.0)
    p1, i1 = first_argmax(pe)
    p2, i2 = first_argmax(jnp.where(lane == i1, -1.0, pe))
    wsum = p1 + p2
    wt_ref[...] = jnp.where(lane == 0, p_g * p1 / wsum, jnp.where(lane == 1, p_g * p2 / wsum, 0.0))

    oh1 = jnp.where(lane == i1, 1.0, 0.0)
    oh2 = jnp.where(lane == i2, 1.0, 0.0)
    r = lax.broadcasted_iota(jnp.int32, (bm, bm), 0)
    c = lax.broadcasted_iota(jnp.int32, (bm, bm), 1)
    before = jnp.where(c < r, 1.0, 0.0).astype(BF16)
    ahead1 = jnp.dot(before, oh1.astype(BF16), preferred_element_type=F32)
    ahead2 = jnp.dot(before, oh2.astype(BF16), preferred_element_type=F32)
    cnt = cnt_ref[...]
    tot1 = jnp.sum(oh1, axis=0, keepdims=True)
    rank1 = jnp.sum(oh1 * (cnt + ahead1), axis=-1, keepdims=True)
    rank2 = jnp.sum(oh2 * (cnt + tot1 + ahead2), axis=-1, keepdims=True)
    cnt_new = cnt + tot1 + jnp.sum(oh2, axis=0, keepdims=True)
    cnt_ref[...] = cnt_new
    cnt_out_ref[...] = cnt_new
    ei_ref[...] = jnp.where(lane == 0, i1, jnp.where(lane == 1, i2, jnp.where(
        lane == 2, rank1.astype(jnp.int32), jnp.where(lane == 3, rank2.astype(jnp.int32), 0))))


def router(cfg, xa, xb, nw, wr, br, bm):
    d = xa.shape[1]
    n_a = xa.shape[0] // bm
    n_b = xb.shape[0] // bm
    m = xa.shape[0] + xb.shape[0]
    nch = d // (2 * LANES)
    const2 = lambda i: (0, 0)
    return pl.pallas_call(
        functools.partial(_router_kernel, cfg, n_a),
        out_shape=(jax.ShapeDtypeStruct((m, LANES), jnp.int32), jax.ShapeDtypeStruct((m, LANES), F32),
                   jax.ShapeDtypeStruct((1, LANES), F32), jax.ShapeDtypeStruct((m, nch, LANES), jnp.uint32)),
        grid=(n_a + n_b,),
        in_specs=[pl.BlockSpec((bm, d), lambda i: (jnp.minimum(i, n_a - 1), 0)),
                  pl.BlockSpec((bm, d), lambda i: (jnp.maximum(i - n_a, 0), 0)),
                  pl.BlockSpec((1, d), const2),
                  pl.BlockSpec((d, LANES), const2), pl.BlockSpec((1, LANES), const2)],
        out_specs=(pl.BlockSpec((bm, LANES), lambda i: (i, 0)), pl.BlockSpec((bm, LANES), lambda i: (i, 0)),
                   pl.BlockSpec((1, LANES), const2), pl.BlockSpec((bm, nch, LANES), lambda i: (i, 0, 0))),
        scratch_shapes=[pltpu.VMEM((1, LANES), F32), pltpu.VMEM((bm, d), F32)],
        compiler_params=_cparams(("arbitrary",)),
        name="router",
    )(xa, xb, nw, wr, br)


def _pack_bf16_pairs(x):
    half = x.shape[1] // 2
    bits = lambda v: lax.bitcast_convert_type(v.astype(BF16).astype(F32), jnp.uint32)
    return bits(x[:, :half]) | (bits(x[:, half:]) >> 16)


def _unpack_bf16_pairs(p):
    hi = lax.bitcast_convert_type(p & jnp.uint32(0xFFFF0000), F32)
    lo = lax.bitcast_convert_type(p << 16, F32)
    return jnp.concatenate([hi, lo], axis=-1)


def _rows_to_lanes(g):
    t = pltpu.einshape("rcl->crl", g)
    return jnp.concatenate([t[c] for c in range(t.shape[0])], axis=-1)


def _lanes_to_rows(x):
    parts = jnp.stack([x[:, c * LANES:(c + 1) * LANES] for c in range(x.shape[1] // LANES)], axis=0)
    return pltpu.einshape("crl->rcl", parts)


GATHER_GROUP = 8


def _gather_rows(idx_ref, src_hbm, dst, sem, n_groups):
    def body(g, carry):
        for u in range(GATHER_GROUP):
            r = g * GATHER_GROUP + u
            pltpu.make_async_copy(src_hbm.at[idx_ref[0, 0, r]], dst.at[r], sem).start()
        return carry
    lax.fori_loop(0, n_groups, body, 0)


def _wait_rows(src_hbm, dst, sem, n):
    pltpu.make_async_copy(src_hbm.at[pl.ds(0, n)], dst.at[pl.ds(0, n)], sem).wait()


def _moe_kernel(cfg, te_ref, nr_ref, first_ref, wslot_ref, nexte_ref, nv_ref, src_ref, srcn_ref, x_hbm,
                wg_hbm, wu_hbm, wd_hbm, ys_ref, xbuf, sem, wgf, wuf, wdf, wsem, wgb, wub, wdb):
    j = pl.program_id(0)
    n_valid = nv_ref[0]

    def weight_copies(e, slot):
        return (pltpu.make_async_copy(wg_hbm.at[e], wgf.at[slot], wsem.at[slot, 0]),
                pltpu.make_async_copy(wu_hbm.at[e], wuf.at[slot], wsem.at[slot, 1]),
                pltpu.make_async_copy(wd_hbm.at[e], wdf.at[slot], wsem.at[slot, 2]))

    @pl.when(j == 0)
    def _():
        for cp in weight_copies(te_ref[0], 0):
            cp.start()
        xbuf[...] = jnp.zeros_like(xbuf)
        _gather_rows(src_ref, x_hbm, xbuf.at[0], sem.at[0], nr_ref[0] // GATHER_GROUP)

    @pl.when(j + 1 < n_valid)
    def _():
        nslot = (j + 1) % 2
        _gather_rows(srcn_ref, x_hbm, xbuf.at[nslot], sem.at[nslot], nr_ref[j + 1] // GATHER_GROUP)

    @pl.when(j < n_valid)
    def _():
        @pl.when(first_ref[j] == 1)
        def _():
            ws = wslot_ref[j]
            for cp in weight_copies(te_ref[j], ws):
                cp.wait()

            @pl.when(nexte_ref[j] >= 0)
            def _():
                for cp in weight_copies(nexte_ref[j], 1 - ws):
                    cp.start()

            wgb[...] = wgf[ws].astype(BF16)
            wub[...] = wuf[ws].astype(BF16)
            wdb[...] = wdf[ws].astype(BF16)

        slot = j % 2
        _wait_rows(x_hbm, xbuf.at[slot], sem.at[slot], nr_ref[j])
        hb = _unpack_bf16_pairs(_rows_to_lanes(xbuf[slot])).astype(BF16)
        hid = (_silu(jnp.dot(hb, wgb[...], preferred_element_type=F32))
               * jnp.dot(hb, wub[...], preferred_element_type=F32))
        y = jnp.dot(hid.astype(BF16), wdb[...], preferred_element_type=F32)
        ys_ref[...] = _lanes_to_rows(_pack_bf16_pairs(y))

    @pl.when(j >= n_valid)
    def _():
        ys_ref[...] = jnp.zeros_like(ys_ref)


def moe_routed(cfg, tables, src, x_rows, wg, wu, wd, tm):
    n_tiles = src.shape[0]
    d = cfg.d_model
    de = cfg.d_expert
    nch = x_rows.shape[1]
    return pl.pallas_call(
        functools.partial(_moe_kernel, cfg),
        out_shape=jax.ShapeDtypeStruct((n_tiles * tm, nch, LANES), jnp.uint32),
        grid_spec=pltpu.PrefetchScalarGridSpec(
            num_scalar_prefetch=len(tables),
            grid=(n_tiles,),
            in_specs=[pl.BlockSpec((1, 1, tm), lambda j, *_: (j, 0, 0), memory_space=pltpu.SMEM),
                      pl.BlockSpec((1, 1, tm), lambda j, *_: (jnp.minimum(j + 1, n_tiles - 1), 0, 0),
                                   memory_space=pltpu.SMEM),
                      pl.BlockSpec(memory_space=pl.ANY),
                      pl.BlockSpec(memory_space=pl.ANY), pl.BlockSpec(memory_space=pl.ANY),
                      pl.BlockSpec(memory_space=pl.ANY)],
            out_specs=pl.BlockSpec((tm, nch, LANES), lambda j, *_: (j, 0, 0)),
            scratch_shapes=[pltpu.VMEM((2, tm, nch, LANES), jnp.uint32), pltpu.SemaphoreType.DMA((2,)),
                            pltpu.VMEM((2, d, de), F32), pltpu.VMEM((2, d, de), F32), pltpu.VMEM((2, de, d), F32),
                            pltpu.SemaphoreType.DMA((2, 3)),
                            pltpu.VMEM((d, de), BF16), pltpu.VMEM((d, de), BF16), pltpu.VMEM((de, d), BF16)]),
        compiler_params=_cparams(("arbitrary",)),
        name="moe",
    )(*tables, src, src, x_rows, wg, wu, wd)


def _combine_kernel(n_steps, d0_ref, d1_ref, d0n_ref, d1n_ref, wt_ref, x1_ref, fw_ref, ys_hbm, y_ref, gbuf, sem):
    i = pl.program_id(0)
    groups = x1_ref.shape[0] // GATHER_GROUP

    def fetch(r0_ref, r1_ref, slot):
        _gather_rows(r0_ref, ys_hbm, gbuf.at[slot, 0], sem.at[slot, 0], groups)
        _gather_rows(r1_ref, ys_hbm, gbuf.at[slot, 1], sem.at[slot, 1], groups)

    @pl.when(i == 0)
    def _():
        fetch(d0_ref, d1_ref, 0)

    @pl.when(i + 1 < n_steps)
    def _():
        fetch(d0n_ref, d1n_ref, (i + 1) % 2)

    slot = i % 2
    bm = x1_ref.shape[0]
    _wait_rows(ys_hbm, gbuf.at[slot, 0], sem.at[slot, 0], bm)
    _wait_rows(ys_hbm, gbuf.at[slot, 1], sem.at[slot, 1], bm)
    wt = wt_ref[...]
    acc = (x1_ref[...] + wt[:, 0:1] * _unpack_bf16_pairs(_rows_to_lanes(gbuf[slot, 0]))
           + wt[:, 1:2] * _unpack_bf16_pairs(_rows_to_lanes(gbuf[slot, 1])))
    y_ref[...] = _rms(acc, fw_ref[...])


def moe_combine(cfg, dest0, dest1, wt, wt_blk0, x1, fw, ys, bm):
    m, d = x1.shape
    nch = ys.shape[1]
    n = m // bm
    cur = lambda: pl.BlockSpec((1, 1, bm), lambda i: (i, 0, 0), memory_space=pltpu.SMEM)
    nxt = lambda: pl.BlockSpec((1, 1, bm), lambda i: (jnp.minimum(i + 1, n - 1), 0, 0), memory_space=pltpu.SMEM)
    return pl.pallas_call(
        functools.partial(_combine_kernel, n),
        out_shape=jax.ShapeDtypeStruct((m, d), F32),
        grid=(n,),
        in_specs=[cur(), cur(), nxt(), nxt(),
                  pl.BlockSpec((bm, LANES), lambda i: (i + wt_blk0, 0)),
                  pl.BlockSpec((bm, d), lambda i: (i, 0)), pl.BlockSpec((1, d), lambda i: (0, 0)),
                  pl.BlockSpec(memory_space=pl.ANY)],
        out_specs=pl.BlockSpec((bm, d), lambda i: (i, 0)),
        scratch_shapes=[pltpu.VMEM((2, 2, bm, nch, LANES), jnp.uint32), pltpu.SemaphoreType.DMA((2, 2))],
        compiler_params=_cparams(("arbitrary",)),
        name="moe_combine",
    )(dest0, dest1, dest0, dest1, wt, x1, fw, ys)


def _route_tables(cfg, ei, cnt, tm, n_tiles):
    ne = cfg.n_experts
    i32 = jnp.int32
    counts = cnt[0, :ne].astype(i32)
    tiles_e = (counts + tm - 1) // tm
    tile_end = jnp.cumsum(tiles_e)
    tile_start = tile_end - tiles_e
    row_off = tile_start * tm
    tile_id = jnp.arange(n_tiles, dtype=i32)
    tile_expert = jnp.minimum(jnp.sum((tile_id[:, None] >= tile_end[None, :]).astype(i32), axis=1), ne - 1)
    used = jnp.clip(counts[tile_expert] - (tile_id - tile_start[tile_expert]) * tm, 0, tm)
    used = jnp.where(tile_id < tile_end[-1], used, 0)
    tile_rows = (used + GATHER_GROUP - 1) // GATHER_GROUP * GATHER_GROUP
    n_valid = tile_end[-1]
    prev_expert = jnp.concatenate([jnp.full((1,), -1, i32), tile_expert[:-1]])
    first = ((tile_expert != prev_expert) & (tile_id < n_valid)).astype(i32)
    wslot = (jnp.cumsum(first) - 1) % 2
    e_id = jnp.arange(ne, dtype=i32)
    later = (e_id[None, :] > e_id[:, None]) & (tiles_e[None, :] > 0)
    next_e = jnp.min(jnp.where(later, e_id[None, :], ne), axis=1)
    next_e = jnp.where(next_e < ne, next_e, -1)[tile_expert]
    tables = (tile_expert, tile_rows, first, wslot.astype(i32), next_e.astype(i32), n_valid[None].astype(i32))
    picks = ei[:, 0:4].T
    pick_off = jnp.sum(jnp.where(picks[None, 0:2] == e_id[:, None, None], row_off[:, None, None], 0), axis=0)
    dest = pick_off + picks[2:4]
    tok = jnp.broadcast_to(jnp.arange(ei.shape[0], dtype=i32)[None, :], dest.shape)
    src = jnp.zeros((n_tiles * tm,), i32).at[dest.reshape(-1)].set(
        tok.reshape(-1), unique_indices=True, mode="promise_in_bounds")
    return tables, src.reshape(n_tiles, 1, tm), dest


def _pad_hist(hist):
    return jnp.pad(hist, ((0, 0), (SUBLANES - hist.shape[1], 0), (0, 0)))


def _tile(m, pref):
    return pref if m % pref == 0 else m


def _mixer_segment(cfg, proj, row0, dcol, dT, n_seq, seq_len, hist_xbc, hist_xm, s0, c0, n0, m0, p):
    d = cfg.d_model
    m = n_seq * seq_len
    q = min(cfg.chunk, seq_len)
    nc = seq_len // q
    xa, qa, ka, va, g, gT = conv_qkv(cfg, proj, hist_xbc, hist_xm, p["cwx"], p["cbx"], p["cwm"], p["cbm"],
                                     p["wq"], p["wk"], p["wv"], p["wg"], p["wgT"], p["bg_row"], p["bg_col"],
                                     row0, n_seq, seq_len, q)
    dTc = dT[:cfg.ssd_heads].reshape(cfg.ssd_heads, n_seq, nc, q).transpose(1, 2, 0, 3)
    nb = 2 if (nc == 1 and n_seq % 2 == 0) else 1
    seq3 = lambda a: a.reshape(n_seq, seq_len, a.shape[-1])
    ys, s_new = ssd_scan(cfg, seq3(xa), seq3(dcol), dTc, s0, p["a_row"], p["a_col"], p["dskip"], n_seq, seq_len, q, nb)
    hm, c_new, n_new, m_new = mlstm_scan(cfg, seq3(qa), seq3(ka), seq3(va), seq3(g), gT, c0, n0, m0,
                                         n_seq, seq_len, q, nb)
    ys = ys.reshape(m, d)
    hm = hm.reshape(m, d)
    keep = cfg.conv_w - 1
    groups = proj.reshape(proj.shape[0] // SUBLANES, SUBLANES, proj.shape[1])
    first = (row0 + seq_len) // SUBLANES - 1
    step = seq_len // SUBLANES
    tail = lax.slice(groups, (first, SUBLANES - keep, 0),
                     (first + (n_seq - 1) * step + 1, SUBLANES, proj.shape[1]), (step, 1, 1))
    tail_xbc = tail[:, :, 3 * d:]
    tail_xm = tail[:, :, d:2 * d]
    return ys, hm, (tail_xbc, s_new, tail_xm, c_new, n_new, m_new)


MOE_TILE = 256


def _ffn(cfg, segments, p):
    d = cfg.d_model
    x1s = [out_proj(cfg, ys, proj, row0, hm, x2d, p["ssd_norm_w"], p["mlstm_norm_w"], p["w_out"],
                    _tile(x2d.shape[0], 256), d) for x2d, proj, row0, ys, hm in segments]
    bm = 256 if all(x1.shape[0] % 256 == 0 for x1 in x1s) else 128
    ei, wt, cnt, x_rows = router(cfg, x1s[0], x1s[1], p["norm_ffn_w"], p["wr"], p["br"], bm)
    n_tok = ei.shape[0]
    n_tiles = (2 * n_tok + cfg.n_experts * (MOE_TILE - 1)) // MOE_TILE
    tables, src, dest = _route_tables(cfg, ei, cnt, MOE_TILE, n_tiles)
    ys_sorted = moe_routed(cfg, tables, src, x_rows, p["w_gate"], p["w_up"], p["w_down"], MOE_TILE)
    outs = []
    off = 0
    for x1 in x1s:
        m = x1.shape[0]
        dseg = dest[:, off:off + m].reshape(2, m // bm, 1, bm)
        outs.append(moe_combine(cfg, dseg[0], dseg[1], wt, off // bm, x1, p["final_norm_w"], ys_sorted, bm))
        off += m
    return outs


def _prep_params(cfg, norm_mix_w, w_in, conv_ssd_w, conv_ssd_b, dt_bias, a_log, d_skip, ssd_norm_w,
                 conv_mlstm_w, conv_mlstm_b, w_q, w_k, w_v, w_igate, b_igate, w_fgate, b_fgate, mlstm_norm_w,
                 w_out, norm_ffn_w, w_group, b_group, w_router, b_router, w_gate, w_up, w_down, final_norm_w):
    d = cfg.d_model
    hs = cfg.ssd_heads
    nh = cfg.ml_heads
    o_xbc = d + cfg.xbc_dim
    row = lambda v: v.reshape(1, -1).astype(F32)
    pad_lanes = lambda a: jnp.pad(a, ((0, 0), (0, LANES - a.shape[1])))
    w_in_t = w_in.T.astype(F32)
    w_dt_t = jnp.pad(w_in_t[o_xbc:o_xbc + hs], ((0, LANES - hs), (0, 0)))
    a = -jnp.exp(a_log.astype(F32))
    w_gates = jnp.concatenate([w_igate, w_fgate], axis=1)
    b_gates = jnp.concatenate([b_igate, b_fgate]).astype(F32)
    ne = cfg.n_experts
    wr = pad_lanes(jnp.concatenate([w_router, w_group], axis=1))
    br = pad_lanes(jnp.concatenate([b_router, b_group]).reshape(1, -1).astype(F32))
    return dict(
        norm_mix_w=row(norm_mix_w),
        w_in_t=w_in_t, w_dt=w_dt_t.T.astype(BF16), w_dtT=w_dt_t.astype(BF16),
        bdt_row=pad_lanes(row(dt_bias)), bdt_col=pad_lanes(row(dt_bias)).T,
        cwx=conv_ssd_w.astype(F32), cbx=row(conv_ssd_b), cwm=conv_mlstm_w.astype(F32), cbm=row(conv_mlstm_b),
        wq=w_q.astype(BF16), wk=w_k.astype(BF16), wv=w_v.astype(BF16),
        wg=pad_lanes(w_gates).astype(BF16), wgT=w_gates.T.astype(BF16),
        bg_row=pad_lanes(row(b_gates)), bg_col=b_gates.reshape(-1, 1),
        a_row=pad_lanes(row(a)), a_col=a.reshape(-1, 1),
        dskip=row(jnp.repeat(d_skip.astype(F32), cfg.ssd_head_dim)),
        ssd_norm_w=row(ssd_norm_w), mlstm_norm_w=row(mlstm_norm_w), w_out=w_out.astype(BF16),
        norm_ffn_w=row(norm_ffn_w), wr=wr.astype(BF16), br=br,
        w_gate=w_gate.astype(F32), w_up=w_up.astype(F32), w_down=w_down.astype(F32),
        final_norm_w=row(final_norm_w),
    )


def forward(cfg, x_prompt, x_sample, state_ssd_conv, state_ssd, state_mlstm_conv, state_mlstm_c,
            state_mlstm_n, state_mlstm_m, meta_tokens, *weights):
    d = cfg.d_model
    nh = cfg.ml_heads
    hd = cfg.ml_head_dim
    assert state_ssd.shape[0] == 1, "single-layer kernel"
    p = _prep_params(cfg, *[w[0] for w in weights[:-1]], weights[-1])
    bp, lp, _ = x_prompt.shape
    bs, ls, _ = x_sample.shape
    n_meta = meta_tokens.shape[0]

    xp = x_prompt.reshape(bp * lp, d)
    xs = x_sample.reshape(bs * ls, d)
    mp, ms = bp * lp, bs * ls
    norm_args = (p["norm_mix_w"], p["w_dt"], p["w_dtT"], p["bdt_row"], p["bdt_col"])
    h_meta, d_meta, dT_meta = pre_norm(meta_tokens.astype(F32), None, *norm_args, n_meta)
    h, dcol, dT = pre_norm(xp, xs, *norm_args, 512 if mp % 512 == 0 and ms % 512 == 0 else 128)
    proj, proj_meta = in_proj(cfg, h, h_meta, p["w_in_t"], _tile(mp + ms, 1024), d // 2)

    zeros = lambda *s: jnp.zeros(s, F32)
    _, _, st_meta = _mixer_segment(
        cfg, proj_meta, 0, d_meta, dT_meta, 1, n_meta, zeros(1, SUBLANES, cfg.xbc_dim), zeros(1, SUBLANES, d),
        zeros(1, d, cfg.ssd_state), zeros(1, d, hd), zeros(1, nh, hd), zeros(1, nh, LANES), p)
    mt_xbc, mt_s, mt_xm, mt_c, mt_n, mt_m = st_meta
    rep = lambda a: jnp.broadcast_to(a, (bp,) + a.shape[1:])

    ys_p, hm_p, st_p = _mixer_segment(
        cfg, proj, 0, dcol[:mp], dT[:, :mp], bp, lp, rep(_pad_hist(mt_xbc)), rep(_pad_hist(mt_xm)),
        rep(mt_s), rep(mt_c), rep(mt_n), rep(mt_m), p)

    m0 = jnp.broadcast_to(state_mlstm_m[0].astype(F32)[:, :, None], (bs, nh, LANES))
    ys_s, hm_s, st_s = _mixer_segment(
        cfg, proj, mp, dcol[mp:], dT[:, mp:], bs, ls, _pad_hist(state_ssd_conv[0]),
        _pad_hist(state_mlstm_conv[0]), state_ssd[0].reshape(bs, d, cfg.ssd_state),
        state_mlstm_c[0].reshape(bs, d, hd), state_mlstm_n[0], m0, p)
    y_p, y_s = _ffn(cfg, [(xp, proj, 0, ys_p, hm_p), (xs, proj, mp, ys_s, hm_s)], p)
    y_prompt = y_p.reshape(bp, lp, d)
    y_sample = y_s.reshape(bs, ls, d)

    def pack(st, b):
        t_xbc, s_new, t_xm, c_new, n_new, m_new = st
        return (t_xbc[None], s_new.reshape(1, b, cfg.ssd_heads, cfg.ssd_head_dim, cfg.ssd_state),
                t_xm[None], c_new.reshape(1, b, nh, hd, hd), n_new[None], m_new[None, :, :, 0])

    return (y_prompt, y_sample) + pack(st_p, bp) + pack(st_s, bs)


def kernel(x_prompt, x_sample, state_ssd_conv, state_ssd, state_mlstm_conv, state_mlstm_c, state_mlstm_n, state_mlstm_m, meta_tokens, norm_mix_w, w_in, conv_ssd_w, conv_ssd_b, dt_bias, a_log, d_skip, ssd_norm_w, conv_mlstm_w, conv_mlstm_b, w_q, w_k, w_v, w_igate, b_igate, w_fgate, b_fgate, mlstm_norm_w, w_out, norm_ffn_w, w_group, b_group, w_router, b_router, w_gate, w_up, w_down, final_norm_w):
    return forward(Cfg(), x_prompt, x_sample, state_ssd_conv, state_ssd, state_mlstm_conv, state_mlstm_c,
                   state_mlstm_n, state_mlstm_m, meta_tokens, norm_mix_w, w_in, conv_ssd_w, conv_ssd_b, dt_bias,
                   a_log, d_skip, ssd_norm_w, conv_mlstm_w, conv_mlstm_b, w_q, w_k, w_v, w_igate, b_igate,
                   w_fgate, b_fgate, mlstm_norm_w, w_out, norm_ffn_w, w_group, b_group, w_router, b_router,
                   w_gate, w_up, w_down, final_norm_w)
```

```python
import functools
from typing import NamedTuple

import jax
import jax.numpy as jnp
from jax import lax
from jax.experimental import pallas as pl
from jax.experimental.pallas import tpu as pltpu

F32 = jnp.float32
BF16 = jnp.bfloat16
EPS = 1e-6
LANES = 128
SUBLANES = 8
VMEM_LIMIT = 52 * 1024 * 1024
HI = lax.Precision.HIGHEST


class Cfg(NamedTuple):
    d_model: int = 2048
    ssd_heads: int = 32
    ssd_head_dim: int = 64
    ssd_groups: int = 4
    ssd_state: int = 128
    ml_heads: int = 8
    ml_head_dim: int = 256
    n_groups: int = 4
    experts_per_group: int = 8
    d_expert: int = 512
    n_meta: int = 16
    conv_w: int = 4
    chunk: int = 128

    @property
    def bc_dim(self):
        return self.ssd_groups * self.ssd_state

    @property
    def xbc_dim(self):
        return self.d_model + 2 * self.bc_dim

    @property
    def n_experts(self):
        return self.n_groups * self.experts_per_group


def _cparams(sem):
    return pltpu.CompilerParams(dimension_semantics=sem, vmem_limit_bytes=VMEM_LIMIT)


def _softplus(x):
    return jnp.maximum(x, 0.0) + jnp.log1p(jnp.exp(-jnp.abs(x)))


def _sigmoid(x):
    return 1.0 / (1.0 + jnp.exp(-x))


def _silu(x):
    return x * _sigmoid(x)


def _rms(x, w):
    return x * lax.rsqrt(jnp.mean(x * x, axis=-1, keepdims=True) + EPS) * w


def _prenorm_kernel(x_ref, nw_ref, wdt_ref, wdtT_ref, bdt_row_ref, bdt_col_ref, h_ref, d_ref, dT_ref):
    hb = _rms(x_ref[...], nw_ref[...]).astype(BF16)
    h_ref[...] = hb
    dt = jnp.dot(hb, wdt_ref[...], preferred_element_type=F32)
    d_ref[...] = _softplus(dt + bdt_row_ref[...])
    dtT = lax.dot_general(wdtT_ref[...], hb, (((1,), (1,)), ((), ())), preferred_element_type=F32)
    dT_ref[...] = _softplus(dtT + bdt_col_ref[...])


def _prenorm_pair_kernel(n_a, xa_ref, xb_ref, nw_ref, wdt_ref, wdtT_ref, bdt_row_ref, bdt_col_ref,
                         h_ref, d_ref, dT_ref, x_ref):
    i = pl.program_id(0)

    @pl.when(i < n_a)
    def _():
        x_ref[...] = xa_ref[...]

    @pl.when(i >= n_a)
    def _():
        x_ref[...] = xb_ref[...]

    _prenorm_kernel(x_ref, nw_ref, wdt_ref, wdtT_ref, bdt_row_ref, bdt_col_ref, h_ref, d_ref, dT_ref)


def pre_norm(xa, xb, norm_w, w_dt, w_dtT, bdt_row, bdt_col, bm):
    d = xa.shape[1]
    n_a = xa.shape[0] // bm
    n_b = 0 if xb is None else xb.shape[0] // bm
    m = (n_a + n_b) * bm
    const2 = lambda i: (0, 0)
    w_specs = [pl.BlockSpec((1, d), const2), pl.BlockSpec((d, LANES), const2), pl.BlockSpec((LANES, d), const2),
               pl.BlockSpec((1, LANES), const2), pl.BlockSpec((LANES, 1), const2)]
    if xb is None:
        body, x_specs, xs, scratch = _prenorm_kernel, [pl.BlockSpec((bm, d), lambda i: (i, 0))], (xa,), []
    else:
        body = functools.partial(_prenorm_pair_kernel, n_a)
        x_specs = [pl.BlockSpec((bm, d), lambda i: (jnp.minimum(i, n_a - 1), 0)),
                   pl.BlockSpec((bm, d), lambda i: (jnp.maximum(i - n_a, 0), 0))]
        xs, scratch = (xa, xb), [pltpu.VMEM((bm, d), F32)]
    return pl.pallas_call(
        body,
        out_shape=(jax.ShapeDtypeStruct((m, d), BF16), jax.ShapeDtypeStruct((m, LANES), F32),
                   jax.ShapeDtypeStruct((LANES, m), F32)),
        grid=(n_a + n_b,),
        in_specs=x_specs + w_specs,
        out_specs=(pl.BlockSpec((bm, d), lambda i: (i, 0)), pl.BlockSpec((bm, LANES), lambda i: (i, 0)),
                   pl.BlockSpec((LANES, bm), lambda i: (0, i))),
        scratch_shapes=scratch,
        compiler_params=_cparams(("arbitrary",)),
        name="pre_norm",
    )(*xs, norm_w, w_dt, w_dtT, bdt_row, bdt_col)


def _inproj_kernel(h_ref, hs_ref, wt_ref, proj_ref, projs_ref, w_ref):
    nt_dims = (((1,), (1,)), ((), ()))

    @pl.when(pl.program_id(1) == 0)
    def _():
        w_ref[...] = wt_ref[...].astype(BF16)
        projs_ref[...] = lax.dot_general(hs_ref[...], w_ref[...], nt_dims, preferred_element_type=F32)

    proj_ref[...] = lax.dot_general(h_ref[...], w_ref[...], nt_dims, preferred_element_type=F32)


def in_proj(cfg, h, h_small, w_in_t, bm, bn):
    m, d = h.shape
    ms = h_small.shape[0]
    nz = d // bn
    nx = cfg.xbc_dim // bn
    n_a = nz + nx
    n_blocks = n_a + 2 * nz
    skip = cfg.ssd_heads
    assert skip % SUBLANES == 0
    w_row = lambda j: pl.multiple_of(jnp.where(j < n_a, j * bn, j * bn + skip), SUBLANES)
    out_col = lambda j: jnp.where(j < nz, j, jnp.where(j < n_a, j + 2 * nz, j - nx))
    return pl.pallas_call(
        _inproj_kernel,
        out_shape=(jax.ShapeDtypeStruct((m, n_blocks * bn), F32), jax.ShapeDtypeStruct((ms, n_blocks * bn), F32)),
        grid=(n_blocks, m // bm),
        in_specs=[pl.BlockSpec((bm, d), lambda j, i: (i, 0)),
                  pl.BlockSpec((ms, d), lambda j, i: (0, 0)),
                  pl.BlockSpec((pl.Element(bn), pl.Element(d)), lambda j, i: (w_row(j), 0))],
        out_specs=(pl.BlockSpec((bm, bn), lambda j, i: (i, out_col(j))),
                   pl.BlockSpec((ms, bn), lambda j, i: (0, out_col(j)))),
        scratch_shapes=[pltpu.VMEM((bn, d), BF16)],
        compiler_params=_cparams(("arbitrary", "arbitrary")),
        name="in_proj",
    )(h, h_small, w_in_t)


CONV_LANES = 256


def _causal_conv(u, prev, w_ref, b_ref, cols, conv_w):
    lt = u.shape[0]
    row8 = lax.broadcasted_iota(jnp.int32, (SUBLANES, u.shape[1]), 0)
    acc = u * w_ref[conv_w - 1:conv_w, cols] + b_ref[:, cols]
    for s in range(1, conv_w):
        rolled = pltpu.roll(u, s, axis=0)
        head = jnp.where(row8 < s, pltpu.roll(prev, s, axis=0), rolled[0:SUBLANES])
        shifted = head if lt == SUBLANES else jnp.concatenate([head, rolled[SUBLANES:]], axis=0)
        acc = acc + shifted * w_ref[conv_w - 1 - s:conv_w - s, cols]
    return acc


def _conv_qkv_kernel(cfg, xbc_ref, xm_ref, hxbc_ref, hxm_ref, cwx_ref, cbx_ref, cwm_ref, cbm_ref,
                     wq_ref, wk_ref, wv_ref, wg_ref, wgT_ref, bg_row_ref, bg_col_ref,
                     xa_ref, q_ref, k_ref, v_ref, g_ref, gT_ref, px_ref, pm_ref):
    lt = xbc_ref.shape[0]
    hd = cfg.ml_head_dim
    nh = cfg.ml_heads

    @pl.when(pl.program_id(1) == 0)
    def _():
        px_ref[...] = hxbc_ref[0]
        pm_ref[...] = hxm_ref[0]

    for c0 in range(0, xbc_ref.shape[1], CONV_LANES):
        cols = slice(c0, min(c0 + CONV_LANES, xbc_ref.shape[1]))
        u = xbc_ref[:, cols]
        xa_ref[:, cols] = _silu(_causal_conv(u, px_ref[:, cols], cwx_ref, cbx_ref, cols, cfg.conv_w)).astype(BF16)
        px_ref[:, cols] = u[lt - SUBLANES:lt]

    d = nh * hd
    kscale = hd ** -0.5
    nt = (((1,), (1,)), ((), ()))
    gcol = jnp.zeros(g_ref.shape, F32) + bg_row_ref[...]
    grow = jnp.zeros(gT_ref.shape[2:], F32) + bg_col_ref[...]
    for h in range(nh):
        sl = slice(h * hd, (h + 1) * hd)
        xm = xm_ref[:, sl]
        xc = _silu(_causal_conv(xm, pm_ref[:, sl], cwm_ref, cbm_ref, sl, cfg.conv_w)).astype(BF16)
        pm_ref[:, sl] = xm[lt - SUBLANES:lt]
        qh = jnp.dot(xc, wq_ref[h], preferred_element_type=F32)
        kh = jnp.dot(xc, wk_ref[h], preferred_element_type=F32) * kscale
        vh = jnp.dot(xm.astype(BF16), wv_ref[h], preferred_element_type=F32)
        q_ref[:, sl] = qh.astype(BF16)
        k_ref[:, sl] = kh.astype(BF16)
        v_ref[:, sl] = vh.astype(BF16)
        for part, val in enumerate((qh, kh, vh)):
            vb = val.astype(BF16)
            rows = slice(part * d + h * hd, part * d + (h + 1) * hd)
            gcol = gcol + jnp.dot(vb, wg_ref[rows, :], preferred_element_type=F32)
            grow = grow + lax.dot_general(wgT_ref[:, rows], vb, nt, preferred_element_type=F32)
    lane = lax.broadcasted_iota(jnp.int32, gcol.shape, 1)
    g_ref[...] = jnp.where(lane < nh, gcol, -_softplus(-gcol))
    row = lax.broadcasted_iota(jnp.int32, grow.shape, 0)
    gT_ref[0, 0] = jnp.where(row < nh, grow, -_softplus(-grow))


def conv_qkv(cfg, proj, hist_xbc, hist_xm, cwx, cbx, cwm, cbm, wq, wk, wv, wg, wgT, bg_row, bg_col,
             row0, n_seq, seq_len, lt):
    d = cfg.d_model
    xbc = cfg.xbc_dim
    m = n_seq * seq_len
    nt = seq_len // lt
    ng = 2 * cfg.ml_heads
    xbc_blk = (3 * d) // xbc
    blk0 = row0 // lt
    row = lambda s, l: (s * nt + l, 0)
    const2 = lambda s, l: (0, 0)
    const3 = lambda s, l: (0, 0, 0)
    return pl.pallas_call(
        functools.partial(_conv_qkv_kernel, cfg),
        out_shape=(jax.ShapeDtypeStruct((m, xbc), BF16),
                   jax.ShapeDtypeStruct((m, d), BF16), jax.ShapeDtypeStruct((m, d), BF16),
                   jax.ShapeDtypeStruct((m, d), BF16),
                   jax.ShapeDtypeStruct((m, LANES), F32),
                   jax.ShapeDtypeStruct((n_seq, nt, ng, lt), F32)),
        grid=(n_seq, nt),
        in_specs=[pl.BlockSpec((lt, xbc), lambda s, l: (blk0 + s * nt + l, xbc_blk)),
                  pl.BlockSpec((lt, d), lambda s, l: (blk0 + s * nt + l, 1)),
                  pl.BlockSpec((1, SUBLANES, xbc), lambda s, l: (s, 0, 0)),
                  pl.BlockSpec((1, SUBLANES, d), lambda s, l: (s, 0, 0)),
                  pl.BlockSpec((cfg.conv_w, xbc), const2), pl.BlockSpec((1, xbc), const2),
                  pl.BlockSpec((cfg.conv_w, d), const2), pl.BlockSpec((1, d), const2),
                  pl.BlockSpec(wq.shape, const3), pl.BlockSpec(wk.shape, const3),
                  pl.BlockSpec(wv.shape, const3),
                  pl.BlockSpec(wg.shape, const2), pl.BlockSpec(wgT.shape, const2),
                  pl.BlockSpec((1, LANES), const2), pl.BlockSpec((ng, 1), const2)],
        out_specs=(pl.BlockSpec((lt, xbc), row), pl.BlockSpec((lt, d), row), pl.BlockSpec((lt, d), row),
                   pl.BlockSpec((lt, d), row), pl.BlockSpec((lt, LANES), row),
                   pl.BlockSpec((1, 1, ng, lt), lambda s, l: (s, l, 0, 0))),
        scratch_shapes=[pltpu.VMEM((SUBLANES, xbc), F32), pltpu.VMEM((SUBLANES, d), F32)],
        compiler_params=_cparams(("arbitrary", "arbitrary")),
        name="conv_qkv",
    )(proj, proj, hist_xbc, hist_xm, cwx, cbx, cwm, cbm, wq, wk, wv, wg, wgT, bg_row, bg_col)


def _tri(q, lower):
    r = lax.broadcasted_iota(jnp.int32, (q, q), 0)
    c = lax.broadcasted_iota(jnp.int32, (q, q), 1)
    return (c <= r) if lower else (r <= c)


def _ssd_chunk(cfg, xa_ref, d_ref, dT_ref, arow_ref, acol_ref, dskip_ref, y_ref, st_ref):
    q = xa_ref.shape[0]
    dm = cfg.d_model
    ns = cfg.ssd_state
    hp = cfg.ssd_head_dim
    hpg = cfg.ssd_heads // cfg.ssd_groups
    heads_per_tile = LANES // hp
    n_tiles = cfg.ssd_heads // heads_per_tile

    causal = _tri(q, True)
    tril = causal.astype(F32)
    triu = _tri(q, False).astype(F32)
    dcol = d_ref[...]
    drow = dT_ref[...]
    acum = jnp.dot(tril, dcol * arow_ref[...], precision=HI, preferred_element_type=F32)
    acumT = jnp.dot(drow * acol_ref[...], triu, precision=HI, preferred_element_type=F32)
    nt_dims = (((1,), (1,)), ((), ()))
    tn_dims = (((0,), (0,)), ((), ()))
    lane = lax.broadcasted_iota(jnp.int32, (q, LANES), 1)
    srow = lax.broadcasted_iota(jnp.int32, (LANES, ns), 0)

    cbs = []
    bgs = []
    cgs = []
    for g in range(cfg.ssd_groups):
        bg = xa_ref[:, dm + g * ns: dm + (g + 1) * ns].astype(BF16)
        cg = xa_ref[:, dm + cfg.bc_dim + g * ns: dm + cfg.bc_dim + (g + 1) * ns].astype(BF16)
        cbs.append(lax.dot_general(cg, bg, nt_dims, preferred_element_type=F32))
        bgs.append(bg)
        cgs.append(cg)

    for t in range(n_tiles):
        h0 = t * heads_per_tile
        g = h0 // hpg
        cols = slice(t * LANES, (t + 1) * LANES)
        x = xa_ref[:, cols].astype(F32)
        dsel = jnp.zeros((q, LANES), F32)
        esel = jnp.zeros((q, LANES), F32)
        tsel = jnp.zeros((q, LANES), F32)
        rdec = jnp.zeros((LANES, ns), F32)
        for i in range(heads_per_tile):
            h = h0 + i
            in_head = (lane >= i * hp) & (lane < (i + 1) * hp)
            a_col = acum[:, h:h + 1]
            a_last = acum[q - 1:q, h:h + 1]
            dsel = jnp.where(in_head, dcol[:, h:h + 1], dsel)
            esel = jnp.where(in_head, jnp.exp(a_col), esel)
            tsel = jnp.where(in_head, jnp.exp(a_last - a_col), tsel)
            rdec = jnp.where((srow >= i * hp) & (srow < (i + 1) * hp), jnp.exp(a_last), rdec)
        xd = x * dsel
        y = x * dskip_ref[:, cols]
        for i in range(heads_per_tile):
            h = h0 + i
            in_head = (lane >= i * hp) & (lane < (i + 1) * hp)
            seg = jnp.where(causal, acum[:, h:h + 1] - acumT[h:h + 1, :], -jnp.inf)
            w = (cbs[g] * jnp.exp(seg)).astype(BF16)
            xdh = jnp.where(in_head, xd, 0.0).astype(BF16)
            y = y + jnp.dot(w, xdh, preferred_element_type=F32)
        s_old = st_ref[cols, :]
        ys = lax.dot_general(cgs[g], s_old.astype(BF16), nt_dims, preferred_element_type=F32)
        y_ref[:, cols] = (y + esel * ys).astype(BF16)
        upd = lax.dot_general((xd * tsel).astype(BF16), bgs[g], tn_dims, preferred_element_type=F32)
        st_ref[cols, :] = rdec * s_old + upd


def _ssd_kernel(cfg, n_chunks, xa_ref, d_ref, dT_ref, s0_ref, arow_ref, acol_ref, dskip_ref,
                y_ref, sout_ref, st_ref):
    @pl.when(pl.program_id(1) == 0)
    def _():
        st_ref[...] = s0_ref[...]

    for b in range(xa_ref.shape[0]):
        _ssd_chunk(cfg, xa_ref.at[b], d_ref.at[b], dT_ref.at[b, 0], arow_ref, acol_ref, dskip_ref,
                   y_ref.at[b], st_ref.at[b])

    @pl.when(pl.program_id(1) == n_chunks - 1)
    def _():
        sout_ref[...] = st_ref[...]


def ssd_scan(cfg, xa, d, dT, s0, a_row, a_col, dskip, n_seq, seq_len, q, nb):
    dm = cfg.d_model
    nc = seq_len // q
    blk3 = lambda s, c: (s, c, 0)
    st3 = lambda s, c: (s, 0, 0)
    const2 = lambda s, c: (0, 0)
    return pl.pallas_call(
        functools.partial(_ssd_kernel, cfg, nc),
        out_shape=(jax.ShapeDtypeStruct((n_seq, seq_len, dm), BF16),
                   jax.ShapeDtypeStruct((n_seq, dm, cfg.ssd_state), F32)),
        grid=(n_seq // nb, nc),
        in_specs=[pl.BlockSpec((nb, q, cfg.xbc_dim), blk3),
                  pl.BlockSpec((nb, q, LANES), blk3),
                  pl.BlockSpec((nb, 1, cfg.ssd_heads, q), lambda s, c: (s, c, 0, 0)),
                  pl.BlockSpec((nb, dm, cfg.ssd_state), st3),
                  pl.BlockSpec((1, LANES), const2),
                  pl.BlockSpec((cfg.ssd_heads, 1), const2),
                  pl.BlockSpec((1, dm), const2)],
        out_specs=(pl.BlockSpec((nb, q, dm), blk3),
                   pl.BlockSpec((nb, dm, cfg.ssd_state), st3)),
        scratch_shapes=[pltpu.VMEM((nb, dm, cfg.ssd_state), F32)],
        compiler_params=_cparams(("arbitrary", "arbitrary")),
        name="ssd_scan",
    )(xa, d, dT, s0, a_row, a_col, dskip)


def _mlstm_chunk(cfg, q_ref, k_ref, v_ref, g_ref, gT_ref, h_ref, c_ref, n_ref, m_ref):
    ql = q_ref.shape[0]
    hd = cfg.ml_head_dim
    nh = cfg.ml_heads

    causal = _tri(ql, True)
    gcol = g_ref[...]
    grow = gT_ref[...]
    bcum = jnp.dot(causal.astype(F32), gcol, precision=HI, preferred_element_type=F32)
    bcumT = jnp.dot(grow, _tri(ql, False).astype(F32), precision=HI, preferred_element_type=F32)
    nt_dims = (((1,), (1,)), ((), ()))
    tn_dims = (((0,), (0,)), ((), ()))

    for h in range(nh):
        sl = slice(h * hd, (h + 1) * hd)
        b_col = bcum[:, nh + h:nh + h + 1]
        b_row = bcumT[nh + h:nh + h + 1, :]
        i_col = gcol[:, h:h + 1]
        i_row = grow[h:h + 1, :]
        m_prev = m_ref[h:h + 1, 0:1]
        dlog = jnp.where(causal, b_col - b_row + i_row, -jnp.inf)
        inter = b_col + m_prev
        mt = jnp.maximum(inter, jnp.max(dlog, axis=1, keepdims=True))
        qb = q_ref[:, sl]
        kb = k_ref[:, sl]
        vb = v_ref[:, sl]
        qh = qb.astype(F32)
        kh = kb.astype(F32)
        vh = vb.astype(F32)
        s = lax.dot_general(qb, kb, nt_dims, preferred_element_type=F32) * jnp.exp(dlog - mt)
        gdec = jnp.exp(inter - mt)
        c_old = c_ref[sl, :]
        n_old = n_ref[h:h + 1, :]
        qc = lax.dot_general(qb, c_old.astype(BF16), nt_dims, preferred_element_type=F32)
        num = jnp.dot(s.astype(BF16), vb, preferred_element_type=F32) + gdec * qc
        den = jnp.sum(s, axis=1, keepdims=True) + gdec * jnp.sum(qh * n_old, axis=1, keepdims=True)
        h_ref[:, sl] = (num / jnp.maximum(jnp.abs(den), jnp.exp(-mt))).astype(BF16)
        m_new = mt[ql - 1:ql, :]
        gs = jnp.exp(b_col[ql - 1:ql, :] - b_col + i_col - m_new)
        gc = jnp.exp(inter[ql - 1:ql, :] - m_new)
        upd = lax.dot_general((vh * gs).astype(BF16), kb, tn_dims, preferred_element_type=F32)
        c_ref[sl, :] = gc * c_old + upd
        n_ref[h:h + 1, :] = gc * n_old + jnp.sum(gs * kh, axis=0, keepdims=True)
        m_ref[h:h + 1, :] = jnp.broadcast_to(m_new, (1, LANES))


def _mlstm_kernel(cfg, n_chunks, q_ref, k_ref, v_ref, g_ref, gT_ref, c0_ref, n0_ref, m0_ref,
                  h_ref, cout_ref, nout_ref, mout_ref, c_ref, n_ref, m_ref):
    @pl.when(pl.program_id(1) == 0)
    def _():
        c_ref[...] = c0_ref[...]
        n_ref[...] = n0_ref[...]
        m_ref[...] = m0_ref[...]

    for b in range(q_ref.shape[0]):
        _mlstm_chunk(cfg, q_ref.at[b], k_ref.at[b], v_ref.at[b], g_ref.at[b], gT_ref.at[b, 0],
                     h_ref.at[b], c_ref.at[b], n_ref.at[b], m_ref.at[b])

    @pl.when(pl.program_id(1) == n_chunks - 1)
    def _():
        cout_ref[...] = c_ref[...]
        nout_ref[...] = n_ref[...]
        mout_ref[...] = m_ref[...]


def mlstm_scan(cfg, qa, ka, va, g, gT, c0, n0, m0, n_seq, seq_len, q, nb):
    d = cfg.d_model
    hd = cfg.ml_head_dim
    nh = cfg.ml_heads
    nc = seq_len // q
    blk3 = lambda s, c: (s, c, 0)
    st3 = lambda s, c: (s, 0, 0)
    return pl.pallas_call(
        functools.partial(_mlstm_kernel, cfg, nc),
        out_shape=(jax.ShapeDtypeStruct((n_seq, seq_len, d), BF16),
                   jax.ShapeDtypeStruct((n_seq, d, hd), F32),
                   jax.ShapeDtypeStruct((n_seq, nh, hd), F32),
                   jax.ShapeDtypeStruct((n_seq, nh, LANES), F32)),
        grid=(n_seq // nb, nc),
        in_specs=[pl.BlockSpec((nb, q, d), blk3), pl.BlockSpec((nb, q, d), blk3), pl.BlockSpec((nb, q, d), blk3),
                  pl.BlockSpec((nb, q, LANES), blk3),
                  pl.BlockSpec((nb, 1, 2 * nh, q), lambda s, c: (s, c, 0, 0)),
                  pl.BlockSpec((nb, d, hd), st3), pl.BlockSpec((nb, nh, hd), st3),
                  pl.BlockSpec((nb, nh, LANES), st3)],
        out_specs=(pl.BlockSpec((nb, q, d), blk3),
                   pl.BlockSpec((nb, d, hd), st3), pl.BlockSpec((nb, nh, hd), st3),
                   pl.BlockSpec((nb, nh, LANES), st3)),
        scratch_shapes=[pltpu.VMEM((nb, d, hd), F32), pltpu.VMEM((nb, nh, hd), F32),
                        pltpu.VMEM((nb, nh, LANES), F32)],
        compiler_params=_cparams(("arbitrary", "arbitrary")),
        name="mlstm_scan",
    )(qa, ka, va, g, gT, c0, n0, m0)


def _group_norm(x, w_ref, col0, groups, width):
    parts = []
    for g in range(groups):
        seg = x[:, g * width:(g + 1) * width]
        parts.append(seg * lax.rsqrt(jnp.mean(seg * seg, axis=-1, keepdims=True) + EPS)
                     * w_ref[:, col0 + g * width: col0 + (g + 1) * width])
    return parts


def _outproj_kernel(cfg, ys_ref, z_ref, hm_ref, o_ref, x_ref, nws_ref, nwm_ref, w_ref, out_ref):
    d = cfg.d_model
    ws = d // cfg.ssd_groups
    yz = ys_ref[...].astype(F32) * _silu(z_ref[...])
    ssd_half = jnp.concatenate([part.astype(BF16) for part in _group_norm(yz, nws_ref, 0, cfg.ssd_groups, ws)],
                               axis=-1)
    acc = x_ref[...] + jnp.dot(ssd_half, w_ref[0:d, :], preferred_element_type=F32)
    gate = _sigmoid(o_ref[...])
    wm = cfg.ml_head_dim
    ml_half = jnp.concatenate(
        [(part * gate[:, g * wm:(g + 1) * wm]).astype(BF16)
         for g, part in enumerate(_group_norm(hm_ref[...].astype(F32), nwm_ref, 0, cfg.ml_heads, wm))], axis=-1)
    out_ref[...] = acc + jnp.dot(ml_half, w_ref[d:2 * d, :], preferred_element_type=F32)


def out_proj(cfg, ys, proj, proj_row0, hm, x, nws, nwm, w_out, bm):
    m, d = x.shape
    blk0 = proj_row0 // bm
    const2 = lambda i: (0, 0)
    full = lambda i: (i, 0)
    return pl.pallas_call(
        functools.partial(_outproj_kernel, cfg),
        out_shape=jax.ShapeDtypeStruct((m, d), F32),
        grid=(m // bm,),
        in_specs=[pl.BlockSpec((bm, d), full),
                  pl.BlockSpec((bm, d), lambda i: (i + blk0, 0)),
                  pl.BlockSpec((bm, d), full),
                  pl.BlockSpec((bm, d), lambda i: (i + blk0, 2)),
                  pl.BlockSpec((bm, d), full),
                  pl.BlockSpec((1, d), const2), pl.BlockSpec((1, d), const2),
                  pl.BlockSpec((2 * d, d), const2, pipeline_mode=pl.Buffered(1))],
        out_specs=pl.BlockSpec((bm, d), full),
        compiler_params=_cparams(("arbitrary",)),
        name="out_proj",
    )(ys, proj, hm, proj, x, nws, nwm, w_out)


def _router_kernel(cfg, n_a, xa_ref, xb_ref, nw_ref, wr_ref, br_ref, ei_ref, wt_ref, cnt_out_ref, rows_ref,
                   cnt_ref, x_ref):
    ne = cfg.n_experts
    epg = cfg.experts_per_group
    ngr = cfg.n_groups
    bm = x_ref.shape[0]
    i = pl.program_id(0)

    @pl.when(i == 0)
    def _():
        cnt_ref[...] = jnp.zeros_like(cnt_ref)

    @pl.when(i < n_a)
    def _():
        x_ref[...] = xa_ref[...]

    @pl.when(i >= n_a)
    def _():
        x_ref[...] = xb_ref[...]

    hb = _rms(x_ref[...], nw_ref[...]).astype(BF16)
    rows_ref[...] = _lanes_to_rows(_pack_bf16_pairs(hb))
    logits = jnp.dot(hb, wr_ref[...], preferred_element_type=F32) + br_ref[...]
    lane = lax.broadcasted_iota(jnp.int32, logits.shape, 1)
    big = jnp.int32(2 ** 30)
    neg = -jnp.inf

    def first_argmax(vals):
        mx = jnp.max(vals, axis=-1, keepdims=True)
        idx = jnp.min(jnp.where(vals == mx, lane, big), axis=-1, keepdims=True)
        return mx, idx

    is_group = (lane >= ne) & (lane < ne + ngr)
    gl = jnp.where(is_group, logits, neg)
    gmax, gidx = first_argmax(gl)
    p_g = 1.0 / jnp.sum(jnp.exp(gl - gmax), axis=-1, keepdims=True)
    e_lo = (gidx - ne) * epg
    in_sel = (lane >= e_lo) & (lane < e_lo + epg)
    el = jnp.where(in_sel, logits, neg)
    pe = jnp.exp(el - jnp.max(el, axis=-1, keepdims=True))
    pe = jnp.where(in_sel, pe / jnp.sum(pe, axis=-1, keepdims=True), -1.0)
    p1, i1 = first_argmax(pe)
    p2, i2 = first_argmax(jnp.where(lane == i1, -1.0, pe))
    wsum = p1 + p2
    wt_ref[...] = jnp.where(lane == 0, p_g * p1 / wsum, jnp.where(lane == 1, p_g * p2 / wsum, 0.0))

    oh1 = jnp.where(lane == i1, 1.0, 0.0)
    oh2 = jnp.where(lane == i2, 1.0, 0.0)
    r = lax.broadcasted_iota(jnp.int32, (bm, bm), 0)
    c = lax.broadcasted_iota(jnp.int32, (bm, bm), 1)
    before = jnp.where(c < r, 1.0, 0.0).astype(BF16)
    ahead1 = jnp.dot(before, oh1.astype(BF16), preferred_element_type=F32)
    ahead2 = jnp.dot(before, oh2.astype(BF16), preferred_element_type=F32)
    cnt = cnt_ref[...]
    tot1 = jnp.sum(oh1, axis=0, keepdims=True)
    rank1 = jnp.sum(oh1 * (cnt + ahead1), axis=-1, keepdims=True)
    rank2 = jnp.sum(oh2 * (cnt + tot1 + ahead2), axis=-1, keepdims=True)
    cnt_new = cnt + tot1 + jnp.sum(oh2, axis=0, keepdims=True)
    cnt_ref[...] = cnt_new
    cnt_out_ref[...] = cnt_new
    ei_ref[...] = jnp.where(lane == 0, i1, jnp.where(lane == 1, i2, jnp.where(
        lane == 2, rank1.astype(jnp.int32), jnp.where(lane == 3, rank2.astype(jnp.int32), 0))))


def router(cfg, xa, xb, nw, wr, br, bm):
    d = xa.shape[1]
    n_a = xa.shape[0] // bm
    n_b = xb.shape[0] // bm
    m = xa.shape[0] + xb.shape[0]
    nch = d // (2 * LANES)
    const2 = lambda i: (0, 0)
    return pl.pallas_call(
        functools.partial(_router_kernel, cfg, n_a),
        out_shape=(jax.ShapeDtypeStruct((m, LANES), jnp.int32), jax.ShapeDtypeStruct((m, LANES), F32),
                   jax.ShapeDtypeStruct((1, LANES), F32), jax.ShapeDtypeStruct((m, nch, LANES), jnp.uint32)),
        grid=(n_a + n_b,),
        in_specs=[pl.BlockSpec((bm, d), lambda i: (jnp.minimum(i, n_a - 1), 0)),
                  pl.BlockSpec((bm, d), lambda i: (jnp.maximum(i - n_a, 0), 0)),
                  pl.BlockSpec((1, d), const2),
                  pl.BlockSpec((d, LANES), const2), pl.BlockSpec((1, LANES), const2)],
        out_specs=(pl.BlockSpec((bm, LANES), lambda i: (i, 0)), pl.BlockSpec((bm, LANES), lambda i: (i, 0)),
                   pl.BlockSpec((1, LANES), const2), pl.BlockSpec((bm, nch, LANES), lambda i: (i, 0, 0))),
        scratch_shapes=[pltpu.VMEM((1, LANES), F32), pltpu.VMEM((bm, d), F32)],
        compiler_params=_cparams(("arbitrary",)),
        name="router",
    )(xa, xb, nw, wr, br)


def _pack_bf16_pairs(x):
    half = x.shape[1] // 2
    bits = lambda v: lax.bitcast_convert_type(v.astype(BF16).astype(F32), jnp.uint32)
    return bits(x[:, :half]) | (bits(x[:, half:]) >> 16)


def _unpack_bf16_pairs(p):
    hi = lax.bitcast_convert_type(p & jnp.uint32(0xFFFF0000), F32)
    lo = lax.bitcast_convert_type(p << 16, F32)
    return jnp.concatenate([hi, lo], axis=-1)


def _rows_to_lanes(g):
    t = pltpu.einshape("rcl->crl", g)
    return jnp.concatenate([t[c] for c in range(t.shape[0])], axis=-1)


def _lanes_to_rows(x):
    parts = jnp.stack([x[:, c * LANES:(c + 1) * LANES] for c in range(x.shape[1] // LANES)], axis=0)
    return pltpu.einshape("crl->rcl", parts)


GATHER_GROUP = 8


def _gather_rows(idx_ref, src_hbm, dst, sem, n_groups):
    def body(g, carry):
        for u in range(GATHER_GROUP):
            r = g * GATHER_GROUP + u
            pltpu.make_async_copy(src_hbm.at[idx_ref[0, 0, r]], dst.at[r], sem).start()
        return carry
    lax.fori_loop(0, n_groups, body, 0)


def _wait_rows(src_hbm, dst, sem, n):
    pltpu.make_async_copy(src_hbm.at[pl.ds(0, n)], dst.at[pl.ds(0, n)], sem).wait()


def _moe_kernel(cfg, te_ref, nr_ref, first_ref, wslot_ref, nexte_ref, nv_ref, src_ref, srcn_ref, x_hbm,
                wg_hbm, wu_hbm, wd_hbm, ys_ref, xbuf, sem, wgf, wuf, wdf, wsem, wgb, wub, wdb):
    j = pl.program_id(0)
    n_valid = nv_ref[0]

    def weight_copies(e, slot):
        return (pltpu.make_async_copy(wg_hbm.at[e], wgf.at[slot], wsem.at[slot, 0]),
                pltpu.make_async_copy(wu_hbm.at[e], wuf.at[slot], wsem.at[slot, 1]),
                pltpu.make_async_copy(wd_hbm.at[e], wdf.at[slot], wsem.at[slot, 2]))

    @pl.when(j == 0)
    def _():
        for cp in weight_copies(te_ref[0], 0):
            cp.start()
        xbuf[...] = jnp.zeros_like(xbuf)
        _gather_rows(src_ref, x_hbm, xbuf.at[0], sem.at[0], nr_ref[0] // GATHER_GROUP)

    @pl.when(j + 1 < n_valid)
    def _():
        nslot = (j + 1) % 2
        _gather_rows(srcn_ref, x_hbm, xbuf.at[nslot], sem.at[nslot], nr_ref[j + 1] // GATHER_GROUP)

    @pl.when(j < n_valid)
    def _():
        @pl.when(first_ref[j] == 1)
        def _():
            ws = wslot_ref[j]
            for cp in weight_copies(te_ref[j], ws):
                cp.wait()

            @pl.when(nexte_ref[j] >= 0)
            def _():
                for cp in weight_copies(nexte_ref[j], 1 - ws):
                    cp.start()

            wgb[...] = wgf[ws].astype(BF16)
            wub[...] = wuf[ws].astype(BF16)
            wdb[...] = wdf[ws].astype(BF16)

        slot = j % 2
        _wait_rows(x_hbm, xbuf.at[slot], sem.at[slot], nr_ref[j])
        hb = _unpack_bf16_pairs(_rows_to_lanes(xbuf[slot])).astype(BF16)
        hid = (_silu(jnp.dot(hb, wgb[...], preferred_element_type=F32))
               * jnp.dot(hb, wub[...], preferred_element_type=F32))
        y = jnp.dot(hid.astype(BF16), wdb[...], preferred_element_type=F32)
        ys_ref[...] = _lanes_to_rows(_pack_bf16_pairs(y))

    @pl.when(j >= n_valid)
    def _():
        ys_ref[...] = jnp.zeros_like(ys_ref)


def moe_routed(cfg, tables, src, x_rows, wg, wu, wd, tm):
    n_tiles = src.shape[0]
    d = cfg.d_model
    de = cfg.d_expert
    nch = x_rows.shape[1]
    return pl.pallas_call(
        functools.partial(_moe_kernel, cfg),
        out_shape=jax.ShapeDtypeStruct((n_tiles * tm, nch, LANES), jnp.uint32),
        grid_spec=pltpu.PrefetchScalarGridSpec(
            num_scalar_prefetch=len(tables),
            grid=(n_tiles,),
            in_specs=[pl.BlockSpec((1, 1, tm), lambda j, *_: (j, 0, 0), memory_space=pltpu.SMEM),
                      pl.BlockSpec((1, 1, tm), lambda j, *_: (jnp.minimum(j + 1, n_tiles - 1), 0, 0),
                                   memory_space=pltpu.SMEM),
                      pl.BlockSpec(memory_space=pl.ANY),
                      pl.BlockSpec(memory_space=pl.ANY), pl.BlockSpec(memory_space=pl.ANY),
                      pl.BlockSpec(memory_space=pl.ANY)],
            out_specs=pl.BlockSpec((tm, nch, LANES), lambda j, *_: (j, 0, 0)),
            scratch_shapes=[pltpu.VMEM((2, tm, nch, LANES), jnp.uint32), pltpu.SemaphoreType.DMA((2,)),
                            pltpu.VMEM((2, d, de), F32), pltpu.VMEM((2, d, de), F32), pltpu.VMEM((2, de, d), F32),
                            pltpu.SemaphoreType.DMA((2, 3)),
                            pltpu.VMEM((d, de), BF16), pltpu.VMEM((d, de), BF16), pltpu.VMEM((de, d), BF16)]),
        compiler_params=_cparams(("arbitrary",)),
        name="moe",
    )(*tables, src, src, x_rows, wg, wu, wd)


def _combine_kernel(n_steps, d0_ref, d1_ref, d0n_ref, d1n_ref, wt_ref, x1_ref, fw_ref, ys_hbm, y_ref, gbuf, sem):
    i = pl.program_id(0)
    groups = x1_ref.shape[0] // GATHER_GROUP

    def fetch(r0_ref, r1_ref, slot):
        _gather_rows(r0_ref, ys_hbm, gbuf.at[slot, 0], sem.at[slot, 0], groups)
        _gather_rows(r1_ref, ys_hbm, gbuf.at[slot, 1], sem.at[slot, 1], groups)

    @pl.when(i == 0)
    def _():
        fetch(d0_ref, d1_ref, 0)

    @pl.when(i + 1 < n_steps)
    def _():
        fetch(d0n_ref, d1n_ref, (i + 1) % 2)

    slot = i % 2
    bm = x1_ref.shape[0]
    _wait_rows(ys_hbm, gbuf.at[slot, 0], sem.at[slot, 0], bm)
    _wait_rows(ys_hbm, gbuf.at[slot, 1], sem.at[slot, 1], bm)
    wt = wt_ref[...]
    acc = (x1_ref[...] + wt[:, 0:1] * _unpack_bf16_pairs(_rows_to_lanes(gbuf[slot, 0]))
           + wt[:, 1:2] * _unpack_bf16_pairs(_rows_to_lanes(gbuf[slot, 1])))
    y_ref[...] = _rms(acc, fw_ref[...])


def moe_combine(cfg, dest0, dest1, wt, wt_blk0, x1, fw, ys, bm):
    m, d = x1.shape
    nch = ys.shape[1]
    n = m // bm
    cur = lambda: pl.BlockSpec((1, 1, bm), lambda i: (i, 0, 0), memory_space=pltpu.SMEM)
    nxt = lambda: pl.BlockSpec((1, 1, bm), lambda i: (jnp.minimum(i + 1, n - 1), 0, 0), memory_space=pltpu.SMEM)
    return pl.pallas_call(
        functools.partial(_combine_kernel, n),
        out_shape=jax.ShapeDtypeStruct((m, d), F32),
        grid=(n,),
        in_specs=[cur(), cur(), nxt(), nxt(),
                  pl.BlockSpec((bm, LANES), lambda i: (i + wt_blk0, 0)),
                  pl.BlockSpec((bm, d), lambda i: (i, 0)), pl.BlockSpec((1, d), lambda i: (0, 0)),
                  pl.BlockSpec(memory_space=pl.ANY)],
        out_specs=pl.BlockSpec((bm, d), lambda i: (i, 0)),
        scratch_shapes=[pltpu.VMEM((2, 2, bm, nch, LANES), jnp.uint32), pltpu.SemaphoreType.DMA((2, 2))],
        compiler_params=_cparams(("arbitrary",)),
        name="moe_combine",
    )(dest0, dest1, dest0, dest1, wt, x1, fw, ys)


def _route_tables(cfg, ei, cnt, tm, n_tiles):
    ne = cfg.n_experts
    i32 = jnp.int32
    counts = cnt[0, :ne].astype(i32)
    tiles_e = (counts + tm - 1) // tm
    tile_end = jnp.cumsum(tiles_e)
    tile_start = tile_end - tiles_e
    row_off = tile_start * tm
    tile_id = jnp.arange(n_tiles, dtype=i32)
    tile_expert = jnp.minimum(jnp.sum((tile_id[:, None] >= tile_end[None, :]).astype(i32), axis=1), ne - 1)
    used = jnp.clip(counts[tile_expert] - (tile_id - tile_start[tile_expert]) * tm, 0, tm)
    used = jnp.where(tile_id < tile_end[-1], used, 0)
    tile_rows = (used + GATHER_GROUP - 1) // GATHER_GROUP * GATHER_GROUP
    n_valid = tile_end[-1]
    prev_expert = jnp.concatenate([jnp.full((1,), -1, i32), tile_expert[:-1]])
    first = ((tile_expert != prev_expert) & (tile_id < n_valid)).astype(i32)
    wslot = (jnp.cumsum(first) - 1) % 2
    e_id = jnp.arange(ne, dtype=i32)
    later = (e_id[None, :] > e_id[:, None]) & (tiles_e[None, :] > 0)
    next_e = jnp.min(jnp.where(later, e_id[None, :], ne), axis=1)
    next_e = jnp.where(next_e < ne, next_e, -1)[tile_expert]
    tables = (tile_expert, tile_rows, first, wslot.astype(i32), next_e.astype(i32), n_valid[None].astype(i32))
    picks = ei[:, 0:4].T
    pick_off = jnp.sum(jnp.where(picks[None, 0:2] == e_id[:, None, None], row_off[:, None, None], 0), axis=0)
    dest = pick_off + picks[2:4]
    tok = jnp.broadcast_to(jnp.arange(ei.shape[0], dtype=i32)[None, :], dest.shape)
    src = jnp.zeros((n_tiles * tm,), i32).at[dest.reshape(-1)].set(
        tok.reshape(-1), unique_indices=True, mode="promise_in_bounds")
    return tables, src.reshape(n_tiles, 1, tm), dest


def _pad_hist(hist):
    return jnp.pad(hist, ((0, 0), (SUBLANES - hist.shape[1], 0), (0, 0)))


def _tile(m, pref):
    return pref if m % pref == 0 else m


def _mixer_segment(cfg, proj, row0, dcol, dT, n_seq, seq_len, hist_xbc, hist_xm, s0, c0, n0, m0, p):
    d = cfg.d_model
    m = n_seq * seq_len
    q = min(cfg.chunk, seq_len)
    nc = seq_len // q
    xa, qa, ka, va, g, gT = conv_qkv(cfg, proj, hist_xbc, hist_xm, p["cwx"], p["cbx"], p["cwm"], p["cbm"],
                                     p["wq"], p["wk"], p["wv"], p["wg"], p["wgT"], p["bg_row"], p["bg_col"],
                                     row0, n_seq, seq_len, q)
    dTc = dT[:cfg.ssd_heads].reshape(cfg.ssd_heads, n_seq, nc, q).transpose(1, 2, 0, 3)
    nb = 2 if (nc == 1 and n_seq % 2 == 0) else 1
    seq3 = lambda a: a.reshape(n_seq, seq_len, a.shape[-1])
    ys, s_new = ssd_scan(cfg, seq3(xa), seq3(dcol), dTc, s0, p["a_row"], p["a_col"], p["dskip"], n_seq, seq_len, q, nb)
    hm, c_new, n_new, m_new = mlstm_scan(cfg, seq3(qa), seq3(ka), seq3(va), seq3(g), gT, c0, n0, m0,
                                         n_seq, seq_len, q, nb)
    ys = ys.reshape(m, d)
    hm = hm.reshape(m, d)
    keep = cfg.conv_w - 1
    groups = proj.reshape(proj.shape[0] // SUBLANES, SUBLANES, proj.shape[1])
    first = (row0 + seq_len) // SUBLANES - 1
    step = seq_len // SUBLANES
    tail = lax.slice(groups, (first, SUBLANES - keep, 0),
                     (first + (n_seq - 1) * step + 1, SUBLANES, proj.shape[1]), (step, 1, 1))
    tail_xbc = tail[:, :, 3 * d:]
    tail_xm = tail[:, :, d:2 * d]
    return ys, hm, (tail_xbc, s_new, tail_xm, c_new, n_new, m_new)


MOE_TILE = 256


def _ffn(cfg, segments, p):
    d = cfg.d_model
    x1s = [out_proj(cfg, ys, proj, row0, hm, x2d, p["ssd_norm_w"], p["mlstm_norm_w"], p["w_out"],
                    _tile(x2d.shape[0], 256)) for x2d, proj, row0, ys, hm in segments]
    bm = 256 if all(x1.shape[0] % 256 == 0 for x1 in x1s) else 128
    ei, wt, cnt, x_rows = router(cfg, x1s[0], x1s[1], p["norm_ffn_w"], p["wr"], p["br"], bm)
    n_tok = ei.shape[0]
    n_tiles = (2 * n_tok + cfg.n_experts * (MOE_TILE - 1)) // MOE_TILE
    tables, src, dest = _route_tables(cfg, ei, cnt, MOE_TILE, n_tiles)
    ys_sorted = moe_routed(cfg, tables, src, x_rows, p["w_gate"], p["w_up"], p["w_down"], MOE_TILE)
    outs = []
    off = 0
    for x1 in x1s:
        m = x1.shape[0]
        dseg = dest[:, off:off + m].reshape(2, m // bm, 1, bm)
        outs.append(moe_combine(cfg, dseg[0], dseg[1], wt, off // bm, x1, p["final_norm_w"], ys_sorted, bm))
        off += m
    return outs


def _prep_params(cfg, norm_mix_w, w_in, conv_ssd_w, conv_ssd_b, dt_bias, a_log, d_skip, ssd_norm_w,
                 conv_mlstm_w, conv_mlstm_b, w_q, w_k, w_v, w_igate, b_igate, w_fgate, b_fgate, mlstm_norm_w,
                 w_out, norm_ffn_w, w_group, b_group, w_router, b_router, w_gate, w_up, w_down, final_norm_w):
    d = cfg.d_model
    hs = cfg.ssd_heads
    nh = cfg.ml_heads
    o_xbc = d + cfg.xbc_dim
    row = lambda v: v.reshape(1, -1).astype(F32)
    pad_lanes = lambda a: jnp.pad(a, ((0, 0), (0, LANES - a.shape[1])))
    w_in_t = w_in.T.astype(F32)
    w_dt_t = jnp.pad(w_in_t[o_xbc:o_xbc + hs], ((0, LANES - hs), (0, 0)))
    a = -jnp.exp(a_log.astype(F32))
    w_gates = jnp.concatenate([w_igate, w_fgate], axis=1)
    b_gates = jnp.concatenate([b_igate, b_fgate]).astype(F32)
    ne = cfg.n_experts
    wr = pad_lanes(jnp.concatenate([w_router, w_group], axis=1))
    br = pad_lanes(jnp.concatenate([b_router, b_group]).reshape(1, -1).astype(F32))
    return dict(
        norm_mix_w=row(norm_mix_w),
        w_in_t=w_in_t, w_dt=w_dt_t.T.astype(BF16), w_dtT=w_dt_t.astype(BF16),
        bdt_row=pad_lanes(row(dt_bias)), bdt_col=pad_lanes(row(dt_bias)).T,
        cwx=conv_ssd_w.astype(F32), cbx=row(conv_ssd_b), cwm=conv_mlstm_w.astype(F32), cbm=row(conv_mlstm_b),
        wq=w_q.astype(BF16), wk=w_k.astype(BF16), wv=w_v.astype(BF16),
        wg=pad_lanes(w_gates).astype(BF16), wgT=w_gates.T.astype(BF16),
        bg_row=pad_lanes(row(b_gates)), bg_col=b_gates.reshape(-1, 1),
        a_row=pad_lanes(row(a)), a_col=a.reshape(-1, 1),
        dskip=row(jnp.repeat(d_skip.astype(F32), cfg.ssd_head_dim)),
        ssd_norm_w=row(ssd_norm_w), mlstm_norm_w=row(mlstm_norm_w), w_out=w_out.astype(BF16),
        norm_ffn_w=row(norm_ffn_w), wr=wr.astype(BF16), br=br,
        w_gate=w_gate.astype(F32), w_up=w_up.astype(F32), w_down=w_down.astype(F32),
        final_norm_w=row(final_norm_w),
    )


def forward(cfg, x_prompt, x_sample, state_ssd_conv, state_ssd, state_mlstm_conv, state_mlstm_c,
            state_mlstm_n, state_mlstm_m, meta_tokens, *weights):
    d = cfg.d_model
    nh = cfg.ml_heads
    hd = cfg.ml_head_dim
    assert state_ssd.shape[0] == 1, "single-layer kernel"
    p = _prep_params(cfg, *[w[0] for w in weights[:-1]], weights[-1])
    bp, lp, _ = x_prompt.shape
    bs, ls, _ = x_sample.shape
    n_meta = meta_tokens.shape[0]

    xp = x_prompt.reshape(bp * lp, d)
    xs = x_sample.reshape(bs * ls, d)
    mp, ms = bp * lp, bs * ls
    norm_args = (p["norm_mix_w"], p["w_dt"], p["w_dtT"], p["bdt_row"], p["bdt_col"])
    h_meta, d_meta, dT_meta = pre_norm(meta_tokens.astype(F32), None, *norm_args, n_meta)
    h, dcol, dT = pre_norm(xp, xs, *norm_args, 512 if mp % 512 == 0 and ms % 512 == 0 else 128)
    proj, proj_meta = in_proj(cfg, h, h_meta, p["w_in_t"], _tile(mp + ms, 1024), d // 2)

    zeros = lambda *s: jnp.zeros(s, F32)
    _, _, st_meta = _mixer_segment(
        cfg, proj_meta, 0, d_meta, dT_meta, 1, n_meta, zeros(1, SUBLANES, cfg.xbc_dim), zeros(1, SUBLANES, d),
        zeros(1, d, cfg.ssd_state), zeros(1, d, hd), zeros(1, nh, hd), zeros(1, nh, LANES), p)
    mt_xbc, mt_s, mt_xm, mt_c, mt_n, mt_m = st_meta
    rep = lambda a: jnp.broadcast_to(a, (bp,) + a.shape[1:])

    ys_p, hm_p, st_p = _mixer_segment(
        cfg, proj, 0, dcol[:mp], dT[:, :mp], bp, lp, rep(_pad_hist(mt_xbc)), rep(_pad_hist(mt_xm)),
        rep(mt_s), rep(mt_c), rep(mt_n), rep(mt_m), p)

    m0 = jnp.broadcast_to(state_mlstm_m[0].astype(F32)[:, :, None], (bs, nh, LANES))
    ys_s, hm_s, st_s = _mixer_segment(
        cfg, proj, mp, dcol[mp:], dT[:, mp:], bs, ls, _pad_hist(state_ssd_conv[0]),
        _pad_hist(state_mlstm_conv[0]), state_ssd[0].reshape(bs, d, cfg.ssd_state),
        state_mlstm_c[0].reshape(bs, d, hd), state_mlstm_n[0], m0, p)
    y_p, y_s = _ffn(cfg, [(xp, proj, 0, ys_p, hm_p), (xs, proj, mp, ys_s, hm_s)], p)
    y_prompt = y_p.reshape(bp, lp, d)
    y_sample = y_s.reshape(bs, ls, d)

    def pack(st, b):
        t_xbc, s_new, t_xm, c_new, n_new, m_new = st
        return (t_xbc[None], s_new.reshape(1, b, cfg.ssd_heads, cfg.ssd_head_dim, cfg.ssd_state),
                t_xm[None], c_new.reshape(1, b, nh, hd, hd), n_new[None], m_new[None, :, :, 0])

    return (y_prompt, y_sample) + pack(st_p, bp) + pack(st_s, bs)


def kernel(x_prompt, x_sample, state_ssd_conv, state_ssd, state_mlstm_conv, state_mlstm_c, state_mlstm_n, state_mlstm_m, meta_tokens, norm_mix_w, w_in, conv_ssd_w, conv_ssd_b, dt_bias, a_log, d_skip, ssd_norm_w, conv_mlstm_w, conv_mlstm_b, w_q, w_k, w_v, w_igate, b_igate, w_fgate, b_fgate, mlstm_norm_w, w_out, norm_ffn_w, w_group, b_group, w_router, b_router, w_gate, w_up, w_down, final_norm_w):
    return forward(Cfg(), x_prompt, x_sample, state_ssd_conv, state_ssd, state_mlstm_conv, state_mlstm_c,
                   state_mlstm_n, state_mlstm_m, meta_tokens, norm_mix_w, w_in, conv_ssd_w, conv_ssd_b, dt_bias,
                   a_log, d_skip, ssd_norm_w, conv_mlstm_w, conv_mlstm_b, w_q, w_k, w_v, w_igate, b_igate,
                   w_fgate, b_fgate, mlstm_norm_w, w_out, norm_ffn_w, w_group, b_group, w_router, b_router,
                   w_gate, w_up, w_down, final_norm_w)
```

```python
import functools
from typing import NamedTuple

import jax
import jax.numpy as jnp
from jax import lax
from jax.experimental import pallas as pl
from jax.experimental.pallas import tpu as pltpu

F32 = jnp.float32
BF16 = jnp.bfloat16
EPS = 1e-6
LANES = 128
SUBLANES = 8
VMEM_LIMIT = 56 * 1024 * 1024
HI = lax.Precision.HIGHEST


class Cfg(NamedTuple):
    d_model: int = 2048
    ssd_heads: int = 32
    ssd_head_dim: int = 64
    ssd_groups: int = 4
    ssd_state: int = 128
    ml_heads: int = 8
    ml_head_dim: int = 256
    n_groups: int = 4
    experts_per_group: int = 8
    d_expert: int = 512
    n_meta: int = 16
    conv_w: int = 4
    chunk: int = 128

    @property
    def bc_dim(self):
        return self.ssd_groups * self.ssd_state

    @property
    def xbc_dim(self):
        return self.d_model + 2 * self.bc_dim

    @property
    def n_experts(self):
        return self.n_groups * self.experts_per_group


def _cparams(sem):
    return pltpu.CompilerParams(dimension_semantics=sem, vmem_limit_bytes=VMEM_LIMIT)


def _softplus(x):
    return jnp.maximum(x, 0.0) + jnp.log1p(jnp.exp(-jnp.abs(x)))


def _sigmoid(x):
    return 1.0 / (1.0 + jnp.exp(-x))


def _silu(x):
    return x * _sigmoid(x)


def _rms(x, w):
    return x * lax.rsqrt(jnp.mean(x * x, axis=-1, keepdims=True) + EPS) * w


def _prenorm_kernel(x_ref, nw_ref, wdt_ref, wdtT_ref, bdt_row_ref, bdt_col_ref, h_ref, d_ref, dT_ref):
    hb = _rms(x_ref[...], nw_ref[...]).astype(BF16)
    h_ref[...] = hb
    dt = jnp.dot(hb, wdt_ref[...], preferred_element_type=F32)
    d_ref[...] = _softplus(dt + bdt_row_ref[...])
    dtT = lax.dot_general(wdtT_ref[...], hb, (((1,), (1,)), ((), ())), preferred_element_type=F32)
    dT_ref[...] = _softplus(dtT + bdt_col_ref[...])


def _prenorm_pair_kernel(n_a, xa_ref, xb_ref, nw_ref, wdt_ref, wdtT_ref, bdt_row_ref, bdt_col_ref,
                         h_ref, d_ref, dT_ref, x_ref):
    i = pl.program_id(0)

    @pl.when(i < n_a)
    def _():
        x_ref[...] = xa_ref[...]

    @pl.when(i >= n_a)
    def _():
        x_ref[...] = xb_ref[...]

    _prenorm_kernel(x_ref, nw_ref, wdt_ref, wdtT_ref, bdt_row_ref, bdt_col_ref, h_ref, d_ref, dT_ref)


def pre_norm(xa, xb, norm_w, w_dt, w_dtT, bdt_row, bdt_col, bm):
    d = xa.shape[1]
    n_a = xa.shape[0] // bm
    n_b = 0 if xb is None else xb.shape[0] // bm
    m = (n_a + n_b) * bm
    const2 = lambda i: (0, 0)
    w_specs = [pl.BlockSpec((1, d), const2), pl.BlockSpec((d, LANES), const2), pl.BlockSpec((LANES, d), const2),
               pl.BlockSpec((1, LANES), const2), pl.BlockSpec((LANES, 1), const2)]
    if xb is None:
        body, x_specs, xs, scratch = _prenorm_kernel, [pl.BlockSpec((bm, d), lambda i: (i, 0))], (xa,), []
    else:
        body = functools.partial(_prenorm_pair_kernel, n_a)
        x_specs = [pl.BlockSpec((bm, d), lambda i: (jnp.minimum(i, n_a - 1), 0)),
                   pl.BlockSpec((bm, d), lambda i: (jnp.maximum(i - n_a, 0), 0))]
        xs, scratch = (xa, xb), [pltpu.VMEM((bm, d), F32)]
    return pl.pallas_call(
        body,
        out_shape=(jax.ShapeDtypeStruct((m, d), BF16), jax.ShapeDtypeStruct((m, LANES), F32),
                   jax.ShapeDtypeStruct((LANES, m), F32)),
        grid=(n_a + n_b,),
        in_specs=x_specs + w_specs,
        out_specs=(pl.BlockSpec((bm, d), lambda i: (i, 0)), pl.BlockSpec((bm, LANES), lambda i: (i, 0)),
                   pl.BlockSpec((LANES, bm), lambda i: (0, i))),
        scratch_shapes=scratch,
        compiler_params=_cparams(("arbitrary",)),
        name="pre_norm",
    )(*xs, norm_w, w_dt, w_dtT, bdt_row, bdt_col)


def _inproj_kernel(h_ref, hs_ref, wt_ref, proj_ref, projs_ref, w_ref):
    nt_dims = (((1,), (1,)), ((), ()))

    @pl.when(pl.program_id(1) == 0)
    def _():
        w_ref[...] = wt_ref[...].astype(BF16)
        projs_ref[...] = lax.dot_general(hs_ref[...], w_ref[...], nt_dims, preferred_element_type=F32)

    proj_ref[...] = lax.dot_general(h_ref[...], w_ref[...], nt_dims, preferred_element_type=F32)


def in_proj(cfg, h, h_small, w_in_t, bm, bn):
    m, d = h.shape
    ms = h_small.shape[0]
    nz = d // bn
    nx = cfg.xbc_dim // bn
    n_a = nz + nx
    n_blocks = n_a + 2 * nz
    skip = cfg.ssd_heads
    assert skip % SUBLANES == 0
    w_row = lambda j: pl.multiple_of(jnp.where(j < n_a, j * bn, j * bn + skip), SUBLANES)
    out_col = lambda j: jnp.where(j < nz, j, jnp.where(j < n_a, j + 2 * nz, j - nx))
    return pl.pallas_call(
        _inproj_kernel,
        out_shape=(jax.ShapeDtypeStruct((m, n_blocks * bn), F32), jax.ShapeDtypeStruct((ms, n_blocks * bn), F32)),
        grid=(n_blocks, m // bm),
        in_specs=[pl.BlockSpec((bm, d), lambda j, i: (i, 0)),
                  pl.BlockSpec((ms, d), lambda j, i: (0, 0)),
                  pl.BlockSpec((pl.Element(bn), pl.Element(d)), lambda j, i: (w_row(j), 0))],
        out_specs=(pl.BlockSpec((bm, bn), lambda j, i: (i, out_col(j))),
                   pl.BlockSpec((ms, bn), lambda j, i: (0, out_col(j)))),
        scratch_shapes=[pltpu.VMEM((bn, d), BF16)],
        compiler_params=_cparams(("arbitrary", "arbitrary")),
        name="in_proj",
    )(h, h_small, w_in_t)


CONV_LANES = 256


def _causal_conv(u, prev, w_ref, b_ref, cols, conv_w):
    lt = u.shape[0]
    row8 = lax.broadcasted_iota(jnp.int32, (SUBLANES, u.shape[1]), 0)
    acc = u * w_ref[conv_w - 1:conv_w, cols] + b_ref[:, cols]
    for s in range(1, conv_w):
        rolled = pltpu.roll(u, s, axis=0)
        head = jnp.where(row8 < s, pltpu.roll(prev, s, axis=0), rolled[0:SUBLANES])
        shifted = head if lt == SUBLANES else jnp.concatenate([head, rolled[SUBLANES:]], axis=0)
        acc = acc + shifted * w_ref[conv_w - 1 - s:conv_w - s, cols]
    return acc


def _conv_qkv_tile(cfg, xbc_ref, xm_ref, cwx_ref, cbx_ref, cwm_ref, cbm_ref, wq_ref, wk_ref, wv_ref, wg_ref, wgT_ref,
                   bg_row_ref, bg_col_ref, xa_ref, q_ref, k_ref, v_ref, g_ref, gT_ref, px_ref, pm_ref):
    lt = xbc_ref.shape[0]
    hd = cfg.ml_head_dim
    nh = cfg.ml_heads

    for c0 in range(0, xbc_ref.shape[1], CONV_LANES):
        cols = slice(c0, min(c0 + CONV_LANES, xbc_ref.shape[1]))
        u = xbc_ref[:, cols]
        xa_ref[:, cols] = _silu(_causal_conv(u, px_ref[:, cols], cwx_ref, cbx_ref, cols, cfg.conv_w))
        px_ref[:, cols] = u[lt - SUBLANES:lt]

    d = nh * hd
    kscale = hd ** -0.5
    nt = (((1,), (1,)), ((), ()))
    gcol = jnp.zeros(g_ref.shape, F32) + bg_row_ref[...]
    grow = jnp.zeros(gT_ref.shape, F32) + bg_col_ref[...]
    for h in range(nh):
        sl = slice(h * hd, (h + 1) * hd)
        xm = xm_ref[:, sl]
        xc = _silu(_causal_conv(xm, pm_ref[:, sl], cwm_ref, cbm_ref, sl, cfg.conv_w)).astype(BF16)
        pm_ref[:, sl] = xm[lt - SUBLANES:lt]
        qh = jnp.dot(xc, wq_ref[h], preferred_element_type=F32)
        kh = jnp.dot(xc, wk_ref[h], preferred_element_type=F32) * kscale
        vh = jnp.dot(xm.astype(BF16), wv_ref[h], preferred_element_type=F32)
        q_ref[:, sl] = qh
        k_ref[:, sl] = kh
        v_ref[:, sl] = vh
        for part, val in enumerate((qh, kh, vh)):
            vb = val.astype(BF16)
            rows = slice(part * d + h * hd, part * d + (h + 1) * hd)
            gcol = gcol + jnp.dot(vb, wg_ref[rows, :], preferred_element_type=F32)
            grow = grow + lax.dot_general(wgT_ref[:, rows], vb, nt, preferred_element_type=F32)
    lane = lax.broadcasted_iota(jnp.int32, gcol.shape, 1)
    g_ref[...] = jnp.where(lane < nh, gcol, -_softplus(-gcol))
    row = lax.broadcasted_iota(jnp.int32, grow.shape, 0)
    gT_ref[...] = jnp.where(row < nh, grow, -_softplus(-grow))


def _sub_tiles(n, fn):
    if n == 1:
        fn(0)
    else:
        def body(i, carry):
            fn(i)
            return carry
        lax.fori_loop(0, n, body, 0)


def _conv_qkv_kernel(cfg, lt, xbc_ref, xm_ref, hxbc_ref, hxm_ref, cwx_ref, cbx_ref, cwm_ref, cbm_ref,
                     wq_ref, wk_ref, wv_ref, wg_ref, wgT_ref, bg_row_ref, bg_col_ref,
                     xa_ref, q_ref, k_ref, v_ref, g_ref, gT_ref, px_ref, pm_ref):
    @pl.when(pl.program_id(1) == 0)
    def _():
        px_ref[...] = hxbc_ref[0]
        pm_ref[...] = hxm_ref[0]

    def tile(i):
        rows = pl.ds(pl.multiple_of(i * lt, lt), lt)
        _conv_qkv_tile(cfg, xbc_ref.at[rows], xm_ref.at[rows], cwx_ref, cbx_ref, cwm_ref, cbm_ref,
                       wq_ref, wk_ref, wv_ref, wg_ref, wgT_ref, bg_row_ref, bg_col_ref,
                       xa_ref.at[rows], q_ref.at[rows], k_ref.at[rows], v_ref.at[rows], g_ref.at[rows],
                       gT_ref.at[0, i], px_ref, pm_ref)

    _sub_tiles(xbc_ref.shape[0] // lt, tile)


def conv_qkv(cfg, proj, hist_xbc, hist_xm, cwx, cbx, cwm, cbm, wq, wk, wv, wg, wgT, bg_row, bg_col,
             row0, n_seq, seq_len, lt, tiles_per_step):
    d = cfg.d_model
    xbc = cfg.xbc_dim
    m = n_seq * seq_len
    rows = lt * tiles_per_step
    nt = seq_len // rows
    ng = 2 * cfg.ml_heads
    xbc_blk = (3 * d) // xbc
    blk0 = row0 // rows
    row = lambda s, l: (s * nt + l, 0)
    const2 = lambda s, l: (0, 0)
    const3 = lambda s, l: (0, 0, 0)
    return pl.pallas_call(
        functools.partial(_conv_qkv_kernel, cfg, lt),
        out_shape=(jax.ShapeDtypeStruct((m, xbc), F32),
                   jax.ShapeDtypeStruct((m, d), F32), jax.ShapeDtypeStruct((m, d), F32),
                   jax.ShapeDtypeStruct((m, d), F32),
                   jax.ShapeDtypeStruct((m, LANES), F32),
                   jax.ShapeDtypeStruct((n_seq, seq_len // lt, ng, lt), F32)),
        grid=(n_seq, nt),
        in_specs=[pl.BlockSpec((rows, xbc), lambda s, l: (blk0 + s * nt + l, xbc_blk)),
                  pl.BlockSpec((rows, d), lambda s, l: (blk0 + s * nt + l, 1)),
                  pl.BlockSpec((1, SUBLANES, xbc), lambda s, l: (s, 0, 0)),
                  pl.BlockSpec((1, SUBLANES, d), lambda s, l: (s, 0, 0)),
                  pl.BlockSpec((cfg.conv_w, xbc), const2), pl.BlockSpec((1, xbc), const2),
                  pl.BlockSpec((cfg.conv_w, d), const2), pl.BlockSpec((1, d), const2),
                  pl.BlockSpec(wq.shape, const3), pl.BlockSpec(wk.shape, const3),
                  pl.BlockSpec(wv.shape, const3),
                  pl.BlockSpec(wg.shape, const2), pl.BlockSpec(wgT.shape, const2),
                  pl.BlockSpec((1, LANES), const2), pl.BlockSpec((ng, 1), const2)],
        out_specs=(pl.BlockSpec((rows, xbc), row), pl.BlockSpec((rows, d), row), pl.BlockSpec((rows, d), row),
                   pl.BlockSpec((rows, d), row), pl.BlockSpec((rows, LANES), row),
                   pl.BlockSpec((1, tiles_per_step, ng, lt), lambda s, l: (s, l, 0, 0))),
        scratch_shapes=[pltpu.VMEM((SUBLANES, xbc), F32), pltpu.VMEM((SUBLANES, d), F32)],
        compiler_params=_cparams(("arbitrary", "arbitrary")),
        name="conv_qkv",
    )(proj, proj, hist_xbc, hist_xm, cwx, cbx, cwm, cbm, wq, wk, wv, wg, wgT, bg_row, bg_col)


def _tri(q, lower):
    r = lax.broadcasted_iota(jnp.int32, (q, q), 0)
    c = lax.broadcasted_iota(jnp.int32, (q, q), 1)
    return (c <= r) if lower else (r <= c)


def _ssd_chunk(cfg, xa_ref, d_ref, dT_ref, arow_ref, acol_ref, dskip_ref, y_ref, st_ref):
    q = xa_ref.shape[0]
    dm = cfg.d_model
    ns = cfg.ssd_state
    hp = cfg.ssd_head_dim
    hpg = cfg.ssd_heads // cfg.ssd_groups
    heads_per_tile = LANES // hp
    n_tiles = cfg.ssd_heads // heads_per_tile

    causal = _tri(q, True)
    tril = causal.astype(F32)
    triu = _tri(q, False).astype(F32)
    dcol = d_ref[...]
    drow = dT_ref[...]
    acum = jnp.dot(tril, dcol * arow_ref[...], precision=HI, preferred_element_type=F32)
    acumT = jnp.dot(drow * acol_ref[...], triu, precision=HI, preferred_element_type=F32)
    nt_dims = (((1,), (1,)), ((), ()))
    tn_dims = (((0,), (0,)), ((), ()))
    lane = lax.broadcasted_iota(jnp.int32, (q, LANES), 1)
    srow = lax.broadcasted_iota(jnp.int32, (LANES, ns), 0)

    cbs = []
    bgs = []
    cgs = []
    for g in range(cfg.ssd_groups):
        bg = xa_ref[:, dm + g * ns: dm + (g + 1) * ns].astype(BF16)
        cg = xa_ref[:, dm + cfg.bc_dim + g * ns: dm + cfg.bc_dim + (g + 1) * ns].astype(BF16)
        cbs.append(lax.dot_general(cg, bg, nt_dims, preferred_element_type=F32))
        bgs.append(bg)
        cgs.append(cg)

    for t in range(n_tiles):
        h0 = t * heads_per_tile
        g = h0 // hpg
        cols = slice(t * LANES, (t + 1) * LANES)
        x = xa_ref[:, cols]
        dsel = jnp.zeros((q, LANES), F32)
        esel = jnp.zeros((q, LANES), F32)
        tsel = jnp.zeros((q, LANES), F32)
        rdec = jnp.zeros((LANES, ns), F32)
        for i in range(heads_per_tile):
            h = h0 + i
            in_head = (lane >= i * hp) & (lane < (i + 1) * hp)
            a_col = acum[:, h:h + 1]
            a_last = acum[q - 1:q, h:h + 1]
            dsel = jnp.where(in_head, dcol[:, h:h + 1], dsel)
            esel = jnp.where(in_head, jnp.exp(a_col), esel)
            tsel = jnp.where(in_head, jnp.exp(a_last - a_col), tsel)
            rdec = jnp.where((srow >= i * hp) & (srow < (i + 1) * hp), jnp.exp(a_last), rdec)
        xd = x * dsel
        y = x * dskip_ref[:, cols]
        for i in range(heads_per_tile):
            h = h0 + i
            in_head = (lane >= i * hp) & (lane < (i + 1) * hp)
            seg = jnp.where(causal, acum[:, h:h + 1] - acumT[h:h + 1, :], -jnp.inf)
            w = (cbs[g] * jnp.exp(seg)).astype(BF16)
            xdh = jnp.where(in_head, xd, 0.0).astype(BF16)
            y = y + jnp.dot(w, xdh, preferred_element_type=F32)
        s_old = st_ref[cols, :]
        ys = lax.dot_general(cgs[g], s_old.astype(BF16), nt_dims, preferred_element_type=F32)
        y_ref[:, cols] = y + esel * ys
        upd = lax.dot_general((xd * tsel).astype(BF16), bgs[g], tn_dims, preferred_element_type=F32)
        st_ref[cols, :] = rdec * s_old + upd


def _ssd_kernel(cfg, n_steps, q, xa_ref, d_ref, dT_ref, s0_ref, arow_ref, acol_ref, dskip_ref,
                y_ref, sout_ref, st_ref):
    @pl.when(pl.program_id(1) == 0)
    def _():
        st_ref[...] = s0_ref[...]

    for b in range(xa_ref.shape[0]):
        def chunk(c, b=b):
            rows = pl.ds(pl.multiple_of(c * q, q), q)
            _ssd_chunk(cfg, xa_ref.at[b, rows], d_ref.at[b, rows], dT_ref.at[b, c], arow_ref, acol_ref, dskip_ref,
                       y_ref.at[b, rows], st_ref.at[b])
        _sub_tiles(xa_ref.shape[1] // q, chunk)

    @pl.when(pl.program_id(1) == n_steps - 1)
    def _():
        sout_ref[...] = st_ref[...]


def ssd_scan(cfg, xa, d, dT, s0, a_row, a_col, dskip, n_seq, seq_len, q, nb, cps):
    dm = cfg.d_model
    nc = seq_len // (q * cps)
    blk3 = lambda s, c: (s, c, 0)
    st3 = lambda s, c: (s, 0, 0)
    const2 = lambda s, c: (0, 0)
    return pl.pallas_call(
        functools.partial(_ssd_kernel, cfg, nc, q),
        out_shape=(jax.ShapeDtypeStruct((n_seq, seq_len, dm), F32),
                   jax.ShapeDtypeStruct((n_seq, dm, cfg.ssd_state), F32)),
        grid=(n_seq // nb, nc),
        in_specs=[pl.BlockSpec((nb, q * cps, cfg.xbc_dim), blk3),
                  pl.BlockSpec((nb, q * cps, LANES), blk3),
                  pl.BlockSpec((nb, cps, cfg.ssd_heads, q), lambda s, c: (s, c, 0, 0)),
                  pl.BlockSpec((nb, dm, cfg.ssd_state), st3),
                  pl.BlockSpec((1, LANES), const2),
                  pl.BlockSpec((cfg.ssd_heads, 1), const2),
                  pl.BlockSpec((1, dm), const2)],
        out_specs=(pl.BlockSpec((nb, q * cps, dm), blk3),
                   pl.BlockSpec((nb, dm, cfg.ssd_state), st3)),
        scratch_shapes=[pltpu.VMEM((nb, dm, cfg.ssd_state), F32)],
        compiler_params=_cparams(("arbitrary", "arbitrary")),
        name="ssd_scan",
    )(xa, d, dT, s0, a_row, a_col, dskip)


def _mlstm_chunk(cfg, q_ref, k_ref, v_ref, g_ref, gT_ref, h_ref, c_ref, n_ref, m_ref):
    ql = q_ref.shape[0]
    hd = cfg.ml_head_dim
    nh = cfg.ml_heads

    causal = _tri(ql, True)
    gcol = g_ref[...]
    grow = gT_ref[...]
    bcum = jnp.dot(causal.astype(F32), gcol, precision=HI, preferred_element_type=F32)
    bcumT = jnp.dot(grow, _tri(ql, False).astype(F32), precision=HI, preferred_element_type=F32)
    nt_dims = (((1,), (1,)), ((), ()))
    tn_dims = (((0,), (0,)), ((), ()))

    for h in range(nh):
        sl = slice(h * hd, (h + 1) * hd)
        b_col = bcum[:, nh + h:nh + h + 1]
        b_row = bcumT[nh + h:nh + h + 1, :]
        i_col = gcol[:, h:h + 1]
        i_row = grow[h:h + 1, :]
        m_prev = m_ref[h:h + 1, 0:1]
        dlog = jnp.where(causal, b_col - b_row + i_row, -jnp.inf)
        inter = b_col + m_prev
        mt = jnp.maximum(inter, jnp.max(dlog, axis=1, keepdims=True))
        qh = q_ref[:, sl]
        kh = k_ref[:, sl]
        vh = v_ref[:, sl]
        qb = qh.astype(BF16)
        kb = kh.astype(BF16)
        s = lax.dot_general(qb, kb, nt_dims, preferred_element_type=F32) * jnp.exp(dlog - mt)
        gdec = jnp.exp(inter - mt)
        c_old = c_ref[sl, :]
        n_old = n_ref[h:h + 1, :]
        qc = lax.dot_general(qb, c_old.astype(BF16), nt_dims, preferred_element_type=F32)
        num = jnp.dot(s.astype(BF16), vh.astype(BF16), preferred_element_type=F32) + gdec * qc
        den = jnp.sum(s, axis=1, keepdims=True) + gdec * jnp.sum(qh * n_old, axis=1, keepdims=True)
        h_ref[:, sl] = num / jnp.maximum(jnp.abs(den), jnp.exp(-mt))
        m_new = mt[ql - 1:ql, :]
        gs = jnp.exp(b_col[ql - 1:ql, :] - b_col + i_col - m_new)
        gc = jnp.exp(inter[ql - 1:ql, :] - m_new)
        upd = lax.dot_general((vh * gs).astype(BF16), kb, tn_dims, preferred_element_type=F32)
        c_ref[sl, :] = gc * c_old + upd
        n_ref[h:h + 1, :] = gc * n_old + jnp.sum(gs * kh, axis=0, keepdims=True)
        m_ref[h:h + 1, :] = jnp.broadcast_to(m_new, (1, LANES))


def _mlstm_kernel(cfg, n_steps, q, q_ref, k_ref, v_ref, g_ref, gT_ref, c0_ref, n0_ref, m0_ref,
                  h_ref, cout_ref, nout_ref, mout_ref, c_ref, n_ref, m_ref):
    @pl.when(pl.program_id(1) == 0)
    def _():
        c_ref[...] = c0_ref[...]
        n_ref[...] = n0_ref[...]
        m_ref[...] = m0_ref[...]

    for b in range(q_ref.shape[0]):
        def chunk(c, b=b):
            rows = pl.ds(pl.multiple_of(c * q, q), q)
            _mlstm_chunk(cfg, q_ref.at[b, rows], k_ref.at[b, rows], v_ref.at[b, rows], g_ref.at[b, rows],
                         gT_ref.at[b, c], h_ref.at[b, rows], c_ref.at[b], n_ref.at[b], m_ref.at[b])
        _sub_tiles(q_ref.shape[1] // q, chunk)

    @pl.when(pl.program_id(1) == n_steps - 1)
    def _():
        cout_ref[...] = c_ref[...]
        nout_ref[...] = n_ref[...]
        mout_ref[...] = m_ref[...]


def mlstm_scan(cfg, qa, ka, va, g, gT, c0, n0, m0, n_seq, seq_len, q, nb, cps):
    d = cfg.d_model
    hd = cfg.ml_head_dim
    nh = cfg.ml_heads
    nc = seq_len // (q * cps)
    rows = q * cps
    blk3 = lambda s, c: (s, c, 0)
    st3 = lambda s, c: (s, 0, 0)
    return pl.pallas_call(
        functools.partial(_mlstm_kernel, cfg, nc, q),
        out_shape=(jax.ShapeDtypeStruct((n_seq, seq_len, d), F32),
                   jax.ShapeDtypeStruct((n_seq, d, hd), F32),
                   jax.ShapeDtypeStruct((n_seq, nh, hd), F32),
                   jax.ShapeDtypeStruct((n_seq, nh, LANES), F32)),
        grid=(n_seq // nb, nc),
        in_specs=[pl.BlockSpec((nb, rows, d), blk3), pl.BlockSpec((nb, rows, d), blk3),
                  pl.BlockSpec((nb, rows, d), blk3),
                  pl.BlockSpec((nb, rows, LANES), blk3),
                  pl.BlockSpec((nb, cps, 2 * nh, q), lambda s, c: (s, c, 0, 0)),
                  pl.BlockSpec((nb, d, hd), st3), pl.BlockSpec((nb, nh, hd), st3),
                  pl.BlockSpec((nb, nh, LANES), st3)],
        out_specs=(pl.BlockSpec((nb, rows, d), blk3),
                   pl.BlockSpec((nb, d, hd), st3), pl.BlockSpec((nb, nh, hd), st3),
                   pl.BlockSpec((nb, nh, LANES), st3)),
        scratch_shapes=[pltpu.VMEM((nb, d, hd), F32), pltpu.VMEM((nb, nh, hd), F32),
                        pltpu.VMEM((nb, nh, LANES), F32)],
        compiler_params=_cparams(("arbitrary", "arbitrary")),
        name="mlstm_scan",
    )(qa, ka, va, g, gT, c0, n0, m0)


def _group_norm(x, w_ref, col0, groups, width):
    parts = []
    for g in range(groups):
        seg = x[:, g * width:(g + 1) * width]
        parts.append(seg * lax.rsqrt(jnp.mean(seg * seg, axis=-1, keepdims=True) + EPS)
                     * w_ref[:, col0 + g * width: col0 + (g + 1) * width])
    return parts


def _outproj_kernel(cfg, ys_ref, z_ref, hm_ref, o_ref, x_ref, nws_ref, nwm_ref, w_ref, out_ref):
    d = cfg.d_model
    ws = d // cfg.ssd_groups
    yz = ys_ref[...] * _silu(z_ref[...])
    ssd_half = jnp.concatenate([part.astype(BF16) for part in _group_norm(yz, nws_ref, 0, cfg.ssd_groups, ws)],
                               axis=-1)
    acc = x_ref[...] + jnp.dot(ssd_half, w_ref[0:d, :], preferred_element_type=F32)
    gate = _sigmoid(o_ref[...])
    wm = cfg.ml_head_dim
    ml_half = jnp.concatenate(
        [(part * gate[:, g * wm:(g + 1) * wm]).astype(BF16)
         for g, part in enumerate(_group_norm(hm_ref[...], nwm_ref, 0, cfg.ml_heads, wm))], axis=-1)
    out_ref[...] = acc + jnp.dot(ml_half, w_ref[d:2 * d, :], preferred_element_type=F32)


def out_proj(cfg, ys, proj, proj_row0, hm, x, nws, nwm, w_out, bm):
    m, d = x.shape
    blk0 = proj_row0 // bm
    const2 = lambda i: (0, 0)
    full = lambda i: (i, 0)
    return pl.pallas_call(
        functools.partial(_outproj_kernel, cfg),
        out_shape=jax.ShapeDtypeStruct((m, d), F32),
        grid=(m // bm,),
        in_specs=[pl.BlockSpec((bm, d), full),
                  pl.BlockSpec((bm, d), lambda i: (i + blk0, 0)),
                  pl.BlockSpec((bm, d), full),
                  pl.BlockSpec((bm, d), lambda i: (i + blk0, 2)),
                  pl.BlockSpec((bm, d), full),
                  pl.BlockSpec((1, d), const2), pl.BlockSpec((1, d), const2),
                  pl.BlockSpec((2 * d, d), const2, pipeline_mode=pl.Buffered(1))],
        out_specs=pl.BlockSpec((bm, d), full),
        compiler_params=_cparams(("arbitrary",)),
        name="out_proj",
    )(ys, proj, hm, proj, x, nws, nwm, w_out)


def _router_kernel(cfg, n_a, xa_ref, xb_ref, nw_ref, wr_ref, br_ref, ei_ref, wt_ref, cnt_out_ref, rows_ref,
                   cnt_ref, x_ref):
    ne = cfg.n_experts
    epg = cfg.experts_per_group
    ngr = cfg.n_groups
    bm = x_ref.shape[0]
    i = pl.program_id(0)

    @pl.when(i == 0)
    def _():
        cnt_ref[...] = jnp.zeros_like(cnt_ref)

    @pl.when(i < n_a)
    def _():
        x_ref[...] = xa_ref[...]

    @pl.when(i >= n_a)
    def _():
        x_ref[...] = xb_ref[...]

    hb = _rms(x_ref[...], nw_ref[...]).astype(BF16)
    rows_ref[...] = _lanes_to_rows(_pack_bf16_pairs(hb))
    logits = jnp.dot(hb, wr_ref[...], preferred_element_type=F32) + br_ref[...]
    lane = lax.broadcasted_iota(jnp.int32, logits.shape, 1)
    big = jnp.int32(2 ** 30)
    neg = -jnp.inf

    def first_argmax(vals):
        mx = jnp.max(vals, axis=-1, keepdims=True)
        idx = jnp.min(jnp.where(vals == mx, lane, big), axis=-1, keepdims=True)
        return mx, idx

    is_group = (lane >= ne) & (lane < ne + ngr)
    gl = jnp.where(is_group, logits, neg)
    gmax, gidx = first_argmax(gl)
    p_g = 1.0 / jnp.sum(jnp.exp(gl - gmax), axis=-1, keepdims=True)
    e_lo = (gidx - ne) * epg
    in_sel = (lane >= e_lo) & (lane < e_lo + epg)
    el = jnp.where(in_sel, logits, neg)
    pe = jnp.exp(el - jnp.max(el, axis=-1, keepdims=True))
    pe = jnp.where(in_sel, pe / jnp.sum(pe, axis=-1, keepdims=True), -1.0)
    p1, i1 = first_argmax(pe)
    p2, i2 = first_argmax(jnp.where(lane == i1, -1.0, pe))
    wsum = p1 + p2
    wt_ref[...] = jnp.where(lane == 0, p_g * p1 / wsum, jnp.where(lane == 1, p_g * p2 / wsum, 0.0))

    oh1 = jnp.where(lane == i1, 1.0, 0.0)
    oh2 = jnp.where(lane == i2, 1.0, 0.0)
    r = lax.broadcasted_iota(jnp.int32, (bm, bm), 0)
    c = lax.broadcasted_iota(jnp.int32, (bm, bm), 1)
    before = jnp.where(c < r, 1.0, 0.0).astype(BF16)
    ahead1 = jnp.dot(before, oh1.astype(BF16), preferred_element_type=F32)
    ahead2 = jnp.dot(before, oh2.astype(BF16), preferred_element_type=F32)
    cnt = cnt_ref[...]
    tot1 = jnp.sum(oh1, axis=0, keepdims=True)
    rank1 = jnp.sum(oh1 * (cnt + ahead1), axis=-1, keepdims=True)
    rank2 = jnp.sum(oh2 * (cnt + tot1 + ahead2), axis=-1, keepdims=True)
    cnt_new = cnt + tot1 + jnp.sum(oh2, axis=0, keepdims=True)
    cnt_ref[...] = cnt_new
    cnt_out_ref[...] = cnt_new
    ei_ref[...] = jnp.where(lane == 0, i1, jnp.where(lane == 1, i2, jnp.where(
        lane == 2, rank1.astype(jnp.int32), jnp.where(lane == 3, rank2.astype(jnp.int32), 0))))


def router(cfg, xa, xb, nw, wr, br, bm):
    d = xa.shape[1]
    n_a = xa.shape[0] // bm
    n_b = xb.shape[0] // bm
    m = xa.shape[0] + xb.shape[0]
    nch = d // (2 * LANES)
    const2 = lambda i: (0, 0)
    return pl.pallas_call(
        functools.partial(_router_kernel, cfg, n_a),
        out_shape=(jax.ShapeDtypeStruct((m, LANES), jnp.int32), jax.ShapeDtypeStruct((m, LANES), F32),
                   jax.ShapeDtypeStruct((1, LANES), F32), jax.ShapeDtypeStruct((m, nch, LANES), jnp.uint32)),
        grid=(n_a + n_b,),
        in_specs=[pl.BlockSpec((bm, d), lambda i: (jnp.minimum(i, n_a - 1), 0)),
                  pl.BlockSpec((bm, d), lambda i: (jnp.maximum(i - n_a, 0), 0)),
                  pl.BlockSpec((1, d), const2),
                  pl.BlockSpec((d, LANES), const2), pl.BlockSpec((1, LANES), const2)],
        out_specs=(pl.BlockSpec((bm, LANES), lambda i: (i, 0)), pl.BlockSpec((bm, LANES), lambda i: (i, 0)),
                   pl.BlockSpec((1, LANES), const2), pl.BlockSpec((bm, nch, LANES), lambda i: (i, 0, 0))),
        scratch_shapes=[pltpu.VMEM((1, LANES), F32), pltpu.VMEM((bm, d), F32)],
        compiler_params=_cparams(("arbitrary",)),
        name="router",
    )(xa, xb, nw, wr, br)


def _pack_bf16_pairs(x):
    half = x.shape[1] // 2
    bits = lambda v: lax.bitcast_convert_type(v.astype(BF16).astype(F32), jnp.uint32)
    return bits(x[:, :half]) | (bits(x[:, half:]) >> 16)


def _unpack_bf16_pairs(p):
    hi = lax.bitcast_convert_type(p & jnp.uint32(0xFFFF0000), F32)
    lo = lax.bitcast_convert_type(p << 16, F32)
    return jnp.concatenate([hi, lo], axis=-1)


def _rows_to_lanes(g):
    t = pltpu.einshape("rcl->crl", g)
    return jnp.concatenate([t[c] for c in range(t.shape[0])], axis=-1)


def _lanes_to_rows(x):
    parts = jnp.stack([x[:, c * LANES:(c + 1) * LANES] for c in range(x.shape[1] // LANES)], axis=0)
    return pltpu.einshape("crl->rcl", parts)


GATHER_GROUP = 8


def _gather_rows(idx_ref, src_hbm, dst, sem, n_groups):
    def body(g, carry):
        for u in range(GATHER_GROUP):
            r = g * GATHER_GROUP + u
            pltpu.make_async_copy(src_hbm.at[idx_ref[0, 0, r]], dst.at[r], sem).start()
        return carry
    lax.fori_loop(0, n_groups, body, 0)


def _wait_rows(src_hbm, dst, sem, n):
    pltpu.make_async_copy(src_hbm.at[pl.ds(0, n)], dst.at[pl.ds(0, n)], sem).wait()


def _moe_kernel(cfg, te_ref, nr_ref, first_ref, wslot_ref, nexte_ref, nv_ref, src_ref, srcn_ref, x_hbm,
                wg_hbm, wu_hbm, wd_hbm, ys_ref, xbuf, sem, wgf, wuf, wdf, wsem, wgb, wub, wdb):
    j = pl.program_id(0)
    n_valid = nv_ref[0]

    def weight_copies(e, slot):
        return (pltpu.make_async_copy(wg_hbm.at[e], wgf.at[slot], wsem.at[slot, 0]),
                pltpu.make_async_copy(wu_hbm.at[e], wuf.at[slot], wsem.at[slot, 1]),
                pltpu.make_async_copy(wd_hbm.at[e], wdf.at[slot], wsem.at[slot, 2]))

    @pl.when(j == 0)
    def _():
        for cp in weight_copies(te_ref[0], 0):
            cp.start()
        xbuf[...] = jnp.zeros_like(xbuf)
        _gather_rows(src_ref, x_hbm, xbuf.at[0], sem.at[0], nr_ref[0] // GATHER_GROUP)

    @pl.when(j + 1 < n_valid)
    def _():
        nslot = (j + 1) % 2
        _gather_rows(srcn_ref, x_hbm, xbuf.at[nslot], sem.at[nslot], nr_ref[j + 1] // GATHER_GROUP)

    @pl.when(j < n_valid)
    def _():
        @pl.when(first_ref[j] == 1)
        def _():
            ws = wslot_ref[j]
            for cp in weight_copies(te_ref[j], ws):
                cp.wait()

            @pl.when(nexte_ref[j] >= 0)
            def _():
                for cp in weight_copies(nexte_ref[j], 1 - ws):
                    cp.start()

            wgb[...] = wgf[ws].astype(BF16)
            wub[...] = wuf[ws].astype(BF16)
            wdb[...] = wdf[ws].astype(BF16)

        slot = j % 2
        _wait_rows(x_hbm, xbuf.at[slot], sem.at[slot], nr_ref[j])
        hb = _unpack_bf16_pairs(_rows_to_lanes(xbuf[slot])).astype(BF16)
        hid = (_silu(jnp.dot(hb, wgb[...], preferred_element_type=F32))
               * jnp.dot(hb, wub[...], preferred_element_type=F32))
        y = jnp.dot(hid.astype(BF16), wdb[...], preferred_element_type=F32)
        ys_ref[...] = _lanes_to_rows(_pack_bf16_pairs(y))

    @pl.when(j >= n_valid)
    def _():
        ys_ref[...] = jnp.zeros_like(ys_ref)


def moe_routed(cfg, tables, src, x_rows, wg, wu, wd, tm):
    n_tiles = src.shape[0]
    d = cfg.d_model
    de = cfg.d_expert
    nch = x_rows.shape[1]
    return pl.pallas_call(
        functools.partial(_moe_kernel, cfg),
        out_shape=jax.ShapeDtypeStruct((n_tiles * tm, nch, LANES), jnp.uint32),
        grid_spec=pltpu.PrefetchScalarGridSpec(
            num_scalar_prefetch=len(tables),
            grid=(n_tiles,),
            in_specs=[pl.BlockSpec((1, 1, tm), lambda j, *_: (j, 0, 0), memory_space=pltpu.SMEM),
                      pl.BlockSpec((1, 1, tm), lambda j, *_: (jnp.minimum(j + 1, n_tiles - 1), 0, 0),
                                   memory_space=pltpu.SMEM),
                      pl.BlockSpec(memory_space=pl.ANY),
                      pl.BlockSpec(memory_space=pl.ANY), pl.BlockSpec(memory_space=pl.ANY),
                      pl.BlockSpec(memory_space=pl.ANY)],
            out_specs=pl.BlockSpec((tm, nch, LANES), lambda j, *_: (j, 0, 0)),
            scratch_shapes=[pltpu.VMEM((2, tm, nch, LANES), jnp.uint32), pltpu.SemaphoreType.DMA((2,)),
                            pltpu.VMEM((2, d, de), F32), pltpu.VMEM((2, d, de), F32), pltpu.VMEM((2, de, d), F32),
                            pltpu.SemaphoreType.DMA((2, 3)),
                            pltpu.VMEM((d, de), BF16), pltpu.VMEM((d, de), BF16), pltpu.VMEM((de, d), BF16)]),
        compiler_params=_cparams(("arbitrary",)),
        name="moe",
    )(*tables, src, src, x_rows, wg, wu, wd)


def _combine_kernel(n_steps, d0_ref, d1_ref, d0n_ref, d1n_ref, wt_ref, x1_ref, fw_ref, ys_hbm, y_ref, gbuf, sem):
    i = pl.program_id(0)
    groups = x1_ref.shape[0] // GATHER_GROUP

    def fetch(r0_ref, r1_ref, slot):
        _gather_rows(r0_ref, ys_hbm, gbuf.at[slot, 0], sem.at[slot, 0], groups)
        _gather_rows(r1_ref, ys_hbm, gbuf.at[slot, 1], sem.at[slot, 1], groups)

    @pl.when(i == 0)
    def _():
        fetch(d0_ref, d1_ref, 0)

    @pl.when(i + 1 < n_steps)
    def _():
        fetch(d0n_ref, d1n_ref, (i + 1) % 2)

    slot = i % 2
    bm = x1_ref.shape[0]
    _wait_rows(ys_hbm, gbuf.at[slot, 0], sem.at[slot, 0], bm)
    _wait_rows(ys_hbm, gbuf.at[slot, 1], sem.at[slot, 1], bm)
    wt = wt_ref[...]
    acc = (x1_ref[...] + wt[:, 0:1] * _unpack_bf16_pairs(_rows_to_lanes(gbuf[slot, 0]))
           + wt[:, 1:2] * _unpack_bf16_pairs(_rows_to_lanes(gbuf[slot, 1])))
    y_ref[...] = _rms(acc, fw_ref[...])


def moe_combine(cfg, dest0, dest1, wt, wt_blk0, x1, fw, ys, bm):
    m, d = x1.shape
    nch = ys.shape[1]
    n = m // bm
    cur = lambda: pl.BlockSpec((1, 1, bm), lambda i: (i, 0, 0), memory_space=pltpu.SMEM)
    nxt = lambda: pl.BlockSpec((1, 1, bm), lambda i: (jnp.minimum(i + 1, n - 1), 0, 0), memory_space=pltpu.SMEM)
    return pl.pallas_call(
        functools.partial(_combine_kernel, n),
        out_shape=jax.ShapeDtypeStruct((m, d), F32),
        grid=(n,),
        in_specs=[cur(), cur(), nxt(), nxt(),
                  pl.BlockSpec((bm, LANES), lambda i: (i + wt_blk0, 0)),
                  pl.BlockSpec((bm, d), lambda i: (i, 0)), pl.BlockSpec((1, d), lambda i: (0, 0)),
                  pl.BlockSpec(memory_space=pl.ANY)],
        out_specs=pl.BlockSpec((bm, d), lambda i: (i, 0)),
        scratch_shapes=[pltpu.VMEM((2, 2, bm, nch, LANES), jnp.uint32), pltpu.SemaphoreType.DMA((2, 2))],
        compiler_params=_cparams(("arbitrary",)),
        name="moe_combine",
    )(dest0, dest1, dest0, dest1, wt, x1, fw, ys)


def _route_tables(cfg, ei, cnt, tm, n_tiles):
    ne = cfg.n_experts
    i32 = jnp.int32
    counts = cnt[0, :ne].astype(i32)
    tiles_e = (counts + tm - 1) // tm
    tile_end = jnp.cumsum(tiles_e)
    tile_start = tile_end - tiles_e
    row_off = tile_start * tm
    tile_id = jnp.arange(n_tiles, dtype=i32)
    tile_expert = jnp.minimum(jnp.sum((tile_id[:, None] >= tile_end[None, :]).astype(i32), axis=1), ne - 1)
    used = jnp.clip(counts[tile_expert] - (tile_id - tile_start[tile_expert]) * tm, 0, tm)
    used = jnp.where(tile_id < tile_end[-1], used, 0)
    tile_rows = (used + GATHER_GROUP - 1) // GATHER_GROUP * GATHER_GROUP
    n_valid = tile_end[-1]
    prev_expert = jnp.concatenate([jnp.full((1,), -1, i32), tile_expert[:-1]])
    first = ((tile_expert != prev_expert) & (tile_id < n_valid)).astype(i32)
    wslot = (jnp.cumsum(first) - 1) % 2
    e_id = jnp.arange(ne, dtype=i32)
    later = (e_id[None, :] > e_id[:, None]) & (tiles_e[None, :] > 0)
    next_e = jnp.min(jnp.where(later, e_id[None, :], ne), axis=1)
    next_e = jnp.where(next_e < ne, next_e, -1)[tile_expert]
    tables = (tile_expert, tile_rows, first, wslot.astype(i32), next_e.astype(i32), n_valid[None].astype(i32))
    picks = ei[:, 0:4].T
    pick_off = jnp.sum(jnp.where(picks[None, 0:2] == e_id[:, None, None], row_off[:, None, None], 0), axis=0)
    dest = pick_off + picks[2:4]
    tok = jnp.broadcast_to(jnp.arange(ei.shape[0], dtype=i32)[None, :], dest.shape)
    src = jnp.zeros((n_tiles * tm,), i32).at[dest.reshape(-1)].set(
        tok.reshape(-1), unique_indices=True, mode="promise_in_bounds")
    return tables, src.reshape(n_tiles, 1, tm), dest


def _pad_hist(hist):
    return jnp.pad(hist, ((0, 0), (SUBLANES - hist.shape[1], 0), (0, 0)))


def _tile(m, pref):
    return pref if m % pref == 0 else m


CHUNKS_PER_STEP = 4
CONV_TILES_PER_STEP = 2


def _mixer_segment(cfg, proj, row0, dcol, dT, n_seq, seq_len, hist_xbc, hist_xm, s0, c0, n0, m0, p):
    d = cfg.d_model
    m = n_seq * seq_len
    q = min(cfg.chunk, seq_len)
    nc = seq_len // q
    cps = CHUNKS_PER_STEP if nc % CHUNKS_PER_STEP == 0 else 1
    xa, qa, ka, va, g, gT = conv_qkv(cfg, proj, hist_xbc, hist_xm, p["cwx"], p["cbx"], p["cwm"], p["cbm"],
                                     p["wq"], p["wk"], p["wv"], p["wg"], p["wgT"], p["bg_row"], p["bg_col"],
                                     row0, n_seq, seq_len, q,
                                     CONV_TILES_PER_STEP if nc % CONV_TILES_PER_STEP == 0 else 1)
    dTc = dT[:cfg.ssd_heads].reshape(cfg.ssd_heads, n_seq, nc, q).transpose(1, 2, 0, 3)
    nb = 2 if (nc == 1 and n_seq % 2 == 0) else 1
    seq3 = lambda a: a.reshape(n_seq, seq_len, a.shape[-1])
    ys, s_new = ssd_scan(cfg, seq3(xa), seq3(dcol), dTc, s0, p["a_row"], p["a_col"], p["dskip"],
                         n_seq, seq_len, q, nb, cps)
    hm, c_new, n_new, m_new = mlstm_scan(cfg, seq3(qa), seq3(ka), seq3(va), seq3(g), gT, c0, n0, m0,
                                         n_seq, seq_len, q, nb, cps)
    ys = ys.reshape(m, d)
    hm = hm.reshape(m, d)
    keep = cfg.conv_w - 1
    groups = proj.reshape(proj.shape[0] // SUBLANES, SUBLANES, proj.shape[1])
    first = (row0 + seq_len) // SUBLANES - 1
    step = seq_len // SUBLANES
    tail = lax.slice(groups, (first, SUBLANES - keep, 0),
                     (first + (n_seq - 1) * step + 1, SUBLANES, proj.shape[1]), (step, 1, 1))
    tail_xbc = tail[:, :, 3 * d:]
    tail_xm = tail[:, :, d:2 * d]
    return ys, hm, (tail_xbc, s_new, tail_xm, c_new, n_new, m_new)


MOE_TILE = 256


def _ffn(cfg, segments, p):
    d = cfg.d_model
    x1s = [out_proj(cfg, ys, proj, row0, hm, x2d, p["ssd_norm_w"], p["mlstm_norm_w"], p["w_out"],
                    _tile(x2d.shape[0], 256)) for x2d, proj, row0, ys, hm in segments]
    bm = 512 if all(x1.shape[0] % 512 == 0 for x1 in x1s) else 128
    ei, wt, cnt, x_rows = router(cfg, x1s[0], x1s[1], p["norm_ffn_w"], p["wr"], p["br"], bm)
    n_tok = ei.shape[0]
    n_tiles = (2 * n_tok + cfg.n_experts * (MOE_TILE - 1)) // MOE_TILE
    tables, src, dest = _route_tables(cfg, ei, cnt, MOE_TILE, n_tiles)
    ys_sorted = moe_routed(cfg, tables, src, x_rows, p["w_gate"], p["w_up"], p["w_down"], MOE_TILE)
    outs = []
    off = 0
    for x1 in x1s:
        m = x1.shape[0]
        dseg = dest[:, off:off + m].reshape(2, m // bm, 1, bm)
        outs.append(moe_combine(cfg, dseg[0], dseg[1], wt, off // bm, x1, p["final_norm_w"], ys_sorted, bm))
        off += m
    return outs


def _prep_params(cfg, norm_mix_w, w_in, conv_ssd_w, conv_ssd_b, dt_bias, a_log, d_skip, ssd_norm_w,
                 conv_mlstm_w, conv_mlstm_b, w_q, w_k, w_v, w_igate, b_igate, w_fgate, b_fgate, mlstm_norm_w,
                 w_out, norm_ffn_w, w_group, b_group, w_router, b_router, w_gate, w_up, w_down, final_norm_w):
    d = cfg.d_model
    hs = cfg.ssd_heads
    nh = cfg.ml_heads
    o_xbc = d + cfg.xbc_dim
    row = lambda v: v.reshape(1, -1).astype(F32)
    pad_lanes = lambda a: jnp.pad(a, ((0, 0), (0, LANES - a.shape[1])))
    w_in_t = w_in.T.astype(F32)
    w_dt_t = jnp.pad(w_in_t[o_xbc:o_xbc + hs], ((0, LANES - hs), (0, 0)))
    a = -jnp.exp(a_log.astype(F32))
    w_gates = jnp.concatenate([w_igate, w_fgate], axis=1)
    b_gates = jnp.concatenate([b_igate, b_fgate]).astype(F32)
    ne = cfg.n_experts
    wr = pad_lanes(jnp.concatenate([w_router, w_group], axis=1))
    br = pad_lanes(jnp.concatenate([b_router, b_group]).reshape(1, -1).astype(F32))
    return dict(
        norm_mix_w=row(norm_mix_w),
        w_in_t=w_in_t, w_dt=w_dt_t.T.astype(BF16), w_dtT=w_dt_t.astype(BF16),
        bdt_row=pad_lanes(row(dt_bias)), bdt_col=pad_lanes(row(dt_bias)).T,
        cwx=conv_ssd_w.astype(F32), cbx=row(conv_ssd_b), cwm=conv_mlstm_w.astype(F32), cbm=row(conv_mlstm_b),
        wq=w_q.astype(BF16), wk=w_k.astype(BF16), wv=w_v.astype(BF16),
        wg=pad_lanes(w_gates).astype(BF16), wgT=w_gates.T.astype(BF16),
        bg_row=pad_lanes(row(b_gates)), bg_col=b_gates.reshape(-1, 1),
        a_row=pad_lanes(row(a)), a_col=a.reshape(-1, 1),
        dskip=row(jnp.repeat(d_skip.astype(F32), cfg.ssd_head_dim)),
        ssd_norm_w=row(ssd_norm_w), mlstm_norm_w=row(mlstm_norm_w), w_out=w_out.astype(BF16),
        norm_ffn_w=row(norm_ffn_w), wr=wr.astype(BF16), br=br,
        w_gate=w_gate.astype(F32), w_up=w_up.astype(F32), w_down=w_down.astype(F32),
        final_norm_w=row(final_norm_w),
    )


def forward(cfg, x_prompt, x_sample, state_ssd_conv, state_ssd, state_mlstm_conv, state_mlstm_c,
            state_mlstm_n, state_mlstm_m, meta_tokens, *weights):
    d = cfg.d_model
    nh = cfg.ml_heads
    hd = cfg.ml_head_dim
    assert state_ssd.shape[0] == 1, "single-layer kernel"
    p = _prep_params(cfg, *[w[0] for w in weights[:-1]], weights[-1])
    bp, lp, _ = x_prompt.shape
    bs, ls, _ = x_sample.shape
    n_meta = meta_tokens.shape[0]

    xp = x_prompt.reshape(bp * lp, d)
    xs = x_sample.reshape(bs * ls, d)
    mp, ms = bp * lp, bs * ls
    norm_args = (p["norm_mix_w"], p["w_dt"], p["w_dtT"], p["bdt_row"], p["bdt_col"])
    h_meta, d_meta, dT_meta = pre_norm(meta_tokens.astype(F32), None, *norm_args, n_meta)
    h, dcol, dT = pre_norm(xp, xs, *norm_args, 512 if mp % 512 == 0 and ms % 512 == 0 else 128)
    proj, proj_meta = in_proj(cfg, h, h_meta, p["w_in_t"], _tile(mp + ms, 1536), d // 2)

    zeros = lambda *s: jnp.zeros(s, F32)
    _, _, st_meta = _mixer_segment(
        cfg, proj_meta, 0, d_meta, dT_meta, 1, n_meta, zeros(1, SUBLANES, cfg.xbc_dim), zeros(1, SUBLANES, d),
        zeros(1, d, cfg.ssd_state), zeros(1, d, hd), zeros(1, nh, hd), zeros(1, nh, LANES), p)
    mt_xbc, mt_s, mt_xm, mt_c, mt_n, mt_m = st_meta
    rep = lambda a: jnp.broadcast_to(a, (bp,) + a.shape[1:])

    ys_p, hm_p, st_p = _mixer_segment(
        cfg, proj, 0, dcol[:mp], dT[:, :mp], bp, lp, rep(_pad_hist(mt_xbc)), rep(_pad_hist(mt_xm)),
        rep(mt_s), rep(mt_c), rep(mt_n), rep(mt_m), p)

    m0 = jnp.broadcast_to(state_mlstm_m[0].astype(F32)[:, :, None], (bs, nh, LANES))
    ys_s, hm_s, st_s = _mixer_segment(
        cfg, proj, mp, dcol[mp:], dT[:, mp:], bs, ls, _pad_hist(state_ssd_conv[0]),
        _pad_hist(state_mlstm_conv[0]), state_ssd[0].reshape(bs, d, cfg.ssd_state),
        state_mlstm_c[0].reshape(bs, d, hd), state_mlstm_n[0], m0, p)
    y_p, y_s = _ffn(cfg, [(xp, proj, 0, ys_p, hm_p), (xs, proj, mp, ys_s, hm_s)], p)
    y_prompt = y_p.reshape(bp, lp, d)
    y_sample = y_s.reshape(bs, ls, d)

    def pack(st, b):
        t_xbc, s_new, t_xm, c_new, n_new, m_new = st
        return (t_xbc[None], s_new.reshape(1, b, cfg.ssd_heads, cfg.ssd_head_dim, cfg.ssd_state),
                t_xm[None], c_new.reshape(1, b, nh, hd, hd), n_new[None], m_new[None, :, :, 0])

    return (y_prompt, y_sample) + pack(st_p, bp) + pack(st_s, bs)


def kernel(x_prompt, x_sample, state_ssd_conv, state_ssd, state_mlstm_conv, state_mlstm_c, state_mlstm_n, state_mlstm_m, meta_tokens, norm_mix_w, w_in, conv_ssd_w, conv_ssd_b, dt_bias, a_log, d_skip, ssd_norm_w, conv_mlstm_w, conv_mlstm_b, w_q, w_k, w_v, w_igate, b_igate, w_fgate, b_fgate, mlstm_norm_w, w_out, norm_ffn_w, w_group, b_group, w_router, b_router, w_gate, w_up, w_down, final_norm_w):
    return forward(Cfg(), x_prompt, x_sample, state_ssd_conv, state_ssd, state_mlstm_conv, state_mlstm_c,
                   state_mlstm_n, state_mlstm_m, meta_tokens, norm_mix_w, w_in, conv_ssd_w, conv_ssd_b, dt_bias,
                   a_log, d_skip, ssd_norm_w, conv_mlstm_w, conv_mlstm_b, w_q, w_k, w_v, w_igate, b_igate,
                   w_fgate, b_fgate, mlstm_norm_w, w_out, norm_ffn_w, w_group, b_group, w_router, b_router,
                   w_gate, w_up, w_down, final_norm_w)
```

```python
import functools
from typing import NamedTuple

import jax
import jax.numpy as jnp
from jax import lax
from jax.experimental import pallas as pl
from jax.experimental.pallas import tpu as pltpu

F32 = jnp.float32
BF16 = jnp.bfloat16
EPS = 1e-6
LANES = 128
SUBLANES = 8
VMEM_LIMIT = 56 * 1024 * 1024
HI = lax.Precision.HIGHEST


class Cfg(NamedTuple):
    d_model: int = 2048
    ssd_heads: int = 32
    ssd_head_dim: int = 64
    ssd_groups: int = 4
    ssd_state: int = 128
    ml_heads: int = 8
    ml_head_dim: int = 256
    n_groups: int = 4
    experts_per_group: int = 8
    d_expert: int = 512
    n_meta: int = 16
    conv_w: int = 4
    chunk: int = 128

    @property
    def bc_dim(self):
        return self.ssd_groups * self.ssd_state

    @property
    def xbc_dim(self):
        return self.d_model + 2 * self.bc_dim

    @property
    def n_experts(self):
        return self.n_groups * self.experts_per_group


def _cparams(sem):
    return pltpu.CompilerParams(dimension_semantics=sem, vmem_limit_bytes=VMEM_LIMIT)


def _softplus(x):
    return jnp.maximum(x, 0.0) + jnp.log1p(jnp.exp(-jnp.abs(x)))


def _sigmoid(x):
    return 1.0 / (1.0 + jnp.exp(-x))


def _silu(x):
    return x * _sigmoid(x)


def _rms(x, w):
    return x * lax.rsqrt(jnp.mean(x * x, axis=-1, keepdims=True) + EPS) * w


def _prenorm_kernel(x_ref, nw_ref, wdt_ref, wdtT_ref, bdt_row_ref, bdt_col_ref, h_ref, d_ref, dT_ref):
    hb = _rms(x_ref[...], nw_ref[...]).astype(BF16)
    h_ref[...] = hb
    dt = jnp.dot(hb, wdt_ref[...], preferred_element_type=F32)
    d_ref[...] = _softplus(dt + bdt_row_ref[...])
    dtT = lax.dot_general(wdtT_ref[...], hb, (((1,), (1,)), ((), ())), preferred_element_type=F32)
    dT_ref[...] = _softplus(dtT + bdt_col_ref[...])


def _prenorm_pair_kernel(n_a, xa_ref, xb_ref, nw_ref, wdt_ref, wdtT_ref, bdt_row_ref, bdt_col_ref,
                         h_ref, d_ref, dT_ref, x_ref):
    i = pl.program_id(0)

    @pl.when(i < n_a)
    def _():
        x_ref[...] = xa_ref[...]

    @pl.when(i >= n_a)
    def _():
        x_ref[...] = xb_ref[...]

    _prenorm_kernel(x_ref, nw_ref, wdt_ref, wdtT_ref, bdt_row_ref, bdt_col_ref, h_ref, d_ref, dT_ref)


def pre_norm(xa, xb, norm_w, w_dt, w_dtT, bdt_row, bdt_col, bm):
    d = xa.shape[1]
    n_a = xa.shape[0] // bm
    n_b = 0 if xb is None else xb.shape[0] // bm
    m = (n_a + n_b) * bm
    const2 = lambda i: (0, 0)
    w_specs = [pl.BlockSpec((1, d), const2), pl.BlockSpec((d, LANES), const2), pl.BlockSpec((LANES, d), const2),
               pl.BlockSpec((1, LANES), const2), pl.BlockSpec((LANES, 1), const2)]
    if xb is None:
        body, x_specs, xs, scratch = _prenorm_kernel, [pl.BlockSpec((bm, d), lambda i: (i, 0))], (xa,), []
    else:
        body = functools.partial(_prenorm_pair_kernel, n_a)
        x_specs = [pl.BlockSpec((bm, d), lambda i: (jnp.minimum(i, n_a - 1), 0)),
                   pl.BlockSpec((bm, d), lambda i: (jnp.maximum(i - n_a, 0), 0))]
        xs, scratch = (xa, xb), [pltpu.VMEM((bm, d), F32)]
    return pl.pallas_call(
        body,
        out_shape=(jax.ShapeDtypeStruct((m, d), BF16), jax.ShapeDtypeStruct((m, LANES), F32),
                   jax.ShapeDtypeStruct((LANES, m), F32)),
        grid=(n_a + n_b,),
        in_specs=x_specs + w_specs,
        out_specs=(pl.BlockSpec((bm, d), lambda i: (i, 0)), pl.BlockSpec((bm, LANES), lambda i: (i, 0)),
                   pl.BlockSpec((LANES, bm), lambda i: (0, i))),
        scratch_shapes=scratch,
        compiler_params=_cparams(("arbitrary",)),
        name="pre_norm",
    )(*xs, norm_w, w_dt, w_dtT, bdt_row, bdt_col)


def _inproj_kernel(h_ref, hs_ref, wt_ref, proj_ref, projs_ref, w_ref):
    nt_dims = (((1,), (1,)), ((), ()))

    @pl.when(pl.program_id(1) == 0)
    def _():
        w_ref[...] = wt_ref[...].astype(BF16)
        projs_ref[...] = lax.dot_general(hs_ref[...], w_ref[...], nt_dims, preferred_element_type=F32)

    proj_ref[...] = lax.dot_general(h_ref[...], w_ref[...], nt_dims, preferred_element_type=F32)


def in_proj(cfg, h, h_small, w_in_t, bm, bn):
    m, d = h.shape
    ms = h_small.shape[0]
    nz = d // bn
    nx = cfg.xbc_dim // bn
    n_a = nz + nx
    n_blocks = n_a + 2 * nz
    skip = cfg.ssd_heads
    assert skip % SUBLANES == 0
    w_row = lambda j: pl.multiple_of(jnp.where(j < n_a, j * bn, j * bn + skip), SUBLANES)
    out_col = lambda j: jnp.where(j < nz, j, jnp.where(j < n_a, j + 2 * nz, j - nx))
    return pl.pallas_call(
        _inproj_kernel,
        out_shape=(jax.ShapeDtypeStruct((m, n_blocks * bn), F32), jax.ShapeDtypeStruct((ms, n_blocks * bn), F32)),
        grid=(n_blocks, m // bm),
        in_specs=[pl.BlockSpec((bm, d), lambda j, i: (i, 0)),
                  pl.BlockSpec((ms, d), lambda j, i: (0, 0)),
                  pl.BlockSpec((pl.Element(bn), pl.Element(d)), lambda j, i: (w_row(j), 0))],
        out_specs=(pl.BlockSpec((bm, bn), lambda j, i: (i, out_col(j))),
                   pl.BlockSpec((ms, bn), lambda j, i: (0, out_col(j)))),
        scratch_shapes=[pltpu.VMEM((bn, d), BF16)],
        compiler_params=_cparams(("arbitrary", "arbitrary")),
        name="in_proj",
    )(h, h_small, w_in_t)


CONV_LANES = 256


def _causal_conv(u, prev, w_ref, b_ref, cols, conv_w):
    lt = u.shape[0]
    row8 = lax.broadcasted_iota(jnp.int32, (SUBLANES, u.shape[1]), 0)
    acc = u * w_ref[conv_w - 1:conv_w, cols] + b_ref[:, cols]
    for s in range(1, conv_w):
        rolled = pltpu.roll(u, s, axis=0)
        head = jnp.where(row8 < s, pltpu.roll(prev, s, axis=0), rolled[0:SUBLANES])
        shifted = head if lt == SUBLANES else jnp.concatenate([head, rolled[SUBLANES:]], axis=0)
        acc = acc + shifted * w_ref[conv_w - 1 - s:conv_w - s, cols]
    return acc


def _conv_qkv_tile(cfg, xbc_ref, xm_ref, cwx_ref, cbx_ref, cwm_ref, cbm_ref, wq_ref, wk_ref, wv_ref, wg_ref, wgT_ref,
                   bg_row_ref, bg_col_ref, xa_ref, q_ref, k_ref, v_ref, g_ref, gT_ref, px_ref, pm_ref):
    lt = xbc_ref.shape[0]
    hd = cfg.ml_head_dim
    nh = cfg.ml_heads

    for c0 in range(0, xbc_ref.shape[1], CONV_LANES):
        cols = slice(c0, min(c0 + CONV_LANES, xbc_ref.shape[1]))
        u = xbc_ref[:, cols]
        xa_ref[:, cols] = _silu(_causal_conv(u, px_ref[:, cols], cwx_ref, cbx_ref, cols, cfg.conv_w))
        px_ref[:, cols] = u[lt - SUBLANES:lt]

    d = nh * hd
    kscale = hd ** -0.5
    nt = (((1,), (1,)), ((), ()))
    gcol = jnp.zeros(g_ref.shape, F32) + bg_row_ref[...]
    grow = jnp.zeros(gT_ref.shape, F32) + bg_col_ref[...]
    for h in range(nh):
        sl = slice(h * hd, (h + 1) * hd)
        xm = xm_ref[:, sl]
        xc = _silu(_causal_conv(xm, pm_ref[:, sl], cwm_ref, cbm_ref, sl, cfg.conv_w)).astype(BF16)
        pm_ref[:, sl] = xm[lt - SUBLANES:lt]
        qh = jnp.dot(xc, wq_ref[h], preferred_element_type=F32)
        kh = jnp.dot(xc, wk_ref[h], preferred_element_type=F32) * kscale
        vh = jnp.dot(xm.astype(BF16), wv_ref[h], preferred_element_type=F32)
        q_ref[:, sl] = qh
        k_ref[:, sl] = kh
        v_ref[:, sl] = vh
        for part, val in enumerate((qh, kh, vh)):
            vb = val.astype(BF16)
            rows = slice(part * d + h * hd, part * d + (h + 1) * hd)
            gcol = gcol + jnp.dot(vb, wg_ref[rows, :], preferred_element_type=F32)
            grow = grow + lax.dot_general(wgT_ref[:, rows], vb, nt, preferred_element_type=F32)
    lane = lax.broadcasted_iota(jnp.int32, gcol.shape, 1)
    g_ref[...] = jnp.where(lane < nh, gcol, -_softplus(-gcol))
    row = lax.broadcasted_iota(jnp.int32, grow.shape, 0)
    gT_ref[...] = jnp.where(row < nh, grow, -_softplus(-grow))


def _sub_tiles(n, fn):
    if n == 1:
        fn(0)
    else:
        def body(i, carry):
            fn(i)
            return carry
        lax.fori_loop(0, n, body, 0)


def _conv_qkv_kernel(cfg, lt, xbc_ref, xm_ref, hxbc_ref, hxm_ref, cwx_ref, cbx_ref, cwm_ref, cbm_ref,
                     wq_ref, wk_ref, wv_ref, wg_ref, wgT_ref, bg_row_ref, bg_col_ref,
                     xa_ref, q_ref, k_ref, v_ref, g_ref, gT_ref, px_ref, pm_ref):
    @pl.when(pl.program_id(1) == 0)
    def _():
        px_ref[...] = hxbc_ref[0]
        pm_ref[...] = hxm_ref[0]

    def tile(i):
        rows = pl.ds(pl.multiple_of(i * lt, lt), lt)
        _conv_qkv_tile(cfg, xbc_ref.at[rows], xm_ref.at[rows], cwx_ref, cbx_ref, cwm_ref, cbm_ref,
                       wq_ref, wk_ref, wv_ref, wg_ref, wgT_ref, bg_row_ref, bg_col_ref,
                       xa_ref.at[rows], q_ref.at[rows], k_ref.at[rows], v_ref.at[rows], g_ref.at[rows],
                       gT_ref.at[0, i], px_ref, pm_ref)

    _sub_tiles(xbc_ref.shape[0] // lt, tile)


def conv_qkv(cfg, proj, hist_xbc, hist_xm, cwx, cbx, cwm, cbm, wq, wk, wv, wg, wgT, bg_row, bg_col,
             row0, n_seq, seq_len, lt, tiles_per_step):
    d = cfg.d_model
    xbc = cfg.xbc_dim
    m = n_seq * seq_len
    rows = lt * tiles_per_step
    nt = seq_len // rows
    ng = 2 * cfg.ml_heads
    xbc_blk = (3 * d) // xbc
    blk0 = row0 // rows
    row = lambda s, l: (s * nt + l, 0)
    const2 = lambda s, l: (0, 0)
    const3 = lambda s, l: (0, 0, 0)
    return pl.pallas_call(
        functools.partial(_conv_qkv_kernel, cfg, lt),
        out_shape=(jax.ShapeDtypeStruct((m, xbc), F32),
                   jax.ShapeDtypeStruct((m, d), F32), jax.ShapeDtypeStruct((m, d), F32),
                   jax.ShapeDtypeStruct((m, d), F32),
                   jax.ShapeDtypeStruct((m, LANES), F32),
                   jax.ShapeDtypeStruct((n_seq, seq_len // lt, ng, lt), F32)),
        grid=(n_seq, nt),
        in_specs=[pl.BlockSpec((rows, xbc), lambda s, l: (blk0 + s * nt + l, xbc_blk)),
                  pl.BlockSpec((rows, d), lambda s, l: (blk0 + s * nt + l, 1)),
                  pl.BlockSpec((1, SUBLANES, xbc), lambda s, l: (s, 0, 0)),
                  pl.BlockSpec((1, SUBLANES, d), lambda s, l: (s, 0, 0)),
                  pl.BlockSpec((cfg.conv_w, xbc), const2), pl.BlockSpec((1, xbc), const2),
                  pl.BlockSpec((cfg.conv_w, d), const2), pl.BlockSpec((1, d), const2),
                  pl.BlockSpec(wq.shape, const3), pl.BlockSpec(wk.shape, const3),
                  pl.BlockSpec(wv.shape, const3),
                  pl.BlockSpec(wg.shape, const2), pl.BlockSpec(wgT.shape, const2),
                  pl.BlockSpec((1, LANES), const2), pl.BlockSpec((ng, 1), const2)],
        out_specs=(pl.BlockSpec((rows, xbc), row), pl.BlockSpec((rows, d), row), pl.BlockSpec((rows, d), row),
                   pl.BlockSpec((rows, d), row), pl.BlockSpec((rows, LANES), row),
                   pl.BlockSpec((1, tiles_per_step, ng, lt), lambda s, l: (s, l, 0, 0))),
        scratch_shapes=[pltpu.VMEM((SUBLANES, xbc), F32), pltpu.VMEM((SUBLANES, d), F32)],
        compiler_params=_cparams(("arbitrary", "arbitrary")),
        name="conv_qkv",
    )(proj, proj, hist_xbc, hist_xm, cwx, cbx, cwm, cbm, wq, wk, wv, wg, wgT, bg_row, bg_col)


def _tri(q, lower):
    r = lax.broadcasted_iota(jnp.int32, (q, q), 0)
    c = lax.broadcasted_iota(jnp.int32, (q, q), 1)
    return (c <= r) if lower else (r <= c)


def _ssd_chunk(cfg, xa_ref, d_ref, dT_ref, arow_ref, acol_ref, dskip_ref, y_ref, st_ref):
    q = xa_ref.shape[0]
    dm = cfg.d_model
    ns = cfg.ssd_state
    hp = cfg.ssd_head_dim
    hpg = cfg.ssd_heads // cfg.ssd_groups
    heads_per_tile = LANES // hp
    n_tiles = cfg.ssd_heads // heads_per_tile

    causal = _tri(q, True)
    tril = causal.astype(F32)
    triu = _tri(q, False).astype(F32)
    dcol = d_ref[...]
    drow = dT_ref[...]
    acum = jnp.dot(tril, dcol * arow_ref[...], precision=HI, preferred_element_type=F32)
    acumT = jnp.dot(drow * acol_ref[...], triu, precision=HI, preferred_element_type=F32)
    nt_dims = (((1,), (1,)), ((), ()))
    tn_dims = (((0,), (0,)), ((), ()))
    lane = lax.broadcasted_iota(jnp.int32, (q, LANES), 1)
    srow = lax.broadcasted_iota(jnp.int32, (LANES, ns), 0)

    cbs = []
    bgs = []
    cgs = []
    for g in range(cfg.ssd_groups):
        bg = xa_ref[:, dm + g * ns: dm + (g + 1) * ns].astype(BF16)
        cg = xa_ref[:, dm + cfg.bc_dim + g * ns: dm + cfg.bc_dim + (g + 1) * ns].astype(BF16)
        cbs.append(lax.dot_general(cg, bg, nt_dims, preferred_element_type=F32))
        bgs.append(bg)
        cgs.append(cg)

    for t in range(n_tiles):
        h0 = t * heads_per_tile
        g = h0 // hpg
        cols = slice(t * LANES, (t + 1) * LANES)
        x = xa_ref[:, cols]
        dsel = jnp.zeros((q, LANES), F32)
        esel = jnp.zeros((q, LANES), F32)
        tsel = jnp.zeros((q, LANES), F32)
        rdec = jnp.zeros((LANES, ns), F32)
        for i in range(heads_per_tile):
            h = h0 + i
            in_head = (lane >= i * hp) & (lane < (i + 1) * hp)
            a_col = acum[:, h:h + 1]
            a_last = acum[q - 1:q, h:h + 1]
            dsel = jnp.where(in_head, dcol[:, h:h + 1], dsel)
            esel = jnp.where(in_head, jnp.exp(a_col), esel)
            tsel = jnp.where(in_head, jnp.exp(a_last - a_col), tsel)
            rdec = jnp.where((srow >= i * hp) & (srow < (i + 1) * hp), jnp.exp(a_last), rdec)
        xd = x * dsel
        y = x * dskip_ref[:, cols]
        for i in range(heads_per_tile):
            h = h0 + i
            in_head = (lane >= i * hp) & (lane < (i + 1) * hp)
            seg = jnp.where(causal, acum[:, h:h + 1] - acumT[h:h + 1, :], -jnp.inf)
            w = (cbs[g] * jnp.exp(seg)).astype(BF16)
            xdh = jnp.where(in_head, xd, 0.0).astype(BF16)
            y = y + jnp.dot(w, xdh, preferred_element_type=F32)
        s_old = st_ref[cols, :]
        ys = lax.dot_general(cgs[g], s_old.astype(BF16), nt_dims, preferred_element_type=F32)
        y_ref[:, cols] = y + esel * ys
        upd = lax.dot_general((xd * tsel).astype(BF16), bgs[g], tn_dims, preferred_element_type=F32)
        st_ref[cols, :] = rdec * s_old + upd


def _ssd_kernel(cfg, n_steps, q, xa_ref, d_ref, dT_ref, s0_ref, arow_ref, acol_ref, dskip_ref,
                y_ref, sout_ref, st_ref):
    @pl.when(pl.program_id(1) == 0)
    def _():
        st_ref[...] = s0_ref[...]

    for b in range(xa_ref.shape[0]):
        def chunk(c, b=b):
            rows = pl.ds(pl.multiple_of(c * q, q), q)
            _ssd_chunk(cfg, xa_ref.at[b, rows], d_ref.at[b, rows], dT_ref.at[b, c], arow_ref, acol_ref, dskip_ref,
                       y_ref.at[b, rows], st_ref.at[b])
        _sub_tiles(xa_ref.shape[1] // q, chunk)

    @pl.when(pl.program_id(1) == n_steps - 1)
    def _():
        sout_ref[...] = st_ref[...]


def ssd_scan(cfg, xa, d, dT, s0, a_row, a_col, dskip, n_seq, seq_len, q, nb, cps):
    dm = cfg.d_model
    nc = seq_len // (q * cps)
    blk3 = lambda s, c: (s, c, 0)
    st3 = lambda s, c: (s, 0, 0)
    const2 = lambda s, c: (0, 0)
    return pl.pallas_call(
        functools.partial(_ssd_kernel, cfg, nc, q),
        out_shape=(jax.ShapeDtypeStruct((n_seq, seq_len, dm), F32),
                   jax.ShapeDtypeStruct((n_seq, dm, cfg.ssd_state), F32)),
        grid=(n_seq // nb, nc),
        in_specs=[pl.BlockSpec((nb, q * cps, cfg.xbc_dim), blk3),
                  pl.BlockSpec((nb, q * cps, LANES), blk3),
                  pl.BlockSpec((nb, cps, cfg.ssd_heads, q), lambda s, c: (s, c, 0, 0)),
                  pl.BlockSpec((nb, dm, cfg.ssd_state), st3),
                  pl.BlockSpec((1, LANES), const2),
                  pl.BlockSpec((cfg.ssd_heads, 1), const2),
                  pl.BlockSpec((1, dm), const2)],
        out_specs=(pl.BlockSpec((nb, q * cps, dm), blk3),
                   pl.BlockSpec((nb, dm, cfg.ssd_state), st3)),
        scratch_shapes=[pltpu.VMEM((nb, dm, cfg.ssd_state), F32)],
        compiler_params=_cparams(("arbitrary", "arbitrary")),
        name="ssd_scan",
    )(xa, d, dT, s0, a_row, a_col, dskip)


def _mlstm_chunk(cfg, q_ref, k_ref, v_ref, g_ref, gT_ref, h_ref, c_ref, n_ref, m_ref):
    ql = q_ref.shape[0]
    hd = cfg.ml_head_dim
    nh = cfg.ml_heads

    causal = _tri(ql, True)
    gcol = g_ref[...]
    grow = gT_ref[...]
    bcum = jnp.dot(causal.astype(F32), gcol, precision=HI, preferred_element_type=F32)
    bcumT = jnp.dot(grow, _tri(ql, False).astype(F32), precision=HI, preferred_element_type=F32)
    nt_dims = (((1,), (1,)), ((), ()))
    tn_dims = (((0,), (0,)), ((), ()))

    for h in range(nh):
        sl = slice(h * hd, (h + 1) * hd)
        b_col = bcum[:, nh + h:nh + h + 1]
        b_row = bcumT[nh + h:nh + h + 1, :]
        i_col = gcol[:, h:h + 1]
        i_row = grow[h:h + 1, :]
        m_prev = m_ref[h:h + 1, 0:1]
        dlog = jnp.where(causal, b_col - b_row + i_row, -jnp.inf)
        inter = b_col + m_prev
        mt = jnp.maximum(inter, jnp.max(dlog, axis=1, keepdims=True))
        qh = q_ref[:, sl]
        kh = k_ref[:, sl]
        vh = v_ref[:, sl]
        qb = qh.astype(BF16)
        kb = kh.astype(BF16)
        s = lax.dot_general(qb, kb, nt_dims, preferred_element_type=F32) * jnp.exp(dlog - mt)
        gdec = jnp.exp(inter - mt)
        c_old = c_ref[sl, :]
        n_old = n_ref[h:h + 1, :]
        qc = lax.dot_general(qb, c_old.astype(BF16), nt_dims, preferred_element_type=F32)
        num = jnp.dot(s.astype(BF16), vh.astype(BF16), preferred_element_type=F32) + gdec * qc
        den = jnp.sum(s, axis=1, keepdims=True) + gdec * jnp.sum(qh * n_old, axis=1, keepdims=True)
        h_ref[:, sl] = num / jnp.maximum(jnp.abs(den), jnp.exp(-mt))
        m_new = mt[ql - 1:ql, :]
        gs = jnp.exp(b_col[ql - 1:ql, :] - b_col + i_col - m_new)
        gc = jnp.exp(inter[ql - 1:ql, :] - m_new)
        upd = lax.dot_general((vh * gs).astype(BF16), kb, tn_dims, preferred_element_type=F32)
        c_ref[sl, :] = gc * c_old + upd
        n_ref[h:h + 1, :] = gc * n_old + jnp.sum(gs * kh, axis=0, keepdims=True)
        m_ref[h:h + 1, :] = jnp.broadcast_to(m_new, (1, LANES))


def _mlstm_kernel(cfg, n_steps, q, q_ref, k_ref, v_ref, g_ref, gT_ref, c0_ref, n0_ref, m0_ref,
                  h_ref, cout_ref, nout_ref, mout_ref, c_ref, n_ref, m_ref):
    @pl.when(pl.program_id(1) == 0)
    def _():
        c_ref[...] = c0_ref[...]
        n_ref[...] = n0_ref[...]
        m_ref[...] = m0_ref[...]

    for b in range(q_ref.shape[0]):
        def chunk(c, b=b):
            rows = pl.ds(pl.multiple_of(c * q, q), q)
            _mlstm_chunk(cfg, q_ref.at[b, rows], k_ref.at[b, rows], v_ref.at[b, rows], g_ref.at[b, rows],
                         gT_ref.at[b, c], h_ref.at[b, rows], c_ref.at[b], n_ref.at[b], m_ref.at[b])
        _sub_tiles(q_ref.shape[1] // q, chunk)

    @pl.when(pl.program_id(1) == n_steps - 1)
    def _():
        cout_ref[...] = c_ref[...]
        nout_ref[...] = n_ref[...]
        mout_ref[...] = m_ref[...]


def mlstm_scan(cfg, qa, ka, va, g, gT, c0, n0, m0, n_seq, seq_len, q, nb, cps):
    d = cfg.d_model
    hd = cfg.ml_head_dim
    nh = cfg.ml_heads
    nc = seq_len // (q * cps)
    rows = q * cps
    blk3 = lambda s, c: (s, c, 0)
    st3 = lambda s, c: (s, 0, 0)
    return pl.pallas_call(
        functools.partial(_mlstm_kernel, cfg, nc, q),
        out_shape=(jax.ShapeDtypeStruct((n_seq, seq_len, d), F32),
                   jax.ShapeDtypeStruct((n_seq, d, hd), F32),
                   jax.ShapeDtypeStruct((n_seq, nh, hd), F32),
                   jax.ShapeDtypeStruct((n_seq, nh, LANES), F32)),
        grid=(n_seq // nb, nc),
        in_specs=[pl.BlockSpec((nb, rows, d), blk3), pl.BlockSpec((nb, rows, d), blk3),
                  pl.BlockSpec((nb, rows, d), blk3),
                  pl.BlockSpec((nb, rows, LANES), blk3),
                  pl.BlockSpec((nb, cps, 2 * nh, q), lambda s, c: (s, c, 0, 0)),
                  pl.BlockSpec((nb, d, hd), st3), pl.BlockSpec((nb, nh, hd), st3),
                  pl.BlockSpec((nb, nh, LANES), st3)],
        out_specs=(pl.BlockSpec((nb, rows, d), blk3),
                   pl.BlockSpec((nb, d, hd), st3), pl.BlockSpec((nb, nh, hd), st3),
                   pl.BlockSpec((nb, nh, LANES), st3)),
        scratch_shapes=[pltpu.VMEM((nb, d, hd), F32), pltpu.VMEM((nb, nh, hd), F32),
                        pltpu.VMEM((nb, nh, LANES), F32)],
        compiler_params=_cparams(("arbitrary", "arbitrary")),
        name="mlstm_scan",
    )(qa, ka, va, g, gT, c0, n0, m0)


def _group_norm(x, w_ref, col0, groups, width):
    parts = []
    for g in range(groups):
        seg = x[:, g * width:(g + 1) * width]
        parts.append(seg * lax.rsqrt(jnp.mean(seg * seg, axis=-1, keepdims=True) + EPS)
                     * w_ref[:, col0 + g * width: col0 + (g + 1) * width])
    return parts


def _outproj_kernel(cfg, ys_ref, z_ref, hm_ref, o_ref, x_ref, nws_ref, nwm_ref, w_ref, out_ref):
    d = cfg.d_model
    ws = d // cfg.ssd_groups
    yz = ys_ref[...] * _silu(z_ref[...])
    ssd_half = jnp.concatenate([part.astype(BF16) for part in _group_norm(yz, nws_ref, 0, cfg.ssd_groups, ws)],
                               axis=-1)
    acc = x_ref[...] + jnp.dot(ssd_half, w_ref[0:d, :], preferred_element_type=F32)
    gate = _sigmoid(o_ref[...])
    wm = cfg.ml_head_dim
    ml_half = jnp.concatenate(
        [(part * gate[:, g * wm:(g + 1) * wm]).astype(BF16)
         for g, part in enumerate(_group_norm(hm_ref[...], nwm_ref, 0, cfg.ml_heads, wm))], axis=-1)
    out_ref[...] = acc + jnp.dot(ml_half, w_ref[d:2 * d, :], preferred_element_type=F32)


def out_proj(cfg, ys, proj, proj_row0, hm, x, nws, nwm, w_out, bm):
    m, d = x.shape
    blk0 = proj_row0 // bm
    const2 = lambda i: (0, 0)
    full = lambda i: (i, 0)
    return pl.pallas_call(
        functools.partial(_outproj_kernel, cfg),
        out_shape=jax.ShapeDtypeStruct((m, d), F32),
        grid=(m // bm,),
        in_specs=[pl.BlockSpec((bm, d), full),
                  pl.BlockSpec((bm, d), lambda i: (i + blk0, 0)),
                  pl.BlockSpec((bm, d), full),
                  pl.BlockSpec((bm, d), lambda i: (i + blk0, 2)),
                  pl.BlockSpec((bm, d), full),
                  pl.BlockSpec((1, d), const2), pl.BlockSpec((1, d), const2),
                  pl.BlockSpec((2 * d, d), const2, pipeline_mode=pl.Buffered(1))],
        out_specs=pl.BlockSpec((bm, d), full),
        compiler_params=_cparams(("arbitrary",)),
        name="out_proj",
    )(ys, proj, hm, proj, x, nws, nwm, w_out)


def _router_kernel(cfg, n_a, xa_ref, xb_ref, nw_ref, wr_ref, br_ref, ei_ref, wt_ref, cnt_out_ref, rows_ref,
                   cnt_ref, x_ref):
    ne = cfg.n_experts
    epg = cfg.experts_per_group
    ngr = cfg.n_groups
    bm = x_ref.shape[0]
    i = pl.program_id(0)

    @pl.when(i == 0)
    def _():
        cnt_ref[...] = jnp.zeros_like(cnt_ref)

    @pl.when(i < n_a)
    def _():
        x_ref[...] = xa_ref[...]

    @pl.when(i >= n_a)
    def _():
        x_ref[...] = xb_ref[...]

    hb = _rms(x_ref[...], nw_ref[...]).astype(BF16)
    rows_ref[...] = _lanes_to_rows(_pack_bf16_pairs(hb))
    logits = jnp.dot(hb, wr_ref[...], preferred_element_type=F32) + br_ref[...]
    lane = lax.broadcasted_iota(jnp.int32, logits.shape, 1)
    big = jnp.int32(2 ** 30)
    neg = -jnp.inf

    def first_argmax(vals):
        mx = jnp.max(vals, axis=-1, keepdims=True)
        idx = jnp.min(jnp.where(vals == mx, lane, big), axis=-1, keepdims=True)
        return mx, idx

    is_group = (lane >= ne) & (lane < ne + ngr)
    gl = jnp.where(is_group, logits, neg)
    gmax, gidx = first_argmax(gl)
    p_g = 1.0 / jnp.sum(jnp.exp(gl - gmax), axis=-1, keepdims=True)
    e_lo = (gidx - ne) * epg
    in_sel = (lane >= e_lo) & (lane < e_lo + epg)
    el = jnp.where(in_sel, logits, neg)
    pe = jnp.exp(el - jnp.max(el, axis=-1, keepdims=True))
    pe = jnp.where(in_sel, pe / jnp.sum(pe, axis=-1, keepdims=True), -1.0)
    p1, i1 = first_argmax(pe)
    p2, i2 = first_argmax(jnp.where(lane == i1, -1.0, pe))
    wsum = p1 + p2
    wt_ref[...] = jnp.where(lane == 0, p_g * p1 / wsum, jnp.where(lane == 1, p_g * p2 / wsum, 0.0))

    oh1 = jnp.where(lane == i1, 1.0, 0.0)
    oh2 = jnp.where(lane == i2, 1.0, 0.0)
    r = lax.broadcasted_iota(jnp.int32, (bm, bm), 0)
    c = lax.broadcasted_iota(jnp.int32, (bm, bm), 1)
    before = jnp.where(c < r, 1.0, 0.0).astype(BF16)
    ahead1 = jnp.dot(before, oh1.astype(BF16), preferred_element_type=F32)
    ahead2 = jnp.dot(before, oh2.astype(BF16), preferred_element_type=F32)
    cnt = cnt_ref[...]
    tot1 = jnp.sum(oh1, axis=0, keepdims=True)
    rank1 = jnp.sum(oh1 * (cnt + ahead1), axis=-1, keepdims=True)
    rank2 = jnp.sum(oh2 * (cnt + tot1 + ahead2), axis=-1, keepdims=True)
    cnt_new = cnt + tot1 + jnp.sum(oh2, axis=0, keepdims=True)
    cnt_ref[...] = cnt_new
    cnt_out_ref[...] = cnt_new
    ei_ref[...] = jnp.where(lane == 0, i1, jnp.where(lane == 1, i2, jnp.where(
        lane == 2, rank1.astype(jnp.int32), jnp.where(lane == 3, rank2.astype(jnp.int32), 0))))


def router(cfg, xa, xb, nw, wr, br, bm):
    d = xa.shape[1]
    n_a = xa.shape[0] // bm
    n_b = xb.shape[0] // bm
    m = xa.shape[0] + xb.shape[0]
    nch = d // (2 * LANES)
    const2 = lambda i: (0, 0)
    return pl.pallas_call(
        functools.partial(_router_kernel, cfg, n_a),
        out_shape=(jax.ShapeDtypeStruct((m, LANES), jnp.int32), jax.ShapeDtypeStruct((m, LANES), F32),
                   jax.ShapeDtypeStruct((1, LANES), F32), jax.ShapeDtypeStruct((m, nch, LANES), jnp.uint32)),
        grid=(n_a + n_b,),
        in_specs=[pl.BlockSpec((bm, d), lambda i: (jnp.minimum(i, n_a - 1), 0)),
                  pl.BlockSpec((bm, d), lambda i: (jnp.maximum(i - n_a, 0), 0)),
                  pl.BlockSpec((1, d), const2),
                  pl.BlockSpec((d, LANES), const2), pl.BlockSpec((1, LANES), const2)],
        out_specs=(pl.BlockSpec((bm, LANES), lambda i: (i, 0)), pl.BlockSpec((bm, LANES), lambda i: (i, 0)),
                   pl.BlockSpec((1, LANES), const2), pl.BlockSpec((bm, nch, LANES), lambda i: (i, 0, 0))),
        scratch_shapes=[pltpu.VMEM((1, LANES), F32), pltpu.VMEM((bm, d), F32)],
        compiler_params=_cparams(("arbitrary",)),
        name="router",
    )(xa, xb, nw, wr, br)


def _pack_bf16_pairs(x):
    half = x.shape[1] // 2
    bits = lambda v: lax.bitcast_convert_type(v.astype(BF16).astype(F32), jnp.uint32)
    return bits(x[:, :half]) | (bits(x[:, half:]) >> 16)


def _unpack_bf16_pairs(p):
    hi = lax.bitcast_convert_type(p & jnp.uint32(0xFFFF0000), F32)
    lo = lax.bitcast_convert_type(p << 16, F32)
    return jnp.concatenate([hi, lo], axis=-1)


def _rows_to_lanes(g):
    t = pltpu.einshape("rcl->crl", g)
    return jnp.concatenate([t[c] for c in range(t.shape[0])], axis=-1)


def _lanes_to_rows(x):
    parts = jnp.stack([x[:, c * LANES:(c + 1) * LANES] for c in range(x.shape[1] // LANES)], axis=0)
    return pltpu.einshape("crl->rcl", parts)


GATHER_GROUP = 8


def _gather_rows(idx_ref, src_hbm, dst, sem, n_groups):
    def body(g, carry):
        for u in range(GATHER_GROUP):
            r = g * GATHER_GROUP + u
            pltpu.make_async_copy(src_hbm.at[idx_ref[0, 0, r]], dst.at[r], sem).start()
        return carry
    lax.fori_loop(0, n_groups, body, 0)


def _wait_rows(src_hbm, dst, sem, n):
    pltpu.make_async_copy(src_hbm.at[pl.ds(0, n)], dst.at[pl.ds(0, n)], sem).wait()


def _moe_kernel(cfg, te_ref, nr_ref, first_ref, wslot_ref, nexte_ref, nv_ref, src_ref, srcn_ref, x_hbm,
                wg_hbm, wu_hbm, wd_hbm, ys_ref, xbuf, sem, wgf, wuf, wdf, wsem, wgb, wub, wdb):
    j = pl.program_id(0)
    n_valid = nv_ref[0]

    def weight_copies(e, slot):
        return (pltpu.make_async_copy(wg_hbm.at[e], wgf.at[slot], wsem.at[slot, 0]),
                pltpu.make_async_copy(wu_hbm.at[e], wuf.at[slot], wsem.at[slot, 1]),
                pltpu.make_async_copy(wd_hbm.at[e], wdf.at[slot], wsem.at[slot, 2]))

    @pl.when(j == 0)
    def _():
        for cp in weight_copies(te_ref[0], 0):
            cp.start()
        xbuf[...] = jnp.zeros_like(xbuf)
        _gather_rows(src_ref, x_hbm, xbuf.at[0], sem.at[0], nr_ref[0] // GATHER_GROUP)

    @pl.when(j + 1 < n_valid)
    def _():
        nslot = (j + 1) % 2
        _gather_rows(srcn_ref, x_hbm, xbuf.at[nslot], sem.at[nslot], nr_ref[j + 1] // GATHER_GROUP)

    @pl.when(j < n_valid)
    def _():
        @pl.when(first_ref[j] == 1)
        def _():
            ws = wslot_ref[j]
            for cp in weight_copies(te_ref[j], ws):
                cp.wait()

            @pl.when(nexte_ref[j] >= 0)
            def _():
                for cp in weight_copies(nexte_ref[j], 1 - ws):
                    cp.start()

            wgb[...] = wgf[ws].astype(BF16)
            wub[...] = wuf[ws].astype(BF16)
            wdb[...] = wdf[ws].astype(BF16)

        slot = j % 2
        _wait_rows(x_hbm, xbuf.at[slot], sem.at[slot], nr_ref[j])
        hb = _unpack_bf16_pairs(_rows_to_lanes(xbuf[slot])).astype(BF16)
        hid = (_silu(jnp.dot(hb, wgb[...], preferred_element_type=F32))
               * jnp.dot(hb, wub[...], preferred_element_type=F32))
        y = jnp.dot(hid.astype(BF16), wdb[...], preferred_element_type=F32)
        ys_ref[...] = _lanes_to_rows(_pack_bf16_pairs(y))

    @pl.when(j >= n_valid)
    def _():
        ys_ref[...] = jnp.zeros_like(ys_ref)


def moe_routed(cfg, tables, src, x_rows, wg, wu, wd, tm):
    n_tiles = src.shape[0]
    d = cfg.d_model
    de = cfg.d_expert
    nch = x_rows.shape[1]
    return pl.pallas_call(
        functools.partial(_moe_kernel, cfg),
        out_shape=jax.ShapeDtypeStruct((n_tiles * tm, nch, LANES), jnp.uint32),
        grid_spec=pltpu.PrefetchScalarGridSpec(
            num_scalar_prefetch=len(tables),
            grid=(n_tiles,),
            in_specs=[pl.BlockSpec((1, 1, tm), lambda j, *_: (j, 0, 0), memory_space=pltpu.SMEM),
                      pl.BlockSpec((1, 1, tm), lambda j, *_: (jnp.minimum(j + 1, n_tiles - 1), 0, 0),
                                   memory_space=pltpu.SMEM),
                      pl.BlockSpec(memory_space=pl.ANY),
                      pl.BlockSpec(memory_space=pl.ANY), pl.BlockSpec(memory_space=pl.ANY),
                      pl.BlockSpec(memory_space=pl.ANY)],
            out_specs=pl.BlockSpec((tm, nch, LANES), lambda j, *_: (j, 0, 0)),
            scratch_shapes=[pltpu.VMEM((2, tm, nch, LANES), jnp.uint32), pltpu.SemaphoreType.DMA((2,)),
                            pltpu.VMEM((2, d, de), F32), pltpu.VMEM((2, d, de), F32), pltpu.VMEM((2, de, d), F32),
                            pltpu.SemaphoreType.DMA((2, 3)),
                            pltpu.VMEM((d, de), BF16), pltpu.VMEM((d, de), BF16), pltpu.VMEM((de, d), BF16)]),
        compiler_params=_cparams(("arbitrary",)),
        name="moe",
    )(*tables, src, src, x_rows, wg, wu, wd)


def _combine_kernel(n_steps, d0_ref, d1_ref, d0n_ref, d1n_ref, wt_ref, x1_ref, fw_ref, ys_hbm, y_ref, gbuf, sem):
    i = pl.program_id(0)
    groups = x1_ref.shape[0] // GATHER_GROUP

    def fetch(r0_ref, r1_ref, slot):
        _gather_rows(r0_ref, ys_hbm, gbuf.at[slot, 0], sem.at[slot, 0], groups)
        _gather_rows(r1_ref, ys_hbm, gbuf.at[slot, 1], sem.at[slot, 1], groups)

    @pl.when(i == 0)
    def _():
        fetch(d0_ref, d1_ref, 0)

    @pl.when(i + 1 < n_steps)
    def _():
        fetch(d0n_ref, d1n_ref, (i + 1) % 2)

    slot = i % 2
    bm = x1_ref.shape[0]
    _wait_rows(ys_hbm, gbuf.at[slot, 0], sem.at[slot, 0], bm)
    _wait_rows(ys_hbm, gbuf.at[slot, 1], sem.at[slot, 1], bm)
    wt = wt_ref[...]
    acc = (x1_ref[...] + wt[:, 0:1] * _unpack_bf16_pairs(_rows_to_lanes(gbuf[slot, 0]))
           + wt[:, 1:2] * _unpack_bf16_pairs(_rows_to_lanes(gbuf[slot, 1])))
    y_ref[...] = _rms(acc, fw_ref[...])


def moe_combine(cfg, dest0, dest1, wt, wt_blk0, x1, fw, ys, bm):
    m, d = x1.shape
    nch = ys.shape[1]
    n = m // bm
    cur = lambda: pl.BlockSpec((1, 1, bm), lambda i: (i, 0, 0), memory_space=pltpu.SMEM)
    nxt = lambda: pl.BlockSpec((1, 1, bm), lambda i: (jnp.minimum(i + 1, n - 1), 0, 0), memory_space=pltpu.SMEM)
    return pl.pallas_call(
        functools.partial(_combine_kernel, n),
        out_shape=jax.ShapeDtypeStruct((m, d), F32),
        grid=(n,),
        in_specs=[cur(), cur(), nxt(), nxt(),
                  pl.BlockSpec((bm, LANES), lambda i: (i + wt_blk0, 0)),
                  pl.BlockSpec((bm, d), lambda i: (i, 0)), pl.BlockSpec((1, d), lambda i: (0, 0)),
                  pl.BlockSpec(memory_space=pl.ANY)],
        out_specs=pl.BlockSpec((bm, d), lambda i: (i, 0)),
        scratch_shapes=[pltpu.VMEM((2, 2, bm, nch, LANES), jnp.uint32), pltpu.SemaphoreType.DMA((2, 2))],
        compiler_params=_cparams(("arbitrary",)),
        name="moe_combine",
    )(dest0, dest1, dest0, dest1, wt, x1, fw, ys)


def _route_tables(cfg, ei, cnt, tm, n_tiles):
    ne = cfg.n_experts
    i32 = jnp.int32
    counts = cnt[0, :ne].astype(i32)
    tiles_e = (counts + tm - 1) // tm
    tile_end = jnp.cumsum(tiles_e)
    tile_start = tile_end - tiles_e
    row_off = tile_start * tm
    tile_id = jnp.arange(n_tiles, dtype=i32)
    tile_expert = jnp.minimum(jnp.sum((tile_id[:, None] >= tile_end[None, :]).astype(i32), axis=1), ne - 1)
    used = jnp.clip(counts[tile_expert] - (tile_id - tile_start[tile_expert]) * tm, 0, tm)
    used = jnp.where(tile_id < tile_end[-1], used, 0)
    tile_rows = (used + GATHER_GROUP - 1) // GATHER_GROUP * GATHER_GROUP
    n_valid = tile_end[-1]
    prev_expert = jnp.concatenate([jnp.full((1,), -1, i32), tile_expert[:-1]])
    first = ((tile_expert != prev_expert) & (tile_id < n_valid)).astype(i32)
    wslot = (jnp.cumsum(first) - 1) % 2
    e_id = jnp.arange(ne, dtype=i32)
    later = (e_id[None, :] > e_id[:, None]) & (tiles_e[None, :] > 0)
    next_e = jnp.min(jnp.where(later, e_id[None, :], ne), axis=1)
    next_e = jnp.where(next_e < ne, next_e, -1)[tile_expert]
    tables = (tile_expert, tile_rows, first, wslot.astype(i32), next_e.astype(i32), n_valid[None].astype(i32))
    picks = ei[:, 0:4].T
    pick_off = jnp.sum(jnp.where(picks[None, 0:2] == e_id[:, None, None], row_off[:, None, None], 0), axis=0)
    dest = pick_off + picks[2:4]
    tok = jnp.broadcast_to(jnp.arange(ei.shape[0], dtype=i32)[None, :], dest.shape)
    src = jnp.zeros((n_tiles * tm,), i32).at[dest.reshape(-1)].set(
        tok.reshape(-1), unique_indices=True, mode="promise_in_bounds")
    return tables, src.reshape(n_tiles, 1, tm), dest


def _pad_hist(hist):
    return jnp.pad(hist, ((0, 0), (SUBLANES - hist.shape[1], 0), (0, 0)))


def _tile(m, pref):
    return pref if m % pref == 0 else m


CHUNKS_PER_STEP = 4
CONV_TILES_PER_STEP = 1


def _mixer_segment(cfg, proj, row0, dcol, dT, n_seq, seq_len, hist_xbc, hist_xm, s0, c0, n0, m0, p):
    d = cfg.d_model
    m = n_seq * seq_len
    q = min(cfg.chunk, seq_len)
    nc = seq_len // q
    cps = CHUNKS_PER_STEP if nc % CHUNKS_PER_STEP == 0 else 1
    xa, qa, ka, va, g, gT = conv_qkv(cfg, proj, hist_xbc, hist_xm, p["cwx"], p["cbx"], p["cwm"], p["cbm"],
                                     p["wq"], p["wk"], p["wv"], p["wg"], p["wgT"], p["bg_row"], p["bg_col"],
                                     row0, n_seq, seq_len, q,
                                     CONV_TILES_PER_STEP if nc % CONV_TILES_PER_STEP == 0 else 1)
    dTc = dT[:cfg.ssd_heads].reshape(cfg.ssd_heads, n_seq, nc, q).transpose(1, 2, 0, 3)
    nb = 2 if (nc == 1 and n_seq % 2 == 0) else 1
    seq3 = lambda a: a.reshape(n_seq, seq_len, a.shape[-1])
    ys, s_new = ssd_scan(cfg, seq3(xa), seq3(dcol), dTc, s0, p["a_row"], p["a_col"], p["dskip"],
                         n_seq, seq_len, q, nb, cps)
    hm, c_new, n_new, m_new = mlstm_scan(cfg, seq3(qa), seq3(ka), seq3(va), seq3(g), gT, c0, n0, m0,
                                         n_seq, seq_len, q, nb, cps)
    ys = ys.reshape(m, d)
    hm = hm.reshape(m, d)
    keep = cfg.conv_w - 1
    groups = proj.reshape(proj.shape[0] // SUBLANES, SUBLANES, proj.shape[1])
    first = (row0 + seq_len) // SUBLANES - 1
    step = seq_len // SUBLANES
    tail = lax.slice(groups, (first, SUBLANES - keep, 0),
                     (first + (n_seq - 1) * step + 1, SUBLANES, proj.shape[1]), (step, 1, 1))
    tail_xbc = tail[:, :, 3 * d:]
    tail_xm = tail[:, :, d:2 * d]
    return ys, hm, (tail_xbc, s_new, tail_xm, c_new, n_new, m_new)


MOE_TILE = 256


def _ffn(cfg, segments, p):
    d = cfg.d_model
    x1s = [out_proj(cfg, ys, proj, row0, hm, x2d, p["ssd_norm_w"], p["mlstm_norm_w"], p["w_out"],
                    _tile(x2d.shape[0], 256)) for x2d, proj, row0, ys, hm in segments]
    fits = lambda t: all(x1.shape[0] % t == 0 for x1 in x1s)
    ei, wt, cnt, x_rows = router(cfg, x1s[0], x1s[1], p["norm_ffn_w"], p["wr"], p["br"], 512 if fits(512) else 128)
    bm = 256 if fits(256) else 128
    n_tok = ei.shape[0]
    n_tiles = (2 * n_tok + cfg.n_experts * (MOE_TILE - 1)) // MOE_TILE
    tables, src, dest = _route_tables(cfg, ei, cnt, MOE_TILE, n_tiles)
    ys_sorted = moe_routed(cfg, tables, src, x_rows, p["w_gate"], p["w_up"], p["w_down"], MOE_TILE)
    outs = []
    off = 0
    for x1 in x1s:
        m = x1.shape[0]
        dseg = dest[:, off:off + m].reshape(2, m // bm, 1, bm)
        outs.append(moe_combine(cfg, dseg[0], dseg[1], wt, off // bm, x1, p["final_norm_w"], ys_sorted, bm))
        off += m
    return outs


def _prep_params(cfg, norm_mix_w, w_in, conv_ssd_w, conv_ssd_b, dt_bias, a_log, d_skip, ssd_norm_w,
                 conv_mlstm_w, conv_mlstm_b, w_q, w_k, w_v, w_igate, b_igate, w_fgate, b_fgate, mlstm_norm_w,
                 w_out, norm_ffn_w, w_group, b_group, w_router, b_router, w_gate, w_up, w_down, final_norm_w):
    d = cfg.d_model
    hs = cfg.ssd_heads
    nh = cfg.ml_heads
    o_xbc = d + cfg.xbc_dim
    row = lambda v: v.reshape(1, -1).astype(F32)
    pad_lanes = lambda a: jnp.pad(a, ((0, 0), (0, LANES - a.shape[1])))
    w_in_t = w_in.T.astype(F32)
    w_dt_t = jnp.pad(w_in_t[o_xbc:o_xbc + hs], ((0, LANES - hs), (0, 0)))
    a = -jnp.exp(a_log.astype(F32))
    w_gates = jnp.concatenate([w_igate, w_fgate], axis=1)
    b_gates = jnp.concatenate([b_igate, b_fgate]).astype(F32)
    ne = cfg.n_experts
    wr = pad_lanes(jnp.concatenate([w_router, w_group], axis=1))
    br = pad_lanes(jnp.concatenate([b_router, b_group]).reshape(1, -1).astype(F32))
    return dict(
        norm_mix_w=row(norm_mix_w),
        w_in_t=w_in_t, w_dt=w_dt_t.T.astype(BF16), w_dtT=w_dt_t.astype(BF16),
        bdt_row=pad_lanes(row(dt_bias)), bdt_col=pad_lanes(row(dt_bias)).T,
        cwx=conv_ssd_w.astype(F32), cbx=row(conv_ssd_b), cwm=conv_mlstm_w.astype(F32), cbm=row(conv_mlstm_b),
        wq=w_q.astype(BF16), wk=w_k.astype(BF16), wv=w_v.astype(BF16),
        wg=pad_lanes(w_gates).astype(BF16), wgT=w_gates.T.astype(BF16),
        bg_row=pad_lanes(row(b_gates)), bg_col=b_gates.reshape(-1, 1),
        a_row=pad_lanes(row(a)), a_col=a.reshape(-1, 1),
        dskip=row(jnp.repeat(d_skip.astype(F32), cfg.ssd_head_dim)),
        ssd_norm_w=row(ssd_norm_w), mlstm_norm_w=row(mlstm_norm_w), w_out=w_out.astype(BF16),
        norm_ffn_w=row(norm_ffn_w), wr=wr.astype(BF16), br=br,
        w_gate=w_gate.astype(F32), w_up=w_up.astype(F32), w_down=w_down.astype(F32),
        final_norm_w=row(final_norm_w),
    )


def forward(cfg, x_prompt, x_sample, state_ssd_conv, state_ssd, state_mlstm_conv, state_mlstm_c,
            state_mlstm_n, state_mlstm_m, meta_tokens, *weights):
    d = cfg.d_model
    nh = cfg.ml_heads
    hd = cfg.ml_head_dim
    assert state_ssd.shape[0] == 1, "single-layer kernel"
    p = _prep_params(cfg, *[w[0] for w in weights[:-1]], weights[-1])
    bp, lp, _ = x_prompt.shape
    bs, ls, _ = x_sample.shape
    n_meta = meta_tokens.shape[0]

    xp = x_prompt.reshape(bp * lp, d)
    xs = x_sample.reshape(bs * ls, d)
    mp, ms = bp * lp, bs * ls
    norm_args = (p["norm_mix_w"], p["w_dt"], p["w_dtT"], p["bdt_row"], p["bdt_col"])
    h_meta, d_meta, dT_meta = pre_norm(meta_tokens.astype(F32), None, *norm_args, n_meta)
    h, dcol, dT = pre_norm(xp, xs, *norm_args, 512 if mp % 512 == 0 and ms % 512 == 0 else 128)
    proj, proj_meta = in_proj(cfg, h, h_meta, p["w_in_t"], _tile(mp + ms, 1536), d // 2)

    zeros = lambda *s: jnp.zeros(s, F32)
    _, _, st_meta = _mixer_segment(
        cfg, proj_meta, 0, d_meta, dT_meta, 1, n_meta, zeros(1, SUBLANES, cfg.xbc_dim), zeros(1, SUBLANES, d),
        zeros(1, d, cfg.ssd_state), zeros(1, d, hd), zeros(1, nh, hd), zeros(1, nh, LANES), p)
    mt_xbc, mt_s, mt_xm, mt_c, mt_n, mt_m = st_meta
    rep = lambda a: jnp.broadcast_to(a, (bp,) + a.shape[1:])

    ys_p, hm_p, st_p = _mixer_segment(
        cfg, proj, 0, dcol[:mp], dT[:, :mp], bp, lp, rep(_pad_hist(mt_xbc)), rep(_pad_hist(mt_xm)),
        rep(mt_s), rep(mt_c), rep(mt_n), rep(mt_m), p)

    m0 = jnp.broadcast_to(state_mlstm_m[0].astype(F32)[:, :, None], (bs, nh, LANES))
    ys_s, hm_s, st_s = _mixer_segment(
        cfg, proj, mp, dcol[mp:], dT[:, mp:], bs, ls, _pad_hist(state_ssd_conv[0]),
        _pad_hist(state_mlstm_conv[0]), state_ssd[0].reshape(bs, d, cfg.ssd_state),
        state_mlstm_c[0].reshape(bs, d, hd), state_mlstm_n[0], m0, p)
    y_p, y_s = _ffn(cfg, [(xp, proj, 0, ys_p, hm_p), (xs, proj, mp, ys_s, hm_s)], p)
    y_prompt = y_p.reshape(bp, lp, d)
    y_sample = y_s.reshape(bs, ls, d)

    def pack(st, b):
        t_xbc, s_new, t_xm, c_new, n_new, m_new = st
        return (t_xbc[None], s_new.reshape(1, b, cfg.ssd_heads, cfg.ssd_head_dim, cfg.ssd_state),
                t_xm[None], c_new.reshape(1, b, nh, hd, hd), n_new[None], m_new[None, :, :, 0])

    return (y_prompt, y_sample) + pack(st_p, bp) + pack(st_s, bs)


def kernel(x_prompt, x_sample, state_ssd_conv, state_ssd, state_mlstm_conv, state_mlstm_c, state_mlstm_n, state_mlstm_m, meta_tokens, norm_mix_w, w_in, conv_ssd_w, conv_ssd_b, dt_bias, a_log, d_skip, ssd_norm_w, conv_mlstm_w, conv_mlstm_b, w_q, w_k, w_v, w_igate, b_igate, w_fgate, b_fgate, mlstm_norm_w, w_out, norm_ffn_w, w_group, b_group, w_router, b_router, w_gate, w_up, w_down, final_norm_w):
    return forward(Cfg(), x_prompt, x_sample, state_ssd_conv, state_ssd, state_mlstm_conv, state_mlstm_c,
                   state_mlstm_n, state_mlstm_m, meta_tokens, norm_mix_w, w_in, conv_ssd_w, conv_ssd_b, dt_bias,
                   a_log, d_skip, ssd_norm_w, conv_mlstm_w, conv_mlstm_b, w_q, w_k, w_v, w_igate, b_igate,
                   w_fgate, b_fgate, mlstm_norm_w, w_out, norm_ffn_w, w_group, b_group, w_router, b_router,
                   w_gate, w_up, w_down, final_norm_w)
```

```python
import functools
from typing import NamedTuple

import jax
import jax.numpy as jnp
from jax import lax
from jax.experimental import pallas as pl
from jax.experimental.pallas import tpu as pltpu

F32 = jnp.float32
BF16 = jnp.bfloat16
EPS = 1e-6
LANES = 128
SUBLANES = 8
VMEM_LIMIT = 56 * 1024 * 1024
HI = lax.Precision.HIGHEST


class Cfg(NamedTuple):
    d_model: int = 2048
    ssd_heads: int = 32
    ssd_head_dim: int = 64
    ssd_groups: int = 4
    ssd_state: int = 128
    ml_heads: int = 8
    ml_head_dim: int = 256
    n_groups: int = 4
    experts_per_group: int = 8
    d_expert: int = 512
    n_meta: int = 16
    conv_w: int = 4
    chunk: int = 128

    @property
    def bc_dim(self):
        return self.ssd_groups * self.ssd_state

    @property
    def xbc_dim(self):
        return self.d_model + 2 * self.bc_dim

    @property
    def n_experts(self):
        return self.n_groups * self.experts_per_group


def _cparams(sem):
    return pltpu.CompilerParams(dimension_semantics=sem, vmem_limit_bytes=VMEM_LIMIT)


def _softplus(x):
    return jnp.maximum(x, 0.0) + jnp.log1p(jnp.exp(-jnp.abs(x)))


def _sigmoid(x):
    return 1.0 / (1.0 + jnp.exp(-x))


def _silu(x):
    return x * _sigmoid(x)


def _rms(x, w):
    return x * lax.rsqrt(jnp.mean(x * x, axis=-1, keepdims=True) + EPS) * w


def _prenorm_kernel(x_ref, nw_ref, wdt_ref, wdtT_ref, bdt_row_ref, bdt_col_ref, h_ref, d_ref, dT_ref):
    hb = _rms(x_ref[...], nw_ref[...]).astype(BF16)
    h_ref[...] = hb
    dt = jnp.dot(hb, wdt_ref[...], preferred_element_type=F32)
    d_ref[...] = _softplus(dt + bdt_row_ref[...])
    dtT = lax.dot_general(wdtT_ref[...], hb, (((1,), (1,)), ((), ())), preferred_element_type=F32)
    dT_ref[...] = _softplus(dtT + bdt_col_ref[...])


def _prenorm_pair_kernel(n_a, xa_ref, xb_ref, nw_ref, wdt_ref, wdtT_ref, bdt_row_ref, bdt_col_ref,
                         h_ref, d_ref, dT_ref, x_ref):
    i = pl.program_id(0)

    @pl.when(i < n_a)
    def _():
        x_ref[...] = xa_ref[...]

    @pl.when(i >= n_a)
    def _():
        x_ref[...] = xb_ref[...]

    _prenorm_kernel(x_ref, nw_ref, wdt_ref, wdtT_ref, bdt_row_ref, bdt_col_ref, h_ref, d_ref, dT_ref)


def pre_norm(xa, xb, norm_w, w_dt, w_dtT, bdt_row, bdt_col, bm):
    d = xa.shape[1]
    n_a = xa.shape[0] // bm
    n_b = 0 if xb is None else xb.shape[0] // bm
    m = (n_a + n_b) * bm
    const2 = lambda i: (0, 0)
    w_specs = [pl.BlockSpec((1, d), const2), pl.BlockSpec((d, LANES), const2), pl.BlockSpec((LANES, d), const2),
               pl.BlockSpec((1, LANES), const2), pl.BlockSpec((LANES, 1), const2)]
    if xb is None:
        body, x_specs, xs, scratch = _prenorm_kernel, [pl.BlockSpec((bm, d), lambda i: (i, 0))], (xa,), []
    else:
        body = functools.partial(_prenorm_pair_kernel, n_a)
        x_specs = [pl.BlockSpec((bm, d), lambda i: (jnp.minimum(i, n_a - 1), 0)),
                   pl.BlockSpec((bm, d), lambda i: (jnp.maximum(i - n_a, 0), 0))]
        xs, scratch = (xa, xb), [pltpu.VMEM((bm, d), F32)]
    return pl.pallas_call(
        body,
        out_shape=(jax.ShapeDtypeStruct((m, d), BF16), jax.ShapeDtypeStruct((m, LANES), F32),
                   jax.ShapeDtypeStruct((LANES, m), F32)),
        grid=(n_a + n_b,),
        in_specs=x_specs + w_specs,
        out_specs=(pl.BlockSpec((bm, d), lambda i: (i, 0)), pl.BlockSpec((bm, LANES), lambda i: (i, 0)),
                   pl.BlockSpec((LANES, bm), lambda i: (0, i))),
        scratch_shapes=scratch,
        compiler_params=_cparams(("arbitrary",)),
        name="pre_norm",
    )(*xs, norm_w, w_dt, w_dtT, bdt_row, bdt_col)


def _inproj_kernel(h_ref, hs_ref, wt_ref, proj_ref, projs_ref, w_ref):
    nt_dims = (((1,), (1,)), ((), ()))

    @pl.when(pl.program_id(1) == 0)
    def _():
        w_ref[...] = wt_ref[...].astype(BF16)
        projs_ref[...] = lax.dot_general(hs_ref[...], w_ref[...], nt_dims, preferred_element_type=F32)

    proj_ref[...] = lax.dot_general(h_ref[...], w_ref[...], nt_dims, preferred_element_type=F32)


def in_proj(cfg, h, h_small, w_in_t, bm, bn):
    m, d = h.shape
    ms = h_small.shape[0]
    nz = d // bn
    nx = cfg.xbc_dim // bn
    n_a = nz + nx
    n_blocks = n_a + 2 * nz
    skip = cfg.ssd_heads
    assert skip % SUBLANES == 0
    w_row = lambda j: pl.multiple_of(jnp.where(j < n_a, j * bn, j * bn + skip), SUBLANES)
    out_col = lambda j: jnp.where(j < nz, j, jnp.where(j < n_a, j + 2 * nz, j - nx))
    return pl.pallas_call(
        _inproj_kernel,
        out_shape=(jax.ShapeDtypeStruct((m, n_blocks * bn), F32), jax.ShapeDtypeStruct((ms, n_blocks * bn), F32)),
        grid=(n_blocks, m // bm),
        in_specs=[pl.BlockSpec((bm, d), lambda j, i: (i, 0)),
                  pl.BlockSpec((ms, d), lambda j, i: (0, 0)),
                  pl.BlockSpec((pl.Element(bn), pl.Element(d)), lambda j, i: (w_row(j), 0))],
        out_specs=(pl.BlockSpec((bm, bn), lambda j, i: (i, out_col(j))),
                   pl.BlockSpec((ms, bn), lambda j, i: (0, out_col(j)))),
        scratch_shapes=[pltpu.VMEM((bn, d), BF16)],
        compiler_params=_cparams(("arbitrary", "arbitrary")),
        name="in_proj",
    )(h, h_small, w_in_t)


CONV_LANES = 256


def _causal_conv(u, prev, w_ref, b_ref, cols, conv_w):
    lt = u.shape[0]
    row8 = lax.broadcasted_iota(jnp.int32, (SUBLANES, u.shape[1]), 0)
    acc = u * w_ref[conv_w - 1:conv_w, cols] + b_ref[:, cols]
    for s in range(1, conv_w):
        rolled = pltpu.roll(u, s, axis=0)
        head = jnp.where(row8 < s, pltpu.roll(prev, s, axis=0), rolled[0:SUBLANES])
        shifted = head if lt == SUBLANES else jnp.concatenate([head, rolled[SUBLANES:]], axis=0)
        acc = acc + shifted * w_ref[conv_w - 1 - s:conv_w - s, cols]
    return acc


def _conv_qkv_tile(cfg, xbc_ref, xm_ref, cwx_ref, cbx_ref, cwm_ref, cbm_ref, wq_ref, wk_ref, wv_ref, wg_ref, wgT_ref,
                   bg_row_ref, bg_col_ref, xa_ref, q_ref, k_ref, v_ref, g_ref, gT_ref, px_ref, pm_ref):
    lt = xbc_ref.shape[0]
    hd = cfg.ml_head_dim
    nh = cfg.ml_heads

    for c0 in range(0, xbc_ref.shape[1], CONV_LANES):
        cols = slice(c0, min(c0 + CONV_LANES, xbc_ref.shape[1]))
        u = xbc_ref[:, cols]
        xa_ref[:, cols] = _silu(_causal_conv(u, px_ref[:, cols], cwx_ref, cbx_ref, cols, cfg.conv_w))
        px_ref[:, cols] = u[lt - SUBLANES:lt]

    d = nh * hd
    kscale = hd ** -0.5
    nt = (((1,), (1,)), ((), ()))
    gcol = jnp.zeros(g_ref.shape, F32) + bg_row_ref[...]
    grow = jnp.zeros(gT_ref.shape, F32) + bg_col_ref[...]
    for h in range(nh):
        sl = slice(h * hd, (h + 1) * hd)
        xm = xm_ref[:, sl]
        xc = _silu(_causal_conv(xm, pm_ref[:, sl], cwm_ref, cbm_ref, sl, cfg.conv_w)).astype(BF16)
        pm_ref[:, sl] = xm[lt - SUBLANES:lt]
        qh = jnp.dot(xc, wq_ref[h], preferred_element_type=F32)
        kh = jnp.dot(xc, wk_ref[h], preferred_element_type=F32) * kscale
        vh = jnp.dot(xm.astype(BF16), wv_ref[h], preferred_element_type=F32)
        q_ref[:, sl] = qh
        k_ref[:, sl] = kh
        v_ref[:, sl] = vh
        for part, val in enumerate((qh, kh, vh)):
            vb = val.astype(BF16)
            rows = slice(part * d + h * hd, part * d + (h + 1) * hd)
            gcol = gcol + jnp.dot(vb, wg_ref[rows, :], preferred_element_type=F32)
            grow = grow + lax.dot_general(wgT_ref[:, rows], vb, nt, preferred_element_type=F32)
    lane = lax.broadcasted_iota(jnp.int32, gcol.shape, 1)
    g_ref[...] = jnp.where(lane < nh, gcol, -_softplus(-gcol))
    row = lax.broadcasted_iota(jnp.int32, grow.shape, 0)
    gT_ref[...] = jnp.where(row < nh, grow, -_softplus(-grow))


def _sub_tiles(n, fn):
    if n == 1:
        fn(0)
    else:
        def body(i, carry):
            fn(i)
            return carry
        lax.fori_loop(0, n, body, 0)


def _conv_qkv_kernel(cfg, lt, xbc_ref, xm_ref, hxbc_ref, hxm_ref, cwx_ref, cbx_ref, cwm_ref, cbm_ref,
                     wq_ref, wk_ref, wv_ref, wg_ref, wgT_ref, bg_row_ref, bg_col_ref,
                     xa_ref, q_ref, k_ref, v_ref, g_ref, gT_ref, px_ref, pm_ref):
    @pl.when(pl.program_id(1) == 0)
    def _():
        px_ref[...] = hxbc_ref[0]
        pm_ref[...] = hxm_ref[0]

    def tile(i):
        rows = pl.ds(pl.multiple_of(i * lt, lt), lt)
        _conv_qkv_tile(cfg, xbc_ref.at[rows], xm_ref.at[rows], cwx_ref, cbx_ref, cwm_ref, cbm_ref,
                       wq_ref, wk_ref, wv_ref, wg_ref, wgT_ref, bg_row_ref, bg_col_ref,
                       xa_ref.at[rows], q_ref.at[rows], k_ref.at[rows], v_ref.at[rows], g_ref.at[rows],
                       gT_ref.at[0, i], px_ref, pm_ref)

    _sub_tiles(xbc_ref.shape[0] // lt, tile)


def conv_qkv(cfg, proj, hist_xbc, hist_xm, cwx, cbx, cwm, cbm, wq, wk, wv, wg, wgT, bg_row, bg_col,
             row0, n_seq, seq_len, lt, tiles_per_step):
    d = cfg.d_model
    xbc = cfg.xbc_dim
    m = n_seq * seq_len
    rows = lt * tiles_per_step
    nt = seq_len // rows
    ng = 2 * cfg.ml_heads
    xbc_blk = (3 * d) // xbc
    blk0 = row0 // rows
    row = lambda s, l: (s * nt + l, 0)
    const2 = lambda s, l: (0, 0)
    const3 = lambda s, l: (0, 0, 0)
    return pl.pallas_call(
        functools.partial(_conv_qkv_kernel, cfg, lt),
        out_shape=(jax.ShapeDtypeStruct((m, xbc), F32),
                   jax.ShapeDtypeStruct((m, d), F32), jax.ShapeDtypeStruct((m, d), F32),
                   jax.ShapeDtypeStruct((m, d), F32),
                   jax.ShapeDtypeStruct((m, LANES), F32),
                   jax.ShapeDtypeStruct((n_seq, seq_len // lt, ng, lt), F32)),
        grid=(n_seq, nt),
        in_specs=[pl.BlockSpec((rows, xbc), lambda s, l: (blk0 + s * nt + l, xbc_blk)),
                  pl.BlockSpec((rows, d), lambda s, l: (blk0 + s * nt + l, 1)),
                  pl.BlockSpec((1, SUBLANES, xbc), lambda s, l: (s, 0, 0)),
                  pl.BlockSpec((1, SUBLANES, d), lambda s, l: (s, 0, 0)),
                  pl.BlockSpec((cfg.conv_w, xbc), const2), pl.BlockSpec((1, xbc), const2),
                  pl.BlockSpec((cfg.conv_w, d), const2), pl.BlockSpec((1, d), const2),
                  pl.BlockSpec(wq.shape, const3), pl.BlockSpec(wk.shape, const3),
                  pl.BlockSpec(wv.shape, const3),
                  pl.BlockSpec(wg.shape, const2), pl.BlockSpec(wgT.shape, const2),
                  pl.BlockSpec((1, LANES), const2), pl.BlockSpec((ng, 1), const2)],
        out_specs=(pl.BlockSpec((rows, xbc), row), pl.BlockSpec((rows, d), row), pl.BlockSpec((rows, d), row),
                   pl.BlockSpec((rows, d), row), pl.BlockSpec((rows, LANES), row),
                   pl.BlockSpec((1, tiles_per_step, ng, lt), lambda s, l: (s, l, 0, 0))),
        scratch_shapes=[pltpu.VMEM((SUBLANES, xbc), F32), pltpu.VMEM((SUBLANES, d), F32)],
        compiler_params=_cparams(("arbitrary", "arbitrary")),
        name="conv_qkv",
    )(proj, proj, hist_xbc, hist_xm, cwx, cbx, cwm, cbm, wq, wk, wv, wg, wgT, bg_row, bg_col)


def _tri(q, lower):
    r = lax.broadcasted_iota(jnp.int32, (q, q), 0)
    c = lax.broadcasted_iota(jnp.int32, (q, q), 1)
    return (c <= r) if lower else (r <= c)


def _ssd_chunk(cfg, xa_ref, d_ref, dT_ref, arow_ref, acol_ref, dskip_ref, y_ref, st_ref):
    q = xa_ref.shape[0]
    dm = cfg.d_model
    ns = cfg.ssd_state
    hp = cfg.ssd_head_dim
    hpg = cfg.ssd_heads // cfg.ssd_groups
    heads_per_tile = LANES // hp
    n_tiles = cfg.ssd_heads // heads_per_tile

    causal = _tri(q, True)
    tril = causal.astype(F32)
    triu = _tri(q, False).astype(F32)
    dcol = d_ref[...]
    drow = dT_ref[...]
    acum = jnp.dot(tril, dcol * arow_ref[...], precision=HI, preferred_element_type=F32)
    acumT = jnp.dot(drow * acol_ref[...], triu, precision=HI, preferred_element_type=F32)
    nt_dims = (((1,), (1,)), ((), ()))
    tn_dims = (((0,), (0,)), ((), ()))
    lane = lax.broadcasted_iota(jnp.int32, (q, LANES), 1)
    srow = lax.broadcasted_iota(jnp.int32, (LANES, ns), 0)

    cbs = []
    bgs = []
    cgs = []
    for g in range(cfg.ssd_groups):
        bg = xa_ref[:, dm + g * ns: dm + (g + 1) * ns].astype(BF16)
        cg = xa_ref[:, dm + cfg.bc_dim + g * ns: dm + cfg.bc_dim + (g + 1) * ns].astype(BF16)
        cbs.append(lax.dot_general(cg, bg, nt_dims, preferred_element_type=F32))
        bgs.append(bg)
        cgs.append(cg)

    for t in range(n_tiles):
        h0 = t * heads_per_tile
        g = h0 // hpg
        cols = slice(t * LANES, (t + 1) * LANES)
        x = xa_ref[:, cols]
        dsel = jnp.zeros((q, LANES), F32)
        asel = jnp.zeros((q, LANES), F32)
        rdec = jnp.zeros((LANES, ns), F32)
        for i in range(heads_per_tile):
            h = h0 + i
            in_head = (lane >= i * hp) & (lane < (i + 1) * hp)
            a_last = acum[q - 1:q, h:h + 1]
            dsel = jnp.where(in_head, dcol[:, h:h + 1], dsel)
            asel = jnp.where(in_head, acum[:, h:h + 1], asel)
            rdec = jnp.where((srow >= i * hp) & (srow < (i + 1) * hp), jnp.exp(a_last), rdec)
        esel = jnp.exp(asel)
        tsel = jnp.exp(asel[q - 1:q, :] - asel)
        xd = x * dsel
        y = x * dskip_ref[:, cols]
        for i in range(heads_per_tile):
            h = h0 + i
            in_head = (lane >= i * hp) & (lane < (i + 1) * hp)
            seg = jnp.where(causal, acum[:, h:h + 1] - acumT[h:h + 1, :], -jnp.inf)
            w = (cbs[g] * jnp.exp(seg)).astype(BF16)
            xdh = jnp.where(in_head, xd, 0.0).astype(BF16)
            y = y + jnp.dot(w, xdh, preferred_element_type=F32)
        s_old = st_ref[cols, :]
        ys = lax.dot_general(cgs[g], s_old.astype(BF16), nt_dims, preferred_element_type=F32)
        y_ref[:, cols] = y + esel * ys
        upd = lax.dot_general((xd * tsel).astype(BF16), bgs[g], tn_dims, preferred_element_type=F32)
        st_ref[cols, :] = rdec * s_old + upd


def _ssd_kernel(cfg, n_steps, q, xa_ref, d_ref, dT_ref, s0_ref, arow_ref, acol_ref, dskip_ref,
                y_ref, sout_ref, st_ref):
    @pl.when(pl.program_id(1) == 0)
    def _():
        st_ref[...] = s0_ref[...]

    for b in range(xa_ref.shape[0]):
        def chunk(c, b=b):
            rows = pl.ds(pl.multiple_of(c * q, q), q)
            _ssd_chunk(cfg, xa_ref.at[b, rows], d_ref.at[b, rows], dT_ref.at[b, c], arow_ref, acol_ref, dskip_ref,
                       y_ref.at[b, rows], st_ref.at[b])
        _sub_tiles(xa_ref.shape[1] // q, chunk)

    @pl.when(pl.program_id(1) == n_steps - 1)
    def _():
        sout_ref[...] = st_ref[...]


def ssd_scan(cfg, xa, d, dT, s0, a_row, a_col, dskip, n_seq, seq_len, q, nb, cps):
    dm = cfg.d_model
    nc = seq_len // (q * cps)
    blk3 = lambda s, c: (s, c, 0)
    st3 = lambda s, c: (s, 0, 0)
    const2 = lambda s, c: (0, 0)
    return pl.pallas_call(
        functools.partial(_ssd_kernel, cfg, nc, q),
        out_shape=(jax.ShapeDtypeStruct((n_seq, seq_len, dm), F32),
                   jax.ShapeDtypeStruct((n_seq, dm, cfg.ssd_state), F32)),
        grid=(n_seq // nb, nc),
        in_specs=[pl.BlockSpec((nb, q * cps, cfg.xbc_dim), blk3),
                  pl.BlockSpec((nb, q * cps, LANES), blk3),
                  pl.BlockSpec((nb, cps, cfg.ssd_heads, q), lambda s, c: (s, c, 0, 0)),
                  pl.BlockSpec((nb, dm, cfg.ssd_state), st3),
                  pl.BlockSpec((1, LANES), const2),
                  pl.BlockSpec((cfg.ssd_heads, 1), const2),
                  pl.BlockSpec((1, dm), const2)],
        out_specs=(pl.BlockSpec((nb, q * cps, dm), blk3),
                   pl.BlockSpec((nb, dm, cfg.ssd_state), st3)),
        scratch_shapes=[pltpu.VMEM((nb, dm, cfg.ssd_state), F32)],
        compiler_params=_cparams(("arbitrary", "arbitrary")),
        name="ssd_scan",
    )(xa, d, dT, s0, a_row, a_col, dskip)


def _mlstm_chunk(cfg, q_ref, k_ref, v_ref, g_ref, gT_ref, h_ref, c_ref, n_ref, m_ref):
    ql = q_ref.shape[0]
    hd = cfg.ml_head_dim
    nh = cfg.ml_heads

    causal = _tri(ql, True)
    gcol = g_ref[...]
    grow = gT_ref[...]
    bcum = jnp.dot(causal.astype(F32), gcol, precision=HI, preferred_element_type=F32)
    bcumT = jnp.dot(grow, _tri(ql, False).astype(F32), precision=HI, preferred_element_type=F32)
    nt_dims = (((1,), (1,)), ((), ()))
    tn_dims = (((0,), (0,)), ((), ()))

    for h in range(nh):
        sl = slice(h * hd, (h + 1) * hd)
        b_col = bcum[:, nh + h:nh + h + 1]
        b_row = bcumT[nh + h:nh + h + 1, :]
        i_col = gcol[:, h:h + 1]
        i_row = grow[h:h + 1, :]
        m_prev = m_ref[h:h + 1, 0:1]
        dlog = jnp.where(causal, b_col - b_row + i_row, -jnp.inf)
        inter = b_col + m_prev
        mt = jnp.maximum(inter, jnp.max(dlog, axis=1, keepdims=True))
        qh = q_ref[:, sl]
        kh = k_ref[:, sl]
        vh = v_ref[:, sl]
        qb = qh.astype(BF16)
        kb = kh.astype(BF16)
        s = lax.dot_general(qb, kb, nt_dims, preferred_element_type=F32) * jnp.exp(dlog - mt)
        gdec = jnp.exp(inter - mt)
        c_old = c_ref[sl, :]
        n_old = n_ref[h:h + 1, :]
        qc = lax.dot_general(qb, c_old.astype(BF16), nt_dims, preferred_element_type=F32)
        num = jnp.dot(s.astype(BF16), vh.astype(BF16), preferred_element_type=F32) + gdec * qc
        den = jnp.sum(s, axis=1, keepdims=True) + gdec * jnp.sum(qh * n_old, axis=1, keepdims=True)
        h_ref[:, sl] = num / jnp.maximum(jnp.abs(den), jnp.exp(-mt))
        m_new = mt[ql - 1:ql, :]
        gs = jnp.exp(b_col[ql - 1:ql, :] - b_col + i_col - m_new)
        gc = jnp.exp(inter[ql - 1:ql, :] - m_new)
        upd = lax.dot_general((vh * gs).astype(BF16), kb, tn_dims, preferred_element_type=F32)
        c_ref[sl, :] = gc * c_old + upd
        n_ref[h:h + 1, :] = gc * n_old + jnp.sum(gs * kh, axis=0, keepdims=True)
        m_ref[h:h + 1, :] = jnp.broadcast_to(m_new, (1, LANES))


def _mlstm_kernel(cfg, n_steps, q, q_ref, k_ref, v_ref, g_ref, gT_ref, c0_ref, n0_ref, m0_ref,
                  h_ref, cout_ref, nout_ref, mout_ref, c_ref, n_ref, m_ref):
    @pl.when(pl.program_id(1) == 0)
    def _():
        c_ref[...] = c0_ref[...]
        n_ref[...] = n0_ref[...]
        m_ref[...] = m0_ref[...]

    for b in range(q_ref.shape[0]):
        def chunk(c, b=b):
            rows = pl.ds(pl.multiple_of(c * q, q), q)
            _mlstm_chunk(cfg, q_ref.at[b, rows], k_ref.at[b, rows], v_ref.at[b, rows], g_ref.at[b, rows],
                         gT_ref.at[b, c], h_ref.at[b, rows], c_ref.at[b], n_ref.at[b], m_ref.at[b])
        _sub_tiles(q_ref.shape[1] // q, chunk)

    @pl.when(pl.program_id(1) == n_steps - 1)
    def _():
        cout_ref[...] = c_ref[...]
        nout_ref[...] = n_ref[...]
        mout_ref[...] = m_ref[...]


def mlstm_scan(cfg, qa, ka, va, g, gT, c0, n0, m0, n_seq, seq_len, q, nb, cps):
    d = cfg.d_model
    hd = cfg.ml_head_dim
    nh = cfg.ml_heads
    nc = seq_len // (q * cps)
    rows = q * cps
    blk3 = lambda s, c: (s, c, 0)
    st3 = lambda s, c: (s, 0, 0)
    return pl.pallas_call(
        functools.partial(_mlstm_kernel, cfg, nc, q),
        out_shape=(jax.ShapeDtypeStruct((n_seq, seq_len, d), F32),
                   jax.ShapeDtypeStruct((n_seq, d, hd), F32),
                   jax.ShapeDtypeStruct((n_seq, nh, hd), F32),
                   jax.ShapeDtypeStruct((n_seq, nh, LANES), F32)),
        grid=(n_seq // nb, nc),
        in_specs=[pl.BlockSpec((nb, rows, d), blk3), pl.BlockSpec((nb, rows, d), blk3),
                  pl.BlockSpec((nb, rows, d), blk3),
                  pl.BlockSpec((nb, rows, LANES), blk3),
                  pl.BlockSpec((nb, cps, 2 * nh, q), lambda s, c: (s, c, 0, 0)),
                  pl.BlockSpec((nb, d, hd), st3), pl.BlockSpec((nb, nh, hd), st3),
                  pl.BlockSpec((nb, nh, LANES), st3)],
        out_specs=(pl.BlockSpec((nb, rows, d), blk3),
                   pl.BlockSpec((nb, d, hd), st3), pl.BlockSpec((nb, nh, hd), st3),
                   pl.BlockSpec((nb, nh, LANES), st3)),
        scratch_shapes=[pltpu.VMEM((nb, d, hd), F32), pltpu.VMEM((nb, nh, hd), F32),
                        pltpu.VMEM((nb, nh, LANES), F32)],
        compiler_params=_cparams(("arbitrary", "arbitrary")),
        name="mlstm_scan",
    )(qa, ka, va, g, gT, c0, n0, m0)


def _group_norm(x, w_ref, col0, groups, width):
    parts = []
    for g in range(groups):
        seg = x[:, g * width:(g + 1) * width]
        parts.append(seg * lax.rsqrt(jnp.mean(seg * seg, axis=-1, keepdims=True) + EPS)
                     * w_ref[:, col0 + g * width: col0 + (g + 1) * width])
    return parts


def _outproj_kernel(cfg, ys_ref, z_ref, hm_ref, o_ref, x_ref, nws_ref, nwm_ref, w_ref, out_ref):
    d = cfg.d_model
    ws = d // cfg.ssd_groups
    yz = ys_ref[...] * _silu(z_ref[...])
    ssd_half = jnp.concatenate([part.astype(BF16) for part in _group_norm(yz, nws_ref, 0, cfg.ssd_groups, ws)],
                               axis=-1)
    acc = x_ref[...] + jnp.dot(ssd_half, w_ref[0:d, :], preferred_element_type=F32)
    gate = _sigmoid(o_ref[...])
    wm = cfg.ml_head_dim
    ml_half = jnp.concatenate(
        [(part * gate[:, g * wm:(g + 1) * wm]).astype(BF16)
         for g, part in enumerate(_group_norm(hm_ref[...], nwm_ref, 0, cfg.ml_heads, wm))], axis=-1)
    out_ref[...] = acc + jnp.dot(ml_half, w_ref[d:2 * d, :], preferred_element_type=F32)


def out_proj(cfg, ys, proj, proj_row0, hm, x, nws, nwm, w_out, bm):
    m, d = x.shape
    blk0 = proj_row0 // bm
    const2 = lambda i: (0, 0)
    full = lambda i: (i, 0)
    return pl.pallas_call(
        functools.partial(_outproj_kernel, cfg),
        out_shape=jax.ShapeDtypeStruct((m, d), F32),
        grid=(m // bm,),
        in_specs=[pl.BlockSpec((bm, d), full),
                  pl.BlockSpec((bm, d), lambda i: (i + blk0, 0)),
                  pl.BlockSpec((bm, d), full),
                  pl.BlockSpec((bm, d), lambda i: (i + blk0, 2)),
                  pl.BlockSpec((bm, d), full),
                  pl.BlockSpec((1, d), const2), pl.BlockSpec((1, d), const2),
                  pl.BlockSpec((2 * d, d), const2, pipeline_mode=pl.Buffered(1))],
        out_specs=pl.BlockSpec((bm, d), full),
        compiler_params=_cparams(("arbitrary",)),
        name="out_proj",
    )(ys, proj, hm, proj, x, nws, nwm, w_out)


def _router_kernel(cfg, n_a, xa_ref, xb_ref, nw_ref, wr_ref, br_ref, ei_ref, wt_ref, cnt_out_ref, rows_ref,
                   cnt_ref, x_ref):
    ne = cfg.n_experts
    epg = cfg.experts_per_group
    ngr = cfg.n_groups
    bm = x_ref.shape[0]
    i = pl.program_id(0)

    @pl.when(i == 0)
    def _():
        cnt_ref[...] = jnp.zeros_like(cnt_ref)

    @pl.when(i < n_a)
    def _():
        x_ref[...] = xa_ref[...]

    @pl.when(i >= n_a)
    def _():
        x_ref[...] = xb_ref[...]

    hb = _rms(x_ref[...], nw_ref[...]).astype(BF16)
    rows_ref[...] = _lanes_to_rows(_pack_bf16_pairs(hb))
    logits = jnp.dot(hb, wr_ref[...], preferred_element_type=F32) + br_ref[...]
    lane = lax.broadcasted_iota(jnp.int32, logits.shape, 1)
    big = jnp.int32(2 ** 30)
    neg = -jnp.inf

    def first_argmax(vals):
        mx = jnp.max(vals, axis=-1, keepdims=True)
        idx = jnp.min(jnp.where(vals == mx, lane, big), axis=-1, keepdims=True)
        return mx, idx

    is_group = (lane >= ne) & (lane < ne + ngr)
    gl = jnp.where(is_group, logits, neg)
    gmax, gidx = first_argmax(gl)
    p_g = 1.0 / jnp.sum(jnp.exp(gl - gmax), axis=-1, keepdims=True)
    e_lo = (gidx - ne) * epg
    in_sel = (lane >= e_lo) & (lane < e_lo + epg)
    el = jnp.where(in_sel, logits, neg)
    pe = jnp.exp(el - jnp.max(el, axis=-1, keepdims=True))
    pe = jnp.where(in_sel, pe / jnp.sum(pe, axis=-1, keepdims=True), -1.0)
    p1, i1 = first_argmax(pe)
    p2, i2 = first_argmax(jnp.where(lane == i1, -1.0, pe))
    wsum = p1 + p2
    wt_ref[...] = jnp.where(lane == 0, p_g * p1 / wsum, jnp.where(lane == 1, p_g * p2 / wsum, 0.0))

    oh1 = jnp.where(lane == i1, 1.0, 0.0)
    oh2 = jnp.where(lane == i2, 1.0, 0.0)
    r = lax.broadcasted_iota(jnp.int32, (bm, bm), 0)
    c = lax.broadcasted_iota(jnp.int32, (bm, bm), 1)
    before = jnp.where(c < r, 1.0, 0.0).astype(BF16)
    ahead1 = jnp.dot(before, oh1.astype(BF16), preferred_element_type=F32)
    ahead2 = jnp.dot(before, oh2.astype(BF16), preferred_element_type=F32)
    cnt = cnt_ref[...]
    tot1 = jnp.sum(oh1, axis=0, keepdims=True)
    rank1 = jnp.sum(oh1 * (cnt + ahead1), axis=-1, keepdims=True)
    rank2 = jnp.sum(oh2 * (cnt + tot1 + ahead2), axis=-1, keepdims=True)
    cnt_new = cnt + tot1 + jnp.sum(oh2, axis=0, keepdims=True)
    cnt_ref[...] = cnt_new
    cnt_out_ref[...] = cnt_new
    ei_ref[...] = jnp.where(lane == 0, i1, jnp.where(lane == 1, i2, jnp.where(
        lane == 2, rank1.astype(jnp.int32), jnp.where(lane == 3, rank2.astype(jnp.int32), 0))))


def router(cfg, xa, xb, nw, wr, br, bm):
    d = xa.shape[1]
    n_a = xa.shape[0] // bm
    n_b = xb.shape[0] // bm
    m = xa.shape[0] + xb.shape[0]
    nch = d // (2 * LANES)
    const2 = lambda i: (0, 0)
    return pl.pallas_call(
        functools.partial(_router_kernel, cfg, n_a),
        out_shape=(jax.ShapeDtypeStruct((m, LANES), jnp.int32), jax.ShapeDtypeStruct((m, LANES), F32),
                   jax.ShapeDtypeStruct((1, LANES), F32), jax.ShapeDtypeStruct((m, nch, LANES), jnp.uint32)),
        grid=(n_a + n_b,),
        in_specs=[pl.BlockSpec((bm, d), lambda i: (jnp.minimum(i, n_a - 1), 0)),
                  pl.BlockSpec((bm, d), lambda i: (jnp.maximum(i - n_a, 0), 0)),
                  pl.BlockSpec((1, d), const2),
                  pl.BlockSpec((d, LANES), const2), pl.BlockSpec((1, LANES), const2)],
        out_specs=(pl.BlockSpec((bm, LANES), lambda i: (i, 0)), pl.BlockSpec((bm, LANES), lambda i: (i, 0)),
                   pl.BlockSpec((1, LANES), const2), pl.BlockSpec((bm, nch, LANES), lambda i: (i, 0, 0))),
        scratch_shapes=[pltpu.VMEM((1, LANES), F32), pltpu.VMEM((bm, d), F32)],
        compiler_params=_cparams(("arbitrary",)),
        name="router",
    )(xa, xb, nw, wr, br)


def _pack_bf16_pairs(x):
    half = x.shape[1] // 2
    bits = lambda v: lax.bitcast_convert_type(v.astype(BF16).astype(F32), jnp.uint32)
    return bits(x[:, :half]) | (bits(x[:, half:]) >> 16)


def _unpack_bf16_pairs(p):
    hi = lax.bitcast_convert_type(p & jnp.uint32(0xFFFF0000), F32)
    lo = lax.bitcast_convert_type(p << 16, F32)
    return jnp.concatenate([hi, lo], axis=-1)


def _rows_to_lanes(g):
    t = pltpu.einshape("rcl->crl", g)
    return jnp.concatenate([t[c] for c in range(t.shape[0])], axis=-1)


def _lanes_to_rows(x):
    parts = jnp.stack([x[:, c * LANES:(c + 1) * LANES] for c in range(x.shape[1] // LANES)], axis=0)
    return pltpu.einshape("crl->rcl", parts)


GATHER_GROUP = 8


def _gather_rows(idx_ref, src_hbm, dst, sem, n_groups):
    def body(g, carry):
        for u in range(GATHER_GROUP):
            r = g * GATHER_GROUP + u
            pltpu.make_async_copy(src_hbm.at[idx_ref[0, 0, r]], dst.at[r], sem).start()
        return carry
    lax.fori_loop(0, n_groups, body, 0)


def _wait_rows(src_hbm, dst, sem, n):
    pltpu.make_async_copy(src_hbm.at[pl.ds(0, n)], dst.at[pl.ds(0, n)], sem).wait()


def _moe_kernel(cfg, te_ref, nr_ref, first_ref, wslot_ref, nexte_ref, nv_ref, src_ref, srcn_ref, x_hbm,
                wg_hbm, wu_hbm, wd_hbm, ys_ref, xbuf, sem, wgf, wuf, wdf, wsem, wgb, wub, wdb):
    j = pl.program_id(0)
    n_valid = nv_ref[0]

    def weight_copies(e, slot):
        return (pltpu.make_async_copy(wg_hbm.at[e], wgf.at[slot], wsem.at[slot, 0]),
                pltpu.make_async_copy(wu_hbm.at[e], wuf.at[slot], wsem.at[slot, 1]),
                pltpu.make_async_copy(wd_hbm.at[e], wdf.at[slot], wsem.at[slot, 2]))

    @pl.when(j == 0)
    def _():
        for cp in weight_copies(te_ref[0], 0):
            cp.start()
        xbuf[...] = jnp.zeros_like(xbuf)
        _gather_rows(src_ref, x_hbm, xbuf.at[0], sem.at[0], nr_ref[0] // GATHER_GROUP)

    @pl.when(j + 1 < n_valid)
    def _():
        nslot = (j + 1) % 2
        _gather_rows(srcn_ref, x_hbm, xbuf.at[nslot], sem.at[nslot], nr_ref[j + 1] // GATHER_GROUP)

    @pl.when(j < n_valid)
    def _():
        @pl.when(first_ref[j] == 1)
        def _():
            ws = wslot_ref[j]
            for cp in weight_copies(te_ref[j], ws):
                cp.wait()

            @pl.when(nexte_ref[j] >= 0)
            def _():
                for cp in weight_copies(nexte_ref[j], 1 - ws):
                    cp.start()

            wgb[...] = wgf[ws].astype(BF16)
            wub[...] = wuf[ws].astype(BF16)
            wdb[...] = wdf[ws].astype(BF16)

        slot = j % 2
        _wait_rows(x_hbm, xbuf.at[slot], sem.at[slot], nr_ref[j])
        hb = _unpack_bf16_pairs(_rows_to_lanes(xbuf[slot])).astype(BF16)
        hid = (_silu(jnp.dot(hb, wgb[...], preferred_element_type=F32))
               * jnp.dot(hb, wub[...], preferred_element_type=F32))
        y = jnp.dot(hid.astype(BF16), wdb[...], preferred_element_type=F32)
        ys_ref[...] = _lanes_to_rows(_pack_bf16_pairs(y))

    @pl.when(j >= n_valid)
    def _():
        ys_ref[...] = jnp.zeros_like(ys_ref)


def moe_routed(cfg, tables, src, x_rows, wg, wu, wd, tm):
    n_tiles = src.shape[0]
    d = cfg.d_model
    de = cfg.d_expert
    nch = x_rows.shape[1]
    return pl.pallas_call(
        functools.partial(_moe_kernel, cfg),
        out_shape=jax.ShapeDtypeStruct((n_tiles * tm, nch, LANES), jnp.uint32),
        grid_spec=pltpu.PrefetchScalarGridSpec(
            num_scalar_prefetch=len(tables),
            grid=(n_tiles,),
            in_specs=[pl.BlockSpec((1, 1, tm), lambda j, *_: (j, 0, 0), memory_space=pltpu.SMEM),
                      pl.BlockSpec((1, 1, tm), lambda j, *_: (jnp.minimum(j + 1, n_tiles - 1), 0, 0),
                                   memory_space=pltpu.SMEM),
                      pl.BlockSpec(memory_space=pl.ANY),
                      pl.BlockSpec(memory_space=pl.ANY), pl.BlockSpec(memory_space=pl.ANY),
                      pl.BlockSpec(memory_space=pl.ANY)],
            out_specs=pl.BlockSpec((tm, nch, LANES), lambda j, *_: (j, 0, 0)),
            scratch_shapes=[pltpu.VMEM((2, tm, nch, LANES), jnp.uint32), pltpu.SemaphoreType.DMA((2,)),
                            pltpu.VMEM((2, d, de), F32), pltpu.VMEM((2, d, de), F32), pltpu.VMEM((2, de, d), F32),
                            pltpu.SemaphoreType.DMA((2, 3)),
                            pltpu.VMEM((d, de), BF16), pltpu.VMEM((d, de), BF16), pltpu.VMEM((de, d), BF16)]),
        compiler_params=_cparams(("arbitrary",)),
        name="moe",
    )(*tables, src, src, x_rows, wg, wu, wd)


def _combine_kernel(n_steps, d0_ref, d1_ref, d0n_ref, d1n_ref, wt_ref, x1_ref, fw_ref, ys_hbm, y_ref, gbuf, sem):
    i = pl.program_id(0)
    groups = x1_ref.shape[0] // GATHER_GROUP

    def fetch(r0_ref, r1_ref, slot):
        _gather_rows(r0_ref, ys_hbm, gbuf.at[slot, 0], sem.at[slot, 0], groups)
        _gather_rows(r1_ref, ys_hbm, gbuf.at[slot, 1], sem.at[slot, 1], groups)

    @pl.when(i == 0)
    def _():
        fetch(d0_ref, d1_ref, 0)

    @pl.when(i + 1 < n_steps)
    def _():
        fetch(d0n_ref, d1n_ref, (i + 1) % 2)

    slot = i % 2
    bm = x1_ref.shape[0]
    _wait_rows(ys_hbm, gbuf.at[slot, 0], sem.at[slot, 0], bm)
    _wait_rows(ys_hbm, gbuf.at[slot, 1], sem.at[slot, 1], bm)
    wt = wt_ref[...]
    acc = (x1_ref[...] + wt[:, 0:1] * _unpack_bf16_pairs(_rows_to_lanes(gbuf[slot, 0]))
           + wt[:, 1:2] * _unpack_bf16_pairs(_rows_to_lanes(gbuf[slot, 1])))
    y_ref[...] = _rms(acc, fw_ref[...])


def moe_combine(cfg, dest0, dest1, wt, wt_blk0, x1, fw, ys, bm):
    m, d = x1.shape
    nch = ys.shape[1]
    n = m // bm
    cur = lambda: pl.BlockSpec((1, 1, bm), lambda i: (i, 0, 0), memory_space=pltpu.SMEM)
    nxt = lambda: pl.BlockSpec((1, 1, bm), lambda i: (jnp.minimum(i + 1, n - 1), 0, 0), memory_space=pltpu.SMEM)
    return pl.pallas_call(
        functools.partial(_combine_kernel, n),
        out_shape=jax.ShapeDtypeStruct((m, d), F32),
        grid=(n,),
        in_specs=[cur(), cur(), nxt(), nxt(),
                  pl.BlockSpec((bm, LANES), lambda i: (i + wt_blk0, 0)),
                  pl.BlockSpec((bm, d), lambda i: (i, 0)), pl.BlockSpec((1, d), lambda i: (0, 0)),
                  pl.BlockSpec(memory_space=pl.ANY)],
        out_specs=pl.BlockSpec((bm, d), lambda i: (i, 0)),
        scratch_shapes=[pltpu.VMEM((2, 2, bm, nch, LANES), jnp.uint32), pltpu.SemaphoreType.DMA((2, 2))],
        compiler_params=_cparams(("arbitrary",)),
        name="moe_combine",
    )(dest0, dest1, dest0, dest1, wt, x1, fw, ys)


def _route_tables(cfg, ei, cnt, tm, n_tiles):
    ne = cfg.n_experts
    i32 = jnp.int32
    counts = cnt[0, :ne].astype(i32)
    tiles_e = (counts + tm - 1) // tm
    tile_end = jnp.cumsum(tiles_e)
    tile_start = tile_end - tiles_e
    row_off = tile_start * tm
    tile_id = jnp.arange(n_tiles, dtype=i32)
    tile_expert = jnp.minimum(jnp.sum((tile_id[:, None] >= tile_end[None, :]).astype(i32), axis=1), ne - 1)
    used = jnp.clip(counts[tile_expert] - (tile_id - tile_start[tile_expert]) * tm, 0, tm)
    used = jnp.where(tile_id < tile_end[-1], used, 0)
    tile_rows = (used + GATHER_GROUP - 1) // GATHER_GROUP * GATHER_GROUP
    n_valid = tile_end[-1]
    prev_expert = jnp.concatenate([jnp.full((1,), -1, i32), tile_expert[:-1]])
    first = ((tile_expert != prev_expert) & (tile_id < n_valid)).astype(i32)
    wslot = (jnp.cumsum(first) - 1) % 2
    e_id = jnp.arange(ne, dtype=i32)
    later = (e_id[None, :] > e_id[:, None]) & (tiles_e[None, :] > 0)
    next_e = jnp.min(jnp.where(later, e_id[None, :], ne), axis=1)
    next_e = jnp.where(next_e < ne, next_e, -1)[tile_expert]
    tables = (tile_expert, tile_rows, first, wslot.astype(i32), next_e.astype(i32), n_valid[None].astype(i32))
    picks = ei[:, 0:4].T
    pick_off = jnp.sum(jnp.where(picks[None, 0:2] == e_id[:, None, None], row_off[:, None, None], 0), axis=0)
    dest = pick_off + picks[2:4]
    tok = jnp.broadcast_to(jnp.arange(ei.shape[0], dtype=i32)[None, :], dest.shape)
    src = jnp.zeros((n_tiles * tm,), i32).at[dest.reshape(-1)].set(
        tok.reshape(-1), unique_indices=True, mode="promise_in_bounds")
    return tables, src.reshape(n_tiles, 1, tm), dest


def _pad_hist(hist):
    return jnp.pad(hist, ((0, 0), (SUBLANES - hist.shape[1], 0), (0, 0)))


def _tile(m, pref):
    return pref if m % pref == 0 else m


CHUNKS_PER_STEP = 4
CONV_TILES_PER_STEP = 1


def _mixer_segment(cfg, proj, row0, dcol, dT, n_seq, seq_len, hist_xbc, hist_xm, s0, c0, n0, m0, p):
    d = cfg.d_model
    m = n_seq * seq_len
    q = min(cfg.chunk, seq_len)
    nc = seq_len // q
    cps = CHUNKS_PER_STEP if nc % CHUNKS_PER_STEP == 0 else 1
    xa, qa, ka, va, g, gT = conv_qkv(cfg, proj, hist_xbc, hist_xm, p["cwx"], p["cbx"], p["cwm"], p["cbm"],
                                     p["wq"], p["wk"], p["wv"], p["wg"], p["wgT"], p["bg_row"], p["bg_col"],
                                     row0, n_seq, seq_len, q,
                                     CONV_TILES_PER_STEP if nc % CONV_TILES_PER_STEP == 0 else 1)
    dTc = dT[:cfg.ssd_heads].reshape(cfg.ssd_heads, n_seq, nc, q).transpose(1, 2, 0, 3)
    nb = 2 if (nc == 1 and n_seq % 2 == 0) else 1
    seq3 = lambda a: a.reshape(n_seq, seq_len, a.shape[-1])
    ys, s_new = ssd_scan(cfg, seq3(xa), seq3(dcol), dTc, s0, p["a_row"], p["a_col"], p["dskip"],
                         n_seq, seq_len, q, nb, cps)
    hm, c_new, n_new, m_new = mlstm_scan(cfg, seq3(qa), seq3(ka), seq3(va), seq3(g), gT, c0, n0, m0,
                                         n_seq, seq_len, q, nb, cps)
    ys = ys.reshape(m, d)
    hm = hm.reshape(m, d)
    keep = cfg.conv_w - 1
    groups = proj.reshape(proj.shape[0] // SUBLANES, SUBLANES, proj.shape[1])
    first = (row0 + seq_len) // SUBLANES - 1
    step = seq_len // SUBLANES
    tail = lax.slice(groups, (first, SUBLANES - keep, 0),
                     (first + (n_seq - 1) * step + 1, SUBLANES, proj.shape[1]), (step, 1, 1))
    tail_xbc = tail[:, :, 3 * d:]
    tail_xm = tail[:, :, d:2 * d]
    return ys, hm, (tail_xbc, s_new, tail_xm, c_new, n_new, m_new)


MOE_TILE = 256


def _ffn(cfg, segments, p):
    d = cfg.d_model
    x1s = [out_proj(cfg, ys, proj, row0, hm, x2d, p["ssd_norm_w"], p["mlstm_norm_w"], p["w_out"],
                    _tile(x2d.shape[0], 256)) for x2d, proj, row0, ys, hm in segments]
    fits = lambda t: all(x1.shape[0] % t == 0 for x1 in x1s)
    ei, wt, cnt, x_rows = router(cfg, x1s[0], x1s[1], p["norm_ffn_w"], p["wr"], p["br"], 512 if fits(512) else 128)
    bm = 256 if fits(256) else 128
    n_tok = ei.shape[0]
    n_tiles = (2 * n_tok + cfg.n_experts * (MOE_TILE - 1)) // MOE_TILE
    tables, src, dest = _route_tables(cfg, ei, cnt, MOE_TILE, n_tiles)
    ys_sorted = moe_routed(cfg, tables, src, x_rows, p["w_gate"], p["w_up"], p["w_down"], MOE_TILE)
    outs = []
    off = 0
    for x1 in x1s:
        m = x1.shape[0]
        dseg = dest[:, off:off + m].reshape(2, m // bm, 1, bm)
        outs.append(moe_combine(cfg, dseg[0], dseg[1], wt, off // bm, x1, p["final_norm_w"], ys_sorted, bm))
        off += m
    return outs


def _prep_params(cfg, norm_mix_w, w_in, conv_ssd_w, conv_ssd_b, dt_bias, a_log, d_skip, ssd_norm_w,
                 conv_mlstm_w, conv_mlstm_b, w_q, w_k, w_v, w_igate, b_igate, w_fgate, b_fgate, mlstm_norm_w,
                 w_out, norm_ffn_w, w_group, b_group, w_router, b_router, w_gate, w_up, w_down, final_norm_w):
    d = cfg.d_model
    hs = cfg.ssd_heads
    nh = cfg.ml_heads
    o_xbc = d + cfg.xbc_dim
    row = lambda v: v.reshape(1, -1).astype(F32)
    pad_lanes = lambda a: jnp.pad(a, ((0, 0), (0, LANES - a.shape[1])))
    w_in_t = w_in.T.astype(F32)
    w_dt_t = jnp.pad(w_in_t[o_xbc:o_xbc + hs], ((0, LANES - hs), (0, 0)))
    a = -jnp.exp(a_log.astype(F32))
    w_gates = jnp.concatenate([w_igate, w_fgate], axis=1)
    b_gates = jnp.concatenate([b_igate, b_fgate]).astype(F32)
    ne = cfg.n_experts
    wr = pad_lanes(jnp.concatenate([w_router, w_group], axis=1))
    br = pad_lanes(jnp.concatenate([b_router, b_group]).reshape(1, -1).astype(F32))
    return dict(
        norm_mix_w=row(norm_mix_w),
        w_in_t=w_in_t, w_dt=w_dt_t.T.astype(BF16), w_dtT=w_dt_t.astype(BF16),
        bdt_row=pad_lanes(row(dt_bias)), bdt_col=pad_lanes(row(dt_bias)).T,
        cwx=conv_ssd_w.astype(F32), cbx=row(conv_ssd_b), cwm=conv_mlstm_w.astype(F32), cbm=row(conv_mlstm_b),
        wq=w_q.astype(BF16), wk=w_k.astype(BF16), wv=w_v.astype(BF16),
        wg=pad_lanes(w_gates).astype(BF16), wgT=w_gates.T.astype(BF16),
        bg_row=pad_lanes(row(b_gates)), bg_col=b_gates.reshape(-1, 1),
        a_row=pad_lanes(row(a)), a_col=a.reshape(-1, 1),
        dskip=row(jnp.repeat(d_skip.astype(F32), cfg.ssd_head_dim)),
        ssd_norm_w=row(ssd_norm_w), mlstm_norm_w=row(mlstm_norm_w), w_out=w_out.astype(BF16),
        norm_ffn_w=row(norm_ffn_w), wr=wr.astype(BF16), br=br,
        w_gate=w_gate.astype(F32), w_up=w_up.astype(F32), w_down=w_down.astype(F32),
        final_norm_w=row(final_norm_w),
    )


def forward(cfg, x_prompt, x_sample, state_ssd_conv, state_ssd, state_mlstm_conv, state_mlstm_c,
            state_mlstm_n, state_mlstm_m, meta_tokens, *weights):
    d = cfg.d_model
    nh = cfg.ml_heads
    hd = cfg.ml_head_dim
    assert state_ssd.shape[0] == 1, "single-layer kernel"
    p = _prep_params(cfg, *[w[0] for w in weights[:-1]], weights[-1])
    bp, lp, _ = x_prompt.shape
    bs, ls, _ = x_sample.shape
    n_meta = meta_tokens.shape[0]

    xp = x_prompt.reshape(bp * lp, d)
    xs = x_sample.reshape(bs * ls, d)
    mp, ms = bp * lp, bs * ls
    norm_args = (p["norm_mix_w"], p["w_dt"], p["w_dtT"], p["bdt_row"], p["bdt_col"])
    h_meta, d_meta, dT_meta = pre_norm(meta_tokens.astype(F32), None, *norm_args, n_meta)
    h, dcol, dT = pre_norm(xp, xs, *norm_args, 512 if mp % 512 == 0 and ms % 512 == 0 else 128)
    proj, proj_meta = in_proj(cfg, h, h_meta, p["w_in_t"], _tile(mp + ms, 1536), d // 2)

    zeros = lambda *s: jnp.zeros(s, F32)
    _, _, st_meta = _mixer_segment(
        cfg, proj_meta, 0, d_meta, dT_meta, 1, n_meta, zeros(1, SUBLANES, cfg.xbc_dim), zeros(1, SUBLANES, d),
        zeros(1, d, cfg.ssd_state), zeros(1, d, hd), zeros(1, nh, hd), zeros(1, nh, LANES), p)
    mt_xbc, mt_s, mt_xm, mt_c, mt_n, mt_m = st_meta
    rep = lambda a: jnp.broadcast_to(a, (bp,) + a.shape[1:])

    ys_p, hm_p, st_p = _mixer_segment(
        cfg, proj, 0, dcol[:mp], dT[:, :mp], bp, lp, rep(_pad_hist(mt_xbc)), rep(_pad_hist(mt_xm)),
        rep(mt_s), rep(mt_c), rep(mt_n), rep(mt_m), p)

    m0 = jnp.broadcast_to(state_mlstm_m[0].astype(F32)[:, :, None], (bs, nh, LANES))
    ys_s, hm_s, st_s = _mixer_segment(
        cfg, proj, mp, dcol[mp:], dT[:, mp:], bs, ls, _pad_hist(state_ssd_conv[0]),
        _pad_hist(state_mlstm_conv[0]), state_ssd[0].reshape(bs, d, cfg.ssd_state),
        state_mlstm_c[0].reshape(bs, d, hd), state_mlstm_n[0], m0, p)
    y_p, y_s = _ffn(cfg, [(xp, proj, 0, ys_p, hm_p), (xs, proj, mp, ys_s, hm_s)], p)
    y_prompt = y_p.reshape(bp, lp, d)
    y_sample = y_s.reshape(bs, ls, d)

    def pack(st, b):
        t_xbc, s_new, t_xm, c_new, n_new, m_new = st
        return (t_xbc[None], s_new.reshape(1, b, cfg.ssd_heads, cfg.ssd_head_dim, cfg.ssd_state),
                t_xm[None], c_new.reshape(1, b, nh, hd, hd), n_new[None], m_new[None, :, :, 0])

    return (y_prompt, y_sample) + pack(st_p, bp) + pack(st_s, bs)


def kernel(x_prompt, x_sample, state_ssd_conv, state_ssd, state_mlstm_conv, state_mlstm_c, state_mlstm_n, state_mlstm_m, meta_tokens, norm_mix_w, w_in, conv_ssd_w, conv_ssd_b, dt_bias, a_log, d_skip, ssd_norm_w, conv_mlstm_w, conv_mlstm_b, w_q, w_k, w_v, w_igate, b_igate, w_fgate, b_fgate, mlstm_norm_w, w_out, norm_ffn_w, w_group, b_group, w_router, b_router, w_gate, w_up, w_down, final_norm_w):
    return forward(Cfg(), x_prompt, x_sample, state_ssd_conv, state_ssd, state_mlstm_conv, state_mlstm_c,
                   state_mlstm_n, state_mlstm_m, meta_tokens, norm_mix_w, w_in, conv_ssd_w, conv_ssd_b, dt_bias,
                   a_log, d_skip, ssd_norm_w, conv_mlstm_w, conv_mlstm_b, w_q, w_k, w_v, w_igate, b_igate,
                   w_fgate, b_fgate, mlstm_norm_w, w_out, norm_ffn_w, w_group, b_group, w_router, b_router,
                   w_gate, w_up, w_down, final_norm_w)
```

```python
import functools
from typing import NamedTuple

import jax
import jax.numpy as jnp
from jax import lax
from jax.experimental import pallas as pl
from jax.experimental.pallas import tpu as pltpu

F32 = jnp.float32
BF16 = jnp.bfloat16
EPS = 1e-6
LANES = 128
SUBLANES = 8
VMEM_LIMIT = 56 * 1024 * 1024
HI = lax.Precision.HIGHEST


class Cfg(NamedTuple):
    d_model: int = 2048
    ssd_heads: int = 32
    ssd_head_dim: int = 64
    ssd_groups: int = 4
    ssd_state: int = 128
    ml_heads: int = 8
    ml_head_dim: int = 256
    n_groups: int = 4
    experts_per_group: int = 8
    d_expert: int = 512
    conv_w: int = 4
    chunk: int = 128

    @property
    def bc_dim(self):
        return self.ssd_groups * self.ssd_state

    @property
    def xbc_dim(self):
        return self.d_model + 2 * self.bc_dim

    @property
    def n_experts(self):
        return self.n_groups * self.experts_per_group


def _cparams(sem):
    return pltpu.CompilerParams(dimension_semantics=sem, vmem_limit_bytes=VMEM_LIMIT)


def _softplus(x):
    return jnp.maximum(x, 0.0) + jnp.log1p(jnp.exp(-jnp.abs(x)))


def _sigmoid(x):
    return 1.0 / (1.0 + jnp.exp(-x))


def _silu(x):
    return x * _sigmoid(x)


def _rms(x, w):
    return x * lax.rsqrt(jnp.mean(x * x, axis=-1, keepdims=True) + EPS) * w


def _prenorm_kernel(x_ref, nw_ref, wdt_ref, wdtT_ref, bdt_row_ref, bdt_col_ref, h_ref, d_ref, dT_ref):
    hb = _rms(x_ref[...], nw_ref[...]).astype(BF16)
    h_ref[...] = hb
    dt = jnp.dot(hb, wdt_ref[...], preferred_element_type=F32)
    d_ref[...] = _softplus(dt + bdt_row_ref[...])
    dtT = lax.dot_general(wdtT_ref[...], hb, (((1,), (1,)), ((), ())), preferred_element_type=F32)
    dT_ref[...] = _softplus(dtT + bdt_col_ref[...])


def _prenorm_pair_kernel(n_a, xa_ref, xb_ref, nw_ref, wdt_ref, wdtT_ref, bdt_row_ref, bdt_col_ref,
                         h_ref, d_ref, dT_ref, x_ref):
    i = pl.program_id(0)

    @pl.when(i < n_a)
    def _():
        x_ref[...] = xa_ref[...]

    @pl.when(i >= n_a)
    def _():
        x_ref[...] = xb_ref[...]

    _prenorm_kernel(x_ref, nw_ref, wdt_ref, wdtT_ref, bdt_row_ref, bdt_col_ref, h_ref, d_ref, dT_ref)


def pre_norm(xa, xb, norm_w, w_dt, w_dtT, bdt_row, bdt_col, bm):
    d = xa.shape[1]
    n_a = xa.shape[0] // bm
    n_b = 0 if xb is None else xb.shape[0] // bm
    m = (n_a + n_b) * bm
    const2 = lambda i: (0, 0)
    w_specs = [pl.BlockSpec((1, d), const2), pl.BlockSpec((d, LANES), const2), pl.BlockSpec((LANES, d), const2),
               pl.BlockSpec((1, LANES), const2), pl.BlockSpec((LANES, 1), const2)]
    if xb is None:
        body, x_specs, xs, scratch = _prenorm_kernel, [pl.BlockSpec((bm, d), lambda i: (i, 0))], (xa,), []
    else:
        body = functools.partial(_prenorm_pair_kernel, n_a)
        x_specs = [pl.BlockSpec((bm, d), lambda i: (jnp.minimum(i, n_a - 1), 0)),
                   pl.BlockSpec((bm, d), lambda i: (jnp.maximum(i - n_a, 0), 0))]
        xs, scratch = (xa, xb), [pltpu.VMEM((bm, d), F32)]
    return pl.pallas_call(
        body,
        out_shape=(jax.ShapeDtypeStruct((m, d), BF16), jax.ShapeDtypeStruct((m, LANES), F32),
                   jax.ShapeDtypeStruct((LANES, m), F32)),
        grid=(n_a + n_b,),
        in_specs=x_specs + w_specs,
        out_specs=(pl.BlockSpec((bm, d), lambda i: (i, 0)), pl.BlockSpec((bm, LANES), lambda i: (i, 0)),
                   pl.BlockSpec((LANES, bm), lambda i: (0, i))),
        scratch_shapes=scratch,
        compiler_params=_cparams(("arbitrary",)),
        name="pre_norm",
    )(*xs, norm_w, w_dt, w_dtT, bdt_row, bdt_col)


def _inproj_kernel(h_ref, hs_ref, wt_ref, proj_ref, projs_ref, w_ref):
    nt_dims = (((1,), (1,)), ((), ()))

    @pl.when(pl.program_id(1) == 0)
    def _():
        w_ref[...] = wt_ref[...].astype(BF16)
        projs_ref[...] = lax.dot_general(hs_ref[...], w_ref[...], nt_dims, preferred_element_type=F32)

    proj_ref[...] = lax.dot_general(h_ref[...], w_ref[...], nt_dims, preferred_element_type=F32)


def in_proj(cfg, h, h_small, w_in_t, bm, bn):
    m, d = h.shape
    ms = h_small.shape[0]
    nz = d // bn
    nx = cfg.xbc_dim // bn
    n_a = nz + nx
    n_blocks = n_a + 2 * nz
    skip = cfg.ssd_heads
    assert skip % SUBLANES == 0
    w_row = lambda j: pl.multiple_of(jnp.where(j < n_a, j * bn, j * bn + skip), SUBLANES)
    out_col = lambda j: jnp.where(j < nz, j, jnp.where(j < n_a, j + 2 * nz, j - nx))
    return pl.pallas_call(
        _inproj_kernel,
        out_shape=(jax.ShapeDtypeStruct((m, n_blocks * bn), F32), jax.ShapeDtypeStruct((ms, n_blocks * bn), F32)),
        grid=(n_blocks, m // bm),
        in_specs=[pl.BlockSpec((bm, d), lambda j, i: (i, 0)),
                  pl.BlockSpec((ms, d), lambda j, i: (0, 0)),
                  pl.BlockSpec((pl.Element(bn), pl.Element(d)), lambda j, i: (w_row(j), 0))],
        out_specs=(pl.BlockSpec((bm, bn), lambda j, i: (i, out_col(j))),
                   pl.BlockSpec((ms, bn), lambda j, i: (0, out_col(j)))),
        scratch_shapes=[pltpu.VMEM((bn, d), BF16)],
        compiler_params=_cparams(("arbitrary", "arbitrary")),
        name="in_proj",
    )(h, h_small, w_in_t)


CONV_LANES = 256


def _causal_conv(u, prev, w_ref, b_ref, cols, conv_w):
    lt = u.shape[0]
    row8 = lax.broadcasted_iota(jnp.int32, (SUBLANES, u.shape[1]), 0)
    acc = u * w_ref[conv_w - 1:conv_w, cols] + b_ref[:, cols]
    for s in range(1, conv_w):
        rolled = pltpu.roll(u, s, axis=0)
        head = jnp.where(row8 < s, pltpu.roll(prev, s, axis=0), rolled[0:SUBLANES])
        shifted = head if lt == SUBLANES else jnp.concatenate([head, rolled[SUBLANES:]], axis=0)
        acc = acc + shifted * w_ref[conv_w - 1 - s:conv_w - s, cols]
    return acc


def _conv_qkv_tile(cfg, xbc_ref, xm_ref, cwx_ref, cbx_ref, cwm_ref, cbm_ref, wq_ref, wk_ref, wv_ref, wg_ref, wgT_ref,
                   bg_row_ref, bg_col_ref, xa_ref, q_ref, k_ref, v_ref, g_ref, gT_ref, px_ref, pm_ref):
    lt = xbc_ref.shape[0]
    hd = cfg.ml_head_dim
    nh = cfg.ml_heads

    for c0 in range(0, xbc_ref.shape[1], CONV_LANES):
        cols = slice(c0, min(c0 + CONV_LANES, xbc_ref.shape[1]))
        u = xbc_ref[:, cols]
        xa_ref[:, cols] = _silu(_causal_conv(u, px_ref[:, cols], cwx_ref, cbx_ref, cols, cfg.conv_w))
        px_ref[:, cols] = u[lt - SUBLANES:lt]

    d = nh * hd
    kscale = hd ** -0.5
    nt = (((1,), (1,)), ((), ()))
    gcol = jnp.zeros(g_ref.shape, F32) + bg_row_ref[...]
    grow = jnp.zeros(gT_ref.shape, F32) + bg_col_ref[...]
    for h in range(nh):
        sl = slice(h * hd, (h + 1) * hd)
        xm = xm_ref[:, sl]
        xc = _silu(_causal_conv(xm, pm_ref[:, sl], cwm_ref, cbm_ref, sl, cfg.conv_w)).astype(BF16)
        pm_ref[:, sl] = xm[lt - SUBLANES:lt]
        qh = jnp.dot(xc, wq_ref[h], preferred_element_type=F32)
        kh = jnp.dot(xc, wk_ref[h], preferred_element_type=F32) * kscale
        vh = jnp.dot(xm.astype(BF16), wv_ref[h], preferred_element_type=F32)
        q_ref[:, sl] = qh
        k_ref[:, sl] = kh
        v_ref[:, sl] = vh
        for part, val in enumerate((qh, kh, vh)):
            vb = val.astype(BF16)
            rows = slice(part * d + h * hd, part * d + (h + 1) * hd)
            gcol = gcol + jnp.dot(vb, wg_ref[rows, :], preferred_element_type=F32)
            grow = grow + lax.dot_general(wgT_ref[:, rows], vb, nt, preferred_element_type=F32)
    lane = lax.broadcasted_iota(jnp.int32, gcol.shape, 1)
    g_ref[...] = jnp.where(lane < nh, gcol, -_softplus(-gcol))
    row = lax.broadcasted_iota(jnp.int32, grow.shape, 0)
    gT_ref[...] = jnp.where(row < nh, grow, -_softplus(-grow))


def _conv_qkv_kernel(cfg, xbc_ref, xm_ref, hxbc_ref, hxm_ref, cwx_ref, cbx_ref, cwm_ref, cbm_ref,
                     wq_ref, wk_ref, wv_ref, wg_ref, wgT_ref, bg_row_ref, bg_col_ref,
                     xa_ref, q_ref, k_ref, v_ref, g_ref, gT_ref, px_ref, pm_ref):
    @pl.when(pl.program_id(1) == 0)
    def _():
        px_ref[...] = hxbc_ref[0]
        pm_ref[...] = hxm_ref[0]

    _conv_qkv_tile(cfg, xbc_ref, xm_ref, cwx_ref, cbx_ref, cwm_ref, cbm_ref, wq_ref, wk_ref, wv_ref, wg_ref, wgT_ref,
                   bg_row_ref, bg_col_ref, xa_ref, q_ref, k_ref, v_ref, g_ref, gT_ref.at[0, 0], px_ref, pm_ref)


def conv_qkv(cfg, proj, hist_xbc, hist_xm, cwx, cbx, cwm, cbm, wq, wk, wv, wg, wgT, bg_row, bg_col,
             row0, n_seq, seq_len, lt):
    d = cfg.d_model
    xbc = cfg.xbc_dim
    m = n_seq * seq_len
    nt = seq_len // lt
    ng = 2 * cfg.ml_heads
    xbc_blk = (3 * d) // xbc
    blk0 = row0 // lt
    row = lambda s, l: (s * nt + l, 0)
    const2 = lambda s, l: (0, 0)
    const3 = lambda s, l: (0, 0, 0)
    return pl.pallas_call(
        functools.partial(_conv_qkv_kernel, cfg),
        out_shape=(jax.ShapeDtypeStruct((m, xbc), F32),
                   jax.ShapeDtypeStruct((m, d), F32), jax.ShapeDtypeStruct((m, d), F32),
                   jax.ShapeDtypeStruct((m, d), F32),
                   jax.ShapeDtypeStruct((m, LANES), F32),
                   jax.ShapeDtypeStruct((n_seq, nt, ng, lt), F32)),
        grid=(n_seq, nt),
        in_specs=[pl.BlockSpec((lt, xbc), lambda s, l: (blk0 + s * nt + l, xbc_blk)),
                  pl.BlockSpec((lt, d), lambda s, l: (blk0 + s * nt + l, 1)),
                  pl.BlockSpec((1, SUBLANES, xbc), lambda s, l: (s, 0, 0)),
                  pl.BlockSpec((1, SUBLANES, d), lambda s, l: (s, 0, 0)),
                  pl.BlockSpec((cfg.conv_w, xbc), const2), pl.BlockSpec((1, xbc), const2),
                  pl.BlockSpec((cfg.conv_w, d), const2), pl.BlockSpec((1, d), const2),
                  pl.BlockSpec(wq.shape, const3), pl.BlockSpec(wk.shape, const3),
                  pl.BlockSpec(wv.shape, const3),
                  pl.BlockSpec(wg.shape, const2), pl.BlockSpec(wgT.shape, const2),
                  pl.BlockSpec((1, LANES), const2), pl.BlockSpec((ng, 1), const2)],
        out_specs=(pl.BlockSpec((lt, xbc), row), pl.BlockSpec((lt, d), row), pl.BlockSpec((lt, d), row),
                   pl.BlockSpec((lt, d), row), pl.BlockSpec((lt, LANES), row),
                   pl.BlockSpec((1, 1, ng, lt), lambda s, l: (s, l, 0, 0))),
        scratch_shapes=[pltpu.VMEM((SUBLANES, xbc), F32), pltpu.VMEM((SUBLANES, d), F32)],
        compiler_params=_cparams(("arbitrary", "arbitrary")),
        name="conv_qkv",
    )(proj, proj, hist_xbc, hist_xm, cwx, cbx, cwm, cbm, wq, wk, wv, wg, wgT, bg_row, bg_col)


def _tri(q, lower):
    r = lax.broadcasted_iota(jnp.int32, (q, q), 0)
    c = lax.broadcasted_iota(jnp.int32, (q, q), 1)
    return (c <= r) if lower else (r <= c)


def _ssd_chunk(cfg, xa_ref, d_ref, dT_ref, arow_ref, acol_ref, dskip_ref, y_ref, st_ref):
    q = xa_ref.shape[0]
    dm = cfg.d_model
    ns = cfg.ssd_state
    hp = cfg.ssd_head_dim
    hpg = cfg.ssd_heads // cfg.ssd_groups
    heads_per_tile = LANES // hp
    n_tiles = cfg.ssd_heads // heads_per_tile

    causal = _tri(q, True)
    tril = causal.astype(F32)
    triu = _tri(q, False).astype(F32)
    dcol = d_ref[...]
    drow = dT_ref[...]
    acum = jnp.dot(tril, dcol * arow_ref[...], precision=HI, preferred_element_type=F32)
    acumT = jnp.dot(drow * acol_ref[...], triu, precision=HI, preferred_element_type=F32)
    nt_dims = (((1,), (1,)), ((), ()))
    tn_dims = (((0,), (0,)), ((), ()))
    lane = lax.broadcasted_iota(jnp.int32, (q, LANES), 1)
    srow = lax.broadcasted_iota(jnp.int32, (LANES, ns), 0)

    cbs = []
    bgs = []
    cgs = []
    for g in range(cfg.ssd_groups):
        bg = xa_ref[:, dm + g * ns: dm + (g + 1) * ns].astype(BF16)
        cg = xa_ref[:, dm + cfg.bc_dim + g * ns: dm + cfg.bc_dim + (g + 1) * ns].astype(BF16)
        cbs.append(lax.dot_general(cg, bg, nt_dims, preferred_element_type=F32))
        bgs.append(bg)
        cgs.append(cg)

    for t in range(n_tiles):
        h0 = t * heads_per_tile
        g = h0 // hpg
        cols = slice(t * LANES, (t + 1) * LANES)
        x = xa_ref[:, cols]
        dsel = jnp.zeros((q, LANES), F32)
        asel = jnp.zeros((q, LANES), F32)
        rdec = jnp.zeros((LANES, ns), F32)
        for i in range(heads_per_tile):
            h = h0 + i
            in_head = (lane >= i * hp) & (lane < (i + 1) * hp)
            a_last = acum[q - 1:q, h:h + 1]
            dsel = jnp.where(in_head, dcol[:, h:h + 1], dsel)
            asel = jnp.where(in_head, acum[:, h:h + 1], asel)
            rdec = jnp.where((srow >= i * hp) & (srow < (i + 1) * hp), jnp.exp(a_last), rdec)
        esel = jnp.exp(asel)
        tsel = jnp.exp(asel[q - 1:q, :] - asel)
        xd = x * dsel
        y = x * dskip_ref[:, cols]
        for i in range(heads_per_tile):
            h = h0 + i
            in_head = (lane >= i * hp) & (lane < (i + 1) * hp)
            seg = jnp.where(causal, acum[:, h:h + 1] - acumT[h:h + 1, :], -jnp.inf)
            w = (cbs[g] * jnp.exp(seg)).astype(BF16)
            xdh = jnp.where(in_head, xd, 0.0).astype(BF16)
            y = y + jnp.dot(w, xdh, preferred_element_type=F32)
        s_old = st_ref[cols, :]
        ys = lax.dot_general(cgs[g], s_old.astype(BF16), nt_dims, preferred_element_type=F32)
        y_ref[:, cols] = y + esel * ys
        upd = lax.dot_general((xd * tsel).astype(BF16), bgs[g], tn_dims, preferred_element_type=F32)
        st_ref[cols, :] = rdec * s_old + upd


def _ssd_kernel(cfg, n_chunks, xa_ref, d_ref, dT_ref, s0_ref, arow_ref, acol_ref, dskip_ref,
                y_ref, sout_ref, st_ref):
    @pl.when(pl.program_id(1) == 0)
    def _():
        st_ref[...] = s0_ref[...]

    for b in range(xa_ref.shape[0]):
        _ssd_chunk(cfg, xa_ref.at[b], d_ref.at[b], dT_ref.at[b, 0], arow_ref, acol_ref, dskip_ref,
                   y_ref.at[b], st_ref.at[b])

    @pl.when(pl.program_id(1) == n_chunks - 1)
    def _():
        sout_ref[...] = st_ref[...]


def ssd_scan(cfg, xa, d, dT, s0, a_row, a_col, dskip, n_seq, seq_len, q, nb):
    dm = cfg.d_model
    nc = seq_len // q
    blk3 = lambda s, c: (s, c, 0)
    st3 = lambda s, c: (s, 0, 0)
    const2 = lambda s, c: (0, 0)
    return pl.pallas_call(
        functools.partial(_ssd_kernel, cfg, nc),
        out_shape=(jax.ShapeDtypeStruct((n_seq, seq_len, dm), F32),
                   jax.ShapeDtypeStruct((n_seq, dm, cfg.ssd_state), F32)),
        grid=(n_seq // nb, nc),
        in_specs=[pl.BlockSpec((nb, q, cfg.xbc_dim), blk3),
                  pl.BlockSpec((nb, q, LANES), blk3),
                  pl.BlockSpec((nb, 1, cfg.ssd_heads, q), lambda s, c: (s, c, 0, 0)),
                  pl.BlockSpec((nb, dm, cfg.ssd_state), st3),
                  pl.BlockSpec((1, LANES), const2),
                  pl.BlockSpec((cfg.ssd_heads, 1), const2),
                  pl.BlockSpec((1, dm), const2)],
        out_specs=(pl.BlockSpec((nb, q, dm), blk3),
                   pl.BlockSpec((nb, dm, cfg.ssd_state), st3)),
        scratch_shapes=[pltpu.VMEM((nb, dm, cfg.ssd_state), F32)],
        compiler_params=_cparams(("arbitrary", "arbitrary")),
        name="ssd_scan",
    )(xa, d, dT, s0, a_row, a_col, dskip)


def _mlstm_chunk(cfg, q_ref, k_ref, v_ref, g_ref, gT_ref, h_ref, c_ref, n_ref, m_ref):
    ql = q_ref.shape[0]
    hd = cfg.ml_head_dim
    nh = cfg.ml_heads

    causal = _tri(ql, True)
    gcol = g_ref[...]
    grow = gT_ref[...]
    bcum = jnp.dot(causal.astype(F32), gcol, precision=HI, preferred_element_type=F32)
    bcumT = jnp.dot(grow, _tri(ql, False).astype(F32), precision=HI, preferred_element_type=F32)
    nt_dims = (((1,), (1,)), ((), ()))
    tn_dims = (((0,), (0,)), ((), ()))

    for h in range(nh):
        sl = slice(h * hd, (h + 1) * hd)
        b_col = bcum[:, nh + h:nh + h + 1]
        b_row = bcumT[nh + h:nh + h + 1, :]
        i_col = gcol[:, h:h + 1]
        i_row = grow[h:h + 1, :]
        m_prev = m_ref[h:h + 1, 0:1]
        dlog = jnp.where(causal, b_col - b_row + i_row, -jnp.inf)
        inter = b_col + m_prev
        mt = jnp.maximum(inter, jnp.max(dlog, axis=1, keepdims=True))
        qh = q_ref[:, sl]
        kh = k_ref[:, sl]
        vh = v_ref[:, sl]
        qb = qh.astype(BF16)
        kb = kh.astype(BF16)
        s = lax.dot_general(qb, kb, nt_dims, preferred_element_type=F32) * jnp.exp(dlog - mt)
        gdec = jnp.exp(inter - mt)
        c_old = c_ref[sl, :]
        n_old = n_ref[h:h + 1, :]
        qc = lax.dot_general(qb, c_old.astype(BF16), nt_dims, preferred_element_type=F32)
        num = jnp.dot(s.astype(BF16), vh.astype(BF16), preferred_element_type=F32) + gdec * qc
        den = jnp.sum(s, axis=1, keepdims=True) + gdec * jnp.sum(qh * n_old, axis=1, keepdims=True)
        h_ref[:, sl] = num / jnp.maximum(jnp.abs(den), jnp.exp(-mt))
        m_new = mt[ql - 1:ql, :]
        gs = jnp.exp(b_col[ql - 1:ql, :] - b_col + i_col - m_new)
        gc = jnp.exp(inter[ql - 1:ql, :] - m_new)
        upd = lax.dot_general((vh * gs).astype(BF16), kb, tn_dims, preferred_element_type=F32)
        c_ref[sl, :] = gc * c_old + upd
        n_ref[h:h + 1, :] = gc * n_old + jnp.sum(gs * kh, axis=0, keepdims=True)
        m_ref[h:h + 1, :] = jnp.broadcast_to(m_new, (1, LANES))


def _mlstm_kernel(cfg, n_chunks, q_ref, k_ref, v_ref, g_ref, gT_ref, c0_ref, n0_ref, m0_ref,
                  h_ref, cout_ref, nout_ref, mout_ref, c_ref, n_ref, m_ref):
    @pl.when(pl.program_id(1) == 0)
    def _():
        c_ref[...] = c0_ref[...]
        n_ref[...] = n0_ref[...]
        m_ref[...] = m0_ref[...]

    for b in range(q_ref.shape[0]):
        _mlstm_chunk(cfg, q_ref.at[b], k_ref.at[b], v_ref.at[b], g_ref.at[b], gT_ref.at[b, 0],
                     h_ref.at[b], c_ref.at[b], n_ref.at[b], m_ref.at[b])

    @pl.when(pl.program_id(1) == n_chunks - 1)
    def _():
        cout_ref[...] = c_ref[...]
        nout_ref[...] = n_ref[...]
        mout_ref[...] = m_ref[...]


def mlstm_scan(cfg, qa, ka, va, g, gT, c0, n0, m0, n_seq, seq_len, q, nb):
    d = cfg.d_model
    hd = cfg.ml_head_dim
    nh = cfg.ml_heads
    nc = seq_len // q
    blk3 = lambda s, c: (s, c, 0)
    st3 = lambda s, c: (s, 0, 0)
    return pl.pallas_call(
        functools.partial(_mlstm_kernel, cfg, nc),
        out_shape=(jax.ShapeDtypeStruct((n_seq, seq_len, d), F32),
                   jax.ShapeDtypeStruct((n_seq, d, hd), F32),
                   jax.ShapeDtypeStruct((n_seq, nh, hd), F32),
                   jax.ShapeDtypeStruct((n_seq, nh, LANES), F32)),
        grid=(n_seq // nb, nc),
        in_specs=[pl.BlockSpec((nb, q, d), blk3), pl.BlockSpec((nb, q, d), blk3), pl.BlockSpec((nb, q, d), blk3),
                  pl.BlockSpec((nb, q, LANES), blk3),
                  pl.BlockSpec((nb, 1, 2 * nh, q), lambda s, c: (s, c, 0, 0)),
                  pl.BlockSpec((nb, d, hd), st3), pl.BlockSpec((nb, nh, hd), st3),
                  pl.BlockSpec((nb, nh, LANES), st3)],
        out_specs=(pl.BlockSpec((nb, q, d), blk3),
                   pl.BlockSpec((nb, d, hd), st3), pl.BlockSpec((nb, nh, hd), st3),
                   pl.BlockSpec((nb, nh, LANES), st3)),
        scratch_shapes=[pltpu.VMEM((nb, d, hd), F32), pltpu.VMEM((nb, nh, hd), F32),
                        pltpu.VMEM((nb, nh, LANES), F32)],
        compiler_params=_cparams(("arbitrary", "arbitrary")),
        name="mlstm_scan",
    )(qa, ka, va, g, gT, c0, n0, m0)


def _group_norm(x, w_ref, col0, groups, width):
    parts = []
    for g in range(groups):
        seg = x[:, g * width:(g + 1) * width]
        parts.append(seg * lax.rsqrt(jnp.mean(seg * seg, axis=-1, keepdims=True) + EPS)
                     * w_ref[:, col0 + g * width: col0 + (g + 1) * width])
    return parts


def _outproj_kernel(cfg, ys_ref, z_ref, hm_ref, o_ref, x_ref, nws_ref, nwm_ref, w_ref, out_ref):
    d = cfg.d_model
    ws = d // cfg.ssd_groups
    yz = ys_ref[...] * _silu(z_ref[...])
    ssd_half = jnp.concatenate([part.astype(BF16) for part in _group_norm(yz, nws_ref, 0, cfg.ssd_groups, ws)],
                               axis=-1)
    acc = x_ref[...] + jnp.dot(ssd_half, w_ref[0:d, :], preferred_element_type=F32)
    gate = _sigmoid(o_ref[...])
    wm = cfg.ml_head_dim
    ml_half = jnp.concatenate(
        [(part * gate[:, g * wm:(g + 1) * wm]).astype(BF16)
         for g, part in enumerate(_group_norm(hm_ref[...], nwm_ref, 0, cfg.ml_heads, wm))], axis=-1)
    out_ref[...] = acc + jnp.dot(ml_half, w_ref[d:2 * d, :], preferred_element_type=F32)


def out_proj(cfg, ys, proj, proj_row0, hm, x, nws, nwm, w_out, bm):
    m, d = x.shape
    blk0 = proj_row0 // bm
    const2 = lambda i: (0, 0)
    full = lambda i: (i, 0)
    return pl.pallas_call(
        functools.partial(_outproj_kernel, cfg),
        out_shape=jax.ShapeDtypeStruct((m, d), F32),
        grid=(m // bm,),
        in_specs=[pl.BlockSpec((bm, d), full),
                  pl.BlockSpec((bm, d), lambda i: (i + blk0, 0)),
                  pl.BlockSpec((bm, d), full),
                  pl.BlockSpec((bm, d), lambda i: (i + blk0, 2)),
                  pl.BlockSpec((bm, d), full),
                  pl.BlockSpec((1, d), const2), pl.BlockSpec((1, d), const2),
                  pl.BlockSpec((2 * d, d), const2, pipeline_mode=pl.Buffered(1))],
        out_specs=pl.BlockSpec((bm, d), full),
        compiler_params=_cparams(("arbitrary",)),
        name="out_proj",
    )(ys, proj, hm, proj, x, nws, nwm, w_out)


def _router_kernel(cfg, n_a, xa_ref, xb_ref, nw_ref, wr_ref, br_ref, ei_ref, wt_ref, cnt_out_ref, rows_ref,
                   cnt_ref, x_ref):
    ne = cfg.n_experts
    epg = cfg.experts_per_group
    ngr = cfg.n_groups
    bm = x_ref.shape[0]
    i = pl.program_id(0)

    @pl.when(i == 0)
    def _():
        cnt_ref[...] = jnp.zeros_like(cnt_ref)

    @pl.when(i < n_a)
    def _():
        x_ref[...] = xa_ref[...]

    @pl.when(i >= n_a)
    def _():
        x_ref[...] = xb_ref[...]

    hb = _rms(x_ref[...], nw_ref[...]).astype(BF16)
    rows_ref[...] = _lanes_to_rows(_pack_bf16_pairs(hb))
    logits = jnp.dot(hb, wr_ref[...], preferred_element_type=F32) + br_ref[...]
    lane = lax.broadcasted_iota(jnp.int32, logits.shape, 1)
    big = jnp.int32(2 ** 30)
    neg = -jnp.inf

    def first_argmax(vals):
        mx = jnp.max(vals, axis=-1, keepdims=True)
        idx = jnp.min(jnp.where(vals == mx, lane, big), axis=-1, keepdims=True)
        return mx, idx

    is_group = (lane >= ne) & (lane < ne + ngr)
    gl = jnp.where(is_group, logits, neg)
    gmax, gidx = first_argmax(gl)
    p_g = 1.0 / jnp.sum(jnp.exp(gl - gmax), axis=-1, keepdims=True)
    e_lo = (gidx - ne) * epg
    in_sel = (lane >= e_lo) & (lane < e_lo + epg)
    el = jnp.where(in_sel, logits, neg)
    pe = jnp.exp(el - jnp.max(el, axis=-1, keepdims=True))
    pe = jnp.where(in_sel, pe / jnp.sum(pe, axis=-1, keepdims=True), -1.0)
    p1, i1 = first_argmax(pe)
    p2, i2 = first_argmax(jnp.where(lane == i1, -1.0, pe))
    wsum = p1 + p2
    wt_ref[...] = jnp.where(lane == 0, p_g * p1 / wsum, jnp.where(lane == 1, p_g * p2 / wsum, 0.0))

    oh1 = jnp.where(lane == i1, 1.0, 0.0)
    oh2 = jnp.where(lane == i2, 1.0, 0.0)
    r = lax.broadcasted_iota(jnp.int32, (bm, bm), 0)
    c = lax.broadcasted_iota(jnp.int32, (bm, bm), 1)
    before = jnp.where(c < r, 1.0, 0.0).astype(BF16)
    ahead1 = jnp.dot(before, oh1.astype(BF16), preferred_element_type=F32)
    ahead2 = jnp.dot(before, oh2.astype(BF16), preferred_element_type=F32)
    cnt = cnt_ref[...]
    tot1 = jnp.sum(oh1, axis=0, keepdims=True)
    rank1 = jnp.sum(oh1 * (cnt + ahead1), axis=-1, keepdims=True)
    rank2 = jnp.sum(oh2 * (cnt + tot1 + ahead2), axis=-1, keepdims=True)
    cnt_new = cnt + tot1 + jnp.sum(oh2, axis=0, keepdims=True)
    cnt_ref[...] = cnt_new
    cnt_out_ref[...] = cnt_new
    ei_ref[...] = jnp.where(lane == 0, i1, jnp.where(lane == 1, i2, jnp.where(
        lane == 2, rank1.astype(jnp.int32), jnp.where(lane == 3, rank2.astype(jnp.int32), 0))))


def router(cfg, xa, xb, nw, wr, br, bm):
    d = xa.shape[1]
    n_a = xa.shape[0] // bm
    n_b = xb.shape[0] // bm
    m = xa.shape[0] + xb.shape[0]
    nch = d // (2 * LANES)
    const2 = lambda i: (0, 0)
    return pl.pallas_call(
        functools.partial(_router_kernel, cfg, n_a),
        out_shape=(jax.ShapeDtypeStruct((m, LANES), jnp.int32), jax.ShapeDtypeStruct((m, LANES), F32),
                   jax.ShapeDtypeStruct((1, LANES), F32), jax.ShapeDtypeStruct((m, nch, LANES), jnp.uint32)),
        grid=(n_a + n_b,),
        in_specs=[pl.BlockSpec((bm, d), lambda i: (jnp.minimum(i, n_a - 1), 0)),
                  pl.BlockSpec((bm, d), lambda i: (jnp.maximum(i - n_a, 0), 0)),
                  pl.BlockSpec((1, d), const2),
                  pl.BlockSpec((d, LANES), const2), pl.BlockSpec((1, LANES), const2)],
        out_specs=(pl.BlockSpec((bm, LANES), lambda i: (i, 0)), pl.BlockSpec((bm, LANES), lambda i: (i, 0)),
                   pl.BlockSpec((1, LANES), const2), pl.BlockSpec((bm, nch, LANES), lambda i: (i, 0, 0))),
        scratch_shapes=[pltpu.VMEM((1, LANES), F32), pltpu.VMEM((bm, d), F32)],
        compiler_params=_cparams(("arbitrary",)),
        name="router",
    )(xa, xb, nw, wr, br)


def _pack_bf16_pairs(x):
    half = x.shape[1] // 2
    bits = lambda v: lax.bitcast_convert_type(v.astype(BF16).astype(F32), jnp.uint32)
    return bits(x[:, :half]) | (bits(x[:, half:]) >> 16)


def _unpack_bf16_pairs(p):
    hi = lax.bitcast_convert_type(p & jnp.uint32(0xFFFF0000), F32)
    lo = lax.bitcast_convert_type(p << 16, F32)
    return jnp.concatenate([hi, lo], axis=-1)


def _rows_to_lanes(g):
    t = pltpu.einshape("rcl->crl", g)
    return jnp.concatenate([t[c] for c in range(t.shape[0])], axis=-1)


def _lanes_to_rows(x):
    parts = jnp.stack([x[:, c * LANES:(c + 1) * LANES] for c in range(x.shape[1] // LANES)], axis=0)
    return pltpu.einshape("crl->rcl", parts)


GATHER_GROUP = 8


def _gather_rows(idx_ref, src_hbm, dst, sem, n_groups):
    def body(g, carry):
        for u in range(GATHER_GROUP):
            r = g * GATHER_GROUP + u
            pltpu.make_async_copy(src_hbm.at[idx_ref[0, 0, r]], dst.at[r], sem).start()
        return carry
    lax.fori_loop(0, n_groups, body, 0)


def _wait_rows(src_hbm, dst, sem, n):
    pltpu.make_async_copy(src_hbm.at[pl.ds(0, n)], dst.at[pl.ds(0, n)], sem).wait()


def _moe_kernel(cfg, te_ref, nr_ref, first_ref, wslot_ref, nexte_ref, nv_ref, src_ref, srcn_ref, x_hbm,
                wg_hbm, wu_hbm, wd_hbm, ys_ref, xbuf, sem, wgf, wuf, wdf, wsem, wgb, wub, wdb):
    j = pl.program_id(0)
    n_valid = nv_ref[0]

    def weight_copies(e, slot):
        return (pltpu.make_async_copy(wg_hbm.at[e], wgf.at[slot], wsem.at[slot, 0]),
                pltpu.make_async_copy(wu_hbm.at[e], wuf.at[slot], wsem.at[slot, 1]),
                pltpu.make_async_copy(wd_hbm.at[e], wdf.at[slot], wsem.at[slot, 2]))

    @pl.when(j == 0)
    def _():
        for cp in weight_copies(te_ref[0], 0):
            cp.start()
        xbuf[...] = jnp.zeros_like(xbuf)
        _gather_rows(src_ref, x_hbm, xbuf.at[0], sem.at[0], nr_ref[0] // GATHER_GROUP)

    @pl.when(j + 1 < n_valid)
    def _():
        nslot = (j + 1) % 2
        _gather_rows(srcn_ref, x_hbm, xbuf.at[nslot], sem.at[nslot], nr_ref[j + 1] // GATHER_GROUP)

    @pl.when(j < n_valid)
    def _():
        @pl.when(first_ref[j] == 1)
        def _():
            ws = wslot_ref[j]
            for cp in weight_copies(te_ref[j], ws):
                cp.wait()

            @pl.when(nexte_ref[j] >= 0)
            def _():
                for cp in weight_copies(nexte_ref[j], 1 - ws):
                    cp.start()

            wgb[...] = wgf[ws].astype(BF16)
            wub[...] = wuf[ws].astype(BF16)
            wdb[...] = wdf[ws].astype(BF16)

        slot = j % 2
        _wait_rows(x_hbm, xbuf.at[slot], sem.at[slot], nr_ref[j])
        hb = _unpack_bf16_pairs(_rows_to_lanes(xbuf[slot])).astype(BF16)
        hid = (_silu(jnp.dot(hb, wgb[...], preferred_element_type=F32))
               * jnp.dot(hb, wub[...], preferred_element_type=F32))
        y = jnp.dot(hid.astype(BF16), wdb[...], preferred_element_type=F32)
        ys_ref[...] = _lanes_to_rows(_pack_bf16_pairs(y))

    @pl.when(j >= n_valid)
    def _():
        ys_ref[...] = jnp.zeros_like(ys_ref)


def moe_routed(cfg, tables, src, x_rows, wg, wu, wd, tm):
    n_tiles = src.shape[0]
    d = cfg.d_model
    de = cfg.d_expert
    nch = x_rows.shape[1]
    return pl.pallas_call(
        functools.partial(_moe_kernel, cfg),
        out_shape=jax.ShapeDtypeStruct((n_tiles * tm, nch, LANES), jnp.uint32),
        grid_spec=pltpu.PrefetchScalarGridSpec(
            num_scalar_prefetch=len(tables),
            grid=(n_tiles,),
            in_specs=[pl.BlockSpec((1, 1, tm), lambda j, *_: (j, 0, 0), memory_space=pltpu.SMEM),
                      pl.BlockSpec((1, 1, tm), lambda j, *_: (jnp.minimum(j + 1, n_tiles - 1), 0, 0),
                                   memory_space=pltpu.SMEM),
                      pl.BlockSpec(memory_space=pl.ANY),
                      pl.BlockSpec(memory_space=pl.ANY), pl.BlockSpec(memory_space=pl.ANY),
                      pl.BlockSpec(memory_space=pl.ANY)],
            out_specs=pl.BlockSpec((tm, nch, LANES), lambda j, *_: (j, 0, 0)),
            scratch_shapes=[pltpu.VMEM((2, tm, nch, LANES), jnp.uint32), pltpu.SemaphoreType.DMA((2,)),
                            pltpu.VMEM((2, d, de), F32), pltpu.VMEM((2, d, de), F32), pltpu.VMEM((2, de, d), F32),
                            pltpu.SemaphoreType.DMA((2, 3)),
                            pltpu.VMEM((d, de), BF16), pltpu.VMEM((d, de), BF16), pltpu.VMEM((de, d), BF16)]),
        compiler_params=_cparams(("arbitrary",)),
        name="moe",
    )(*tables, src, src, x_rows, wg, wu, wd)


def _combine_kernel(n_steps, d0_ref, d1_ref, d0n_ref, d1n_ref, wt_ref, x1_ref, fw_ref, ys_hbm, y_ref, gbuf, sem):
    i = pl.program_id(0)
    groups = x1_ref.shape[0] // GATHER_GROUP

    def fetch(r0_ref, r1_ref, slot):
        _gather_rows(r0_ref, ys_hbm, gbuf.at[slot, 0], sem.at[slot, 0], groups)
        _gather_rows(r1_ref, ys_hbm, gbuf.at[slot, 1], sem.at[slot, 1], groups)

    @pl.when(i == 0)
    def _():
        fetch(d0_ref, d1_ref, 0)

    @pl.when(i + 1 < n_steps)
    def _():
        fetch(d0n_ref, d1n_ref, (i + 1) % 2)

    slot = i % 2
    bm = x1_ref.shape[0]
    _wait_rows(ys_hbm, gbuf.at[slot, 0], sem.at[slot, 0], bm)
    _wait_rows(ys_hbm, gbuf.at[slot, 1], sem.at[slot, 1], bm)
    wt = wt_ref[...]
    acc = (x1_ref[...] + wt[:, 0:1] * _unpack_bf16_pairs(_rows_to_lanes(gbuf[slot, 0]))
           + wt[:, 1:2] * _unpack_bf16_pairs(_rows_to_lanes(gbuf[slot, 1])))
    y_ref[...] = _rms(acc, fw_ref[...])


def moe_combine(cfg, dest0, dest1, wt, wt_blk0, x1, fw, ys, bm):
    m, d = x1.shape
    nch = ys.shape[1]
    n = m // bm
    cur = lambda: pl.BlockSpec((1, 1, bm), lambda i: (i, 0, 0), memory_space=pltpu.SMEM)
    nxt = lambda: pl.BlockSpec((1, 1, bm), lambda i: (jnp.minimum(i + 1, n - 1), 0, 0), memory_space=pltpu.SMEM)
    return pl.pallas_call(
        functools.partial(_combine_kernel, n),
        out_shape=jax.ShapeDtypeStruct((m, d), F32),
        grid=(n,),
        in_specs=[cur(), cur(), nxt(), nxt(),
                  pl.BlockSpec((bm, LANES), lambda i: (i + wt_blk0, 0)),
                  pl.BlockSpec((bm, d), lambda i: (i, 0)), pl.BlockSpec((1, d), lambda i: (0, 0)),
                  pl.BlockSpec(memory_space=pl.ANY)],
        out_specs=pl.BlockSpec((bm, d), lambda i: (i, 0)),
        scratch_shapes=[pltpu.VMEM((2, 2, bm, nch, LANES), jnp.uint32), pltpu.SemaphoreType.DMA((2, 2))],
        compiler_params=_cparams(("arbitrary",)),
        name="moe_combine",
    )(dest0, dest1, dest0, dest1, wt, x1, fw, ys)


def _route_tables(cfg, ei, cnt, tm, n_tiles):
    ne = cfg.n_experts
    i32 = jnp.int32
    counts = cnt[0, :ne].astype(i32)
    tiles_e = (counts + tm - 1) // tm
    tile_end = jnp.cumsum(tiles_e)
    tile_start = tile_end - tiles_e
    row_off = tile_start * tm
    tile_id = jnp.arange(n_tiles, dtype=i32)
    tile_expert = jnp.minimum(jnp.sum((tile_id[:, None] >= tile_end[None, :]).astype(i32), axis=1), ne - 1)
    used = jnp.clip(counts[tile_expert] - (tile_id - tile_start[tile_expert]) * tm, 0, tm)
    used = jnp.where(tile_id < tile_end[-1], used, 0)
    tile_rows = (used + GATHER_GROUP - 1) // GATHER_GROUP * GATHER_GROUP
    n_valid = tile_end[-1]
    prev_expert = jnp.concatenate([jnp.full((1,), -1, i32), tile_expert[:-1]])
    first = ((tile_expert != prev_expert) & (tile_id < n_valid)).astype(i32)
    wslot = (jnp.cumsum(first) - 1) % 2
    e_id = jnp.arange(ne, dtype=i32)
    later = (e_id[None, :] > e_id[:, None]) & (tiles_e[None, :] > 0)
    next_e = jnp.min(jnp.where(later, e_id[None, :], ne), axis=1)
    next_e = jnp.where(next_e < ne, next_e, -1)[tile_expert]
    tables = (tile_expert, tile_rows, first, wslot.astype(i32), next_e.astype(i32), n_valid[None].astype(i32))
    picks = ei[:, 0:4].T
    pick_off = jnp.sum(jnp.where(picks[None, 0:2] == e_id[:, None, None], row_off[:, None, None], 0), axis=0)
    dest = pick_off + picks[2:4]
    tok = jnp.broadcast_to(jnp.arange(ei.shape[0], dtype=i32)[None, :], dest.shape)
    src = jnp.zeros((n_tiles * tm,), i32).at[dest.reshape(-1)].set(
        tok.reshape(-1), unique_indices=True, mode="promise_in_bounds")
    return tables, src.reshape(n_tiles, 1, tm), dest


def _pad_hist(hist):
    return jnp.pad(hist, ((0, 0), (SUBLANES - hist.shape[1], 0), (0, 0)))


def _tile(m, pref):
    return pref if m % pref == 0 else m


def _mixer_segment(cfg, proj, row0, dcol, dT, n_seq, seq_len, hist_xbc, hist_xm, s0, c0, n0, m0, p):
    d = cfg.d_model
    m = n_seq * seq_len
    q = min(cfg.chunk, seq_len)
    nc = seq_len // q
    xa, qa, ka, va, g, gT = conv_qkv(cfg, proj, hist_xbc, hist_xm, p["cwx"], p["cbx"], p["cwm"], p["cbm"],
                                     p["wq"], p["wk"], p["wv"], p["wg"], p["wgT"], p["bg_row"], p["bg_col"],
                                     row0, n_seq, seq_len, q)
    dTc = dT[:cfg.ssd_heads].reshape(cfg.ssd_heads, n_seq, nc, q).transpose(1, 2, 0, 3)
    nb = 2 if (nc == 1 and n_seq % 2 == 0) else 1
    seq3 = lambda a: a.reshape(n_seq, seq_len, a.shape[-1])
    ys, s_new = ssd_scan(cfg, seq3(xa), seq3(dcol), dTc, s0, p["a_row"], p["a_col"], p["dskip"], n_seq, seq_len, q, nb)
    hm, c_new, n_new, m_new = mlstm_scan(cfg, seq3(qa), seq3(ka), seq3(va), seq3(g), gT, c0, n0, m0,
                                         n_seq, seq_len, q, nb)
    ys = ys.reshape(m, d)
    hm = hm.reshape(m, d)
    keep = cfg.conv_w - 1
    groups = proj.reshape(proj.shape[0] // SUBLANES, SUBLANES, proj.shape[1])
    first = (row0 + seq_len) // SUBLANES - 1
    step = seq_len // SUBLANES
    tail = lax.slice(groups, (first, SUBLANES - keep, 0),
                     (first + (n_seq - 1) * step + 1, SUBLANES, proj.shape[1]), (step, 1, 1))
    tail_xbc = tail[:, :, 3 * d:]
    tail_xm = tail[:, :, d:2 * d]
    return ys, hm, (tail_xbc, s_new, tail_xm, c_new, n_new, m_new)


MOE_TILE = 256


def _ffn(cfg, segments, p):
    d = cfg.d_model
    x1s = [out_proj(cfg, ys, proj, row0, hm, x2d, p["ssd_norm_w"], p["mlstm_norm_w"], p["w_out"],
                    _tile(x2d.shape[0], 256)) for x2d, proj, row0, ys, hm in segments]
    fits = lambda t: all(x1.shape[0] % t == 0 for x1 in x1s)
    ei, wt, cnt, x_rows = router(cfg, x1s[0], x1s[1], p["norm_ffn_w"], p["wr"], p["br"], 512 if fits(512) else 128)
    bm = 256 if fits(256) else 128
    n_tok = ei.shape[0]
    n_tiles = (2 * n_tok + cfg.n_experts * (MOE_TILE - 1)) // MOE_TILE
    tables, src, dest = _route_tables(cfg, ei, cnt, MOE_TILE, n_tiles)
    ys_sorted = moe_routed(cfg, tables, src, x_rows, p["w_gate"], p["w_up"], p["w_down"], MOE_TILE)
    outs = []
    off = 0
    for x1 in x1s:
        m = x1.shape[0]
        dseg = dest[:, off:off + m].reshape(2, m // bm, 1, bm)
        outs.append(moe_combine(cfg, dseg[0], dseg[1], wt, off // bm, x1, p["final_norm_w"], ys_sorted, bm))
        off += m
    return outs


def _prep_params(cfg, norm_mix_w, w_in, conv_ssd_w, conv_ssd_b, dt_bias, a_log, d_skip, ssd_norm_w,
                 conv_mlstm_w, conv_mlstm_b, w_q, w_k, w_v, w_igate, b_igate, w_fgate, b_fgate, mlstm_norm_w,
                 w_out, norm_ffn_w, w_group, b_group, w_router, b_router, w_gate, w_up, w_down, final_norm_w):
    d = cfg.d_model
    hs = cfg.ssd_heads
    o_xbc = d + cfg.xbc_dim
    row = lambda v: v.reshape(1, -1).astype(F32)
    pad_lanes = lambda a: jnp.pad(a, ((0, 0), (0, LANES - a.shape[1])))
    w_in_t = w_in.T.astype(F32)
    w_dt_t = jnp.pad(w_in_t[o_xbc:o_xbc + hs], ((0, LANES - hs), (0, 0)))
    a = -jnp.exp(a_log.astype(F32))
    w_gates = jnp.concatenate([w_igate, w_fgate], axis=1)
    b_gates = jnp.concatenate([b_igate, b_fgate]).astype(F32)
    wr = pad_lanes(jnp.concatenate([w_router, w_group], axis=1))
    br = pad_lanes(jnp.concatenate([b_router, b_group]).reshape(1, -1).astype(F32))
    return dict(
        norm_mix_w=row(norm_mix_w),
        w_in_t=w_in_t, w_dt=w_dt_t.T.astype(BF16), w_dtT=w_dt_t.astype(BF16),
        bdt_row=pad_lanes(row(dt_bias)), bdt_col=pad_lanes(row(dt_bias)).T,
        cwx=conv_ssd_w.astype(F32), cbx=row(conv_ssd_b), cwm=conv_mlstm_w.astype(F32), cbm=row(conv_mlstm_b),
        wq=w_q.astype(BF16), wk=w_k.astype(BF16), wv=w_v.astype(BF16),
        wg=pad_lanes(w_gates).astype(BF16), wgT=w_gates.T.astype(BF16),
        bg_row=pad_lanes(row(b_gates)), bg_col=b_gates.reshape(-1, 1),
        a_row=pad_lanes(row(a)), a_col=a.reshape(-1, 1),
        dskip=row(jnp.repeat(d_skip.astype(F32), cfg.ssd_head_dim)),
        ssd_norm_w=row(ssd_norm_w), mlstm_norm_w=row(mlstm_norm_w), w_out=w_out.astype(BF16),
        norm_ffn_w=row(norm_ffn_w), wr=wr.astype(BF16), br=br,
        w_gate=w_gate.astype(F32), w_up=w_up.astype(F32), w_down=w_down.astype(F32),
        final_norm_w=row(final_norm_w),
    )


def forward(cfg, x_prompt, x_sample, state_ssd_conv, state_ssd, state_mlstm_conv, state_mlstm_c,
            state_mlstm_n, state_mlstm_m, meta_tokens, *weights):
    d = cfg.d_model
    nh = cfg.ml_heads
    hd = cfg.ml_head_dim
    assert state_ssd.shape[0] == 1, "single-layer kernel"
    p = _prep_params(cfg, *[w[0] for w in weights[:-1]], weights[-1])
    bp, lp, _ = x_prompt.shape
    bs, ls, _ = x_sample.shape
    n_meta = meta_tokens.shape[0]

    xp = x_prompt.reshape(bp * lp, d)
    xs = x_sample.reshape(bs * ls, d)
    mp, ms = bp * lp, bs * ls
    norm_args = (p["norm_mix_w"], p["w_dt"], p["w_dtT"], p["bdt_row"], p["bdt_col"])
    h_meta, d_meta, dT_meta = pre_norm(meta_tokens.astype(F32), None, *norm_args, n_meta)
    h, dcol, dT = pre_norm(xp, xs, *norm_args, 512 if mp % 512 == 0 and ms % 512 == 0 else 128)
    proj, proj_meta = in_proj(cfg, h, h_meta, p["w_in_t"], _tile(mp + ms, 1536), d // 2)

    zeros = lambda *s: jnp.zeros(s, F32)
    _, _, st_meta = _mixer_segment(
        cfg, proj_meta, 0, d_meta, dT_meta, 1, n_meta, zeros(1, SUBLANES, cfg.xbc_dim), zeros(1, SUBLANES, d),
        zeros(1, d, cfg.ssd_state), zeros(1, d, hd), zeros(1, nh, hd), zeros(1, nh, LANES), p)
    mt_xbc, mt_s, mt_xm, mt_c, mt_n, mt_m = st_meta
    rep = lambda a: jnp.broadcast_to(a, (bp,) + a.shape[1:])

    ys_p, hm_p, st_p = _mixer_segment(
        cfg, proj, 0, dcol[:mp], dT[:, :mp], bp, lp, rep(_pad_hist(mt_xbc)), rep(_pad_hist(mt_xm)),
        rep(mt_s), rep(mt_c), rep(mt_n), rep(mt_m), p)

    m0 = jnp.broadcast_to(state_mlstm_m[0].astype(F32)[:, :, None], (bs, nh, LANES))
    ys_s, hm_s, st_s = _mixer_segment(
        cfg, proj, mp, dcol[mp:], dT[:, mp:], bs, ls, _pad_hist(state_ssd_conv[0]),
        _pad_hist(state_mlstm_conv[0]), state_ssd[0].reshape(bs, d, cfg.ssd_state),
        state_mlstm_c[0].reshape(bs, d, hd), state_mlstm_n[0], m0, p)
    y_p, y_s = _ffn(cfg, [(xp, proj, 0, ys_p, hm_p), (xs, proj, mp, ys_s, hm_s)], p)
    y_prompt = y_p.reshape(bp, lp, d)
    y_sample = y_s.reshape(bs, ls, d)

    def pack(st, b):
        t_xbc, s_new, t_xm, c_new, n_new, m_new = st
        return (t_xbc[None], s_new.reshape(1, b, cfg.ssd_heads, cfg.ssd_head_dim, cfg.ssd_state),
                t_xm[None], c_new.reshape(1, b, nh, hd, hd), n_new[None], m_new[None, :, :, 0])

    return (y_prompt, y_sample) + pack(st_p, bp) + pack(st_s, bs)


def kernel(x_prompt, x_sample, state_ssd_conv, state_ssd, state_mlstm_conv, state_mlstm_c, state_mlstm_n, state_mlstm_m, meta_tokens, norm_mix_w, w_in, conv_ssd_w, conv_ssd_b, dt_bias, a_log, d_skip, ssd_norm_w, conv_mlstm_w, conv_mlstm_b, w_q, w_k, w_v, w_igate, b_igate, w_fgate, b_fgate, mlstm_norm_w, w_out, norm_ffn_w, w_group, b_group, w_router, b_router, w_gate, w_up, w_down, final_norm_w):
    return forward(Cfg(), x_prompt, x_sample, state_ssd_conv, state_ssd, state_mlstm_conv, state_mlstm_c,
                   state_mlstm_n, state_mlstm_m, meta_tokens, norm_mix_w, w_in, conv_ssd_w, conv_ssd_b, dt_bias,
                   a_log, d_skip, ssd_norm_w, conv_mlstm_w, conv_mlstm_b, w_q, w_k, w_v, w_igate, b_igate,
                   w_fgate, b_fgate, mlstm_norm_w, w_out, norm_ffn_w, w_group, b_group, w_router, b_router,
                   w_gate, w_up, w_down, final_norm_w)
```

```python
import functools
from typing import NamedTuple

import jax
import jax.numpy as jnp
from jax import lax
from jax.experimental import pallas as pl
from jax.experimental.pallas import tpu as pltpu

F32 = jnp.float32
BF16 = jnp.bfloat16
EPS = 1e-6
LANES = 128
SUBLANES = 8
VMEM_LIMIT = 56 * 1024 * 1024
HI = lax.Precision.HIGHEST


class Cfg(NamedTuple):
    d_model: int = 2048
    ssd_heads: int = 32
    ssd_head_dim: int = 64
    ssd_groups: int = 4
    ssd_state: int = 128
    ml_heads: int = 8
    ml_head_dim: int = 256
    n_groups: int = 4
    experts_per_group: int = 8
    d_expert: int = 512
    conv_w: int = 4
    chunk: int = 128

    @property
    def bc_dim(self):
        return self.ssd_groups * self.ssd_state

    @property
    def xbc_dim(self):
        return self.d_model + 2 * self.bc_dim

    @property
    def n_experts(self):
        return self.n_groups * self.experts_per_group


def _cparams(sem):
    return pltpu.CompilerParams(dimension_semantics=sem, vmem_limit_bytes=VMEM_LIMIT)


def _softplus(x):
    return jnp.maximum(x, 0.0) + jnp.log1p(jnp.exp(-jnp.abs(x)))


def _sigmoid(x):
    return 1.0 / (1.0 + jnp.exp(-x))


def _silu(x):
    return x * _sigmoid(x)


def _rms(x, w):
    return x * lax.rsqrt(jnp.mean(x * x, axis=-1, keepdims=True) + EPS) * w


def _prenorm_kernel(x_ref, nw_ref, wdt_ref, wdtT_ref, bdt_row_ref, bdt_col_ref, h_ref, d_ref, dT_ref):
    hb = _rms(x_ref[...], nw_ref[...]).astype(BF16)
    h_ref[...] = hb
    dt = jnp.dot(hb, wdt_ref[...], preferred_element_type=F32)
    d_ref[...] = _softplus(dt + bdt_row_ref[...])
    dtT = lax.dot_general(wdtT_ref[...], hb, (((1,), (1,)), ((), ())), preferred_element_type=F32)
    dT_ref[...] = _softplus(dtT + bdt_col_ref[...])


def _prenorm_pair_kernel(n_a, xa_ref, xb_ref, nw_ref, wdt_ref, wdtT_ref, bdt_row_ref, bdt_col_ref,
                         h_ref, d_ref, dT_ref, x_ref):
    i = pl.program_id(0)

    @pl.when(i < n_a)
    def _():
        x_ref[...] = xa_ref[...]

    @pl.when(i >= n_a)
    def _():
        x_ref[...] = xb_ref[...]

    _prenorm_kernel(x_ref, nw_ref, wdt_ref, wdtT_ref, bdt_row_ref, bdt_col_ref, h_ref, d_ref, dT_ref)


def pre_norm(xa, xb, norm_w, w_dt, w_dtT, bdt_row, bdt_col, bm):
    d = xa.shape[1]
    n_a = xa.shape[0] // bm
    n_b = 0 if xb is None else xb.shape[0] // bm
    m = (n_a + n_b) * bm
    const2 = lambda i: (0, 0)
    w_specs = [pl.BlockSpec((1, d), const2), pl.BlockSpec((d, LANES), const2), pl.BlockSpec((LANES, d), const2),
               pl.BlockSpec((1, LANES), const2), pl.BlockSpec((LANES, 1), const2)]
    if xb is None:
        body, x_specs, xs, scratch = _prenorm_kernel, [pl.BlockSpec((bm, d), lambda i: (i, 0))], (xa,), []
    else:
        body = functools.partial(_prenorm_pair_kernel, n_a)
        x_specs = [pl.BlockSpec((bm, d), lambda i: (jnp.minimum(i, n_a - 1), 0)),
                   pl.BlockSpec((bm, d), lambda i: (jnp.maximum(i - n_a, 0), 0))]
        xs, scratch = (xa, xb), [pltpu.VMEM((bm, d), F32)]
    return pl.pallas_call(
        body,
        out_shape=(jax.ShapeDtypeStruct((m, d), BF16), jax.ShapeDtypeStruct((m, LANES), F32),
                   jax.ShapeDtypeStruct((LANES, m), F32)),
        grid=(n_a + n_b,),
        in_specs=x_specs + w_specs,
        out_specs=(pl.BlockSpec((bm, d), lambda i: (i, 0)), pl.BlockSpec((bm, LANES), lambda i: (i, 0)),
                   pl.BlockSpec((LANES, bm), lambda i: (0, i))),
        scratch_shapes=scratch,
        compiler_params=_cparams(("arbitrary",)),
        name="pre_norm",
    )(*xs, norm_w, w_dt, w_dtT, bdt_row, bdt_col)


def _inproj_kernel(h_ref, hs_ref, wt_ref, proj_ref, projs_ref, w_ref):
    nt_dims = (((1,), (1,)), ((), ()))

    @pl.when(pl.program_id(1) == 0)
    def _():
        w_ref[...] = wt_ref[...].astype(BF16)
        projs_ref[...] = lax.dot_general(hs_ref[...], w_ref[...], nt_dims, preferred_element_type=F32)

    proj_ref[...] = lax.dot_general(h_ref[...], w_ref[...], nt_dims, preferred_element_type=F32)


def in_proj(cfg, h, h_small, w_in_t, bm, bn):
    m, d = h.shape
    ms = h_small.shape[0]
    nz = d // bn
    nx = cfg.xbc_dim // bn
    n_a = nz + nx
    n_blocks = n_a + 2 * nz
    skip = cfg.ssd_heads
    assert skip % SUBLANES == 0
    w_row = lambda j: pl.multiple_of(jnp.where(j < n_a, j * bn, j * bn + skip), SUBLANES)
    out_col = lambda j: jnp.where(j < nz, j, jnp.where(j < n_a, j + 2 * nz, j - nx))
    return pl.pallas_call(
        _inproj_kernel,
        out_shape=(jax.ShapeDtypeStruct((m, n_blocks * bn), F32), jax.ShapeDtypeStruct((ms, n_blocks * bn), F32)),
        grid=(n_blocks, m // bm),
        in_specs=[pl.BlockSpec((bm, d), lambda j, i: (i, 0)),
                  pl.BlockSpec((ms, d), lambda j, i: (0, 0)),
                  pl.BlockSpec((pl.Element(bn), pl.Element(d)), lambda j, i: (w_row(j), 0))],
        out_specs=(pl.BlockSpec((bm, bn), lambda j, i: (i, out_col(j))),
                   pl.BlockSpec((ms, bn), lambda j, i: (0, out_col(j)))),
        scratch_shapes=[pltpu.VMEM((bn, d), BF16)],
        compiler_params=_cparams(("arbitrary", "arbitrary")),
        name="in_proj",
    )(h, h_small, w_in_t)


CONV_LANES = 256


def _causal_conv(u, prev, w_ref, b_ref, cols, conv_w):
    lt = u.shape[0]
    row8 = lax.broadcasted_iota(jnp.int32, (SUBLANES, u.shape[1]), 0)
    acc = u * w_ref[conv_w - 1:conv_w, cols] + b_ref[:, cols]
    for s in range(1, conv_w):
        rolled = pltpu.roll(u, s, axis=0)
        head = jnp.where(row8 < s, pltpu.roll(prev, s, axis=0), rolled[0:SUBLANES])
        shifted = head if lt == SUBLANES else jnp.concatenate([head, rolled[SUBLANES:]], axis=0)
        acc = acc + shifted * w_ref[conv_w - 1 - s:conv_w - s, cols]
    return acc


def _conv_qkv_tile(cfg, xbc_ref, xm_ref, cwx_ref, cbx_ref, cwm_ref, cbm_ref, wq_ref, wk_ref, wv_ref, wg_ref, wgT_ref,
                   bg_row_ref, bg_col_ref, xa_ref, q_ref, k_ref, v_ref, g_ref, gT_ref, px_ref, pm_ref):
    lt = xbc_ref.shape[0]
    hd = cfg.ml_head_dim
    nh = cfg.ml_heads

    for c0 in range(0, xbc_ref.shape[1], CONV_LANES):
        cols = slice(c0, min(c0 + CONV_LANES, xbc_ref.shape[1]))
        u = xbc_ref[:, cols]
        xa_ref[:, cols] = _silu(_causal_conv(u, px_ref[:, cols], cwx_ref, cbx_ref, cols, cfg.conv_w))
        px_ref[:, cols] = u[lt - SUBLANES:lt]

    d = nh * hd
    kscale = hd ** -0.5
    nt = (((1,), (1,)), ((), ()))
    gcol = jnp.zeros(g_ref.shape, F32) + bg_row_ref[...]
    grow = jnp.zeros(gT_ref.shape, F32) + bg_col_ref[...]
    for h in range(nh):
        sl = slice(h * hd, (h + 1) * hd)
        xm = xm_ref[:, sl]
        xc = _silu(_causal_conv(xm, pm_ref[:, sl], cwm_ref, cbm_ref, sl, cfg.conv_w)).astype(BF16)
        pm_ref[:, sl] = xm[lt - SUBLANES:lt]
        qh = jnp.dot(xc, wq_ref[h], preferred_element_type=F32)
        kh = jnp.dot(xc, wk_ref[h], preferred_element_type=F32) * kscale
        vh = jnp.dot(xm.astype(BF16), wv_ref[h], preferred_element_type=F32)
        q_ref[:, sl] = qh
        k_ref[:, sl] = kh
        v_ref[:, sl] = vh
        for part, val in enumerate((qh, kh, vh)):
            vb = val.astype(BF16)
            rows = slice(part * d + h * hd, part * d + (h + 1) * hd)
            gcol = gcol + jnp.dot(vb, wg_ref[rows, :], preferred_element_type=F32)
            grow = grow + lax.dot_general(wgT_ref[:, rows], vb, nt, preferred_element_type=F32)
    lane = lax.broadcasted_iota(jnp.int32, gcol.shape, 1)
    g_ref[...] = jnp.where(lane < nh, gcol, -_softplus(-gcol))
    row = lax.broadcasted_iota(jnp.int32, grow.shape, 0)
    gT_ref[...] = jnp.where(row < nh, grow, -_softplus(-grow))


def _conv_qkv_kernel(cfg, xbc_ref, xm_ref, hxbc_ref, hxm_ref, cwx_ref, cbx_ref, cwm_ref, cbm_ref,
                     wq_ref, wk_ref, wv_ref, wg_ref, wgT_ref, bg_row_ref, bg_col_ref,
                     xa_ref, q_ref, k_ref, v_ref, g_ref, gT_ref, px_ref, pm_ref):
    @pl.when(pl.program_id(1) == 0)
    def _():
        px_ref[...] = hxbc_ref[0]
        pm_ref[...] = hxm_ref[0]

    _conv_qkv_tile(cfg, xbc_ref, xm_ref, cwx_ref, cbx_ref, cwm_ref, cbm_ref, wq_ref, wk_ref, wv_ref, wg_ref, wgT_ref,
                   bg_row_ref, bg_col_ref, xa_ref, q_ref, k_ref, v_ref, g_ref, gT_ref.at[0, 0], px_ref, pm_ref)


def conv_qkv(cfg, proj, hist_xbc, hist_xm, cwx, cbx, cwm, cbm, wq, wk, wv, wg, wgT, bg_row, bg_col,
             row0, n_seq, seq_len, lt):
    d = cfg.d_model
    xbc = cfg.xbc_dim
    m = n_seq * seq_len
    nt = seq_len // lt
    ng = 2 * cfg.ml_heads
    xbc_blk = (3 * d) // xbc
    blk0 = row0 // lt
    row = lambda s, l: (s * nt + l, 0)
    const2 = lambda s, l: (0, 0)
    const3 = lambda s, l: (0, 0, 0)
    return pl.pallas_call(
        functools.partial(_conv_qkv_kernel, cfg),
        out_shape=(jax.ShapeDtypeStruct((m, xbc), F32),
                   jax.ShapeDtypeStruct((m, d), F32), jax.ShapeDtypeStruct((m, d), F32),
                   jax.ShapeDtypeStruct((m, d), F32),
                   jax.ShapeDtypeStruct((m, LANES), F32),
                   jax.ShapeDtypeStruct((n_seq, nt, ng, lt), F32)),
        grid=(n_seq, nt),
        in_specs=[pl.BlockSpec((lt, xbc), lambda s, l: (blk0 + s * nt + l, xbc_blk)),
                  pl.BlockSpec((lt, d), lambda s, l: (blk0 + s * nt + l, 1)),
                  pl.BlockSpec((1, SUBLANES, xbc), lambda s, l: (s, 0, 0)),
                  pl.BlockSpec((1, SUBLANES, d), lambda s, l: (s, 0, 0)),
                  pl.BlockSpec((cfg.conv_w, xbc), const2), pl.BlockSpec((1, xbc), const2),
                  pl.BlockSpec((cfg.conv_w, d), const2), pl.BlockSpec((1, d), const2),
                  pl.BlockSpec(wq.shape, const3), pl.BlockSpec(wk.shape, const3),
                  pl.BlockSpec(wv.shape, const3),
                  pl.BlockSpec(wg.shape, const2), pl.BlockSpec(wgT.shape, const2),
                  pl.BlockSpec((1, LANES), const2), pl.BlockSpec((ng, 1), const2)],
        out_specs=(pl.BlockSpec((lt, xbc), row), pl.BlockSpec((lt, d), row), pl.BlockSpec((lt, d), row),
                   pl.BlockSpec((lt, d), row), pl.BlockSpec((lt, LANES), row),
                   pl.BlockSpec((1, 1, ng, lt), lambda s, l: (s, l, 0, 0))),
        scratch_shapes=[pltpu.VMEM((SUBLANES, xbc), F32), pltpu.VMEM((SUBLANES, d), F32)],
        compiler_params=_cparams(("arbitrary", "arbitrary")),
        name="conv_qkv",
    )(proj, proj, hist_xbc, hist_xm, cwx, cbx, cwm, cbm, wq, wk, wv, wg, wgT, bg_row, bg_col)


def _tri(q, lower):
    r = lax.broadcasted_iota(jnp.int32, (q, q), 0)
    c = lax.broadcasted_iota(jnp.int32, (q, q), 1)
    return (c <= r) if lower else (r <= c)


def _ssd_chunk(cfg, xa_ref, d_ref, dT_ref, arow_ref, acol_ref, dskip_ref, y_ref, st_ref):
    q = xa_ref.shape[0]
    dm = cfg.d_model
    ns = cfg.ssd_state
    hp = cfg.ssd_head_dim
    hpg = cfg.ssd_heads // cfg.ssd_groups
    heads_per_tile = LANES // hp
    n_tiles = cfg.ssd_heads // heads_per_tile

    causal = _tri(q, True)
    tril = causal.astype(F32)
    triu = _tri(q, False).astype(F32)
    dcol = d_ref[...]
    drow = dT_ref[...]
    acum = jnp.dot(tril, dcol * arow_ref[...], precision=HI, preferred_element_type=F32)
    acumT = jnp.dot(drow * acol_ref[...], triu, precision=HI, preferred_element_type=F32)
    nt_dims = (((1,), (1,)), ((), ()))
    tn_dims = (((0,), (0,)), ((), ()))
    lane = lax.broadcasted_iota(jnp.int32, (q, LANES), 1)
    srow = lax.broadcasted_iota(jnp.int32, (LANES, ns), 0)

    cbs = []
    bgs = []
    cgs = []
    for g in range(cfg.ssd_groups):
        bg = xa_ref[:, dm + g * ns: dm + (g + 1) * ns].astype(BF16)
        cg = xa_ref[:, dm + cfg.bc_dim + g * ns: dm + cfg.bc_dim + (g + 1) * ns].astype(BF16)
        cbs.append(lax.dot_general(cg, bg, nt_dims, preferred_element_type=F32))
        bgs.append(bg)
        cgs.append(cg)

    for t in range(n_tiles):
        h0 = t * heads_per_tile
        g = h0 // hpg
        cols = slice(t * LANES, (t + 1) * LANES)
        x = xa_ref[:, cols]
        dsel = jnp.zeros((q, LANES), F32)
        asel = jnp.zeros((q, LANES), F32)
        rdec = jnp.zeros((LANES, ns), F32)
        for i in range(heads_per_tile):
            h = h0 + i
            in_head = (lane >= i * hp) & (lane < (i + 1) * hp)
            a_last = acum[q - 1:q, h:h + 1]
            dsel = jnp.where(in_head, dcol[:, h:h + 1], dsel)
            asel = jnp.where(in_head, acum[:, h:h + 1], asel)
            rdec = jnp.where((srow >= i * hp) & (srow < (i + 1) * hp), jnp.exp(a_last), rdec)
        esel = jnp.exp(asel)
        tsel = jnp.exp(asel[q - 1:q, :] - asel)
        xd = x * dsel
        y = x * dskip_ref[:, cols]
        for i in range(heads_per_tile):
            h = h0 + i
            in_head = (lane >= i * hp) & (lane < (i + 1) * hp)
            seg = jnp.where(causal, acum[:, h:h + 1] - acumT[h:h + 1, :], -jnp.inf)
            w = (cbs[g] * jnp.exp(seg)).astype(BF16)
            xdh = jnp.where(in_head, xd, 0.0).astype(BF16)
            y = y + jnp.dot(w, xdh, preferred_element_type=F32)
        s_old = st_ref[cols, :]
        ys = lax.dot_general(cgs[g], s_old.astype(BF16), nt_dims, preferred_element_type=F32)
        y_ref[:, cols] = y + esel * ys
        upd = lax.dot_general((xd * tsel).astype(BF16), bgs[g], tn_dims, preferred_element_type=F32)
        st_ref[cols, :] = rdec * s_old + upd


def _ssd_kernel(cfg, n_chunks, xa_ref, d_ref, dT_ref, s0_ref, arow_ref, acol_ref, dskip_ref,
                y_ref, sout_ref, st_ref):
    @pl.when(pl.program_id(1) == 0)
    def _():
        st_ref[...] = s0_ref[...]

    for b in range(xa_ref.shape[0]):
        _ssd_chunk(cfg, xa_ref.at[b], d_ref.at[b], dT_ref.at[b, 0], arow_ref, acol_ref, dskip_ref,
                   y_ref.at[b], st_ref.at[b])

    @pl.when(pl.program_id(1) == n_chunks - 1)
    def _():
        sout_ref[...] = st_ref[...]


def ssd_scan(cfg, xa, d, dT, s0, a_row, a_col, dskip, n_seq, seq_len, q, nb):
    dm = cfg.d_model
    nc = seq_len // q
    blk3 = lambda s, c: (s, c, 0)
    st3 = lambda s, c: (s, 0, 0)
    const2 = lambda s, c: (0, 0)
    return pl.pallas_call(
        functools.partial(_ssd_kernel, cfg, nc),
        out_shape=(jax.ShapeDtypeStruct((n_seq, seq_len, dm), F32),
                   jax.ShapeDtypeStruct((n_seq, dm, cfg.ssd_state), F32)),
        grid=(n_seq // nb, nc),
        in_specs=[pl.BlockSpec((nb, q, cfg.xbc_dim), blk3),
                  pl.BlockSpec((nb, q, LANES), blk3),
                  pl.BlockSpec((nb, 1, cfg.ssd_heads, q), lambda s, c: (s, c, 0, 0)),
                  pl.BlockSpec((nb, dm, cfg.ssd_state), st3),
                  pl.BlockSpec((1, LANES), const2),
                  pl.BlockSpec((cfg.ssd_heads, 1), const2),
                  pl.BlockSpec((1, dm), const2)],
        out_specs=(pl.BlockSpec((nb, q, dm), blk3),
                   pl.BlockSpec((nb, dm, cfg.ssd_state), st3)),
        scratch_shapes=[pltpu.VMEM((nb, dm, cfg.ssd_state), F32)],
        compiler_params=_cparams(("arbitrary", "arbitrary")),
        name="ssd_scan",
    )(xa, d, dT, s0, a_row, a_col, dskip)


def _mlstm_chunk(cfg, q_ref, k_ref, v_ref, g_ref, gT_ref, h_ref, c_ref, n_ref, m_ref):
    ql = q_ref.shape[0]
    hd = cfg.ml_head_dim
    nh = cfg.ml_heads

    causal = _tri(ql, True)
    gcol = g_ref[...]
    grow = gT_ref[...]
    bcum = jnp.dot(causal.astype(F32), gcol, precision=HI, preferred_element_type=F32)
    bcumT = jnp.dot(grow, _tri(ql, False).astype(F32), precision=HI, preferred_element_type=F32)
    nt_dims = (((1,), (1,)), ((), ()))
    tn_dims = (((0,), (0,)), ((), ()))

    for h in range(nh):
        sl = slice(h * hd, (h + 1) * hd)
        b_col = bcum[:, nh + h:nh + h + 1]
        b_row = bcumT[nh + h:nh + h + 1, :]
        i_col = gcol[:, h:h + 1]
        i_row = grow[h:h + 1, :]
        m_prev = m_ref[h:h + 1, 0:1]
        rel = jnp.where(causal, i_row - b_row, -jnp.inf)
        inter = b_col + m_prev
        mt = jnp.maximum(inter, b_col + jnp.max(rel, axis=1, keepdims=True))
        qh = q_ref[:, sl]
        kh = k_ref[:, sl]
        vh = v_ref[:, sl]
        qb = qh.astype(BF16)
        kb = kh.astype(BF16)
        s = lax.dot_general(qb, kb, nt_dims, preferred_element_type=F32) * jnp.exp(rel + (b_col - mt))
        gdec = jnp.exp(inter - mt)
        c_old = c_ref[sl, :]
        n_old = n_ref[h:h + 1, :]
        qc = lax.dot_general(qb, c_old.astype(BF16), nt_dims, preferred_element_type=F32)
        num = jnp.dot(s.astype(BF16), vh.astype(BF16), preferred_element_type=F32) + gdec * qc
        den = jnp.sum(s, axis=1, keepdims=True) + gdec * jnp.sum(qh * n_old, axis=1, keepdims=True)
        h_ref[:, sl] = num / jnp.maximum(jnp.abs(den), jnp.exp(-mt))
        m_new = mt[ql - 1:ql, :]
        gs = jnp.exp(b_col[ql - 1:ql, :] - b_col + i_col - m_new)
        gc = jnp.exp(inter[ql - 1:ql, :] - m_new)
        upd = lax.dot_general((vh * gs).astype(BF16), kb, tn_dims, preferred_element_type=F32)
        c_ref[sl, :] = gc * c_old + upd
        n_ref[h:h + 1, :] = gc * n_old + jnp.sum(gs * kh, axis=0, keepdims=True)
        m_ref[h:h + 1, :] = jnp.broadcast_to(m_new, (1, LANES))


def _mlstm_kernel(cfg, n_chunks, q_ref, k_ref, v_ref, g_ref, gT_ref, c0_ref, n0_ref, m0_ref,
                  h_ref, cout_ref, nout_ref, mout_ref, c_ref, n_ref, m_ref):
    @pl.when(pl.program_id(1) == 0)
    def _():
        c_ref[...] = c0_ref[...]
        n_ref[...] = n0_ref[...]
        m_ref[...] = m0_ref[...]

    for b in range(q_ref.shape[0]):
        _mlstm_chunk(cfg, q_ref.at[b], k_ref.at[b], v_ref.at[b], g_ref.at[b], gT_ref.at[b, 0],
                     h_ref.at[b], c_ref.at[b], n_ref.at[b], m_ref.at[b])

    @pl.when(pl.program_id(1) == n_chunks - 1)
    def _():
        cout_ref[...] = c_ref[...]
        nout_ref[...] = n_ref[...]
        mout_ref[...] = m_ref[...]


def mlstm_scan(cfg, qa, ka, va, g, gT, c0, n0, m0, n_seq, seq_len, q, nb):
    d = cfg.d_model
    hd = cfg.ml_head_dim
    nh = cfg.ml_heads
    nc = seq_len // q
    blk3 = lambda s, c: (s, c, 0)
    st3 = lambda s, c: (s, 0, 0)
    return pl.pallas_call(
        functools.partial(_mlstm_kernel, cfg, nc),
        out_shape=(jax.ShapeDtypeStruct((n_seq, seq_len, d), F32),
                   jax.ShapeDtypeStruct((n_seq, d, hd), F32),
                   jax.ShapeDtypeStruct((n_seq, nh, hd), F32),
                   jax.ShapeDtypeStruct((n_seq, nh, LANES), F32)),
        grid=(n_seq // nb, nc),
        in_specs=[pl.BlockSpec((nb, q, d), blk3), pl.BlockSpec((nb, q, d), blk3), pl.BlockSpec((nb, q, d), blk3),
                  pl.BlockSpec((nb, q, LANES), blk3),
                  pl.BlockSpec((nb, 1, 2 * nh, q), lambda s, c: (s, c, 0, 0)),
                  pl.BlockSpec((nb, d, hd), st3), pl.BlockSpec((nb, nh, hd), st3),
                  pl.BlockSpec((nb, nh, LANES), st3)],
        out_specs=(pl.BlockSpec((nb, q, d), blk3),
                   pl.BlockSpec((nb, d, hd), st3), pl.BlockSpec((nb, nh, hd), st3),
                   pl.BlockSpec((nb, nh, LANES), st3)),
        scratch_shapes=[pltpu.VMEM((nb, d, hd), F32), pltpu.VMEM((nb, nh, hd), F32),
                        pltpu.VMEM((nb, nh, LANES), F32)],
        compiler_params=_cparams(("arbitrary", "arbitrary")),
        name="mlstm_scan",
    )(qa, ka, va, g, gT, c0, n0, m0)


def _group_norm(x, w_ref, col0, groups, width):
    parts = []
    for g in range(groups):
        seg = x[:, g * width:(g + 1) * width]
        parts.append(seg * lax.rsqrt(jnp.mean(seg * seg, axis=-1, keepdims=True) + EPS)
                     * w_ref[:, col0 + g * width: col0 + (g + 1) * width])
    return parts


def _outproj_kernel(cfg, ys_ref, z_ref, hm_ref, o_ref, x_ref, nws_ref, nwm_ref, w_ref, out_ref):
    d = cfg.d_model
    ws = d // cfg.ssd_groups
    yz = ys_ref[...] * _silu(z_ref[...])
    ssd_half = jnp.concatenate([part.astype(BF16) for part in _group_norm(yz, nws_ref, 0, cfg.ssd_groups, ws)],
                               axis=-1)
    acc = x_ref[...] + jnp.dot(ssd_half, w_ref[0:d, :], preferred_element_type=F32)
    gate = _sigmoid(o_ref[...])
    wm = cfg.ml_head_dim
    ml_half = jnp.concatenate(
        [(part * gate[:, g * wm:(g + 1) * wm]).astype(BF16)
         for g, part in enumerate(_group_norm(hm_ref[...], nwm_ref, 0, cfg.ml_heads, wm))], axis=-1)
    out_ref[...] = acc + jnp.dot(ml_half, w_ref[d:2 * d, :], preferred_element_type=F32)


def out_proj(cfg, ys, proj, proj_row0, hm, x, nws, nwm, w_out, bm):
    m, d = x.shape
    blk0 = proj_row0 // bm
    const2 = lambda i: (0, 0)
    full = lambda i: (i, 0)
    return pl.pallas_call(
        functools.partial(_outproj_kernel, cfg),
        out_shape=jax.ShapeDtypeStruct((m, d), F32),
        grid=(m // bm,),
        in_specs=[pl.BlockSpec((bm, d), full),
                  pl.BlockSpec((bm, d), lambda i: (i + blk0, 0)),
                  pl.BlockSpec((bm, d), full),
                  pl.BlockSpec((bm, d), lambda i: (i + blk0, 2)),
                  pl.BlockSpec((bm, d), full),
                  pl.BlockSpec((1, d), const2), pl.BlockSpec((1, d), const2),
                  pl.BlockSpec((2 * d, d), const2, pipeline_mode=pl.Buffered(1))],
        out_specs=pl.BlockSpec((bm, d), full),
        compiler_params=_cparams(("arbitrary",)),
        name="out_proj",
    )(ys, proj, hm, proj, x, nws, nwm, w_out)


def _router_kernel(cfg, n_a, xa_ref, xb_ref, nw_ref, wr_ref, br_ref, ei_ref, wt_ref, cnt_out_ref, rows_ref,
                   cnt_ref, x_ref):
    ne = cfg.n_experts
    epg = cfg.experts_per_group
    ngr = cfg.n_groups
    bm = x_ref.shape[0]
    i = pl.program_id(0)

    @pl.when(i == 0)
    def _():
        cnt_ref[...] = jnp.zeros_like(cnt_ref)

    @pl.when(i < n_a)
    def _():
        x_ref[...] = xa_ref[...]

    @pl.when(i >= n_a)
    def _():
        x_ref[...] = xb_ref[...]

    hb = _rms(x_ref[...], nw_ref[...]).astype(BF16)
    rows_ref[...] = _lanes_to_rows(_pack_bf16_pairs(hb))
    logits = jnp.dot(hb, wr_ref[...], preferred_element_type=F32) + br_ref[...]
    lane = lax.broadcasted_iota(jnp.int32, logits.shape, 1)
    big = jnp.int32(2 ** 30)
    neg = -jnp.inf

    def first_argmax(vals):
        mx = jnp.max(vals, axis=-1, keepdims=True)
        idx = jnp.min(jnp.where(vals == mx, lane, big), axis=-1, keepdims=True)
        return mx, idx

    is_group = (lane >= ne) & (lane < ne + ngr)
    gl = jnp.where(is_group, logits, neg)
    gmax, gidx = first_argmax(gl)
    p_g = 1.0 / jnp.sum(jnp.exp(gl - gmax), axis=-1, keepdims=True)
    e_lo = (gidx - ne) * epg
    in_sel = (lane >= e_lo) & (lane < e_lo + epg)
    el = jnp.where(in_sel, logits, neg)
    pe = jnp.exp(el - jnp.max(el, axis=-1, keepdims=True))
    pe = jnp.where(in_sel, pe / jnp.sum(pe, axis=-1, keepdims=True), -1.0)
    p1, i1 = first_argmax(pe)
    p2, i2 = first_argmax(jnp.where(lane == i1, -1.0, pe))
    wsum = p1 + p2
    wt_ref[...] = jnp.where(lane == 0, p_g * p1 / wsum, jnp.where(lane == 1, p_g * p2 / wsum, 0.0))

    oh1 = jnp.where(lane == i1, 1.0, 0.0)
    oh2 = jnp.where(lane == i2, 1.0, 0.0)
    r = lax.broadcasted_iota(jnp.int32, (bm, bm), 0)
    c = lax.broadcasted_iota(jnp.int32, (bm, bm), 1)
    before = jnp.where(c < r, 1.0, 0.0).astype(BF16)
    ahead1 = jnp.dot(before, oh1.astype(BF16), preferred_element_type=F32)
    ahead2 = jnp.dot(before, oh2.astype(BF16), preferred_element_type=F32)
    cnt = cnt_ref[...]
    tot1 = jnp.sum(oh1, axis=0, keepdims=True)
    rank1 = jnp.sum(oh1 * (cnt + ahead1), axis=-1, keepdims=True)
    rank2 = jnp.sum(oh2 * (cnt + tot1 + ahead2), axis=-1, keepdims=True)
    cnt_new = cnt + tot1 + jnp.sum(oh2, axis=0, keepdims=True)
    cnt_ref[...] = cnt_new
    cnt_out_ref[...] = cnt_new
    ei_ref[...] = jnp.where(lane == 0, i1, jnp.where(lane == 1, i2, jnp.where(
        lane == 2, rank1.astype(jnp.int32), jnp.where(lane == 3, rank2.astype(jnp.int32), 0))))


def router(cfg, xa, xb, nw, wr, br, bm):
    d = xa.shape[1]
    n_a = xa.shape[0] // bm
    n_b = xb.shape[0] // bm
    m = xa.shape[0] + xb.shape[0]
    nch = d // (2 * LANES)
    const2 = lambda i: (0, 0)
    return pl.pallas_call(
        functools.partial(_router_kernel, cfg, n_a),
        out_shape=(jax.ShapeDtypeStruct((m, LANES), jnp.int32), jax.ShapeDtypeStruct((m, LANES), F32),
                   jax.ShapeDtypeStruct((1, LANES), F32), jax.ShapeDtypeStruct((m, nch, LANES), jnp.uint32)),
        grid=(n_a + n_b,),
        in_specs=[pl.BlockSpec((bm, d), lambda i: (jnp.minimum(i, n_a - 1), 0)),
                  pl.BlockSpec((bm, d), lambda i: (jnp.maximum(i - n_a, 0), 0)),
                  pl.BlockSpec((1, d), const2),
                  pl.BlockSpec((d, LANES), const2), pl.BlockSpec((1, LANES), const2)],
        out_specs=(pl.BlockSpec((bm, LANES), lambda i: (i, 0)), pl.BlockSpec((bm, LANES), lambda i: (i, 0)),
                   pl.BlockSpec((1, LANES), const2), pl.BlockSpec((bm, nch, LANES), lambda i: (i, 0, 0))),
        scratch_shapes=[pltpu.VMEM((1, LANES), F32), pltpu.VMEM((bm, d), F32)],
        compiler_params=_cparams(("arbitrary",)),
        name="router",
    )(xa, xb, nw, wr, br)


def _pack_bf16_pairs(x):
    half = x.shape[1] // 2
    bits = lambda v: lax.bitcast_convert_type(v.astype(BF16).astype(F32), jnp.uint32)
    return bits(x[:, :half]) | (bits(x[:, half:]) >> 16)


def _unpack_bf16_pairs(p):
    hi = lax.bitcast_convert_type(p & jnp.uint32(0xFFFF0000), F32)
    lo = lax.bitcast_convert_type(p << 16, F32)
    return jnp.concatenate([hi, lo], axis=-1)


def _rows_to_lanes(g):
    t = pltpu.einshape("rcl->crl", g)
    return jnp.concatenate([t[c] for c in range(t.shape[0])], axis=-1)


def _lanes_to_rows(x):
    parts = jnp.stack([x[:, c * LANES:(c + 1) * LANES] for c in range(x.shape[1] // LANES)], axis=0)
    return pltpu.einshape("crl->rcl", parts)


GATHER_GROUP = 8


def _gather_rows(idx_ref, src_hbm, dst, sem, n_groups):
    def body(g, carry):
        for u in range(GATHER_GROUP):
            r = g * GATHER_GROUP + u
            pltpu.make_async_copy(src_hbm.at[idx_ref[0, 0, r]], dst.at[r], sem).start()
        return carry
    lax.fori_loop(0, n_groups, body, 0)


def _wait_rows(src_hbm, dst, sem, n):
    pltpu.make_async_copy(src_hbm.at[pl.ds(0, n)], dst.at[pl.ds(0, n)], sem).wait()


def _moe_kernel(cfg, te_ref, nr_ref, first_ref, wslot_ref, nexte_ref, nv_ref, src_ref, srcn_ref, x_hbm,
                wg_hbm, wu_hbm, wd_hbm, ys_ref, xbuf, sem, wgf, wuf, wdf, wsem, wgb, wub, wdb):
    j = pl.program_id(0)
    n_valid = nv_ref[0]

    def weight_copies(e, slot):
        return (pltpu.make_async_copy(wg_hbm.at[e], wgf.at[slot], wsem.at[slot, 0]),
                pltpu.make_async_copy(wu_hbm.at[e], wuf.at[slot], wsem.at[slot, 1]),
                pltpu.make_async_copy(wd_hbm.at[e], wdf.at[slot], wsem.at[slot, 2]))

    @pl.when(j == 0)
    def _():
        for cp in weight_copies(te_ref[0], 0):
            cp.start()
        xbuf[...] = jnp.zeros_like(xbuf)
        _gather_rows(src_ref, x_hbm, xbuf.at[0], sem.at[0], nr_ref[0] // GATHER_GROUP)

    @pl.when(j + 1 < n_valid)
    def _():
        nslot = (j + 1) % 2
        _gather_rows(srcn_ref, x_hbm, xbuf.at[nslot], sem.at[nslot], nr_ref[j + 1] // GATHER_GROUP)

    @pl.when(j < n_valid)
    def _():
        @pl.when(first_ref[j] == 1)
        def _():
            ws = wslot_ref[j]
            for cp in weight_copies(te_ref[j], ws):
                cp.wait()

            @pl.when(nexte_ref[j] >= 0)
            def _():
                for cp in weight_copies(nexte_ref[j], 1 - ws):
                    cp.start()

            wgb[...] = wgf[ws].astype(BF16)
            wub[...] = wuf[ws].astype(BF16)
            wdb[...] = wdf[ws].astype(BF16)

        slot = j % 2
        _wait_rows(x_hbm, xbuf.at[slot], sem.at[slot], nr_ref[j])
        hb = _unpack_bf16_pairs(_rows_to_lanes(xbuf[slot])).astype(BF16)
        hid = (_silu(jnp.dot(hb, wgb[...], preferred_element_type=F32))
               * jnp.dot(hb, wub[...], preferred_element_type=F32))
        y = jnp.dot(hid.astype(BF16), wdb[...], preferred_element_type=F32)
        ys_ref[...] = _lanes_to_rows(_pack_bf16_pairs(y))

    @pl.when(j >= n_valid)
    def _():
        ys_ref[...] = jnp.zeros_like(ys_ref)


def moe_routed(cfg, tables, src, x_rows, wg, wu, wd, tm):
    n_tiles = src.shape[0]
    d = cfg.d_model
    de = cfg.d_expert
    nch = x_rows.shape[1]
    return pl.pallas_call(
        functools.partial(_moe_kernel, cfg),
        out_shape=jax.ShapeDtypeStruct((n_tiles * tm, nch, LANES), jnp.uint32),
        grid_spec=pltpu.PrefetchScalarGridSpec(
            num_scalar_prefetch=len(tables),
            grid=(n_tiles,),
            in_specs=[pl.BlockSpec((1, 1, tm), lambda j, *_: (j, 0, 0), memory_space=pltpu.SMEM),
                      pl.BlockSpec((1, 1, tm), lambda j, *_: (jnp.minimum(j + 1, n_tiles - 1), 0, 0),
                                   memory_space=pltpu.SMEM),
                      pl.BlockSpec(memory_space=pl.ANY),
                      pl.BlockSpec(memory_space=pl.ANY), pl.BlockSpec(memory_space=pl.ANY),
                      pl.BlockSpec(memory_space=pl.ANY)],
            out_specs=pl.BlockSpec((tm, nch, LANES), lambda j, *_: (j, 0, 0)),
            scratch_shapes=[pltpu.VMEM((2, tm, nch, LANES), jnp.uint32), pltpu.SemaphoreType.DMA((2,)),
                            pltpu.VMEM((2, d, de), F32), pltpu.VMEM((2, d, de), F32), pltpu.VMEM((2, de, d), F32),
                            pltpu.SemaphoreType.DMA((2, 3)),
                            pltpu.VMEM((d, de), BF16), pltpu.VMEM((d, de), BF16), pltpu.VMEM((de, d), BF16)]),
        compiler_params=_cparams(("arbitrary",)),
        name="moe",
    )(*tables, src, src, x_rows, wg, wu, wd)


def _combine_kernel(n_steps, d0_ref, d1_ref, d0n_ref, d1n_ref, wt_ref, x1_ref, fw_ref, ys_hbm, y_ref, gbuf, sem):
    i = pl.program_id(0)
    groups = x1_ref.shape[0] // GATHER_GROUP

    def fetch(r0_ref, r1_ref, slot):
        _gather_rows(r0_ref, ys_hbm, gbuf.at[slot, 0], sem.at[slot, 0], groups)
        _gather_rows(r1_ref, ys_hbm, gbuf.at[slot, 1], sem.at[slot, 1], groups)

    @pl.when(i == 0)
    def _():
        fetch(d0_ref, d1_ref, 0)

    @pl.when(i + 1 < n_steps)
    def _():
        fetch(d0n_ref, d1n_ref, (i + 1) % 2)

    slot = i % 2
    bm = x1_ref.shape[0]
    _wait_rows(ys_hbm, gbuf.at[slot, 0], sem.at[slot, 0], bm)
    _wait_rows(ys_hbm, gbuf.at[slot, 1], sem.at[slot, 1], bm)
    wt = wt_ref[...]
    acc = (x1_ref[...] + wt[:, 0:1] * _unpack_bf16_pairs(_rows_to_lanes(gbuf[slot, 0]))
           + wt[:, 1:2] * _unpack_bf16_pairs(_rows_to_lanes(gbuf[slot, 1])))
    y_ref[...] = _rms(acc, fw_ref[...])


def moe_combine(cfg, dest0, dest1, wt, wt_blk0, x1, fw, ys, bm):
    m, d = x1.shape
    nch = ys.shape[1]
    n = m // bm
    cur = lambda: pl.BlockSpec((1, 1, bm), lambda i: (i, 0, 0), memory_space=pltpu.SMEM)
    nxt = lambda: pl.BlockSpec((1, 1, bm), lambda i: (jnp.minimum(i + 1, n - 1), 0, 0), memory_space=pltpu.SMEM)
    return pl.pallas_call(
        functools.partial(_combine_kernel, n),
        out_shape=jax.ShapeDtypeStruct((m, d), F32),
        grid=(n,),
        in_specs=[cur(), cur(), nxt(), nxt(),
                  pl.BlockSpec((bm, LANES), lambda i: (i + wt_blk0, 0)),
                  pl.BlockSpec((bm, d), lambda i: (i, 0)), pl.BlockSpec((1, d), lambda i: (0, 0)),
                  pl.BlockSpec(memory_space=pl.ANY)],
        out_specs=pl.BlockSpec((bm, d), lambda i: (i, 0)),
        scratch_shapes=[pltpu.VMEM((2, 2, bm, nch, LANES), jnp.uint32), pltpu.SemaphoreType.DMA((2, 2))],
        compiler_params=_cparams(("arbitrary",)),
        name="moe_combine",
    )(dest0, dest1, dest0, dest1, wt, x1, fw, ys)


def _route_tables(cfg, ei, cnt, tm, n_tiles):
    ne = cfg.n_experts
    i32 = jnp.int32
    counts = cnt[0, :ne].astype(i32)
    tiles_e = (counts + tm - 1) // tm
    tile_end = jnp.cumsum(tiles_e)
    tile_start = tile_end - tiles_e
    row_off = tile_start * tm
    tile_id = jnp.arange(n_tiles, dtype=i32)
    tile_expert = jnp.minimum(jnp.sum((tile_id[:, None] >= tile_end[None, :]).astype(i32), axis=1), ne - 1)
    used = jnp.clip(counts[tile_expert] - (tile_id - tile_start[tile_expert]) * tm, 0, tm)
    used = jnp.where(tile_id < tile_end[-1], used, 0)
    tile_rows = (used + GATHER_GROUP - 1) // GATHER_GROUP * GATHER_GROUP
    n_valid = tile_end[-1]
    prev_expert = jnp.concatenate([jnp.full((1,), -1, i32), tile_expert[:-1]])
    first = ((tile_expert != prev_expert) & (tile_id < n_valid)).astype(i32)
    wslot = (jnp.cumsum(first) - 1) % 2
    e_id = jnp.arange(ne, dtype=i32)
    later = (e_id[None, :] > e_id[:, None]) & (tiles_e[None, :] > 0)
    next_e = jnp.min(jnp.where(later, e_id[None, :], ne), axis=1)
    next_e = jnp.where(next_e < ne, next_e, -1)[tile_expert]
    tables = (tile_expert, tile_rows, first, wslot.astype(i32), next_e.astype(i32), n_valid[None].astype(i32))
    picks = ei[:, 0:4].T
    pick_off = jnp.sum(jnp.where(picks[None, 0:2] == e_id[:, None, None], row_off[:, None, None], 0), axis=0)
    dest = pick_off + picks[2:4]
    tok = jnp.broadcast_to(jnp.arange(ei.shape[0], dtype=i32)[None, :], dest.shape)
    src = jnp.zeros((n_tiles * tm,), i32).at[dest.reshape(-1)].set(
        tok.reshape(-1), unique_indices=True, mode="promise_in_bounds")
    return tables, src.reshape(n_tiles, 1, tm), dest


def _pad_hist(hist):
    return jnp.pad(hist, ((0, 0), (SUBLANES - hist.shape[1], 0), (0, 0)))


def _tile(m, pref):
    return pref if m % pref == 0 else m


def _mixer_segment(cfg, proj, row0, dcol, dT, n_seq, seq_len, hist_xbc, hist_xm, s0, c0, n0, m0, p):
    d = cfg.d_model
    m = n_seq * seq_len
    q = min(cfg.chunk, seq_len)
    nc = seq_len // q
    xa, qa, ka, va, g, gT = conv_qkv(cfg, proj, hist_xbc, hist_xm, p["cwx"], p["cbx"], p["cwm"], p["cbm"],
                                     p["wq"], p["wk"], p["wv"], p["wg"], p["wgT"], p["bg_row"], p["bg_col"],
                                     row0, n_seq, seq_len, q)
    dTc = dT[:cfg.ssd_heads].reshape(cfg.ssd_heads, n_seq, nc, q).transpose(1, 2, 0, 3)
    nb = 2 if (nc == 1 and n_seq % 2 == 0) else 1
    seq3 = lambda a: a.reshape(n_seq, seq_len, a.shape[-1])
    ys, s_new = ssd_scan(cfg, seq3(xa), seq3(dcol), dTc, s0, p["a_row"], p["a_col"], p["dskip"], n_seq, seq_len, q, nb)
    hm, c_new, n_new, m_new = mlstm_scan(cfg, seq3(qa), seq3(ka), seq3(va), seq3(g), gT, c0, n0, m0,
                                         n_seq, seq_len, q, nb)
    ys = ys.reshape(m, d)
    hm = hm.reshape(m, d)
    keep = cfg.conv_w - 1
    groups = proj.reshape(proj.shape[0] // SUBLANES, SUBLANES, proj.shape[1])
    first = (row0 + seq_len) // SUBLANES - 1
    step = seq_len // SUBLANES
    tail = lax.slice(groups, (first, SUBLANES - keep, 0),
                     (first + (n_seq - 1) * step + 1, SUBLANES, proj.shape[1]), (step, 1, 1))
    tail_xbc = tail[:, :, 3 * d:]
    tail_xm = tail[:, :, d:2 * d]
    return ys, hm, (tail_xbc, s_new, tail_xm, c_new, n_new, m_new)


MOE_TILE = 256


def _ffn(cfg, segments, p):
    d = cfg.d_model
    x1s = [out_proj(cfg, ys, proj, row0, hm, x2d, p["ssd_norm_w"], p["mlstm_norm_w"], p["w_out"],
                    _tile(x2d.shape[0], 256)) for x2d, proj, row0, ys, hm in segments]
    fits = lambda t: all(x1.shape[0] % t == 0 for x1 in x1s)
    ei, wt, cnt, x_rows = router(cfg, x1s[0], x1s[1], p["norm_ffn_w"], p["wr"], p["br"], 512 if fits(512) else 128)
    bm = 256 if fits(256) else 128
    n_tok = ei.shape[0]
    n_tiles = (2 * n_tok + cfg.n_experts * (MOE_TILE - 1)) // MOE_TILE
    tables, src, dest = _route_tables(cfg, ei, cnt, MOE_TILE, n_tiles)
    ys_sorted = moe_routed(cfg, tables, src, x_rows, p["w_gate"], p["w_up"], p["w_down"], MOE_TILE)
    outs = []
    off = 0
    for x1 in x1s:
        m = x1.shape[0]
        dseg = dest[:, off:off + m].reshape(2, m // bm, 1, bm)
        outs.append(moe_combine(cfg, dseg[0], dseg[1], wt, off // bm, x1, p["final_norm_w"], ys_sorted, bm))
        off += m
    return outs


def _prep_params(cfg, norm_mix_w, w_in, conv_ssd_w, conv_ssd_b, dt_bias, a_log, d_skip, ssd_norm_w,
                 conv_mlstm_w, conv_mlstm_b, w_q, w_k, w_v, w_igate, b_igate, w_fgate, b_fgate, mlstm_norm_w,
                 w_out, norm_ffn_w, w_group, b_group, w_router, b_router, w_gate, w_up, w_down, final_norm_w):
    d = cfg.d_model
    hs = cfg.ssd_heads
    o_xbc = d + cfg.xbc_dim
    row = lambda v: v.reshape(1, -1).astype(F32)
    pad_lanes = lambda a: jnp.pad(a, ((0, 0), (0, LANES - a.shape[1])))
    w_in_t = w_in.T.astype(F32)
    w_dt_t = jnp.pad(w_in_t[o_xbc:o_xbc + hs], ((0, LANES - hs), (0, 0)))
    a = -jnp.exp(a_log.astype(F32))
    w_gates = jnp.concatenate([w_igate, w_fgate], axis=1)
    b_gates = jnp.concatenate([b_igate, b_fgate]).astype(F32)
    wr = pad_lanes(jnp.concatenate([w_router, w_group], axis=1))
    br = pad_lanes(jnp.concatenate([b_router, b_group]).reshape(1, -1).astype(F32))
    return dict(
        norm_mix_w=row(norm_mix_w),
        w_in_t=w_in_t, w_dt=w_dt_t.T.astype(BF16), w_dtT=w_dt_t.astype(BF16),
        bdt_row=pad_lanes(row(dt_bias)), bdt_col=pad_lanes(row(dt_bias)).T,
        cwx=conv_ssd_w.astype(F32), cbx=row(conv_ssd_b), cwm=conv_mlstm_w.astype(F32), cbm=row(conv_mlstm_b),
        wq=w_q.astype(BF16), wk=w_k.astype(BF16), wv=w_v.astype(BF16),
        wg=pad_lanes(w_gates).astype(BF16), wgT=w_gates.T.astype(BF16),
        bg_row=pad_lanes(row(b_gates)), bg_col=b_gates.reshape(-1, 1),
        a_row=pad_lanes(row(a)), a_col=a.reshape(-1, 1),
        dskip=row(jnp.repeat(d_skip.astype(F32), cfg.ssd_head_dim)),
        ssd_norm_w=row(ssd_norm_w), mlstm_norm_w=row(mlstm_norm_w), w_out=w_out.astype(BF16),
        norm_ffn_w=row(norm_ffn_w), wr=wr.astype(BF16), br=br,
        w_gate=w_gate.astype(F32), w_up=w_up.astype(F32), w_down=w_down.astype(F32),
        final_norm_w=row(final_norm_w),
    )


def forward(cfg, x_prompt, x_sample, state_ssd_conv, state_ssd, state_mlstm_conv, state_mlstm_c,
            state_mlstm_n, state_mlstm_m, meta_tokens, *weights):
    d = cfg.d_model
    nh = cfg.ml_heads
    hd = cfg.ml_head_dim
    assert state_ssd.shape[0] == 1, "single-layer kernel"
    p = _prep_params(cfg, *[w[0] for w in weights[:-1]], weights[-1])
    bp, lp, _ = x_prompt.shape
    bs, ls, _ = x_sample.shape
    n_meta = meta_tokens.shape[0]

    xp = x_prompt.reshape(bp * lp, d)
    xs = x_sample.reshape(bs * ls, d)
    mp, ms = bp * lp, bs * ls
    norm_args = (p["norm_mix_w"], p["w_dt"], p["w_dtT"], p["bdt_row"], p["bdt_col"])
    h_meta, d_meta, dT_meta = pre_norm(meta_tokens.astype(F32), None, *norm_args, n_meta)
    h, dcol, dT = pre_norm(xp, xs, *norm_args, 512 if mp % 512 == 0 and ms % 512 == 0 else 128)
    proj, proj_meta = in_proj(cfg, h, h_meta, p["w_in_t"], _tile(mp + ms, 1536), d // 2)

    zeros = lambda *s: jnp.zeros(s, F32)
    _, _, st_meta = _mixer_segment(
        cfg, proj_meta, 0, d_meta, dT_meta, 1, n_meta, zeros(1, SUBLANES, cfg.xbc_dim), zeros(1, SUBLANES, d),
        zeros(1, d, cfg.ssd_state), zeros(1, d, hd), zeros(1, nh, hd), zeros(1, nh, LANES), p)
    mt_xbc, mt_s, mt_xm, mt_c, mt_n, mt_m = st_meta
    rep = lambda a: jnp.broadcast_to(a, (bp,) + a.shape[1:])

    ys_p, hm_p, st_p = _mixer_segment(
        cfg, proj, 0, dcol[:mp], dT[:, :mp], bp, lp, rep(_pad_hist(mt_xbc)), rep(_pad_hist(mt_xm)),
        rep(mt_s), rep(mt_c), rep(mt_n), rep(mt_m), p)

    m0 = jnp.broadcast_to(state_mlstm_m[0].astype(F32)[:, :, None], (bs, nh, LANES))
    ys_s, hm_s, st_s = _mixer_segment(
        cfg, proj, mp, dcol[mp:], dT[:, mp:], bs, ls, _pad_hist(state_ssd_conv[0]),
        _pad_hist(state_mlstm_conv[0]), state_ssd[0].reshape(bs, d, cfg.ssd_state),
        state_mlstm_c[0].reshape(bs, d, hd), state_mlstm_n[0], m0, p)
    y_p, y_s = _ffn(cfg, [(xp, proj, 0, ys_p, hm_p), (xs, proj, mp, ys_s, hm_s)], p)
    y_prompt = y_p.reshape(bp, lp, d)
    y_sample = y_s.reshape(bs, ls, d)

    def pack(st, b):
        t_xbc, s_new, t_xm, c_new, n_new, m_new = st
        return (t_xbc[None], s_new.reshape(1, b, cfg.ssd_heads, cfg.ssd_head_dim, cfg.ssd_state),
                t_xm[None], c_new.reshape(1, b, nh, hd, hd), n_new[None], m_new[None, :, :, 0])

    return (y_prompt, y_sample) + pack(st_p, bp) + pack(st_s, bs)


def kernel(x_prompt, x_sample, state_ssd_conv, state_ssd, state_mlstm_conv, state_mlstm_c, state_mlstm_n, state_mlstm_m, meta_tokens, norm_mix_w, w_in, conv_ssd_w, conv_ssd_b, dt_bias, a_log, d_skip, ssd_norm_w, conv_mlstm_w, conv_mlstm_b, w_q, w_k, w_v, w_igate, b_igate, w_fgate, b_fgate, mlstm_norm_w, w_out, norm_ffn_w, w_group, b_group, w_router, b_router, w_gate, w_up, w_down, final_norm_w):
    return forward(Cfg(), x_prompt, x_sample, state_ssd_conv, state_ssd, state_mlstm_conv, state_mlstm_c,
                   state_mlstm_n, state_mlstm_m, meta_tokens, norm_mix_w, w_in, conv_ssd_w, conv_ssd_b, dt_bias,
                   a_log, d_skip, ssd_norm_w, conv_mlstm_w, conv_mlstm_b, w_q, w_k, w_v, w_igate, b_igate,
                   w_fgate, b_fgate, mlstm_norm_w, w_out, norm_ffn_w, w_group, b_group, w_router, b_router,
                   w_gate, w_up, w_down, final_norm_w)
```

```python
import functools
from typing import NamedTuple

import jax
import jax.numpy as jnp
from jax import lax
from jax.experimental import pallas as pl
from jax.experimental.pallas import tpu as pltpu

F32 = jnp.float32
BF16 = jnp.bfloat16
EPS = 1e-6
LANES = 128
SUBLANES = 8
VMEM_LIMIT = 56 * 1024 * 1024
HI = lax.Precision.HIGHEST


class Cfg(NamedTuple):
    d_model: int = 2048
    ssd_heads: int = 32
    ssd_head_dim: int = 64
    ssd_groups: int = 4
    ssd_state: int = 128
    ml_heads: int = 8
    ml_head_dim: int = 256
    n_groups: int = 4
    experts_per_group: int = 8
    d_expert: int = 512
    conv_w: int = 4
    chunk: int = 128

    @property
    def bc_dim(self):
        return self.ssd_groups * self.ssd_state

    @property
    def xbc_dim(self):
        return self.d_model + 2 * self.bc_dim

    @property
    def n_experts(self):
        return self.n_groups * self.experts_per_group


def _cparams(sem):
    return pltpu.CompilerParams(dimension_semantics=sem, vmem_limit_bytes=VMEM_LIMIT)


def _softplus(x):
    return jnp.maximum(x, 0.0) + jnp.log1p(jnp.exp(-jnp.abs(x)))


def _sigmoid(x):
    return 1.0 / (1.0 + jnp.exp(-x))


def _silu(x):
    return x * _sigmoid(x)


def _rms(x, w):
    return x * lax.rsqrt(jnp.mean(x * x, axis=-1, keepdims=True) + EPS) * w


def _prenorm_kernel(x_ref, nw_ref, wdt_ref, wdtT_ref, bdt_row_ref, bdt_col_ref, h_ref, d_ref, dT_ref):
    hb = _rms(x_ref[...], nw_ref[...]).astype(BF16)
    h_ref[...] = hb
    dt = jnp.dot(hb, wdt_ref[...], preferred_element_type=F32)
    d_ref[...] = _softplus(dt + bdt_row_ref[...])
    dtT = lax.dot_general(wdtT_ref[...], hb, (((1,), (1,)), ((), ())), preferred_element_type=F32)
    dT_ref[...] = _softplus(dtT + bdt_col_ref[...])


def _prenorm_pair_kernel(n_a, xa_ref, xb_ref, nw_ref, wdt_ref, wdtT_ref, bdt_row_ref, bdt_col_ref,
                         h_ref, d_ref, dT_ref, x_ref):
    i = pl.program_id(0)

    @pl.when(i < n_a)
    def _():
        x_ref[...] = xa_ref[...]

    @pl.when(i >= n_a)
    def _():
        x_ref[...] = xb_ref[...]

    _prenorm_kernel(x_ref, nw_ref, wdt_ref, wdtT_ref, bdt_row_ref, bdt_col_ref, h_ref, d_ref, dT_ref)


def pre_norm(xa, xb, norm_w, w_dt, w_dtT, bdt_row, bdt_col, bm):
    d = xa.shape[1]
    n_a = xa.shape[0] // bm
    n_b = 0 if xb is None else xb.shape[0] // bm
    m = (n_a + n_b) * bm
    const2 = lambda i: (0, 0)
    w_specs = [pl.BlockSpec((1, d), const2), pl.BlockSpec((d, LANES), const2), pl.BlockSpec((LANES, d), const2),
               pl.BlockSpec((1, LANES), const2), pl.BlockSpec((LANES, 1), const2)]
    if xb is None:
        body, x_specs, xs, scratch = _prenorm_kernel, [pl.BlockSpec((bm, d), lambda i: (i, 0))], (xa,), []
    else:
        body = functools.partial(_prenorm_pair_kernel, n_a)
        x_specs = [pl.BlockSpec((bm, d), lambda i: (jnp.minimum(i, n_a - 1), 0)),
                   pl.BlockSpec((bm, d), lambda i: (jnp.maximum(i - n_a, 0), 0))]
        xs, scratch = (xa, xb), [pltpu.VMEM((bm, d), F32)]
    return pl.pallas_call(
        body,
        out_shape=(jax.ShapeDtypeStruct((m, d), BF16), jax.ShapeDtypeStruct((m, LANES), F32),
                   jax.ShapeDtypeStruct((LANES, m), F32)),
        grid=(n_a + n_b,),
        in_specs=x_specs + w_specs,
        out_specs=(pl.BlockSpec((bm, d), lambda i: (i, 0)), pl.BlockSpec((bm, LANES), lambda i: (i, 0)),
                   pl.BlockSpec((LANES, bm), lambda i: (0, i))),
        scratch_shapes=scratch,
        compiler_params=_cparams(("arbitrary",)),
        name="pre_norm",
    )(*xs, norm_w, w_dt, w_dtT, bdt_row, bdt_col)


def _inproj_kernel(h_ref, hs_ref, wt_ref, proj_ref, projs_ref, w_ref):
    nt_dims = (((1,), (1,)), ((), ()))

    @pl.when(pl.program_id(1) == 0)
    def _():
        w_ref[...] = wt_ref[...].astype(BF16)
        projs_ref[...] = lax.dot_general(hs_ref[...], w_ref[...], nt_dims, preferred_element_type=F32)

    proj_ref[...] = lax.dot_general(h_ref[...], w_ref[...], nt_dims, preferred_element_type=F32)


def in_proj(cfg, h, h_small, w_in_t, bm, bn):
    m, d = h.shape
    ms = h_small.shape[0]
    nz = d // bn
    nx = cfg.xbc_dim // bn
    n_a = nz + nx
    n_blocks = n_a + 2 * nz
    skip = cfg.ssd_heads
    assert skip % SUBLANES == 0
    w_row = lambda j: pl.multiple_of(jnp.where(j < n_a, j * bn, j * bn + skip), SUBLANES)
    out_col = lambda j: jnp.where(j < nz, j, jnp.where(j < n_a, j + 2 * nz, j - nx))
    return pl.pallas_call(
        _inproj_kernel,
        out_shape=(jax.ShapeDtypeStruct((m, n_blocks * bn), F32), jax.ShapeDtypeStruct((ms, n_blocks * bn), F32)),
        grid=(n_blocks, m // bm),
        in_specs=[pl.BlockSpec((bm, d), lambda j, i: (i, 0)),
                  pl.BlockSpec((ms, d), lambda j, i: (0, 0)),
                  pl.BlockSpec((pl.Element(bn), pl.Element(d)), lambda j, i: (w_row(j), 0))],
        out_specs=(pl.BlockSpec((bm, bn), lambda j, i: (i, out_col(j))),
                   pl.BlockSpec((ms, bn), lambda j, i: (0, out_col(j)))),
        scratch_shapes=[pltpu.VMEM((bn, d), BF16)],
        compiler_params=_cparams(("arbitrary", "arbitrary")),
        name="in_proj",
    )(h, h_small, w_in_t)


CONV_LANES = 128


def _causal_conv(u, prev, w_ref, b_ref, cols, conv_w):
    lt = u.shape[0]
    row8 = lax.broadcasted_iota(jnp.int32, (SUBLANES, u.shape[1]), 0)
    acc = u * w_ref[conv_w - 1:conv_w, cols] + b_ref[:, cols]
    for s in range(1, conv_w):
        rolled = pltpu.roll(u, s, axis=0)
        head = jnp.where(row8 < s, pltpu.roll(prev, s, axis=0), rolled[0:SUBLANES])
        shifted = head if lt == SUBLANES else jnp.concatenate([head, rolled[SUBLANES:]], axis=0)
        acc = acc + shifted * w_ref[conv_w - 1 - s:conv_w - s, cols]
    return acc


def _conv_qkv_tile(cfg, xbc_ref, xm_ref, cwx_ref, cbx_ref, cwm_ref, cbm_ref, wq_ref, wk_ref, wv_ref, wg_ref, wgT_ref,
                   bg_row_ref, bg_col_ref, xa_ref, q_ref, k_ref, v_ref, g_ref, gT_ref, px_ref, pm_ref):
    lt = xbc_ref.shape[0]
    hd = cfg.ml_head_dim
    nh = cfg.ml_heads

    for c0 in range(0, xbc_ref.shape[1], CONV_LANES):
        cols = slice(c0, min(c0 + CONV_LANES, xbc_ref.shape[1]))
        u = xbc_ref[:, cols]
        xa_ref[:, cols] = _silu(_causal_conv(u, px_ref[:, cols], cwx_ref, cbx_ref, cols, cfg.conv_w))
        px_ref[:, cols] = u[lt - SUBLANES:lt]

    d = nh * hd
    kscale = hd ** -0.5
    nt = (((1,), (1,)), ((), ()))
    gcol = jnp.zeros(g_ref.shape, F32) + bg_row_ref[...]
    grow = jnp.zeros(gT_ref.shape, F32) + bg_col_ref[...]
    for h in range(nh):
        sl = slice(h * hd, (h + 1) * hd)
        xm = xm_ref[:, sl]
        xc = _silu(_causal_conv(xm, pm_ref[:, sl], cwm_ref, cbm_ref, sl, cfg.conv_w)).astype(BF16)
        pm_ref[:, sl] = xm[lt - SUBLANES:lt]
        qh = jnp.dot(xc, wq_ref[h], preferred_element_type=F32)
        kh = jnp.dot(xc, wk_ref[h], preferred_element_type=F32) * kscale
        vh = jnp.dot(xm.astype(BF16), wv_ref[h], preferred_element_type=F32)
        q_ref[:, sl] = qh
        k_ref[:, sl] = kh
        v_ref[:, sl] = vh
        for part, val in enumerate((qh, kh, vh)):
            vb = val.astype(BF16)
            rows = slice(part * d + h * hd, part * d + (h + 1) * hd)
            gcol = gcol + jnp.dot(vb, wg_ref[rows, :], preferred_element_type=F32)
            grow = grow + lax.dot_general(wgT_ref[:, rows], vb, nt, preferred_element_type=F32)
    lane = lax.broadcasted_iota(jnp.int32, gcol.shape, 1)
    g_ref[...] = jnp.where(lane < nh, gcol, -_softplus(-gcol))
    row = lax.broadcasted_iota(jnp.int32, grow.shape, 0)
    gT_ref[...] = jnp.where(row < nh, grow, -_softplus(-grow))


def _conv_qkv_kernel(cfg, xbc_ref, xm_ref, hxbc_ref, hxm_ref, cwx_ref, cbx_ref, cwm_ref, cbm_ref,
                     wq_ref, wk_ref, wv_ref, wg_ref, wgT_ref, bg_row_ref, bg_col_ref,
                     xa_ref, q_ref, k_ref, v_ref, g_ref, gT_ref, px_ref, pm_ref):
    @pl.when(pl.program_id(1) == 0)
    def _():
        px_ref[...] = hxbc_ref[0]
        pm_ref[...] = hxm_ref[0]

    _conv_qkv_tile(cfg, xbc_ref, xm_ref, cwx_ref, cbx_ref, cwm_ref, cbm_ref, wq_ref, wk_ref, wv_ref, wg_ref, wgT_ref,
                   bg_row_ref, bg_col_ref, xa_ref, q_ref, k_ref, v_ref, g_ref, gT_ref.at[0, 0], px_ref, pm_ref)


def conv_qkv(cfg, proj, hist_xbc, hist_xm, cwx, cbx, cwm, cbm, wq, wk, wv, wg, wgT, bg_row, bg_col,
             row0, n_seq, seq_len, lt):
    d = cfg.d_model
    xbc = cfg.xbc_dim
    m = n_seq * seq_len
    nt = seq_len // lt
    ng = 2 * cfg.ml_heads
    xbc_blk = (3 * d) // xbc
    blk0 = row0 // lt
    row = lambda s, l: (s * nt + l, 0)
    const2 = lambda s, l: (0, 0)
    const3 = lambda s, l: (0, 0, 0)
    return pl.pallas_call(
        functools.partial(_conv_qkv_kernel, cfg),
        out_shape=(jax.ShapeDtypeStruct((m, xbc), F32),
                   jax.ShapeDtypeStruct((m, d), F32), jax.ShapeDtypeStruct((m, d), F32),
                   jax.ShapeDtypeStruct((m, d), F32),
                   jax.ShapeDtypeStruct((m, LANES), F32),
                   jax.ShapeDtypeStruct((n_seq, nt, ng, lt), F32)),
        grid=(n_seq, nt),
        in_specs=[pl.BlockSpec((lt, xbc), lambda s, l: (blk0 + s * nt + l, xbc_blk)),
                  pl.BlockSpec((lt, d), lambda s, l: (blk0 + s * nt + l, 1)),
                  pl.BlockSpec((1, SUBLANES, xbc), lambda s, l: (s, 0, 0)),
                  pl.BlockSpec((1, SUBLANES, d), lambda s, l: (s, 0, 0)),
                  pl.BlockSpec((cfg.conv_w, xbc), const2), pl.BlockSpec((1, xbc), const2),
                  pl.BlockSpec((cfg.conv_w, d), const2), pl.BlockSpec((1, d), const2),
                  pl.BlockSpec(wq.shape, const3), pl.BlockSpec(wk.shape, const3),
                  pl.BlockSpec(wv.shape, const3),
                  pl.BlockSpec(wg.shape, const2), pl.BlockSpec(wgT.shape, const2),
                  pl.BlockSpec((1, LANES), const2), pl.BlockSpec((ng, 1), const2)],
        out_specs=(pl.BlockSpec((lt, xbc), row), pl.BlockSpec((lt, d), row), pl.BlockSpec((lt, d), row),
                   pl.BlockSpec((lt, d), row), pl.BlockSpec((lt, LANES), row),
                   pl.BlockSpec((1, 1, ng, lt), lambda s, l: (s, l, 0, 0))),
        scratch_shapes=[pltpu.VMEM((SUBLANES, xbc), F32), pltpu.VMEM((SUBLANES, d), F32)],
        compiler_params=_cparams(("arbitrary", "arbitrary")),
        name="conv_qkv",
    )(proj, proj, hist_xbc, hist_xm, cwx, cbx, cwm, cbm, wq, wk, wv, wg, wgT, bg_row, bg_col)


def _tri(q, lower):
    r = lax.broadcasted_iota(jnp.int32, (q, q), 0)
    c = lax.broadcasted_iota(jnp.int32, (q, q), 1)
    return (c <= r) if lower else (r <= c)


def _ssd_chunk(cfg, xa_ref, d_ref, dT_ref, arow_ref, acol_ref, dskip_ref, y_ref, st_ref):
    q = xa_ref.shape[0]
    dm = cfg.d_model
    ns = cfg.ssd_state
    hp = cfg.ssd_head_dim
    hpg = cfg.ssd_heads // cfg.ssd_groups
    heads_per_tile = LANES // hp
    n_tiles = cfg.ssd_heads // heads_per_tile

    causal = _tri(q, True)
    tril = causal.astype(F32)
    triu = _tri(q, False).astype(F32)
    dcol = d_ref[...]
    drow = dT_ref[...]
    acum = jnp.dot(tril, dcol * arow_ref[...], precision=HI, preferred_element_type=F32)
    acumT = jnp.dot(drow * acol_ref[...], triu, precision=HI, preferred_element_type=F32)
    nt_dims = (((1,), (1,)), ((), ()))
    tn_dims = (((0,), (0,)), ((), ()))
    lane = lax.broadcasted_iota(jnp.int32, (q, LANES), 1)
    srow = lax.broadcasted_iota(jnp.int32, (LANES, ns), 0)

    cbs = []
    bgs = []
    cgs = []
    for g in range(cfg.ssd_groups):
        bg = xa_ref[:, dm + g * ns: dm + (g + 1) * ns].astype(BF16)
        cg = xa_ref[:, dm + cfg.bc_dim + g * ns: dm + cfg.bc_dim + (g + 1) * ns].astype(BF16)
        cbs.append(lax.dot_general(cg, bg, nt_dims, preferred_element_type=F32))
        bgs.append(bg)
        cgs.append(cg)

    for t in range(n_tiles):
        h0 = t * heads_per_tile
        g = h0 // hpg
        cols = slice(t * LANES, (t + 1) * LANES)
        x = xa_ref[:, cols]
        dsel = jnp.zeros((q, LANES), F32)
        asel = jnp.zeros((q, LANES), F32)
        rdec = jnp.zeros((LANES, ns), F32)
        for i in range(heads_per_tile):
            h = h0 + i
            in_head = (lane >= i * hp) & (lane < (i + 1) * hp)
            a_last = acum[q - 1:q, h:h + 1]
            dsel = jnp.where(in_head, dcol[:, h:h + 1], dsel)
            asel = jnp.where(in_head, acum[:, h:h + 1], asel)
            rdec = jnp.where((srow >= i * hp) & (srow < (i + 1) * hp), jnp.exp(a_last), rdec)
        esel = jnp.exp(asel)
        tsel = jnp.exp(asel[q - 1:q, :] - asel)
        xd = x * dsel
        y = x * dskip_ref[:, cols]
        for i in range(heads_per_tile):
            h = h0 + i
            in_head = (lane >= i * hp) & (lane < (i + 1) * hp)
            seg = jnp.where(causal, acum[:, h:h + 1] - acumT[h:h + 1, :], -jnp.inf)
            w = (cbs[g] * jnp.exp(seg)).astype(BF16)
            xdh = jnp.where(in_head, xd, 0.0).astype(BF16)
            y = y + jnp.dot(w, xdh, preferred_element_type=F32)
        s_old = st_ref[cols, :]
        ys = lax.dot_general(cgs[g], s_old.astype(BF16), nt_dims, preferred_element_type=F32)
        y_ref[:, cols] = y + esel * ys
        upd = lax.dot_general((xd * tsel).astype(BF16), bgs[g], tn_dims, preferred_element_type=F32)
        st_ref[cols, :] = rdec * s_old + upd


def _ssd_kernel(cfg, n_chunks, xa_ref, d_ref, dT_ref, s0_ref, arow_ref, acol_ref, dskip_ref,
                y_ref, sout_ref, st_ref):
    @pl.when(pl.program_id(1) == 0)
    def _():
        st_ref[...] = s0_ref[...]

    for b in range(xa_ref.shape[0]):
        _ssd_chunk(cfg, xa_ref.at[b], d_ref.at[b], dT_ref.at[b, 0], arow_ref, acol_ref, dskip_ref,
                   y_ref.at[b], st_ref.at[b])

    @pl.when(pl.program_id(1) == n_chunks - 1)
    def _():
        sout_ref[...] = st_ref[...]


def ssd_scan(cfg, xa, d, dT, s0, a_row, a_col, dskip, n_seq, seq_len, q, nb):
    dm = cfg.d_model
    nc = seq_len // q
    blk3 = lambda s, c: (s, c, 0)
    st3 = lambda s, c: (s, 0, 0)
    const2 = lambda s, c: (0, 0)
    return pl.pallas_call(
        functools.partial(_ssd_kernel, cfg, nc),
        out_shape=(jax.ShapeDtypeStruct((n_seq, seq_len, dm), F32),
                   jax.ShapeDtypeStruct((n_seq, dm, cfg.ssd_state), F32)),
        grid=(n_seq // nb, nc),
        in_specs=[pl.BlockSpec((nb, q, cfg.xbc_dim), blk3),
                  pl.BlockSpec((nb, q, LANES), blk3),
                  pl.BlockSpec((nb, 1, cfg.ssd_heads, q), lambda s, c: (s, c, 0, 0)),
                  pl.BlockSpec((nb, dm, cfg.ssd_state), st3),
                  pl.BlockSpec((1, LANES), const2),
                  pl.BlockSpec((cfg.ssd_heads, 1), const2),
                  pl.BlockSpec((1, dm), const2)],
        out_specs=(pl.BlockSpec((nb, q, dm), blk3),
                   pl.BlockSpec((nb, dm, cfg.ssd_state), st3)),
        scratch_shapes=[pltpu.VMEM((nb, dm, cfg.ssd_state), F32)],
        compiler_params=_cparams(("arbitrary", "arbitrary")),
        name="ssd_scan",
    )(xa, d, dT, s0, a_row, a_col, dskip)


def _mlstm_chunk(cfg, q_ref, k_ref, v_ref, g_ref, gT_ref, h_ref, c_ref, n_ref, m_ref):
    ql = q_ref.shape[0]
    hd = cfg.ml_head_dim
    nh = cfg.ml_heads

    causal = _tri(ql, True)
    gcol = g_ref[...]
    grow = gT_ref[...]
    bcum = jnp.dot(causal.astype(F32), gcol, precision=HI, preferred_element_type=F32)
    bcumT = jnp.dot(grow, _tri(ql, False).astype(F32), precision=HI, preferred_element_type=F32)
    nt_dims = (((1,), (1,)), ((), ()))
    tn_dims = (((0,), (0,)), ((), ()))

    for h in range(nh):
        sl = slice(h * hd, (h + 1) * hd)
        b_col = bcum[:, nh + h:nh + h + 1]
        b_row = bcumT[nh + h:nh + h + 1, :]
        i_col = gcol[:, h:h + 1]
        i_row = grow[h:h + 1, :]
        m_prev = m_ref[h:h + 1, 0:1]
        rel = jnp.where(causal, i_row - b_row, -jnp.inf)
        inter = b_col + m_prev
        mt = jnp.maximum(inter, b_col + jnp.max(rel, axis=1, keepdims=True))
        qh = q_ref[:, sl]
        kh = k_ref[:, sl]
        vh = v_ref[:, sl]
        qb = qh.astype(BF16)
        kb = kh.astype(BF16)
        s = lax.dot_general(qb, kb, nt_dims, preferred_element_type=F32) * jnp.exp(rel + (b_col - mt))
        gdec = jnp.exp(inter - mt)
        c_old = c_ref[sl, :]
        n_old = n_ref[h:h + 1, :]
        qc = lax.dot_general(qb, c_old.astype(BF16), nt_dims, preferred_element_type=F32)
        num = jnp.dot(s.astype(BF16), vh.astype(BF16), preferred_element_type=F32) + gdec * qc
        den = jnp.sum(s, axis=1, keepdims=True) + gdec * jnp.sum(qh * n_old, axis=1, keepdims=True)
        h_ref[:, sl] = num / jnp.maximum(jnp.abs(den), jnp.exp(-mt))
        m_new = mt[ql - 1:ql, :]
        gs = jnp.exp(b_col[ql - 1:ql, :] - b_col + i_col - m_new)
        gc = jnp.exp(inter[ql - 1:ql, :] - m_new)
        upd = lax.dot_general((vh * gs).astype(BF16), kb, tn_dims, preferred_element_type=F32)
        c_ref[sl, :] = gc * c_old + upd
        n_ref[h:h + 1, :] = gc * n_old + jnp.sum(gs * kh, axis=0, keepdims=True)
        m_ref[h:h + 1, :] = jnp.broadcast_to(m_new, (1, LANES))


def _mlstm_kernel(cfg, n_chunks, q_ref, k_ref, v_ref, g_ref, gT_ref, c0_ref, n0_ref, m0_ref,
                  h_ref, cout_ref, nout_ref, mout_ref, c_ref, n_ref, m_ref):
    @pl.when(pl.program_id(1) == 0)
    def _():
        c_ref[...] = c0_ref[...]
        n_ref[...] = n0_ref[...]
        m_ref[...] = m0_ref[...]

    for b in range(q_ref.shape[0]):
        _mlstm_chunk(cfg, q_ref.at[b], k_ref.at[b], v_ref.at[b], g_ref.at[b], gT_ref.at[b, 0],
                     h_ref.at[b], c_ref.at[b], n_ref.at[b], m_ref.at[b])

    @pl.when(pl.program_id(1) == n_chunks - 1)
    def _():
        cout_ref[...] = c_ref[...]
        nout_ref[...] = n_ref[...]
        mout_ref[...] = m_ref[...]


def mlstm_scan(cfg, qa, ka, va, g, gT, c0, n0, m0, n_seq, seq_len, q, nb):
    d = cfg.d_model
    hd = cfg.ml_head_dim
    nh = cfg.ml_heads
    nc = seq_len // q
    blk3 = lambda s, c: (s, c, 0)
    st3 = lambda s, c: (s, 0, 0)
    return pl.pallas_call(
        functools.partial(_mlstm_kernel, cfg, nc),
        out_shape=(jax.ShapeDtypeStruct((n_seq, seq_len, d), F32),
                   jax.ShapeDtypeStruct((n_seq, d, hd), F32),
                   jax.ShapeDtypeStruct((n_seq, nh, hd), F32),
                   jax.ShapeDtypeStruct((n_seq, nh, LANES), F32)),
        grid=(n_seq // nb, nc),
        in_specs=[pl.BlockSpec((nb, q, d), blk3), pl.BlockSpec((nb, q, d), blk3), pl.BlockSpec((nb, q, d), blk3),
                  pl.BlockSpec((nb, q, LANES), blk3),
                  pl.BlockSpec((nb, 1, 2 * nh, q), lambda s, c: (s, c, 0, 0)),
                  pl.BlockSpec((nb, d, hd), st3), pl.BlockSpec((nb, nh, hd), st3),
                  pl.BlockSpec((nb, nh, LANES), st3)],
        out_specs=(pl.BlockSpec((nb, q, d), blk3),
                   pl.BlockSpec((nb, d, hd), st3), pl.BlockSpec((nb, nh, hd), st3),
                   pl.BlockSpec((nb, nh, LANES), st3)),
        scratch_shapes=[pltpu.VMEM((nb, d, hd), F32), pltpu.VMEM((nb, nh, hd), F32),
                        pltpu.VMEM((nb, nh, LANES), F32)],
        compiler_params=_cparams(("arbitrary", "arbitrary")),
        name="mlstm_scan",
    )(qa, ka, va, g, gT, c0, n0, m0)


def _group_norm(x, w_ref, col0, groups, width):
    parts = []
    for g in range(groups):
        seg = x[:, g * width:(g + 1) * width]
        parts.append(seg * lax.rsqrt(jnp.mean(seg * seg, axis=-1, keepdims=True) + EPS)
                     * w_ref[:, col0 + g * width: col0 + (g + 1) * width])
    return parts


def _outproj_kernel(cfg, ys_ref, z_ref, hm_ref, o_ref, x_ref, nws_ref, nwm_ref, w_ref, out_ref):
    d = cfg.d_model
    ws = d // cfg.ssd_groups
    yz = ys_ref[...] * _silu(z_ref[...])
    ssd_half = jnp.concatenate([part.astype(BF16) for part in _group_norm(yz, nws_ref, 0, cfg.ssd_groups, ws)],
                               axis=-1)
    acc = x_ref[...] + jnp.dot(ssd_half, w_ref[0:d, :], preferred_element_type=F32)
    gate = _sigmoid(o_ref[...])
    wm = cfg.ml_head_dim
    ml_half = jnp.concatenate(
        [(part * gate[:, g * wm:(g + 1) * wm]).astype(BF16)
         for g, part in enumerate(_group_norm(hm_ref[...], nwm_ref, 0, cfg.ml_heads, wm))], axis=-1)
    out_ref[...] = acc + jnp.dot(ml_half, w_ref[d:2 * d, :], preferred_element_type=F32)


def out_proj(cfg, ys, proj, proj_row0, hm, x, nws, nwm, w_out, bm):
    m, d = x.shape
    blk0 = proj_row0 // bm
    const2 = lambda i: (0, 0)
    full = lambda i: (i, 0)
    return pl.pallas_call(
        functools.partial(_outproj_kernel, cfg),
        out_shape=jax.ShapeDtypeStruct((m, d), F32),
        grid=(m // bm,),
        in_specs=[pl.BlockSpec((bm, d), full),
                  pl.BlockSpec((bm, d), lambda i: (i + blk0, 0)),
                  pl.BlockSpec((bm, d), full),
                  pl.BlockSpec((bm, d), lambda i: (i + blk0, 2)),
                  pl.BlockSpec((bm, d), full),
                  pl.BlockSpec((1, d), const2), pl.BlockSpec((1, d), const2),
                  pl.BlockSpec((2 * d, d), const2, pipeline_mode=pl.Buffered(1))],
        out_specs=pl.BlockSpec((bm, d), full),
        compiler_params=_cparams(("arbitrary",)),
        name="out_proj",
    )(ys, proj, hm, proj, x, nws, nwm, w_out)


def _router_kernel(cfg, n_a, xa_ref, xb_ref, nw_ref, wr_ref, br_ref, ei_ref, wt_ref, cnt_out_ref, rows_ref,
                   cnt_ref, x_ref):
    ne = cfg.n_experts
    epg = cfg.experts_per_group
    ngr = cfg.n_groups
    bm = x_ref.shape[0]
    i = pl.program_id(0)

    @pl.when(i == 0)
    def _():
        cnt_ref[...] = jnp.zeros_like(cnt_ref)

    @pl.when(i < n_a)
    def _():
        x_ref[...] = xa_ref[...]

    @pl.when(i >= n_a)
    def _():
        x_ref[...] = xb_ref[...]

    hb = _rms(x_ref[...], nw_ref[...]).astype(BF16)
    rows_ref[...] = _lanes_to_rows(_pack_bf16_pairs(hb))
    logits = jnp.dot(hb, wr_ref[...], preferred_element_type=F32) + br_ref[...]
    lane = lax.broadcasted_iota(jnp.int32, logits.shape, 1)
    big = jnp.int32(2 ** 30)
    neg = -jnp.inf

    def first_argmax(vals):
        mx = jnp.max(vals, axis=-1, keepdims=True)
        idx = jnp.min(jnp.where(vals == mx, lane, big), axis=-1, keepdims=True)
        return mx, idx

    is_group = (lane >= ne) & (lane < ne + ngr)
    gl = jnp.where(is_group, logits, neg)
    gmax, gidx = first_argmax(gl)
    p_g = 1.0 / jnp.sum(jnp.exp(gl - gmax), axis=-1, keepdims=True)
    e_lo = (gidx - ne) * epg
    in_sel = (lane >= e_lo) & (lane < e_lo + epg)
    el = jnp.where(in_sel, logits, neg)
    pe = jnp.exp(el - jnp.max(el, axis=-1, keepdims=True))
    pe = jnp.where(in_sel, pe / jnp.sum(pe, axis=-1, keepdims=True), -1.0)
    p1, i1 = first_argmax(pe)
    p2, i2 = first_argmax(jnp.where(lane == i1, -1.0, pe))
    wsum = p1 + p2
    wt_ref[...] = jnp.where(lane == 0, p_g * p1 / wsum, jnp.where(lane == 1, p_g * p2 / wsum, 0.0))

    oh1 = jnp.where(lane == i1, 1.0, 0.0)
    oh2 = jnp.where(lane == i2, 1.0, 0.0)
    r = lax.broadcasted_iota(jnp.int32, (bm, bm), 0)
    c = lax.broadcasted_iota(jnp.int32, (bm, bm), 1)
    before = jnp.where(c < r, 1.0, 0.0).astype(BF16)
    ahead1 = jnp.dot(before, oh1.astype(BF16), preferred_element_type=F32)
    ahead2 = jnp.dot(before, oh2.astype(BF16), preferred_element_type=F32)
    cnt = cnt_ref[...]
    tot1 = jnp.sum(oh1, axis=0, keepdims=True)
    rank1 = jnp.sum(oh1 * (cnt + ahead1), axis=-1, keepdims=True)
    rank2 = jnp.sum(oh2 * (cnt + tot1 + ahead2), axis=-1, keepdims=True)
    cnt_new = cnt + tot1 + jnp.sum(oh2, axis=0, keepdims=True)
    cnt_ref[...] = cnt_new
    cnt_out_ref[...] = cnt_new
    ei_ref[...] = jnp.where(lane == 0, i1, jnp.where(lane == 1, i2, jnp.where(
        lane == 2, rank1.astype(jnp.int32), jnp.where(lane == 3, rank2.astype(jnp.int32), 0))))


def router(cfg, xa, xb, nw, wr, br, bm):
    d = xa.shape[1]
    n_a = xa.shape[0] // bm
    n_b = xb.shape[0] // bm
    m = xa.shape[0] + xb.shape[0]
    nch = d // (2 * LANES)
    const2 = lambda i: (0, 0)
    return pl.pallas_call(
        functools.partial(_router_kernel, cfg, n_a),
        out_shape=(jax.ShapeDtypeStruct((m, LANES), jnp.int32), jax.ShapeDtypeStruct((m, LANES), F32),
                   jax.ShapeDtypeStruct((1, LANES), F32), jax.ShapeDtypeStruct((m, nch, LANES), jnp.uint32)),
        grid=(n_a + n_b,),
        in_specs=[pl.BlockSpec((bm, d), lambda i: (jnp.minimum(i, n_a - 1), 0)),
                  pl.BlockSpec((bm, d), lambda i: (jnp.maximum(i - n_a, 0), 0)),
                  pl.BlockSpec((1, d), const2),
                  pl.BlockSpec((d, LANES), const2), pl.BlockSpec((1, LANES), const2)],
        out_specs=(pl.BlockSpec((bm, LANES), lambda i: (i, 0)), pl.BlockSpec((bm, LANES), lambda i: (i, 0)),
                   pl.BlockSpec((1, LANES), const2), pl.BlockSpec((bm, nch, LANES), lambda i: (i, 0, 0))),
        scratch_shapes=[pltpu.VMEM((1, LANES), F32), pltpu.VMEM((bm, d), F32)],
        compiler_params=_cparams(("arbitrary",)),
        name="router",
    )(xa, xb, nw, wr, br)


def _pack_bf16_pairs(x):
    half = x.shape[1] // 2
    bits = lambda v: lax.bitcast_convert_type(v.astype(BF16).astype(F32), jnp.uint32)
    return bits(x[:, :half]) | (bits(x[:, half:]) >> 16)


def _unpack_bf16_pairs(p):
    hi = lax.bitcast_convert_type(p & jnp.uint32(0xFFFF0000), F32)
    lo = lax.bitcast_convert_type(p << 16, F32)
    return jnp.concatenate([hi, lo], axis=-1)


def _rows_to_lanes(g):
    t = pltpu.einshape("rcl->crl", g)
    return jnp.concatenate([t[c] for c in range(t.shape[0])], axis=-1)


def _lanes_to_rows(x):
    parts = jnp.stack([x[:, c * LANES:(c + 1) * LANES] for c in range(x.shape[1] // LANES)], axis=0)
    return pltpu.einshape("crl->rcl", parts)


GATHER_GROUP = 8


def _gather_rows(idx_ref, src_hbm, dst, sem, n_groups):
    def body(g, carry):
        for u in range(GATHER_GROUP):
            r = g * GATHER_GROUP + u
            pltpu.make_async_copy(src_hbm.at[idx_ref[0, 0, r]], dst.at[r], sem).start()
        return carry
    lax.fori_loop(0, n_groups, body, 0)


def _wait_rows(src_hbm, dst, sem, n):
    pltpu.make_async_copy(src_hbm.at[pl.ds(0, n)], dst.at[pl.ds(0, n)], sem).wait()


def _moe_kernel(cfg, te_ref, nr_ref, first_ref, wslot_ref, nexte_ref, nv_ref, src_ref, srcn_ref, x_hbm,
                wg_hbm, wu_hbm, wd_hbm, ys_ref, xbuf, sem, wgf, wuf, wdf, wsem, wgb, wub, wdb):
    j = pl.program_id(0)
    n_valid = nv_ref[0]

    def weight_copies(e, slot):
        return (pltpu.make_async_copy(wg_hbm.at[e], wgf.at[slot], wsem.at[slot, 0]),
                pltpu.make_async_copy(wu_hbm.at[e], wuf.at[slot], wsem.at[slot, 1]),
                pltpu.make_async_copy(wd_hbm.at[e], wdf.at[slot], wsem.at[slot, 2]))

    @pl.when(j == 0)
    def _():
        for cp in weight_copies(te_ref[0], 0):
            cp.start()
        xbuf[...] = jnp.zeros_like(xbuf)
        _gather_rows(src_ref, x_hbm, xbuf.at[0], sem.at[0], nr_ref[0] // GATHER_GROUP)

    @pl.when(j + 1 < n_valid)
    def _():
        nslot = (j + 1) % 2
        _gather_rows(srcn_ref, x_hbm, xbuf.at[nslot], sem.at[nslot], nr_ref[j + 1] // GATHER_GROUP)

    @pl.when(j < n_valid)
    def _():
        @pl.when(first_ref[j] == 1)
        def _():
            ws = wslot_ref[j]
            for cp in weight_copies(te_ref[j], ws):
                cp.wait()

            @pl.when(nexte_ref[j] >= 0)
            def _():
                for cp in weight_copies(nexte_ref[j], 1 - ws):
                    cp.start()

            wgb[...] = wgf[ws].astype(BF16)
            wub[...] = wuf[ws].astype(BF16)
            wdb[...] = wdf[ws].astype(BF16)

        slot = j % 2
        _wait_rows(x_hbm, xbuf.at[slot], sem.at[slot], nr_ref[j])
        hb = _unpack_bf16_pairs(_rows_to_lanes(xbuf[slot])).astype(BF16)
        hid = (_silu(jnp.dot(hb, wgb[...], preferred_element_type=F32))
               * jnp.dot(hb, wub[...], preferred_element_type=F32))
        y = jnp.dot(hid.astype(BF16), wdb[...], preferred_element_type=F32)
        ys_ref[...] = _lanes_to_rows(_pack_bf16_pairs(y))

    @pl.when(j >= n_valid)
    def _():
        ys_ref[...] = jnp.zeros_like(ys_ref)


def moe_routed(cfg, tables, src, x_rows, wg, wu, wd, tm):
    n_tiles = src.shape[0]
    d = cfg.d_model
    de = cfg.d_expert
    nch = x_rows.shape[1]
    return pl.pallas_call(
        functools.partial(_moe_kernel, cfg),
        out_shape=jax.ShapeDtypeStruct((n_tiles * tm, nch, LANES), jnp.uint32),
        grid_spec=pltpu.PrefetchScalarGridSpec(
            num_scalar_prefetch=len(tables),
            grid=(n_tiles,),
            in_specs=[pl.BlockSpec((1, 1, tm), lambda j, *_: (j, 0, 0), memory_space=pltpu.SMEM),
                      pl.BlockSpec((1, 1, tm), lambda j, *_: (jnp.minimum(j + 1, n_tiles - 1), 0, 0),
                                   memory_space=pltpu.SMEM),
                      pl.BlockSpec(memory_space=pl.ANY),
                      pl.BlockSpec(memory_space=pl.ANY), pl.BlockSpec(memory_space=pl.ANY),
                      pl.BlockSpec(memory_space=pl.ANY)],
            out_specs=pl.BlockSpec((tm, nch, LANES), lambda j, *_: (j, 0, 0)),
            scratch_shapes=[pltpu.VMEM((2, tm, nch, LANES), jnp.uint32), pltpu.SemaphoreType.DMA((2,)),
                            pltpu.VMEM((2, d, de), F32), pltpu.VMEM((2, d, de), F32), pltpu.VMEM((2, de, d), F32),
                            pltpu.SemaphoreType.DMA((2, 3)),
                            pltpu.VMEM((d, de), BF16), pltpu.VMEM((d, de), BF16), pltpu.VMEM((de, d), BF16)]),
        compiler_params=_cparams(("arbitrary",)),
        name="moe",
    )(*tables, src, src, x_rows, wg, wu, wd)


def _combine_kernel(n_steps, d0_ref, d1_ref, d0n_ref, d1n_ref, wt_ref, x1_ref, fw_ref, ys_hbm, y_ref, gbuf, sem):
    i = pl.program_id(0)
    groups = x1_ref.shape[0] // GATHER_GROUP

    def fetch(r0_ref, r1_ref, slot):
        _gather_rows(r0_ref, ys_hbm, gbuf.at[slot, 0], sem.at[slot, 0], groups)
        _gather_rows(r1_ref, ys_hbm, gbuf.at[slot, 1], sem.at[slot, 1], groups)

    @pl.when(i == 0)
    def _():
        fetch(d0_ref, d1_ref, 0)

    @pl.when(i + 1 < n_steps)
    def _():
        fetch(d0n_ref, d1n_ref, (i + 1) % 2)

    slot = i % 2
    bm = x1_ref.shape[0]
    _wait_rows(ys_hbm, gbuf.at[slot, 0], sem.at[slot, 0], bm)
    _wait_rows(ys_hbm, gbuf.at[slot, 1], sem.at[slot, 1], bm)
    wt = wt_ref[...]
    acc = (x1_ref[...] + wt[:, 0:1] * _unpack_bf16_pairs(_rows_to_lanes(gbuf[slot, 0]))
           + wt[:, 1:2] * _unpack_bf16_pairs(_rows_to_lanes(gbuf[slot, 1])))
    y_ref[...] = _rms(acc, fw_ref[...])


def moe_combine(cfg, dest0, dest1, wt, wt_blk0, x1, fw, ys, bm):
    m, d = x1.shape
    nch = ys.shape[1]
    n = m // bm
    cur = lambda: pl.BlockSpec((1, 1, bm), lambda i: (i, 0, 0), memory_space=pltpu.SMEM)
    nxt = lambda: pl.BlockSpec((1, 1, bm), lambda i: (jnp.minimum(i + 1, n - 1), 0, 0), memory_space=pltpu.SMEM)
    return pl.pallas_call(
        functools.partial(_combine_kernel, n),
        out_shape=jax.ShapeDtypeStruct((m, d), F32),
        grid=(n,),
        in_specs=[cur(), cur(), nxt(), nxt(),
                  pl.BlockSpec((bm, LANES), lambda i: (i + wt_blk0, 0)),
                  pl.BlockSpec((bm, d), lambda i: (i, 0)), pl.BlockSpec((1, d), lambda i: (0, 0)),
                  pl.BlockSpec(memory_space=pl.ANY)],
        out_specs=pl.BlockSpec((bm, d), lambda i: (i, 0)),
        scratch_shapes=[pltpu.VMEM((2, 2, bm, nch, LANES), jnp.uint32), pltpu.SemaphoreType.DMA((2, 2))],
        compiler_params=_cparams(("arbitrary",)),
        name="moe_combine",
    )(dest0, dest1, dest0, dest1, wt, x1, fw, ys)


def _route_tables(cfg, ei, cnt, tm, n_tiles):
    ne = cfg.n_experts
    i32 = jnp.int32
    counts = cnt[0, :ne].astype(i32)
    tiles_e = (counts + tm - 1) // tm
    tile_end = jnp.cumsum(tiles_e)
    tile_start = tile_end - tiles_e
    row_off = tile_start * tm
    tile_id = jnp.arange(n_tiles, dtype=i32)
    tile_expert = jnp.minimum(jnp.sum((tile_id[:, None] >= tile_end[None, :]).astype(i32), axis=1), ne - 1)
    used = jnp.clip(counts[tile_expert] - (tile_id - tile_start[tile_expert]) * tm, 0, tm)
    used = jnp.where(tile_id < tile_end[-1], used, 0)
    tile_rows = (used + GATHER_GROUP - 1) // GATHER_GROUP * GATHER_GROUP
    n_valid = tile_end[-1]
    prev_expert = jnp.concatenate([jnp.full((1,), -1, i32), tile_expert[:-1]])
    first = ((tile_expert != prev_expert) & (tile_id < n_valid)).astype(i32)
    wslot = (jnp.cumsum(first) - 1) % 2
    e_id = jnp.arange(ne, dtype=i32)
    later = (e_id[None, :] > e_id[:, None]) & (tiles_e[None, :] > 0)
    next_e = jnp.min(jnp.where(later, e_id[None, :], ne), axis=1)
    next_e = jnp.where(next_e < ne, next_e, -1)[tile_expert]
    tables = (tile_expert, tile_rows, first, wslot.astype(i32), next_e.astype(i32), n_valid[None].astype(i32))
    picks = ei[:, 0:4].T
    pick_off = jnp.sum(jnp.where(picks[None, 0:2] == e_id[:, None, None], row_off[:, None, None], 0), axis=0)
    dest = pick_off + picks[2:4]
    tok = jnp.broadcast_to(jnp.arange(ei.shape[0], dtype=i32)[None, :], dest.shape)
    src = jnp.zeros((n_tiles * tm,), i32).at[dest.reshape(-1)].set(
        tok.reshape(-1), unique_indices=True, mode="promise_in_bounds")
    return tables, src.reshape(n_tiles, 1, tm), dest


def _pad_hist(hist):
    return jnp.pad(hist, ((0, 0), (SUBLANES - hist.shape[1], 0), (0, 0)))


def _tile(m, pref):
    return pref if m % pref == 0 else m


def _mixer_segment(cfg, proj, row0, dcol, dT, n_seq, seq_len, hist_xbc, hist_xm, s0, c0, n0, m0, p):
    d = cfg.d_model
    m = n_seq * seq_len
    q = min(cfg.chunk, seq_len)
    nc = seq_len // q
    xa, qa, ka, va, g, gT = conv_qkv(cfg, proj, hist_xbc, hist_xm, p["cwx"], p["cbx"], p["cwm"], p["cbm"],
                                     p["wq"], p["wk"], p["wv"], p["wg"], p["wgT"], p["bg_row"], p["bg_col"],
                                     row0, n_seq, seq_len, q)
    dTc = dT[:cfg.ssd_heads].reshape(cfg.ssd_heads, n_seq, nc, q).transpose(1, 2, 0, 3)
    nb = 2 if (nc == 1 and n_seq % 2 == 0) else 1
    seq3 = lambda a: a.reshape(n_seq, seq_len, a.shape[-1])
    ys, s_new = ssd_scan(cfg, seq3(xa), seq3(dcol), dTc, s0, p["a_row"], p["a_col"], p["dskip"], n_seq, seq_len, q, nb)
    hm, c_new, n_new, m_new = mlstm_scan(cfg, seq3(qa), seq3(ka), seq3(va), seq3(g), gT, c0, n0, m0,
                                         n_seq, seq_len, q, nb)
    ys = ys.reshape(m, d)
    hm = hm.reshape(m, d)
    keep = cfg.conv_w - 1
    groups = proj.reshape(proj.shape[0] // SUBLANES, SUBLANES, proj.shape[1])
    first = (row0 + seq_len) // SUBLANES - 1
    step = seq_len // SUBLANES
    tail = lax.slice(groups, (first, SUBLANES - keep, 0),
                     (first + (n_seq - 1) * step + 1, SUBLANES, proj.shape[1]), (step, 1, 1))
    tail_xbc = tail[:, :, 3 * d:]
    tail_xm = tail[:, :, d:2 * d]
    return ys, hm, (tail_xbc, s_new, tail_xm, c_new, n_new, m_new)


MOE_TILE = 256


def _ffn(cfg, segments, p):
    d = cfg.d_model
    x1s = [out_proj(cfg, ys, proj, row0, hm, x2d, p["ssd_norm_w"], p["mlstm_norm_w"], p["w_out"],
                    _tile(x2d.shape[0], 256)) for x2d, proj, row0, ys, hm in segments]
    fits = lambda t: all(x1.shape[0] % t == 0 for x1 in x1s)
    ei, wt, cnt, x_rows = router(cfg, x1s[0], x1s[1], p["norm_ffn_w"], p["wr"], p["br"], 512 if fits(512) else 128)
    bm = 256 if fits(256) else 128
    n_tok = ei.shape[0]
    n_tiles = (2 * n_tok + cfg.n_experts * (MOE_TILE - 1)) // MOE_TILE
    tables, src, dest = _route_tables(cfg, ei, cnt, MOE_TILE, n_tiles)
    ys_sorted = moe_routed(cfg, tables, src, x_rows, p["w_gate"], p["w_up"], p["w_down"], MOE_TILE)
    outs = []
    off = 0
    for x1 in x1s:
        m = x1.shape[0]
        dseg = dest[:, off:off + m].reshape(2, m // bm, 1, bm)
        outs.append(moe_combine(cfg, dseg[0], dseg[1], wt, off // bm, x1, p["final_norm_w"], ys_sorted, bm))
        off += m
    return outs


def _prep_params(cfg, norm_mix_w, w_in, conv_ssd_w, conv_ssd_b, dt_bias, a_log, d_skip, ssd_norm_w,
                 conv_mlstm_w, conv_mlstm_b, w_q, w_k, w_v, w_igate, b_igate, w_fgate, b_fgate, mlstm_norm_w,
                 w_out, norm_ffn_w, w_group, b_group, w_router, b_router, w_gate, w_up, w_down, final_norm_w):
    d = cfg.d_model
    hs = cfg.ssd_heads
    o_xbc = d + cfg.xbc_dim
    row = lambda v: v.reshape(1, -1).astype(F32)
    pad_lanes = lambda a: jnp.pad(a, ((0, 0), (0, LANES - a.shape[1])))
    w_in_t = w_in.T.astype(F32)
    w_dt_t = jnp.pad(w_in_t[o_xbc:o_xbc + hs], ((0, LANES - hs), (0, 0)))
    a = -jnp.exp(a_log.astype(F32))
    w_gates = jnp.concatenate([w_igate, w_fgate], axis=1)
    b_gates = jnp.concatenate([b_igate, b_fgate]).astype(F32)
    wr = pad_lanes(jnp.concatenate([w_router, w_group], axis=1))
    br = pad_lanes(jnp.concatenate([b_router, b_group]).reshape(1, -1).astype(F32))
    return dict(
        norm_mix_w=row(norm_mix_w),
        w_in_t=w_in_t, w_dt=w_dt_t.T.astype(BF16), w_dtT=w_dt_t.astype(BF16),
        bdt_row=pad_lanes(row(dt_bias)), bdt_col=pad_lanes(row(dt_bias)).T,
        cwx=conv_ssd_w.astype(F32), cbx=row(conv_ssd_b), cwm=conv_mlstm_w.astype(F32), cbm=row(conv_mlstm_b),
        wq=w_q.astype(BF16), wk=w_k.astype(BF16), wv=w_v.astype(BF16),
        wg=pad_lanes(w_gates).astype(BF16), wgT=w_gates.T.astype(BF16),
        bg_row=pad_lanes(row(b_gates)), bg_col=b_gates.reshape(-1, 1),
        a_row=pad_lanes(row(a)), a_col=a.reshape(-1, 1),
        dskip=row(jnp.repeat(d_skip.astype(F32), cfg.ssd_head_dim)),
        ssd_norm_w=row(ssd_norm_w), mlstm_norm_w=row(mlstm_norm_w), w_out=w_out.astype(BF16),
        norm_ffn_w=row(norm_ffn_w), wr=wr.astype(BF16), br=br,
        w_gate=w_gate.astype(F32), w_up=w_up.astype(F32), w_down=w_down.astype(F32),
        final_norm_w=row(final_norm_w),
    )


def forward(cfg, x_prompt, x_sample, state_ssd_conv, state_ssd, state_mlstm_conv, state_mlstm_c,
            state_mlstm_n, state_mlstm_m, meta_tokens, *weights):
    d = cfg.d_model
    nh = cfg.ml_heads
    hd = cfg.ml_head_dim
    assert state_ssd.shape[0] == 1, "single-layer kernel"
    p = _prep_params(cfg, *[w[0] for w in weights[:-1]], weights[-1])
    bp, lp, _ = x_prompt.shape
    bs, ls, _ = x_sample.shape
    n_meta = meta_tokens.shape[0]

    xp = x_prompt.reshape(bp * lp, d)
    xs = x_sample.reshape(bs * ls, d)
    mp, ms = bp * lp, bs * ls
    norm_args = (p["norm_mix_w"], p["w_dt"], p["w_dtT"], p["bdt_row"], p["bdt_col"])
    h_meta, d_meta, dT_meta = pre_norm(meta_tokens.astype(F32), None, *norm_args, n_meta)
    h, dcol, dT = pre_norm(xp, xs, *norm_args, 512 if mp % 512 == 0 and ms % 512 == 0 else 128)
    proj, proj_meta = in_proj(cfg, h, h_meta, p["w_in_t"], _tile(mp + ms, 1536), d // 2)

    zeros = lambda *s: jnp.zeros(s, F32)
    _, _, st_meta = _mixer_segment(
        cfg, proj_meta, 0, d_meta, dT_meta, 1, n_meta, zeros(1, SUBLANES, cfg.xbc_dim), zeros(1, SUBLANES, d),
        zeros(1, d, cfg.ssd_state), zeros(1, d, hd), zeros(1, nh, hd), zeros(1, nh, LANES), p)
    mt_xbc, mt_s, mt_xm, mt_c, mt_n, mt_m = st_meta
    rep = lambda a: jnp.broadcast_to(a, (bp,) + a.shape[1:])

    ys_p, hm_p, st_p = _mixer_segment(
        cfg, proj, 0, dcol[:mp], dT[:, :mp], bp, lp, rep(_pad_hist(mt_xbc)), rep(_pad_hist(mt_xm)),
        rep(mt_s), rep(mt_c), rep(mt_n), rep(mt_m), p)

    m0 = jnp.broadcast_to(state_mlstm_m[0].astype(F32)[:, :, None], (bs, nh, LANES))
    ys_s, hm_s, st_s = _mixer_segment(
        cfg, proj, mp, dcol[mp:], dT[:, mp:], bs, ls, _pad_hist(state_ssd_conv[0]),
        _pad_hist(state_mlstm_conv[0]), state_ssd[0].reshape(bs, d, cfg.ssd_state),
        state_mlstm_c[0].reshape(bs, d, hd), state_mlstm_n[0], m0, p)
    y_p, y_s = _ffn(cfg, [(xp, proj, 0, ys_p, hm_p), (xs, proj, mp, ys_s, hm_s)], p)
    y_prompt = y_p.reshape(bp, lp, d)
    y_sample = y_s.reshape(bs, ls, d)

    def pack(st, b):
        t_xbc, s_new, t_xm, c_new, n_new, m_new = st
        return (t_xbc[None], s_new.reshape(1, b, cfg.ssd_heads, cfg.ssd_head_dim, cfg.ssd_state),
                t_xm[None], c_new.reshape(1, b, nh, hd, hd), n_new[None], m_new[None, :, :, 0])

    return (y_prompt, y_sample) + pack(st_p, bp) + pack(st_s, bs)


def kernel(x_prompt, x_sample, state_ssd_conv, state_ssd, state_mlstm_conv, state_mlstm_c, state_mlstm_n, state_mlstm_m, meta_tokens, norm_mix_w, w_in, conv_ssd_w, conv_ssd_b, dt_bias, a_log, d_skip, ssd_norm_w, conv_mlstm_w, conv_mlstm_b, w_q, w_k, w_v, w_igate, b_igate, w_fgate, b_fgate, mlstm_norm_w, w_out, norm_ffn_w, w_group, b_group, w_router, b_router, w_gate, w_up, w_down, final_norm_w):
    return forward(Cfg(), x_prompt, x_sample, state_ssd_conv, state_ssd, state_mlstm_conv, state_mlstm_c,
                   state_mlstm_n, state_mlstm_m, meta_tokens, norm_mix_w, w_in, conv_ssd_w, conv_ssd_b, dt_bias,
                   a_log, d_skip, ssd_norm_w, conv_mlstm_w, conv_mlstm_b, w_q, w_k, w_v, w_igate, b_igate,
                   w_fgate, b_fgate, mlstm_norm_w, w_out, norm_ffn_w, w_group, b_group, w_router, b_router,
                   w_gate, w_up, w_down, final_norm_w)
```
